```python
import math
import jax, jax.numpy as jnp
from jax import lax
import numpy as np


D_MODEL = 2048
BATCH = 4
SEQ = 8192
DEPTH = 2
DEC_BATCH = 4
DEC_SEQ = 4096
PAST_LEN = 128

A_HEADS = 4
A_QK_DIM = 64
A_V_DIM = 2 * A_QK_DIM
B_HEADS = 4
B_DIM = 128
C_HEADS = 4
C_DIM = 128
D_HEADS = 8
D_KV_HEADS = 2
D_DIM = 64
WINDOW = 128
BLOCK = 128
CONV_W = 3
D_FF = 5632
EPS = 1e-6

A_QK_WIDTH = A_HEADS * 2 * A_QK_DIM
A_WIDTH = A_HEADS * A_V_DIM
B_WIDTH = B_HEADS * B_DIM
C_WIDTH = C_HEADS * C_DIM
D_WIDTH = D_HEADS * D_DIM
D_KV_WIDTH = D_KV_HEADS * D_DIM
N_GATES = 4 * C_HEADS
MIX_WIDTH = A_WIDTH + B_WIDTH + C_WIDTH + D_WIDTH
PROJ_SIZES = (A_QK_WIDTH, A_QK_WIDTH, A_WIDTH,
              B_WIDTH, B_WIDTH, B_WIDTH, B_WIDTH,
              C_WIDTH, C_WIDTH, C_WIDTH, C_WIDTH, N_GATES,
              D_WIDTH, D_KV_WIDTH, D_KV_WIDTH)
N_IN = 2 * A_QK_WIDTH + A_WIDTH + 4 * B_WIDTH + 4 * C_WIDTH + N_GATES + D_WIDTH + 2 * D_KV_WIDTH

kernel_name = 'hymba_style_hybrid_bidir_encoder'


def _split_points():
    pts = []
    acc = 0
    for s in PROJ_SIZES[:-1]:
        acc += s
        pts.append(acc)
    return pts


def rms_norm(x, w):
    xf = x.astype(jnp.float32)
    y = xf * lax.rsqrt(jnp.mean(xf * xf, axis=-1, keepdims=True) + EPS)
    return (y * w.astype(jnp.float32)).astype(x.dtype)


def head_norm(x):
    xf = x.astype(jnp.float32)
    return xf * lax.rsqrt(jnp.mean(xf * xf, axis=-1, keepdims=True) + EPS)


def dw_conv_centred(x, w):
    S = x.shape[1]
    half = CONV_W // 2
    xp = jnp.pad(x, ((0, 0), (half, half), (0, 0)))
    out = xp[:, 0:S] * w[0]
    for j in range(1, CONV_W):
        out = out + xp[:, j:j + S] * w[j]
    return out


def alibi_slopes():
    n = A_HEADS + D_HEADS
    return 2.0 ** (-8.0 * jnp.arange(1, n + 1, dtype=jnp.float32) / n)


def flip(x):
    return jnp.flip(x, axis=1)


def diff_attention(q, k, v, lam, lam_init, slopes):
    B, S, H, dv = v.shape
    nb = S // BLOCK
    scale = A_QK_DIM ** -0.5
    key_pos = jnp.arange(S)

    def block(i):
        t0 = i * BLOCK
        qb = lax.dynamic_slice_in_dim(q, t0, BLOCK, axis=1)
        s = jnp.einsum('bqhmd,bshmd->bhmqs', qb, k, preferred_element_type=jnp.float32) * scale
        dist = jnp.abs((t0 + jnp.arange(BLOCK))[:, None] - key_pos[None, :]).astype(jnp.float32)
        s = s - slopes[None, :, None, None, None] * dist
        p = jax.nn.softmax(s, axis=-1)
        w = p[:, :, 0] - lam * p[:, :, 1]
        return jnp.einsum('bhqs,bshd->bqhd', w.astype(v.dtype), v)

    out = lax.map(block, jnp.arange(nb))
    out = out.transpose(1, 0, 2, 3, 4).reshape(B, S, H, dv)
    return head_norm(out) * (1.0 - lam_init)


def window_attention(q, k, v, sink, slopes):
    B, S, H, d = q.shape
    G = H // D_KV_HEADS
    nb = S // BLOCK
    span = BLOCK + 2 * WINDOW
    scale = D_DIM ** -0.5
    kp = jnp.pad(k, ((0, 0), (WINDOW, WINDOW), (0, 0), (0, 0)))
    vp = jnp.pad(v, ((0, 0), (WINDOW, WINDOW), (0, 0), (0, 0)))
    rel = jnp.arange(BLOCK)[:, None] - jnp.arange(span)[None, :] + WINDOW
    dist = jnp.abs(rel).astype(jnp.float32)
    slope_kg = slopes.reshape(D_KV_HEADS, G)[None, :, :, None, None]
    snk = sink.astype(jnp.float32).reshape(D_KV_HEADS, G)[None, :, :, None, None]

    def block(i):
        t0 = i * BLOCK
        qb = lax.dynamic_slice_in_dim(q, t0, BLOCK, axis=1).reshape(B, BLOCK, D_KV_HEADS, G, d)
        kb = lax.dynamic_slice_in_dim(kp, t0, span, axis=1)
        vb = lax.dynamic_slice_in_dim(vp, t0, span, axis=1)
        s = jnp.einsum('bqkgd,bskd->bkgqs', qb, kb, preferred_element_type=jnp.float32) * scale
        s = s - slope_kg * dist
        s_abs = t0 - WINDOW + jnp.arange(span)
        valid = (jnp.abs(rel) <= WINDOW) & (s_abs >= 0)[None, :] & (s_abs < S)[None, :]
        s = jnp.where(valid, s, -jnp.inf)
        m = jnp.maximum(jnp.max(s, axis=-1, keepdims=True), snk)
        p = jnp.exp(s - m)
        denom = jnp.sum(p, axis=-1, keepdims=True) + jnp.exp(snk - m)
        o = jnp.einsum('bkgqs,bskd->bqkgd', (p / denom).astype(v.dtype), vb)
        return o.reshape(B, BLOCK, H * d)

    out = lax.map(block, jnp.arange(nb))
    return out.transpose(1, 0, 2, 3).reshape(B, S, H * d)


def _to_chunks(x):
    B, S, H, d = x.shape
    return x.astype(jnp.float32).reshape(B, S // BLOCK, BLOCK, H, d).transpose(1, 0, 3, 2, 4)


def _from_chunks(y, B, S):
    nc, _, H, C, d = y.shape
    return y.transpose(1, 0, 3, 2, 4).reshape(B, S, H, d)


def retention_chunkwise(q, k, v, log_gamma):
    B, S, H, d = q.shape
    qc, kc, vc = _to_chunks(q), _to_chunks(k), _to_chunks(v)
    idx = jnp.arange(BLOCK, dtype=jnp.float32)
    diff = idx[:, None] - idx[None, :]
    lg = log_gamma.astype(jnp.float32)
    decay = jnp.where(diff >= 0, jnp.exp(lg[:, None, None] * jnp.maximum(diff, 0.0)), 0.0)
    q_decay = jnp.exp(lg[:, None] * (idx + 1.0)[None, :])
    k_decay = jnp.exp(lg[:, None] * (BLOCK - 1.0 - idx)[None, :])
    chunk_decay = jnp.exp(lg * BLOCK)

    def step(R, inp):
        qb, kb, vb = inp
        s = jnp.einsum('bhqd,bhkd->bhqk', qb, kb) * decay
        inner = jnp.einsum('bhqk,bhkd->bhqd', s, vb)
        cross = jnp.einsum('bhqd,bhde->bhqe', qb, R) * q_decay[None, :, :, None]
        R = R * chunk_decay[None, :, None, None] + jnp.einsum('bhkd,bhke->bhde', kb * k_decay[None, :, :, None], vb)
        return R, inner + cross

    R0 = jnp.zeros((B, H, d, d), jnp.float32)
    _, out = lax.scan(step, R0, (qc, kc, vc))
    return _from_chunks(out, B, S)


def mlstm_chunkwise(q, k, v, i_pre, f_pre):
    B, S, H, d = q.shape
    nc = S // BLOCK
    qc, kc, vc = _to_chunks(q), _to_chunks(k), _to_chunks(v)
    ic = i_pre.astype(jnp.float32).reshape(B, nc, BLOCK, H).transpose(1, 0, 3, 2)
    lfc = jax.nn.log_sigmoid(f_pre.astype(jnp.float32)).reshape(B, nc, BLOCK, H).transpose(1, 0, 3, 2)
    tril = jnp.tril(jnp.ones((BLOCK, BLOCK), dtype=bool))

    def step(carry, inp):
        Cs, ns, m = carry
        qb, kb, vb, ib, lfb = inp
        b = jnp.cumsum(lfb, axis=-1)
        dlog = jnp.where(tril, b[..., :, None] - b[..., None, :] + ib[..., None, :], -jnp.inf)
        inter = b + m[..., None]
        mt = jnp.maximum(jnp.max(dlog, axis=-1), inter)
        dw = jnp.exp(dlog - mt[..., None])
        iw = jnp.exp(inter - mt)
        s = jnp.einsum('bhqd,bhkd->bhqk', qb, kb) * dw
        num = jnp.einsum('bhqk,bhkd->bhqd', s, vb) + iw[..., None] * jnp.einsum('bhqd,bhde->bhqe', qb, Cs)
        den = jnp.sum(s, axis=-1) + iw * jnp.einsum('bhqd,bhd->bhq', qb, ns)
        h = num / jnp.maximum(jnp.abs(den), jnp.exp(-mt))[..., None]
        bl = b[..., -1]
        wlog = bl[..., None] - b + ib
        m_new = jnp.maximum(bl + m, jnp.max(wlog, axis=-1))
        cw = jnp.exp(bl + m - m_new)
        kw = kb * jnp.exp(wlog - m_new[..., None])[..., None]
        Cs = cw[..., None, None] * Cs + jnp.einsum('bhkd,bhke->bhde', kw, vb)
        ns = cw[..., None] * ns + jnp.sum(kw, axis=-2)
        return (Cs, ns, m_new), h

    init = (jnp.zeros((B, H, d, d), jnp.float32), jnp.zeros((B, H, d), jnp.float32), jnp.zeros((B, H), jnp.float32))
    _, out = lax.scan(step, init, (qc, kc, vc, ic, lfc))
    return _from_chunks(out, B, S)


def hybrid_layer(x, layer_idx, norm1_w, w_in, qk_conv_w, diff_lambda, ret_decay_logit,
                 mlstm_gate_bias, attn_sink, w_out, norm2_w, w_gate, w_up, ffn_conv_w, w_down):
    B, S, _ = x.shape
    h = rms_norm(x, norm1_w)
    proj = h @ w_in
    (a_q, a_k, a_v, b_q, b_k, b_v, b_g, c_q, c_k, c_v, c_o, c_gates,
     d_q, d_k, d_v) = jnp.split(proj, _split_points(), axis=-1)
    slopes = alibi_slopes()

    lam_init = 0.8 - 0.6 * math.exp(-0.3 * layer_idx)
    dl = diff_lambda.astype(jnp.float32)
    lam = jnp.exp(jnp.dot(dl[0], dl[1])) - jnp.exp(jnp.dot(dl[2], dl[3])) + lam_init
    y_a = diff_attention(a_q.reshape(B, S, A_HEADS, 2, A_QK_DIM), a_k.reshape(B, S, A_HEADS, 2, A_QK_DIM),
                         a_v.reshape(B, S, A_HEADS, A_V_DIM), lam, lam_init, slopes[D_HEADS:])
    y_a = y_a.reshape(B, S, A_WIDTH)

    bq = b_q.reshape(B, S, B_HEADS, B_DIM)
    bk = b_k.reshape(B, S, B_HEADS, B_DIM) * (B_DIM ** -0.5)
    bv = b_v.reshape(B, S, B_HEADS, B_DIM)
    lg = jax.nn.log_sigmoid(ret_decay_logit.astype(jnp.float32))
    ret = retention_chunkwise(bq, bk, bv, lg[0]) + flip(retention_chunkwise(flip(bq), flip(bk), flip(bv), lg[1]))
    y_b = jax.nn.silu(b_g.astype(jnp.float32)) * head_norm(ret).reshape(B, S, B_WIDTH)

    qk = jax.nn.silu(dw_conv_centred(jnp.concatenate([c_q, c_k], axis=-1), qk_conv_w))
    cq, ck = jnp.split(qk, 2, axis=-1)
    cq = cq.reshape(B, S, C_HEADS, C_DIM) * (C_DIM ** -0.5)
    ck = ck.reshape(B, S, C_HEADS, C_DIM)
    cv = c_v.reshape(B, S, C_HEADS, C_DIM)
    gates = c_gates.reshape(B, S, 4, C_HEADS).astype(jnp.float32) + mlstm_gate_bias.astype(jnp.float32)
    hc = mlstm_chunkwise(cq, ck, cv, gates[:, :, 0], gates[:, :, 1]) + flip(
        mlstm_chunkwise(flip(cq), flip(ck), flip(cv), flip(gates[:, :, 2]), flip(gates[:, :, 3])))
    y_c = jax.nn.sigmoid(c_o.astype(jnp.float32)) * head_norm(hc).reshape(B, S, C_WIDTH)

    y_d = window_attention(d_q.reshape(B, S, D_HEADS, D_DIM), d_k.reshape(B, S, D_KV_HEADS, D_DIM),
                           d_v.reshape(B, S, D_KV_HEADS, D_DIM), attn_sink, slopes[:D_HEADS])

    mixed = jnp.concatenate([y_a.astype(x.dtype), y_b.astype(x.dtype), y_c.astype(x.dtype), y_d.astype(x.dtype)], axis=-1)
    x = x + mixed @ w_out

    h = rms_norm(x, norm2_w)
    g = dw_conv_centred(h @ w_gate, ffn_conv_w)
    x = x + (jax.nn.silu(g) * (h @ w_up)) @ w_down
    return x


def setup_inputs(seed: int = 0) -> dict:
    key = jax.random.key(seed)
    ks = jax.random.split(key, 17)
    f32 = jnp.float32

    def nrm(k, shape, scale):
        return jax.random.normal(k, shape, f32) * scale

    gam = 1.0 - jnp.exp(jnp.linspace(math.log(1.0 / 32.0), math.log(1.0 / 512.0), B_HEADS))
    ret_base = jnp.log(gam) - jnp.log1p(-gam)
    f_bias = jnp.linspace(3.0, 6.0, C_HEADS)
    zeros_h = jnp.zeros((C_HEADS,), f32)
    gate_base = jnp.stack([zeros_h, f_bias, zeros_h, f_bias])
    return {
        'x_prompt': nrm(ks[0], (BATCH, SEQ, D_MODEL), 1.0),
        'x_sample': nrm(ks[1], (DEC_BATCH, DEC_SEQ, D_MODEL), 1.0),
        'norm1_w': 1.0 + nrm(ks[2], (DEPTH, D_MODEL), 0.02),
        'w_in': nrm(ks[3], (DEPTH, D_MODEL, N_IN), D_MODEL ** -0.5),
        'qk_conv_w': nrm(ks[4], (DEPTH, CONV_W, 2 * C_WIDTH), CONV_W ** -0.5),
        'diff_lambda': nrm(ks[5], (DEPTH, 4, A_QK_DIM), 0.1),
        'ret_decay_logit': ret_base + nrm(ks[6], (DEPTH, 2, B_HEADS), 0.01),
        'mlstm_gate_bias': gate_base + nrm(ks[7], (DEPTH, 4, C_HEADS), 0.1),
        'attn_sink': nrm(ks[8], (DEPTH, D_HEADS), 0.5),
        'w_out': nrm(ks[9], (DEPTH, MIX_WIDTH, D_MODEL), MIX_WIDTH ** -0.5),
        'norm2_w': 1.0 + nrm(ks[10], (DEPTH, D_MODEL), 0.02),
        'w_gate': nrm(ks[11], (DEPTH, D_MODEL, D_FF), D_MODEL ** -0.5),
        'w_up': nrm(ks[12], (DEPTH, D_MODEL, D_FF), D_MODEL ** -0.5),
        'ffn_conv_w': nrm(ks[13], (DEPTH, CONV_W, D_FF), CONV_W ** -0.5),
        'w_down': nrm(ks[14], (DEPTH, D_FF, D_MODEL), D_FF ** -0.5),
        'final_norm_w': 1.0 + nrm(ks[15], (D_MODEL,), 0.02),
    }


def reference(x_prompt, x_sample, norm1_w, w_in, qk_conv_w, diff_lambda, ret_decay_logit,
              mlstm_gate_bias, attn_sink, w_out, norm2_w, w_gate, w_up, ffn_conv_w, w_down, final_norm_w):
    def trunk(x):
        for l in range(DEPTH):
            x = hybrid_layer(x, l, norm1_w[l], w_in[l], qk_conv_w[l], diff_lambda[l], ret_decay_logit[l],
                             mlstm_gate_bias[l], attn_sink[l], w_out[l], norm2_w[l], w_gate[l], w_up[l],
                             ffn_conv_w[l], w_down[l])
        return rms_norm(x, final_norm_w)

    y_prompt = trunk(x_prompt)
    y_sample = trunk(x_sample)
    return (y_prompt, y_sample)
```

```python
import functools
import math

import jax
import jax.numpy as jnp
from jax import lax
from jax.experimental import pallas as pl
from jax.experimental.pallas import tpu as pltpu

D_MODEL = 2048
DEPTH = 2
A_HEADS = 4
A_QK_DIM = 64
A_V_DIM = 128
B_HEADS = 4
B_DIM = 128
C_HEADS = 4
C_DIM = 128
D_HEADS = 8
D_KV_HEADS = 2
D_DIM = 64
WINDOW = 128
BLOCK = 128
D_FF = 5632
EPS = 1e-6
LANES = 128

_OFF_A = 0
_OFF_B = 1536
_OFF_CQ = 3584
_OFF_CV = 4608
_OFF_GATES = 5632
_OFF_D = 5648
N_IN = 6416

_PB_AQ, _PB_AK, _PB_AV = 0, 4, 8
_PB_BQ, _PB_BK, _PB_BV, _PB_BG = 12, 16, 20, 24
_PB_CV, _PB_CO = 28, 32
_PB_DQ512 = 9
_PB_DK, _PB_DV = 40, 41
PB_WIDTH = 42 * LANES
_PF_CQ, _PF_CK, _PF_GATES = 0, 4, 8
PF_WIDTH = 12 * LANES

_VMEM_LIMIT = 56 * 1024 * 1024
_NEG_INF = float("-inf")


def _cparams(*sem):
    return pltpu.CompilerParams(dimension_semantics=sem, vmem_limit_bytes=_VMEM_LIMIT)


def _dot(a, b):
    return jnp.dot(a, b, preferred_element_type=jnp.float32)


def _dot_nt(a, b):
    return lax.dot_general(a, b, (((1,), (1,)), ((), ())), preferred_element_type=jnp.float32)


def _dot_tn(a, b):
    return lax.dot_general(a, b, (((0,), (0,)), ((), ())), preferred_element_type=jnp.float32)


def _log_sigmoid(x):
    return -(jnp.maximum(-x, 0.0) + jnp.log1p(jnp.exp(-jnp.abs(x))))


def _split_hi_lo(x):
    hi = x.astype(jnp.bfloat16)
    lo = (x - hi.astype(jnp.float32)).astype(jnp.bfloat16)
    return hi, lo


def _norm_matmul_kernel(x_ref, nw_ref, w_ref, o_ref, h_ref):
    @pl.when(pl.program_id(1) == 0)
    def _():
        x = x_ref[...]
        ms = jnp.mean(x * x, axis=-1, keepdims=True)
        h_ref[...] = (x * lax.rsqrt(ms + EPS) * nw_ref[...]).astype(h_ref.dtype)

    o_ref[...] = _dot(h_ref[...], w_ref[...]).astype(o_ref.dtype)


def _norm_matmul(x2d, nw, w, out_dtype, tm, tn):
    t, d = x2d.shape
    n = w.shape[1]
    return pl.pallas_call(
        _norm_matmul_kernel,
        grid=(t // tm, n // tn),
        in_specs=[
            pl.BlockSpec((tm, d), lambda i, j: (i, 0)),
            pl.BlockSpec((1, d), lambda i, j: (0, 0)),
            pl.BlockSpec((d, tn), lambda i, j: (0, j)),
        ],
        out_specs=pl.BlockSpec((tm, tn), lambda i, j: (i, j)),
        out_shape=jax.ShapeDtypeStruct((t, n), out_dtype),
        scratch_shapes=[pltpu.VMEM((tm, d), jnp.bfloat16)],
        compiler_params=_cparams("parallel", "arbitrary"),
        name="norm_in_proj",
    )(x2d, nw, w)


def _diff_attn_kernel(slopes_ref, q_ref, k_ref, v_ref, dl_ref, o_ref,
                      k0_ref, k1_ref, m_ref, l_ref, acc_ref, *, seq, tq, tk, lam_init):
    head = pl.program_id(1)
    qi = pl.program_id(2)

    @pl.when(qi == 0)
    def _():
        kk = k_ref[...]
        lane = lax.broadcasted_iota(jnp.int32, kk.shape, 1)
        zero = jnp.zeros_like(kk)
        k0_ref[...] = jnp.where(lane < A_QK_DIM, kk, zero)
        k1_ref[...] = jnp.where(lane >= A_QK_DIM, kk, zero)

    slope = slopes_ref[head]
    q = q_ref[...] * jnp.asarray(A_QK_DIM ** -0.5, q_ref.dtype)
    rel = (lax.broadcasted_iota(jnp.int32, (tq, tk), 0)
           - lax.broadcasted_iota(jnp.int32, (tq, tk), 1)).astype(jnp.float32)
    q0 = qi * tq

    m_ref[...] = jnp.full(m_ref.shape, _NEG_INF, jnp.float32)
    l_ref[...] = jnp.zeros(l_ref.shape, jnp.float32)
    acc_ref[...] = jnp.zeros(acc_ref.shape, jnp.float32)

    def body(kb, carry):
        k_start = pl.multiple_of(kb * tk, tk)
        off = (q0 - k_start).astype(jnp.float32)
        bias = slope * jnp.abs(rel + off)
        vb = v_ref[pl.ds(k_start, tk), :]
        for mp, kref in enumerate((k0_ref, k1_ref)):
            s = _dot_nt(q, kref[pl.ds(k_start, tk), :]) - bias
            m_old = m_ref[mp]
            m_new = jnp.maximum(m_old, jnp.max(s, axis=1, keepdims=True))
            p = jnp.exp(s - m_new)
            alpha = jnp.exp(m_old - m_new)
            l_ref[mp] = alpha * l_ref[mp] + jnp.sum(p, axis=1, keepdims=True)
            acc_ref[mp] = alpha * acc_ref[mp] + _dot(p.astype(vb.dtype), vb)
            m_ref[mp] = m_new
        return carry

    lax.fori_loop(0, seq // tk, body, 0)

    dl = dl_ref[...]
    lam = (jnp.exp(jnp.sum(dl[0:1] * dl[1:2], axis=1, keepdims=True))
           - jnp.exp(jnp.sum(dl[2:3] * dl[3:4], axis=1, keepdims=True)) + lam_init)
    o = acc_ref[0] / l_ref[0] - lam * (acc_ref[1] / l_ref[1])
    o = o * lax.rsqrt(jnp.mean(o * o, axis=-1, keepdims=True) + EPS)
    o_ref[...] = (o * (1.0 - lam_init)).astype(o_ref.dtype)


def _diff_attention(pb, diff_lambda, slopes_a, layer_idx, tq, tk):
    bsz, seq, _ = pb.shape
    lam_init = 0.8 - 0.6 * math.exp(-0.3 * layer_idx)
    kern = functools.partial(_diff_attn_kernel, seq=seq, tq=tq, tk=tk, lam_init=lam_init)
    return pl.pallas_call(
        kern,
        grid=(bsz, A_HEADS, seq // tq),
        in_specs=[
            pl.BlockSpec(memory_space=pltpu.SMEM),
            pl.BlockSpec((None, tq, LANES), lambda b, h, i: (b, i, _PB_AQ + h)),
            pl.BlockSpec((None, seq, LANES), lambda b, h, i: (b, 0, _PB_AK + h)),
            pl.BlockSpec((None, seq, LANES), lambda b, h, i: (b, 0, _PB_AV + h)),
            pl.BlockSpec((4, A_QK_DIM), lambda b, h, i: (0, 0)),
        ],
        out_specs=pl.BlockSpec((None, tq, LANES), lambda b, h, i: (b, i, h)),
        out_shape=jax.ShapeDtypeStruct((bsz, seq, A_HEADS * A_V_DIM), jnp.bfloat16),
        scratch_shapes=[
            pltpu.VMEM((seq, LANES), jnp.bfloat16),
            pltpu.VMEM((seq, LANES), jnp.bfloat16),
            pltpu.VMEM((2, tq, 1), jnp.float32),
            pltpu.VMEM((2, tq, 1), jnp.float32),
            pltpu.VMEM((2, tq, A_V_DIM), jnp.float32),
        ],
        compiler_params=_cparams("parallel", "parallel", "arbitrary"),
        name="diff_attention",
    )(slopes_a, pb, pb, pb, diff_lambda)


def _retention_kernel(q_ref, k_ref, v_ref, g_ref, dec_ref, o_ref, of_ref, ob_ref, *, seq):
    head = pl.program_id(1)
    nc = seq // BLOCK
    f32 = jnp.float32

    lgv = _log_sigmoid(dec_ref[...])
    rr = lax.broadcasted_iota(jnp.int32, lgv.shape, 0)
    cc = lax.broadcasted_iota(jnp.int32, lgv.shape, 1)

    def pick(row):
        sel = jnp.where((rr == row) & (cc == head), lgv, 0.0)
        return jnp.sum(jnp.sum(sel, axis=1, keepdims=True), axis=0, keepdims=True)

    lg_f, lg_b = pick(0), pick(1)
    ri = lax.broadcasted_iota(jnp.int32, (BLOCK, BLOCK), 0).astype(f32)
    ci = lax.broadcasted_iota(jnp.int32, (BLOCK, BLOCK), 1).astype(f32)
    diff = ri - ci
    decay_f = jnp.where(diff >= 0, jnp.exp(lg_f * jnp.maximum(diff, 0.0)), 0.0)
    decay_b = jnp.where(diff <= 0, jnp.exp(lg_b * jnp.maximum(-diff, 0.0)), 0.0)
    idx = lax.broadcasted_iota(jnp.int32, (BLOCK, 1), 0).astype(f32)
    qdec_f = jnp.exp(lg_f * (idx + 1.0))
    kdec_f = jnp.exp(lg_f * (BLOCK - 1.0 - idx))
    qdec_b = jnp.exp(lg_b * (BLOCK - idx))
    kdec_b = jnp.exp(lg_b * idx)
    cdec_f = jnp.exp(lg_f * BLOCK)
    cdec_b = jnp.exp(lg_b * BLOCK)
    kscale = B_DIM ** -0.5

    def one(t0, state, decay, qdec, kdec, cdec, out_ref):
        qb = q_ref[pl.ds(t0, BLOCK), :]
        kb = k_ref[pl.ds(t0, BLOCK), :].astype(f32) * kscale
        vb = v_ref[pl.ds(t0, BLOCK), :]
        s = _dot_nt(qb, kb.astype(qb.dtype)) * decay
        inner = _dot(s.astype(vb.dtype), vb)
        cross = _dot(qb, state.astype(qb.dtype)) * qdec
        out_ref[pl.ds(t0, BLOCK), :] = inner + cross
        return state * cdec + _dot_tn((kb * kdec).astype(vb.dtype), vb)

    def body(c, carry):
        rf, rb = carry
        tf = pl.multiple_of(c * BLOCK, BLOCK)
        tb = pl.multiple_of((nc - 1 - c) * BLOCK, BLOCK)
        rf = one(tf, rf, decay_f, qdec_f, kdec_f, cdec_f, of_ref)
        rb = one(tb, rb, decay_b, qdec_b, kdec_b, cdec_b, ob_ref)
        return rf, rb

    zero = jnp.zeros((B_DIM, B_DIM), f32)
    lax.fori_loop(0, nc, body, (zero, zero))

    def fin(c, carry):
        t0 = pl.multiple_of(c * BLOCK, BLOCK)
        r = of_ref[pl.ds(t0, BLOCK), :] + ob_ref[pl.ds(t0, BLOCK), :]
        r = r * lax.rsqrt(jnp.mean(r * r, axis=-1, keepdims=True) + EPS)
        g = g_ref[pl.ds(t0, BLOCK), :].astype(f32)
        o_ref[pl.ds(t0, BLOCK), :] = (g * jax.nn.sigmoid(g) * r).astype(o_ref.dtype)
        return carry

    lax.fori_loop(0, nc, fin, 0)


def _retention(pb, dec_pad):
    bsz, seq, _ = pb.shape
    kern = functools.partial(_retention_kernel, seq=seq)

    def col(base):
        return pl.BlockSpec((None, seq, LANES), lambda b, h: (b, 0, base + h))

    return pl.pallas_call(
        kern,
        grid=(bsz, B_HEADS),
        in_specs=[col(_PB_BQ), col(_PB_BK), col(_PB_BV), col(_PB_BG),
                  pl.BlockSpec((8, LANES), lambda b, h: (0, 0))],
        out_specs=pl.BlockSpec((None, seq, LANES), lambda b, h: (b, 0, h)),
        out_shape=jax.ShapeDtypeStruct((bsz, seq, B_HEADS * B_DIM), jnp.bfloat16),
        scratch_shapes=[pltpu.VMEM((seq, B_DIM), jnp.float32), pltpu.VMEM((seq, B_DIM), jnp.float32)],
        compiler_params=_cparams("parallel", "parallel"),
        name="retention",
    )(pb, pb, pb, pb, dec_pad)


def _mlstm_kernel(cq_ref, ck_ref, gt_ref, v_ref, og_ref, wq_ref, wk_ref, gb_ref, o_ref,
                  qc_ref, kc_ref, of_ref, ob_ref, *, seq):
    nc = seq // BLOCK
    f32 = jnp.float32
    bf16 = jnp.bfloat16
    rowi = lax.broadcasted_iota(jnp.int32, (BLOCK, LANES), 0)
    coli = lax.broadcasted_iota(jnp.int32, (BLOCK, LANES), 1)

    def conv_silu(src_ref, w, c, t0):
        x = src_ref[pl.ds(t0, BLOCK), :]
        prev = src_ref[pl.ds(jnp.maximum(t0 - 1, 0), 1), :] * (c > 0).astype(f32)
        nxt = src_ref[pl.ds(jnp.minimum(t0 + BLOCK, seq - 1), 1), :] * (c < nc - 1).astype(f32)
        xd = jnp.where(rowi == 0, prev, pltpu.roll(x, 1, 0))
        xu = jnp.where(rowi == BLOCK - 1, nxt, pltpu.roll(x, BLOCK - 1, 0))
        y = xd * w[0:1] + x * w[1:2] + xu * w[2:3]
        return y * jax.nn.sigmoid(y)

    def prep(c, carry):
        t0 = pl.multiple_of(c * BLOCK, BLOCK)
        qc_ref[pl.ds(t0, BLOCK), :] = (conv_silu(cq_ref, wq_ref[...], c, t0) * (C_DIM ** -0.5)).astype(bf16)
        kc_ref[pl.ds(t0, BLOCK), :] = conv_silu(ck_ref, wk_ref[...], c, t0).astype(bf16)
        return carry

    lax.fori_loop(0, nc, prep, 0)

    tril = rowi >= coli
    triu = rowi <= coli
    tril_b = tril.astype(bf16)
    triu_b = triu.astype(bf16)
    gbias = gb_ref[...]

    def gate_terms(t0, backward):
        gc = gt_ref[pl.ds(t0, BLOCK), :] + gbias
        gr = gc.T
        gi = 2 if backward else 0
        hi, lo = _split_hi_lo(_log_sigmoid(gr[0:8]))
        lanes_op = tril_b if backward else triu_b
        cs_r = _dot(hi, lanes_op) + _dot(lo, lanes_op)
        hi, lo = _split_hi_lo(_log_sigmoid(gc))
        subl_op = triu_b if backward else tril_b
        cs_c = _dot(subl_op, hi) + _dot(subl_op, lo)
        last = 0 if backward else LANES - 1
        return dict(i_row=gr[gi:gi + 1], i_col=gc[:, gi:gi + 1], b_row=cs_r[gi + 1:gi + 2],
                    b_col=cs_c[:, gi + 1:gi + 2], bl=cs_r[gi + 1:gi + 2, last:last + 1],
                    mask=triu if backward else tril)

    def step(t0, g, state, out_ref):
        cs, ns, m = state
        qb = qc_ref[pl.ds(t0, BLOCK), :]
        kb = kc_ref[pl.ds(t0, BLOCK), :]
        vb = v_ref[pl.ds(t0, BLOCK), :]
        a_row = g["i_row"] - g["b_row"]
        a_t = jnp.where(g["mask"], a_row, _NEG_INF)
        n_col = jnp.maximum(jnp.max(a_t, axis=1, keepdims=True), m)
        dw = jnp.exp(a_t - n_col)
        iw = jnp.exp(m - n_col)
        s = _dot_nt(qb, kb) * dw
        num = _dot(s.astype(bf16), vb) + iw * _dot(qb, cs.astype(bf16))
        den = jnp.sum(s, axis=1, keepdims=True) + iw * jnp.sum(qb.astype(f32) * ns, axis=1, keepdims=True)
        mt = g["b_col"] + n_col
        out_ref[pl.ds(t0, BLOCK), :] = num / jnp.maximum(jnp.abs(den), jnp.exp(-mt))
        bl = g["bl"]
        m_new = jnp.maximum(bl + m, jnp.max(bl + a_row, axis=1, keepdims=True))
        cw = jnp.exp(bl + m - m_new)
        kw = kb.astype(f32) * jnp.exp(bl - g["b_col"] + g["i_col"] - m_new)
        cs = cw * cs + _dot_tn(kw.astype(bf16), vb)
        ns = cw * ns + jnp.sum(kw, axis=0, keepdims=True)
        return cs, ns, m_new

    def body(c, carry):
        sf, sb = carry
        tf = pl.multiple_of(c * BLOCK, BLOCK)
        tb = pl.multiple_of((nc - 1 - c) * BLOCK, BLOCK)
        sf = step(tf, gate_terms(tf, False), sf, of_ref)
        sb = step(tb, gate_terms(tb, True), sb, ob_ref)
        return sf, sb

    init = (jnp.zeros((C_DIM, C_DIM), f32), jnp.zeros((1, C_DIM), f32), jnp.zeros((1, 1), f32))
    lax.fori_loop(0, nc, body, (init, init))

    def fin(c, carry):
        t0 = pl.multiple_of(c * BLOCK, BLOCK)
        r = of_ref[pl.ds(t0, BLOCK), :] + ob_ref[pl.ds(t0, BLOCK), :]
        r = r * lax.rsqrt(jnp.mean(r * r, axis=-1, keepdims=True) + EPS)
        og = og_ref[pl.ds(t0, BLOCK), :].astype(f32)
        o_ref[pl.ds(t0, BLOCK), :] = (jax.nn.sigmoid(og) * r).astype(o_ref.dtype)
        return carry

    lax.fori_loop(0, nc, fin, 0)


def _mlstm(pb, pf, wq_conv, wk_conv, gate_bias_pad):
    bsz, seq, _ = pb.shape
    kern = functools.partial(_mlstm_kernel, seq=seq)

    def col(base):
        return pl.BlockSpec((None, seq, LANES), lambda b, h: (b, 0, base + h))

    return pl.pallas_call(
        kern,
        grid=(bsz, C_HEADS),
        in_specs=[col(_PF_CQ), col(_PF_CK), col(_PF_GATES), col(_PB_CV), col(_PB_CO),
                  pl.BlockSpec((3, LANES), lambda b, h: (0, h)),
                  pl.BlockSpec((3, LANES), lambda b, h: (0, h)),
                  pl.BlockSpec((None, 1, LANES), lambda b, h: (h, 0, 0))],
        out_specs=pl.BlockSpec((None, seq, LANES), lambda b, h: (b, 0, h)),
        out_shape=jax.ShapeDtypeStruct((bsz, seq, C_HEADS * C_DIM), jnp.bfloat16),
        scratch_shapes=[pltpu.VMEM((seq, C_DIM), jnp.bfloat16), pltpu.VMEM((seq, C_DIM), jnp.bfloat16),
                        pltpu.VMEM((seq, C_DIM), jnp.float32), pltpu.VMEM((seq, C_DIM), jnp.float32)],
        compiler_params=_cparams("parallel", "parallel"),
        name="mlstm",
    )(pf, pf, pf, pb, pb, wq_conv, wk_conv, gate_bias_pad)


def _window_attn_kernel(slopes_ref, sink_ref, q_ref, k_ref, v_ref, o_ref, kx_ref, vx_ref, *, seq):
    qi = pl.program_id(1)
    f32 = jnp.float32
    span = BLOCK + 2 * WINDOW
    group = D_HEADS // D_KV_HEADS

    @pl.when(qi == 0)
    def _():
        lane = lax.broadcasted_iota(jnp.int32, (seq, LANES), 1)
        for src, dst in ((k_ref, kx_ref), (v_ref, vx_ref)):
            x = src[...].astype(f32)
            x0 = jnp.where(lane < D_DIM, x, 0.0)
            x1 = jnp.where(lane >= D_DIM, x, 0.0)
            dst[0] = x0.astype(dst.dtype)
            dst[1] = pltpu.roll(x0, D_DIM, 1).astype(dst.dtype)
            dst[2] = pltpu.roll(x1, D_DIM, 1).astype(dst.dtype)
            dst[3] = x1.astype(dst.dtype)

    t0 = qi * BLOCK
    ws = pl.multiple_of(jnp.clip(t0 - WINDOW, 0, seq - span), BLOCK)
    rel = (lax.broadcasted_iota(jnp.int32, (BLOCK, span), 0)
           - lax.broadcasted_iota(jnp.int32, (BLOCK, span), 1)) + (t0 - ws)
    dist = jnp.abs(rel)
    valid = dist <= WINDOW
    distf = dist.astype(f32)
    scale = jnp.asarray(D_DIM ** -0.5, q_ref.dtype)

    for pair in range(D_HEADS // 2):
        kv = (2 * pair) // group
        qp = q_ref[:, pair * LANES:(pair + 1) * LANES] * scale
        out = None
        for half in range(2):
            hd = 2 * pair + half
            kk = kx_ref[2 * kv + half, pl.ds(ws, span), :]
            vv = vx_ref[2 * kv + half, pl.ds(ws, span), :]
            s = _dot_nt(qp, kk) - slopes_ref[hd] * distf
            s = jnp.where(valid, s, _NEG_INF)
            snk = sink_ref[hd]
            m = jnp.maximum(jnp.max(s, axis=1, keepdims=True), snk)
            p = jnp.exp(s - m)
            denom = jnp.sum(p, axis=1, keepdims=True) + jnp.exp(snk - m)
            o = _dot((p / denom).astype(vv.dtype), vv)
            out = o if out is None else out + o
        o_ref[:, pair * LANES:(pair + 1) * LANES] = out.astype(o_ref.dtype)


def _window_attention(pb, attn_sink, slopes_d):
    bsz, seq, _ = pb.shape
    kern = functools.partial(_window_attn_kernel, seq=seq)
    width = D_HEADS * D_DIM
    return pl.pallas_call(
        kern,
        grid=(bsz, seq // BLOCK),
        in_specs=[
            pl.BlockSpec(memory_space=pltpu.SMEM),
            pl.BlockSpec(memory_space=pltpu.SMEM),
            pl.BlockSpec((None, BLOCK, width), lambda b, i: (b, i, _PB_DQ512)),
            pl.BlockSpec((None, seq, LANES), lambda b, i: (b, 0, _PB_DK)),
            pl.BlockSpec((None, seq, LANES), lambda b, i: (b, 0, _PB_DV)),
        ],
        out_specs=pl.BlockSpec((None, BLOCK, width), lambda b, i: (b, i, 0)),
        out_shape=jax.ShapeDtypeStruct((bsz, seq, width), jnp.bfloat16),
        scratch_shapes=[pltpu.VMEM((4, seq, LANES), jnp.bfloat16), pltpu.VMEM((4, seq, LANES), jnp.bfloat16)],
        compiler_params=_cparams("parallel", "arbitrary"),
        name="window_attention",
    )(slopes_d, attn_sink, pb, pb, pb)


def _out_proj_kernel(x_ref, ya_ref, yb_ref, yc_ref, yd_ref, w_ref, o_ref):
    acc = x_ref[...]
    for g, y_ref in enumerate((ya_ref, yb_ref, yc_ref, yd_ref)):
        acc = acc + _dot(y_ref[...], w_ref[g * 512:(g + 1) * 512, :])
    o_ref[...] = acc


def _out_proj(x2d, ys, w_out, tm):
    t, d = x2d.shape
    yspec = pl.BlockSpec((tm, 512), lambda i: (i, 0))
    return pl.pallas_call(
        _out_proj_kernel,
        grid=(t // tm,),
        in_specs=[pl.BlockSpec((tm, d), lambda i: (i, 0)), yspec, yspec, yspec, yspec,
                  pl.BlockSpec((d, d), lambda i: (0, 0))],
        out_specs=pl.BlockSpec((tm, d), lambda i: (i, 0)),
        out_shape=jax.ShapeDtypeStruct((t, d), jnp.float32),
        compiler_params=_cparams("parallel"),
        name="out_proj",
    )(x2d, *ys, w_out)


def _ffn_kernel(x_ref, xp_ref, xn_ref, nw_ref, wg_ref, wu_ref, cw_ref, wd_ref, fw_ref, o_ref,
                h_ref, acc_ref, *, seq, tm, final_norm):
    i = pl.program_id(0)
    j = pl.program_id(1)
    f32 = jnp.float32

    def norm(x):
        return x * lax.rsqrt(jnp.mean(x * x, axis=-1, keepdims=True) + EPS) * nw_ref[...]

    @pl.when(j == 0)
    def _():
        h_ref[0:tm, :] = norm(x_ref[...]).astype(h_ref.dtype)
        has_prev = ((i * tm) % seq != 0).astype(f32)
        has_next = (((i + 1) * tm) % seq != 0).astype(f32)
        halo = jnp.concatenate([norm(xp_ref[...]) * has_prev, norm(xn_ref[...]) * has_next], axis=0)
        h_ref[tm:tm + 16, :] = halo.astype(h_ref.dtype)
        acc_ref[...] = jnp.zeros(acc_ref.shape, f32)

    gate = _dot(h_ref[...], wg_ref[...])
    gm = gate[0:tm]
    rowi = lax.broadcasted_iota(jnp.int32, gm.shape, 0)
    g_dn = jnp.where(rowi == 0, gate[tm + 7:tm + 8], pltpu.roll(gm, 1, 0))
    g_up = jnp.where(rowi == tm - 1, gate[tm + 8:tm + 9], pltpu.roll(gm, tm - 1, 0))
    cw = cw_ref[...]
    g = g_dn * cw[0:1] + gm * cw[1:2] + g_up * cw[2:3]
    up = _dot(h_ref[0:tm, :], wu_ref[...])
    act = (g * jax.nn.sigmoid(g) * up).astype(wd_ref.dtype)
    acc_ref[...] += _dot(act, wd_ref[...])

    @pl.when(j == pl.num_programs(1) - 1)
    def _():
        y = x_ref[...] + acc_ref[...]
        if final_norm:
            y = y * lax.rsqrt(jnp.mean(y * y, axis=-1, keepdims=True) + EPS) * fw_ref[...]
        o_ref[...] = y


def _ffn(x2d, seq, nw, wg, wu, cw, wd, fw, final_norm, tm, tn):
    t, d = x2d.shape
    nff = wg.shape[1]
    rb = tm // 8
    last8 = t // 8 - 1
    kern = functools.partial(_ffn_kernel, seq=seq, tm=tm, final_norm=final_norm)
    return pl.pallas_call(
        kern,
        grid=(t // tm, nff // tn),
        in_specs=[
            pl.BlockSpec((tm, d), lambda i, j: (i, 0)),
            pl.BlockSpec((8, d), lambda i, j: (jnp.maximum(i * rb - 1, 0), 0)),
            pl.BlockSpec((8, d), lambda i, j: (jnp.minimum((i + 1) * rb, last8), 0)),
            pl.BlockSpec((1, d), lambda i, j: (0, 0)),
            pl.BlockSpec((d, tn), lambda i, j: (0, j)),
            pl.BlockSpec((d, tn), lambda i, j: (0, j)),
            pl.BlockSpec((3, tn), lambda i, j: (0, j)),
            pl.BlockSpec((tn, d), lambda i, j: (j, 0)),
            pl.BlockSpec((1, d), lambda i, j: (0, 0)),
        ],
        out_specs=pl.BlockSpec((tm, d), lambda i, j: (i, 0)),
        out_shape=jax.ShapeDtypeStruct((t, d), jnp.float32),
        scratch_shapes=[pltpu.VMEM((tm + 16, d), jnp.bfloat16), pltpu.VMEM((tm, d), jnp.float32)],
        compiler_params=_cparams("parallel", "arbitrary"),
        name="ffn",
    )(x2d, x2d, x2d, nw, wg, wu, cw, wd, fw)


def _prepare_layer(l, norm1_w, w_in, qk_conv_w, diff_lambda, ret_decay_logit, mlstm_gate_bias, attn_sink,
                   w_out, norm2_w, w_gate, w_up, ffn_conv_w, w_down):
    bf16 = jnp.bfloat16
    f32 = jnp.float32
    wi = w_in[l]
    w_pb = jnp.concatenate([wi[:, _OFF_A:_OFF_CQ], wi[:, _OFF_CV:_OFF_GATES], wi[:, _OFF_D:N_IN]], axis=1)
    wgates = wi[:, _OFF_GATES:_OFF_D].reshape(D_MODEL, 4, C_HEADS)
    wgates = jnp.transpose(wgates, (0, 2, 1))
    wgates = jnp.pad(wgates, ((0, 0), (0, 0), (0, LANES - 4))).reshape(D_MODEL, C_HEADS * LANES)
    w_pf = jnp.concatenate([wi[:, _OFF_CQ:_OFF_CV], wgates], axis=1)
    gbias = jnp.pad(mlstm_gate_bias[l].astype(f32).T, ((0, 0), (0, LANES - 4))).reshape(C_HEADS, 1, LANES)
    dec = jnp.zeros((8, LANES), f32).at[0:2, 0:B_HEADS].set(ret_decay_logit[l].astype(f32))
    return dict(
        norm1=norm1_w[l].reshape(1, D_MODEL).astype(f32),
        w_pb=w_pb.astype(bf16), w_pf=w_pf.astype(bf16),
        wq_conv=qk_conv_w[l][:, :C_HEADS * C_DIM].astype(f32),
        wk_conv=qk_conv_w[l][:, C_HEADS * C_DIM:].astype(f32),
        gbias=gbias, dec=dec,
        diff_lambda=diff_lambda[l].astype(f32), sink=attn_sink[l].astype(f32),
        w_out=w_out[l].astype(bf16),
        norm2=norm2_w[l].reshape(1, D_MODEL).astype(f32),
        w_gate=w_gate[l].astype(bf16), w_up=w_up[l].astype(bf16),
        ffn_conv=ffn_conv_w[l].astype(f32), w_down=w_down[l].astype(bf16),
    )


def _tile(n, pref):
    t = min(pref, n)
    while n % t:
        t //= 2
    return t


def _layer(x2d, bsz, seq, layer_idx, p, slopes, final_w, final_norm):
    t = bsz * seq
    tm_in = _tile(t, 1024)
    pb = _norm_matmul(x2d, p["norm1"], p["w_pb"], jnp.bfloat16, tm_in, 896).reshape(bsz, seq, PB_WIDTH)
    pf = _norm_matmul(x2d, p["norm1"], p["w_pf"], jnp.float32, tm_in, 768).reshape(bsz, seq, PF_WIDTH)
    y_a = _diff_attention(pb, p["diff_lambda"], slopes[D_HEADS:], layer_idx, _tile(seq, 256), _tile(seq, 512))
    y_b = _retention(pb, p["dec"])
    y_c = _mlstm(pb, pf, p["wq_conv"], p["wk_conv"], p["gbias"])
    y_d = _window_attention(pb, p["sink"], slopes[:D_HEADS])
    ys = [y.reshape(t, 512) for y in (y_a, y_b, y_c, y_d)]
    x1 = _out_proj(x2d, ys, p["w_out"], _tile(t, 512))
    return _ffn(x1, seq, p["norm2"], p["w_gate"], p["w_up"], p["ffn_conv"], p["w_down"], final_w,
                final_norm, _tile(seq, 512), 512)


def _trunk(x, layers, slopes, final_w):
    bsz, seq, d = x.shape
    x2d = x.reshape(bsz * seq, d)
    for l, p in enumerate(layers):
        x2d = _layer(x2d, bsz, seq, l, p, slopes, final_w, l == len(layers) - 1)
    return x2d.reshape(bsz, seq, d)


def kernel(x_prompt, x_sample, norm1_w, w_in, qk_conv_w, diff_lambda, ret_decay_logit, mlstm_gate_bias,
           attn_sink, w_out, norm2_w, w_gate, w_up, ffn_conv_w, w_down, final_norm_w):
    layers = [_prepare_layer(l, norm1_w, w_in, qk_conv_w, diff_lambda, ret_decay_logit, mlstm_gate_bias,
                             attn_sink, w_out, norm2_w, w_gate, w_up, ffn_conv_w, w_down)
              for l in range(DEPTH)]
    n = A_HEADS + D_HEADS
    slopes = 2.0 ** (-8.0 * jnp.arange(1, n + 1, dtype=jnp.float32) / n)
    final_w = final_norm_w.reshape(1, D_MODEL).astype(jnp.float32)
    return (_trunk(x_prompt, layers, slopes, final_w), _trunk(x_sample, layers, slopes, final_w))
```

```python
import functools
import math

import jax
import jax.numpy as jnp
from jax import lax
from jax.experimental import pallas as pl
from jax.experimental.pallas import tpu as pltpu

D_MODEL = 2048
DEPTH = 2
A_HEADS = 4
A_QK_DIM = 64
A_V_DIM = 128
B_HEADS = 4
B_DIM = 128
C_HEADS = 4
C_DIM = 128
D_HEADS = 8
D_KV_HEADS = 2
D_DIM = 64
WINDOW = 128
BLOCK = 128
D_FF = 5632
EPS = 1e-6
LANES = 128

_OFF_A = 0
_OFF_B = 1536
_OFF_CQ = 3584
_OFF_CV = 4608
_OFF_GATES = 5632
_OFF_D = 5648
N_IN = 6416

_PB_AQ, _PB_AK, _PB_AV = 0, 4, 8
_PB_BQ, _PB_BK, _PB_BV, _PB_BG = 12, 16, 20, 24
_PB_CV, _PB_CO = 28, 32
_PB_DQ512 = 9
_PB_DK, _PB_DV = 40, 41
PB_WIDTH = 42 * LANES
_PF_CQ, _PF_CK, _PF_GATES = 0, 4, 8
PF_WIDTH = 12 * LANES

_VMEM_LIMIT = 56 * 1024 * 1024
_NEG_INF = float("-inf")


def _cparams(*sem):
    return pltpu.CompilerParams(dimension_semantics=sem, vmem_limit_bytes=_VMEM_LIMIT)


def _dot(a, b):
    return jnp.dot(a, b, preferred_element_type=jnp.float32)


def _dot_nt(a, b):
    return lax.dot_general(a, b, (((1,), (1,)), ((), ())), preferred_element_type=jnp.float32)


def _dot_tn(a, b):
    return lax.dot_general(a, b, (((0,), (0,)), ((), ())), preferred_element_type=jnp.float32)


def _log_sigmoid(x):
    return -(jnp.maximum(-x, 0.0) + jnp.log1p(jnp.exp(-jnp.abs(x))))


def _split_hi_lo(x):
    hi = x.astype(jnp.bfloat16)
    lo = (x - hi.astype(jnp.float32)).astype(jnp.bfloat16)
    return hi, lo


def _norm_matmul_kernel(x_ref, nw_ref, w_ref, o_ref, h_ref):
    @pl.when(pl.program_id(1) == 0)
    def _():
        x = x_ref[...]
        ms = jnp.mean(x * x, axis=-1, keepdims=True)
        h_ref[...] = (x * lax.rsqrt(ms + EPS) * nw_ref[...]).astype(h_ref.dtype)

    o_ref[...] = _dot(h_ref[...], w_ref[...]).astype(o_ref.dtype)


def _norm_matmul(x2d, nw, w, out_dtype, tm, tn):
    t, d = x2d.shape
    n = w.shape[1]
    return pl.pallas_call(
        _norm_matmul_kernel,
        grid=(t // tm, n // tn),
        in_specs=[
            pl.BlockSpec((tm, d), lambda i, j: (i, 0)),
            pl.BlockSpec((1, d), lambda i, j: (0, 0)),
            pl.BlockSpec((d, tn), lambda i, j: (0, j)),
        ],
        out_specs=pl.BlockSpec((tm, tn), lambda i, j: (i, j)),
        out_shape=jax.ShapeDtypeStruct((t, n), out_dtype),
        scratch_shapes=[pltpu.VMEM((tm, d), jnp.bfloat16)],
        compiler_params=_cparams("parallel", "arbitrary"),
        name="norm_in_proj",
    )(x2d, nw, w)


_LOG2E = 1.4426950408889634
_N_BIAS_LANES = 3


def _diff_attn_kernel(slopes_ref, q_ref, k_ref, v_ref, dl_ref, o_ref,
                      kx_ref, vx_ref, qx_ref, m_ref, acc_ref, sa_ref, sb_ref, *, seq, tq, tk, lam_init):
    head = pl.program_id(1)
    qi = pl.program_id(2)
    f32 = jnp.float32
    bf16 = jnp.bfloat16
    sigma = slopes_ref[head] * _LOG2E
    nk = seq // tk
    prep_rows = min(seq, 512)

    @pl.when(qi == 0)
    def _():
        lane = lax.broadcasted_iota(jnp.int32, (prep_rows, LANES), 1)
        row = lax.broadcasted_iota(jnp.int32, (prep_rows, LANES), 0)

        def prep(c, carry):
            r0 = pl.multiple_of(c * prep_rows, prep_rows)
            kk = k_ref[pl.ds(r0, prep_rows), :].astype(f32)
            col = sigma * (row + r0).astype(f32)
            c1 = col.astype(bf16).astype(f32)
            c2 = (col - c1).astype(bf16).astype(f32)
            c3 = col - c1 - c2
            bias = jnp.where(lane == A_QK_DIM, c1, jnp.where(lane == A_QK_DIM + 1, c2,
                             jnp.where(lane == A_QK_DIM + 2, c3, 0.0)))
            kx_ref[0, pl.ds(r0, prep_rows), :] = jnp.where(lane < A_QK_DIM, kk, bias).astype(bf16)
            kx_ref[1, pl.ds(r0, prep_rows), :] = jnp.where(
                lane < A_QK_DIM, pltpu.roll(kk, A_QK_DIM, 1), bias).astype(bf16)
            vx_ref[pl.ds(r0, prep_rows), 0:A_V_DIM] = v_ref[pl.ds(r0, prep_rows), :]
            vx_ref[pl.ds(r0, prep_rows), A_V_DIM:2 * A_V_DIM] = jnp.ones((prep_rows, A_V_DIM), bf16)
            return carry

        lax.fori_loop(0, seq // prep_rows, prep, 0)

    qs = q_ref[...].astype(f32) * (A_QK_DIM ** -0.5 * _LOG2E)
    qlane = lax.broadcasted_iota(jnp.int32, (tq, LANES), 1)
    is_bias = (qlane >= A_QK_DIM) & (qlane < A_QK_DIM + _N_BIAS_LANES)
    for mp in range(2):
        qm = qs if mp == 0 else pltpu.roll(qs, A_QK_DIM, 1)
        for var, sign in enumerate((1.0, 0.0, -1.0)):
            qx_ref[3 * mp + var] = jnp.where(qlane < A_QK_DIM, qm, jnp.where(is_bias, sign, 0.0)).astype(bf16)

    q0 = qi * tq
    rho = sigma * (lax.broadcasted_iota(jnp.int32, (tq, 1), 0) + q0).astype(f32)
    m_ref[...] = jnp.full(m_ref.shape, _NEG_INF, f32)
    acc_ref[...] = jnp.zeros(acc_ref.shape, f32)

    kd = q0 // tk

    def key_tile(j):
        jm = jnp.minimum(j, nk - 1) - 1
        return jnp.where(j == 0, kd, jnp.where(jm < kd, jm, jm + 1))

    def scores(j, s_ref):
        kb = key_tile(j)
        var = jnp.where(kb < kd, 0, jnp.where(kb > kd, 2, 1))
        k_start = pl.multiple_of(kb * tk, tk)
        for mp in range(2):
            s_ref[mp] = _dot_nt(qx_ref[3 * mp + var], kx_ref[mp, pl.ds(k_start, tk), :])

    def softmax_pv(j, s_ref):
        kb = key_tile(j)
        shift = jnp.where(kb < kd, -1.0, jnp.where(kb > kd, 1.0, 0.0)) * rho
        k_start = pl.multiple_of(kb * tk, tk)
        vb = vx_ref[pl.ds(k_start, tk), :]
        for mp in range(2):
            s = s_ref[mp]
            m_old = m_ref[mp]
            m_new = jnp.maximum(m_old, jnp.max(s, axis=1, keepdims=True) + shift)
            p = jnp.exp2(s - (m_new - shift))
            acc_ref[mp] = jnp.exp2(m_old - m_new) * acc_ref[mp] + _dot(p.astype(bf16), vb)
            m_ref[mp] = m_new

    scores(0, sa_ref)
    rel = (lax.broadcasted_iota(jnp.int32, (tq, tk), 0) - lax.broadcasted_iota(jnp.int32, (tq, tk), 1)
           + (q0 - kd * tk)).astype(f32)
    diag_bias = sigma * jnp.abs(rel)
    for mp in range(2):
        sa_ref[mp] = sa_ref[mp] - diag_bias

    def body(i, carry):
        j = 2 * i
        scores(j + 1, sb_ref)
        softmax_pv(j, sa_ref)
        scores(j + 2, sa_ref)
        softmax_pv(j + 1, sb_ref)
        return carry

    lax.fori_loop(0, nk // 2, body, 0)

    dl = dl_ref[...]
    lam = (jnp.exp(jnp.sum(dl[0:1] * dl[1:2], axis=1, keepdims=True))
           - jnp.exp(jnp.sum(dl[2:3] * dl[3:4], axis=1, keepdims=True)) + lam_init)
    a0 = acc_ref[0]
    a1 = acc_ref[1]
    o = a0[:, 0:A_V_DIM] / a0[:, A_V_DIM:] - lam * (a1[:, 0:A_V_DIM] / a1[:, A_V_DIM:])
    o = o * lax.rsqrt(jnp.mean(o * o, axis=-1, keepdims=True) + EPS)
    o_ref[...] = (o * (1.0 - lam_init)).astype(o_ref.dtype)


def _diff_attention(pb, diff_lambda, slopes_a, layer_idx, tq, tk):
    bsz, seq, _ = pb.shape
    assert seq % (2 * tk) == 0 and tk % tq == 0
    lam_init = 0.8 - 0.6 * math.exp(-0.3 * layer_idx)
    kern = functools.partial(_diff_attn_kernel, seq=seq, tq=tq, tk=tk, lam_init=lam_init)
    return pl.pallas_call(
        kern,
        grid=(bsz, A_HEADS, seq // tq),
        in_specs=[
            pl.BlockSpec(memory_space=pltpu.SMEM),
            pl.BlockSpec((None, tq, LANES), lambda b, h, i: (b, i, _PB_AQ + h)),
            pl.BlockSpec((None, seq, LANES), lambda b, h, i: (b, 0, _PB_AK + h)),
            pl.BlockSpec((None, seq, LANES), lambda b, h, i: (b, 0, _PB_AV + h)),
            pl.BlockSpec((4, A_QK_DIM), lambda b, h, i: (0, 0)),
        ],
        out_specs=pl.BlockSpec((None, tq, LANES), lambda b, h, i: (b, i, h)),
        out_shape=jax.ShapeDtypeStruct((bsz, seq, A_HEADS * A_V_DIM), jnp.bfloat16),
        scratch_shapes=[
            pltpu.VMEM((2, seq, LANES), jnp.bfloat16),
            pltpu.VMEM((seq, 2 * A_V_DIM), jnp.bfloat16),
            pltpu.VMEM((6, tq, LANES), jnp.bfloat16),
            pltpu.VMEM((2, tq, 1), jnp.float32),
            pltpu.VMEM((2, tq, 2 * A_V_DIM), jnp.float32),
            pltpu.VMEM((2, tq, tk), jnp.float32),
            pltpu.VMEM((2, tq, tk), jnp.float32),
        ],
        compiler_params=_cparams("parallel", "parallel", "arbitrary"),
        name="diff_attention",
    )(slopes_a, pb, pb, pb, diff_lambda)


def _retention_kernel(q_ref, k_ref, v_ref, g_ref, dec_ref, o_ref, of_ref, ob_ref, *, seq):
    head = pl.program_id(1)
    nc = seq // BLOCK
    f32 = jnp.float32

    lgv = _log_sigmoid(dec_ref[...])
    rr = lax.broadcasted_iota(jnp.int32, lgv.shape, 0)
    cc = lax.broadcasted_iota(jnp.int32, lgv.shape, 1)

    def pick(row):
        sel = jnp.where((rr == row) & (cc == head), lgv, 0.0)
        return jnp.sum(jnp.sum(sel, axis=1, keepdims=True), axis=0, keepdims=True)

    lg_f, lg_b = pick(0), pick(1)
    ri = lax.broadcasted_iota(jnp.int32, (BLOCK, BLOCK), 0).astype(f32)
    ci = lax.broadcasted_iota(jnp.int32, (BLOCK, BLOCK), 1).astype(f32)
    diff = ri - ci
    decay_f = jnp.where(diff >= 0, jnp.exp(lg_f * jnp.maximum(diff, 0.0)), 0.0)
    decay_b = jnp.where(diff <= 0, jnp.exp(lg_b * jnp.maximum(-diff, 0.0)), 0.0)
    idx = lax.broadcasted_iota(jnp.int32, (BLOCK, 1), 0).astype(f32)
    qdec_f = jnp.exp(lg_f * (idx + 1.0))
    kdec_f = jnp.exp(lg_f * (BLOCK - 1.0 - idx))
    qdec_b = jnp.exp(lg_b * (BLOCK - idx))
    kdec_b = jnp.exp(lg_b * idx)
    cdec_f = jnp.exp(lg_f * BLOCK)
    cdec_b = jnp.exp(lg_b * BLOCK)
    kscale = B_DIM ** -0.5

    def one(t0, state, decay, qdec, kdec, cdec, out_ref):
        qb = q_ref[pl.ds(t0, BLOCK), :]
        kb = k_ref[pl.ds(t0, BLOCK), :].astype(f32) * kscale
        vb = v_ref[pl.ds(t0, BLOCK), :]
        s = _dot_nt(qb, kb.astype(qb.dtype)) * decay
        inner = _dot(s.astype(vb.dtype), vb)
        cross = _dot(qb, state.astype(qb.dtype)) * qdec
        out_ref[pl.ds(t0, BLOCK), :] = inner + cross
        return state * cdec + _dot_tn((kb * kdec).astype(vb.dtype), vb)

    def body(c, carry):
        rf, rb = carry
        tf = pl.multiple_of(c * BLOCK, BLOCK)
        tb = pl.multiple_of((nc - 1 - c) * BLOCK, BLOCK)
        rf = one(tf, rf, decay_f, qdec_f, kdec_f, cdec_f, of_ref)
        rb = one(tb, rb, decay_b, qdec_b, kdec_b, cdec_b, ob_ref)
        return rf, rb

    zero = jnp.zeros((B_DIM, B_DIM), f32)
    lax.fori_loop(0, nc, body, (zero, zero))

    def fin(c, carry):
        t0 = pl.multiple_of(c * BLOCK, BLOCK)
        r = of_ref[pl.ds(t0, BLOCK), :] + ob_ref[pl.ds(t0, BLOCK), :]
        r = r * lax.rsqrt(jnp.mean(r * r, axis=-1, keepdims=True) + EPS)
        g = g_ref[pl.ds(t0, BLOCK), :].astype(f32)
        o_ref[pl.ds(t0, BLOCK), :] = (g * jax.nn.sigmoid(g) * r).astype(o_ref.dtype)
        return carry

    lax.fori_loop(0, nc, fin, 0)


def _retention(pb, dec_pad):
    bsz, seq, _ = pb.shape
    kern = functools.partial(_retention_kernel, seq=seq)

    def col(base):
        return pl.BlockSpec((None, seq, LANES), lambda b, h: (b, 0, base + h))

    return pl.pallas_call(
        kern,
        grid=(bsz, B_HEADS),
        in_specs=[col(_PB_BQ), col(_PB_BK), col(_PB_BV), col(_PB_BG),
                  pl.BlockSpec((8, LANES), lambda b, h: (0, 0))],
        out_specs=pl.BlockSpec((None, seq, LANES), lambda b, h: (b, 0, h)),
        out_shape=jax.ShapeDtypeStruct((bsz, seq, B_HEADS * B_DIM), jnp.bfloat16),
        scratch_shapes=[pltpu.VMEM((seq, B_DIM), jnp.float32), pltpu.VMEM((seq, B_DIM), jnp.float32)],
        compiler_params=_cparams("parallel", "parallel"),
        name="retention",
    )(pb, pb, pb, pb, dec_pad)


def _mlstm_kernel(cq_ref, ck_ref, gt_ref, v_ref, og_ref, wq_ref, wk_ref, gb_ref, o_ref,
                  qc_ref, kc_ref, of_ref, ob_ref, *, seq):
    nc = seq // BLOCK
    f32 = jnp.float32
    bf16 = jnp.bfloat16
    rowi = lax.broadcasted_iota(jnp.int32, (BLOCK, LANES), 0)
    coli = lax.broadcasted_iota(jnp.int32, (BLOCK, LANES), 1)

    def conv_silu(src_ref, w, c, t0):
        x = src_ref[pl.ds(t0, BLOCK), :]
        prev = src_ref[pl.ds(jnp.maximum(t0 - 1, 0), 1), :] * (c > 0).astype(f32)
        nxt = src_ref[pl.ds(jnp.minimum(t0 + BLOCK, seq - 1), 1), :] * (c < nc - 1).astype(f32)
        xd = jnp.where(rowi == 0, prev, pltpu.roll(x, 1, 0))
        xu = jnp.where(rowi == BLOCK - 1, nxt, pltpu.roll(x, BLOCK - 1, 0))
        y = xd * w[0:1] + x * w[1:2] + xu * w[2:3]
        return y * jax.nn.sigmoid(y)

    def prep(c, carry):
        t0 = pl.multiple_of(c * BLOCK, BLOCK)
        qc_ref[pl.ds(t0, BLOCK), :] = (conv_silu(cq_ref, wq_ref[...], c, t0) * (C_DIM ** -0.5)).astype(bf16)
        kc_ref[pl.ds(t0, BLOCK), :] = conv_silu(ck_ref, wk_ref[...], c, t0).astype(bf16)
        return carry

    lax.fori_loop(0, nc, prep, 0)

    tril = rowi >= coli
    triu = rowi <= coli
    tril_b = tril.astype(bf16)
    triu_b = triu.astype(bf16)
    gbias = gb_ref[...]

    def gate_terms(t0, backward):
        gc = gt_ref[pl.ds(t0, BLOCK), :] + gbias
        gr = gc.T
        gi = 2 if backward else 0
        hi, lo = _split_hi_lo(_log_sigmoid(gr[0:8]))
        lanes_op = tril_b if backward else triu_b
        cs_r = _dot(hi, lanes_op) + _dot(lo, lanes_op)
        hi, lo = _split_hi_lo(_log_sigmoid(gc))
        subl_op = triu_b if backward else tril_b
        cs_c = _dot(subl_op, hi) + _dot(subl_op, lo)
        last = 0 if backward else LANES - 1
        return dict(i_row=gr[gi:gi + 1], i_col=gc[:, gi:gi + 1], b_row=cs_r[gi + 1:gi + 2],
                    b_col=cs_c[:, gi + 1:gi + 2], bl=cs_r[gi + 1:gi + 2, last:last + 1],
                    mask=triu if backward else tril)

    def step(t0, g, state, out_ref):
        cs, ns, m = state
        qb = qc_ref[pl.ds(t0, BLOCK), :]
        kb = kc_ref[pl.ds(t0, BLOCK), :]
        vb = v_ref[pl.ds(t0, BLOCK), :]
        a_row = g["i_row"] - g["b_row"]
        a_t = jnp.where(g["mask"], a_row, _NEG_INF)
        n_col = jnp.maximum(jnp.max(a_t, axis=1, keepdims=True), m)
        dw = jnp.exp(a_t - n_col)
        iw = jnp.exp(m - n_col)
        s = _dot_nt(qb, kb) * dw
        num = _dot(s.astype(bf16), vb) + iw * _dot(qb, cs.astype(bf16))
        den = jnp.sum(s, axis=1, keepdims=True) + iw * jnp.sum(qb.astype(f32) * ns, axis=1, keepdims=True)
        mt = g["b_col"] + n_col
        out_ref[pl.ds(t0, BLOCK), :] = num / jnp.maximum(jnp.abs(den), jnp.exp(-mt))
        bl = g["bl"]
        m_new = jnp.maximum(bl + m, jnp.max(bl + a_row, axis=1, keepdims=True))
        cw = jnp.exp(bl + m - m_new)
        kw = kb.astype(f32) * jnp.exp(bl - g["b_col"] + g["i_col"] - m_new)
        cs = cw * cs + _dot_tn(kw.astype(bf16), vb)
        ns = cw * ns + jnp.sum(kw, axis=0, keepdims=True)
        return cs, ns, m_new

    def body(c, carry):
        sf, sb = carry
        tf = pl.multiple_of(c * BLOCK, BLOCK)
        tb = pl.multiple_of((nc - 1 - c) * BLOCK, BLOCK)
        sf = step(tf, gate_terms(tf, False), sf, of_ref)
        sb = step(tb, gate_terms(tb, True), sb, ob_ref)
        return sf, sb

    init = (jnp.zeros((C_DIM, C_DIM), f32), jnp.zeros((1, C_DIM), f32), jnp.zeros((1, 1), f32))
    lax.fori_loop(0, nc, body, (init, init))

    def fin(c, carry):
        t0 = pl.multiple_of(c * BLOCK, BLOCK)
        r = of_ref[pl.ds(t0, BLOCK), :] + ob_ref[pl.ds(t0, BLOCK), :]
        r = r * lax.rsqrt(jnp.mean(r * r, axis=-1, keepdims=True) + EPS)
        og = og_ref[pl.ds(t0, BLOCK), :].astype(f32)
        o_ref[pl.ds(t0, BLOCK), :] = (jax.nn.sigmoid(og) * r).astype(o_ref.dtype)
        return carry

    lax.fori_loop(0, nc, fin, 0)


def _mlstm(pb, pf, wq_conv, wk_conv, gate_bias_pad):
    bsz, seq, _ = pb.shape
    kern = functools.partial(_mlstm_kernel, seq=seq)

    def col(base):
        return pl.BlockSpec((None, seq, LANES), lambda b, h: (b, 0, base + h))

    return pl.pallas_call(
        kern,
        grid=(bsz, C_HEADS),
        in_specs=[col(_PF_CQ), col(_PF_CK), col(_PF_GATES), col(_PB_CV), col(_PB_CO),
                  pl.BlockSpec((3, LANES), lambda b, h: (0, h)),
                  pl.BlockSpec((3, LANES), lambda b, h: (0, h)),
                  pl.BlockSpec((None, 1, LANES), lambda b, h: (h, 0, 0))],
        out_specs=pl.BlockSpec((None, seq, LANES), lambda b, h: (b, 0, h)),
        out_shape=jax.ShapeDtypeStruct((bsz, seq, C_HEADS * C_DIM), jnp.bfloat16),
        scratch_shapes=[pltpu.VMEM((seq, C_DIM), jnp.bfloat16), pltpu.VMEM((seq, C_DIM), jnp.bfloat16),
                        pltpu.VMEM((seq, C_DIM), jnp.float32), pltpu.VMEM((seq, C_DIM), jnp.float32)],
        compiler_params=_cparams("parallel", "parallel"),
        name="mlstm",
    )(pf, pf, pf, pb, pb, wq_conv, wk_conv, gate_bias_pad)


def _window_attn_kernel(slopes_ref, sink_ref, q_ref, k_ref, v_ref, o_ref, kx_ref, vx_ref, *, seq):
    qi = pl.program_id(1)
    f32 = jnp.float32
    span = BLOCK + 2 * WINDOW
    group = D_HEADS // D_KV_HEADS

    @pl.when(qi == 0)
    def _():
        lane = lax.broadcasted_iota(jnp.int32, (seq, LANES), 1)
        for src, dst in ((k_ref, kx_ref), (v_ref, vx_ref)):
            x = src[...].astype(f32)
            x0 = jnp.where(lane < D_DIM, x, 0.0)
            x1 = jnp.where(lane >= D_DIM, x, 0.0)
            dst[0] = x0.astype(dst.dtype)
            dst[1] = pltpu.roll(x0, D_DIM, 1).astype(dst.dtype)
            dst[2] = pltpu.roll(x1, D_DIM, 1).astype(dst.dtype)
            dst[3] = x1.astype(dst.dtype)

    t0 = qi * BLOCK
    ws = pl.multiple_of(jnp.clip(t0 - WINDOW, 0, seq - span), BLOCK)
    rel = (lax.broadcasted_iota(jnp.int32, (BLOCK, span), 0)
           - lax.broadcasted_iota(jnp.int32, (BLOCK, span), 1)) + (t0 - ws)
    dist = jnp.abs(rel)
    valid = dist <= WINDOW
    distf = dist.astype(f32)
    scale = jnp.asarray(D_DIM ** -0.5, q_ref.dtype)

    for pair in range(D_HEADS // 2):
        kv = (2 * pair) // group
        qp = q_ref[:, pair * LANES:(pair + 1) * LANES] * scale
        out = None
        for half in range(2):
            hd = 2 * pair + half
            kk = kx_ref[2 * kv + half, pl.ds(ws, span), :]
            vv = vx_ref[2 * kv + half, pl.ds(ws, span), :]
            s = _dot_nt(qp, kk) - slopes_ref[hd] * distf
            s = jnp.where(valid, s, _NEG_INF)
            snk = sink_ref[hd]
            m = jnp.maximum(jnp.max(s, axis=1, keepdims=True), snk)
            p = jnp.exp(s - m)
            denom = jnp.sum(p, axis=1, keepdims=True) + jnp.exp(snk - m)
            o = _dot((p / denom).astype(vv.dtype), vv)
            out = o if out is None else out + o
        o_ref[:, pair * LANES:(pair + 1) * LANES] = out.astype(o_ref.dtype)


def _window_attention(pb, attn_sink, slopes_d):
    bsz, seq, _ = pb.shape
    kern = functools.partial(_window_attn_kernel, seq=seq)
    width = D_HEADS * D_DIM
    return pl.pallas_call(
        kern,
        grid=(bsz, seq // BLOCK),
        in_specs=[
            pl.BlockSpec(memory_space=pltpu.SMEM),
            pl.BlockSpec(memory_space=pltpu.SMEM),
            pl.BlockSpec((None, BLOCK, width), lambda b, i: (b, i, _PB_DQ512)),
            pl.BlockSpec((None, seq, LANES), lambda b, i: (b, 0, _PB_DK)),
            pl.BlockSpec((None, seq, LANES), lambda b, i: (b, 0, _PB_DV)),
        ],
        out_specs=pl.BlockSpec((None, BLOCK, width), lambda b, i: (b, i, 0)),
        out_shape=jax.ShapeDtypeStruct((bsz, seq, width), jnp.bfloat16),
        scratch_shapes=[pltpu.VMEM((4, seq, LANES), jnp.bfloat16), pltpu.VMEM((4, seq, LANES), jnp.bfloat16)],
        compiler_params=_cparams("parallel", "arbitrary"),
        name="window_attention",
    )(slopes_d, attn_sink, pb, pb, pb)


def _out_proj_kernel(x_ref, ya_ref, yb_ref, yc_ref, yd_ref, w_ref, o_ref):
    acc = x_ref[...]
    for g, y_ref in enumerate((ya_ref, yb_ref, yc_ref, yd_ref)):
        acc = acc + _dot(y_ref[...], w_ref[g * 512:(g + 1) * 512, :])
    o_ref[...] = acc


def _out_proj(x2d, ys, w_out, tm):
    t, d = x2d.shape
    yspec = pl.BlockSpec((tm, 512), lambda i: (i, 0))
    return pl.pallas_call(
        _out_proj_kernel,
        grid=(t // tm,),
        in_specs=[pl.BlockSpec((tm, d), lambda i: (i, 0)), yspec, yspec, yspec, yspec,
                  pl.BlockSpec((d, d), lambda i: (0, 0))],
        out_specs=pl.BlockSpec((tm, d), lambda i: (i, 0)),
        out_shape=jax.ShapeDtypeStruct((t, d), jnp.float32),
        compiler_params=_cparams("parallel"),
        name="out_proj",
    )(x2d, *ys, w_out)


def _ffn_kernel(x_ref, xp_ref, xn_ref, nw_ref, wg_ref, wu_ref, cw_ref, wd_ref, fw_ref, o_ref,
                h_ref, acc_ref, *, seq, tm, final_norm):
    i = pl.program_id(0)
    j = pl.program_id(1)
    f32 = jnp.float32

    def norm(x):
        return x * lax.rsqrt(jnp.mean(x * x, axis=-1, keepdims=True) + EPS) * nw_ref[...]

    @pl.when(j == 0)
    def _():
        h_ref[0:tm, :] = norm(x_ref[...]).astype(h_ref.dtype)
        has_prev = ((i * tm) % seq != 0).astype(f32)
        has_next = (((i + 1) * tm) % seq != 0).astype(f32)
        halo = jnp.concatenate([norm(xp_ref[...]) * has_prev, norm(xn_ref[...]) * has_next], axis=0)
        h_ref[tm:tm + 16, :] = halo.astype(h_ref.dtype)
        acc_ref[...] = jnp.zeros(acc_ref.shape, f32)

    gate = _dot(h_ref[...], wg_ref[...])
    gm = gate[0:tm]
    rowi = lax.broadcasted_iota(jnp.int32, gm.shape, 0)
    g_dn = jnp.where(rowi == 0, gate[tm + 7:tm + 8], pltpu.roll(gm, 1, 0))
    g_up = jnp.where(rowi == tm - 1, gate[tm + 8:tm + 9], pltpu.roll(gm, tm - 1, 0))
    cw = cw_ref[...]
    g = g_dn * cw[0:1] + gm * cw[1:2] + g_up * cw[2:3]
    up = _dot(h_ref[0:tm, :], wu_ref[...])
    act = (g * jax.nn.sigmoid(g) * up).astype(wd_ref.dtype)
    acc_ref[...] += _dot(act, wd_ref[...])

    @pl.when(j == pl.num_programs(1) - 1)
    def _():
        y = x_ref[...] + acc_ref[...]
        if final_norm:
            y = y * lax.rsqrt(jnp.mean(y * y, axis=-1, keepdims=True) + EPS) * fw_ref[...]
        o_ref[...] = y


def _ffn(x2d, seq, nw, wg, wu, cw, wd, fw, final_norm, tm, tn):
    t, d = x2d.shape
    nff = wg.shape[1]
    rb = tm // 8
    last8 = t // 8 - 1
    kern = functools.partial(_ffn_kernel, seq=seq, tm=tm, final_norm=final_norm)
    return pl.pallas_call(
        kern,
        grid=(t // tm, nff // tn),
        in_specs=[
            pl.BlockSpec((tm, d), lambda i, j: (i, 0)),
            pl.BlockSpec((8, d), lambda i, j: (jnp.maximum(i * rb - 1, 0), 0)),
            pl.BlockSpec((8, d), lambda i, j: (jnp.minimum((i + 1) * rb, last8), 0)),
            pl.BlockSpec((1, d), lambda i, j: (0, 0)),
            pl.BlockSpec((d, tn), lambda i, j: (0, j)),
            pl.BlockSpec((d, tn), lambda i, j: (0, j)),
            pl.BlockSpec((3, tn), lambda i, j: (0, j)),
            pl.BlockSpec((tn, d), lambda i, j: (j, 0)),
            pl.BlockSpec((1, d), lambda i, j: (0, 0)),
        ],
        out_specs=pl.BlockSpec((tm, d), lambda i, j: (i, 0)),
        out_shape=jax.ShapeDtypeStruct((t, d), jnp.float32),
        scratch_shapes=[pltpu.VMEM((tm + 16, d), jnp.bfloat16), pltpu.VMEM((tm, d), jnp.float32)],
        compiler_params=_cparams("parallel", "arbitrary"),
        name="ffn",
    )(x2d, x2d, x2d, nw, wg, wu, cw, wd, fw)


def _prepare_layer(l, norm1_w, w_in, qk_conv_w, diff_lambda, ret_decay_logit, mlstm_gate_bias, attn_sink,
                   w_out, norm2_w, w_gate, w_up, ffn_conv_w, w_down):
    bf16 = jnp.bfloat16
    f32 = jnp.float32
    wi = w_in[l]
    w_pb = jnp.concatenate([wi[:, _OFF_A:_OFF_CQ], wi[:, _OFF_CV:_OFF_GATES], wi[:, _OFF_D:N_IN]], axis=1)
    wgates = wi[:, _OFF_GATES:_OFF_D].reshape(D_MODEL, 4, C_HEADS)
    wgates = jnp.transpose(wgates, (0, 2, 1))
    wgates = jnp.pad(wgates, ((0, 0), (0, 0), (0, LANES - 4))).reshape(D_MODEL, C_HEADS * LANES)
    w_pf = jnp.concatenate([wi[:, _OFF_CQ:_OFF_CV], wgates], axis=1)
    gbias = jnp.pad(mlstm_gate_bias[l].astype(f32).T, ((0, 0), (0, LANES - 4))).reshape(C_HEADS, 1, LANES)
    dec = jnp.zeros((8, LANES), f32).at[0:2, 0:B_HEADS].set(ret_decay_logit[l].astype(f32))
    return dict(
        norm1=norm1_w[l].reshape(1, D_MODEL).astype(f32),
        w_pb=w_pb.astype(bf16), w_pf=w_pf.astype(bf16),
        wq_conv=qk_conv_w[l][:, :C_HEADS * C_DIM].astype(f32),
        wk_conv=qk_conv_w[l][:, C_HEADS * C_DIM:].astype(f32),
        gbias=gbias, dec=dec,
        diff_lambda=diff_lambda[l].astype(f32), sink=attn_sink[l].astype(f32),
        w_out=w_out[l].astype(bf16),
        norm2=norm2_w[l].reshape(1, D_MODEL).astype(f32),
        w_gate=w_gate[l].astype(bf16), w_up=w_up[l].astype(bf16),
        ffn_conv=ffn_conv_w[l].astype(f32), w_down=w_down[l].astype(bf16),
    )


def _tile(n, pref):
    t = min(pref, n)
    while n % t:
        t //= 2
    return t


def _layer(x2d, bsz, seq, layer_idx, p, slopes, final_w, final_norm):
    t = bsz * seq
    tm_in = _tile(t, 1024)
    pb = _norm_matmul(x2d, p["norm1"], p["w_pb"], jnp.bfloat16, tm_in, 896).reshape(bsz, seq, PB_WIDTH)
    pf = _norm_matmul(x2d, p["norm1"], p["w_pf"], jnp.float32, tm_in, 768).reshape(bsz, seq, PF_WIDTH)
    y_a = _diff_attention(pb, p["diff_lambda"], slopes[D_HEADS:], layer_idx, _tile(seq, 512), _tile(seq // 2, 512))
    y_b = _retention(pb, p["dec"])
    y_c = _mlstm(pb, pf, p["wq_conv"], p["wk_conv"], p["gbias"])
    y_d = _window_attention(pb, p["sink"], slopes[:D_HEADS])
    ys = [y.reshape(t, 512) for y in (y_a, y_b, y_c, y_d)]
    x1 = _out_proj(x2d, ys, p["w_out"], _tile(t, 512))
    return _ffn(x1, seq, p["norm2"], p["w_gate"], p["w_up"], p["ffn_conv"], p["w_down"], final_w,
                final_norm, _tile(seq, 512), 512)


def _trunk(x, layers, slopes, final_w):
    bsz, seq, d = x.shape
    x2d = x.reshape(bsz * seq, d)
    for l, p in enumerate(layers):
        x2d = _layer(x2d, bsz, seq, l, p, slopes, final_w, l == len(layers) - 1)
    return x2d.reshape(bsz, seq, d)


def kernel(x_prompt, x_sample, norm1_w, w_in, qk_conv_w, diff_lambda, ret_decay_logit, mlstm_gate_bias,
           attn_sink, w_out, norm2_w, w_gate, w_up, ffn_conv_w, w_down, final_norm_w):
    layers = [_prepare_layer(l, norm1_w, w_in, qk_conv_w, diff_lambda, ret_decay_logit, mlstm_gate_bias,
                             attn_sink, w_out, norm2_w, w_gate, w_up, ffn_conv_w, w_down)
              for l in range(DEPTH)]
    n = A_HEADS + D_HEADS
    slopes = 2.0 ** (-8.0 * jnp.arange(1, n + 1, dtype=jnp.float32) / n)
    final_w = final_norm_w.reshape(1, D_MODEL).astype(jnp.float32)
    return (_trunk(x_prompt, layers, slopes, final_w), _trunk(x_sample, layers, slopes, final_w))
```

```python
import functools
import math

import jax
import jax.numpy as jnp
from jax import lax
from jax.experimental import pallas as pl
from jax.experimental.pallas import tpu as pltpu

D_MODEL = 2048
DEPTH = 2
A_HEADS = 4
A_QK_DIM = 64
A_V_DIM = 128
B_HEADS = 4
B_DIM = 128
C_HEADS = 4
C_DIM = 128
D_HEADS = 8
D_KV_HEADS = 2
D_DIM = 64
WINDOW = 128
BLOCK = 128
D_FF = 5632
EPS = 1e-6
LANES = 128

_OFF_A = 0
_OFF_B = 1536
_OFF_CQ = 3584
_OFF_CV = 4608
_OFF_GATES = 5632
_OFF_D = 5648
N_IN = 6416

_PB_AQ, _PB_AK, _PB_AV = 0, 4, 8
_PB_BQ, _PB_BK, _PB_BV, _PB_BG = 12, 16, 20, 24
_PB_CV, _PB_CO = 28, 32
_PB_DQ512 = 9
_PB_DK, _PB_DV = 40, 41
PB_WIDTH = 42 * LANES
_PF_CQ, _PF_CK, _PF_GATES = 0, 4, 8
PF_WIDTH = 9 * LANES
N_GATES = 16

_VMEM_LIMIT = 56 * 1024 * 1024
_SCAN_UNROLL = 2
_NEG_INF = float("-inf")


def _cparams(*sem):
    return pltpu.CompilerParams(dimension_semantics=sem, vmem_limit_bytes=_VMEM_LIMIT)


def _dot(a, b):
    return jnp.dot(a, b, preferred_element_type=jnp.float32)


def _dot_nt(a, b):
    return lax.dot_general(a, b, (((1,), (1,)), ((), ())), preferred_element_type=jnp.float32)


def _dot_tn(a, b):
    return lax.dot_general(a, b, (((0,), (0,)), ((), ())), preferred_element_type=jnp.float32)


def _log_sigmoid(x):
    return -(jnp.maximum(-x, 0.0) + jnp.log1p(jnp.exp(-jnp.abs(x))))


def _split_hi_lo(x):
    hi = x.astype(jnp.bfloat16)
    lo = (x - hi.astype(jnp.float32)).astype(jnp.bfloat16)
    return hi, lo


def _norm_matmul_kernel(x_ref, nw_ref, w_ref, o_ref, h_ref):
    @pl.when(pl.program_id(1) == 0)
    def _():
        x = x_ref[...]
        ms = jnp.mean(x * x, axis=-1, keepdims=True)
        h_ref[...] = (x * lax.rsqrt(ms + EPS) * nw_ref[...]).astype(h_ref.dtype)

    o_ref[...] = _dot(h_ref[...], w_ref[...]).astype(o_ref.dtype)


def _norm_matmul(x2d, nw, w, out_dtype, tm, tn):
    t, d = x2d.shape
    n = w.shape[1]
    return pl.pallas_call(
        _norm_matmul_kernel,
        grid=(t // tm, n // tn),
        in_specs=[
            pl.BlockSpec((tm, d), lambda i, j: (i, 0)),
            pl.BlockSpec((1, d), lambda i, j: (0, 0)),
            pl.BlockSpec((d, tn), lambda i, j: (0, j)),
        ],
        out_specs=pl.BlockSpec((tm, tn), lambda i, j: (i, j)),
        out_shape=jax.ShapeDtypeStruct((t, n), out_dtype),
        scratch_shapes=[pltpu.VMEM((tm, d), jnp.bfloat16)],
        compiler_params=_cparams("parallel", "arbitrary"),
        name="norm_in_proj",
    )(x2d, nw, w)


_LOG2E = 1.4426950408889634
_N_BIAS_LANES = 3


def _diff_attn_kernel(slopes_ref, q_ref, k_ref, v_ref, dl_ref, o_ref,
                      kx_ref, vx_ref, qx_ref, m_ref, acc_ref, sa_ref, sb_ref, *, seq, tq, tk, lam_init):
    head = pl.program_id(1)
    qi = pl.program_id(2)
    f32 = jnp.float32
    bf16 = jnp.bfloat16
    sigma = slopes_ref[head] * _LOG2E
    nk = seq // tk
    prep_rows = min(seq, 512)

    @pl.when(qi == 0)
    def _():
        lane = lax.broadcasted_iota(jnp.int32, (prep_rows, LANES), 1)
        row = lax.broadcasted_iota(jnp.int32, (prep_rows, LANES), 0)

        def prep(c, carry):
            r0 = pl.multiple_of(c * prep_rows, prep_rows)
            kk = k_ref[pl.ds(r0, prep_rows), :].astype(f32)
            col = sigma * (row + r0).astype(f32)
            c1 = col.astype(bf16).astype(f32)
            c2 = (col - c1).astype(bf16).astype(f32)
            c3 = col - c1 - c2
            bias = jnp.where(lane == A_QK_DIM, c1, jnp.where(lane == A_QK_DIM + 1, c2,
                             jnp.where(lane == A_QK_DIM + 2, c3, 0.0)))
            kx_ref[0, pl.ds(r0, prep_rows), :] = jnp.where(lane < A_QK_DIM, kk, bias).astype(bf16)
            kx_ref[1, pl.ds(r0, prep_rows), :] = jnp.where(
                lane < A_QK_DIM, pltpu.roll(kk, A_QK_DIM, 1), bias).astype(bf16)
            vx_ref[pl.ds(r0, prep_rows), 0:A_V_DIM] = v_ref[pl.ds(r0, prep_rows), :]
            vx_ref[pl.ds(r0, prep_rows), A_V_DIM:2 * A_V_DIM] = jnp.ones((prep_rows, A_V_DIM), bf16)
            return carry

        lax.fori_loop(0, seq // prep_rows, prep, 0)

    qs = q_ref[...].astype(f32) * (A_QK_DIM ** -0.5 * _LOG2E)
    qlane = lax.broadcasted_iota(jnp.int32, (tq, LANES), 1)
    is_bias = (qlane >= A_QK_DIM) & (qlane < A_QK_DIM + _N_BIAS_LANES)
    for mp in range(2):
        qm = qs if mp == 0 else pltpu.roll(qs, A_QK_DIM, 1)
        for var, sign in enumerate((1.0, 0.0, -1.0)):
            qx_ref[3 * mp + var] = jnp.where(qlane < A_QK_DIM, qm, jnp.where(is_bias, sign, 0.0)).astype(bf16)

    q0 = qi * tq
    rho = sigma * (lax.broadcasted_iota(jnp.int32, (tq, 1), 0) + q0).astype(f32)
    m_ref[...] = jnp.full(m_ref.shape, _NEG_INF, f32)
    acc_ref[...] = jnp.zeros(acc_ref.shape, f32)

    kd = q0 // tk

    def key_tile(j):
        jm = jnp.minimum(j, nk - 1) - 1
        return jnp.where(j == 0, kd, jnp.where(jm < kd, jm, jm + 1))

    def scores(j, s_ref):
        kb = key_tile(j)
        var = jnp.where(kb < kd, 0, jnp.where(kb > kd, 2, 1))
        k_start = pl.multiple_of(kb * tk, tk)
        for mp in range(2):
            s_ref[mp] = _dot_nt(qx_ref[3 * mp + var], kx_ref[mp, pl.ds(k_start, tk), :])

    def softmax_pv(j, s_ref):
        kb = key_tile(j)
        shift = jnp.where(kb < kd, -1.0, jnp.where(kb > kd, 1.0, 0.0)) * rho
        k_start = pl.multiple_of(kb * tk, tk)
        vb = vx_ref[pl.ds(k_start, tk), :]
        for mp in range(2):
            s = s_ref[mp]
            m_old = m_ref[mp]
            m_new = jnp.maximum(m_old, jnp.max(s, axis=1, keepdims=True) + shift)
            p = jnp.exp2(s - (m_new - shift))
            acc_ref[mp] = jnp.exp2(m_old - m_new) * acc_ref[mp] + _dot(p.astype(bf16), vb)
            m_ref[mp] = m_new

    scores(0, sa_ref)
    rel = (lax.broadcasted_iota(jnp.int32, (tq, tk), 0) - lax.broadcasted_iota(jnp.int32, (tq, tk), 1)
           + (q0 - kd * tk)).astype(f32)
    diag_bias = sigma * jnp.abs(rel)
    for mp in range(2):
        sa_ref[mp] = sa_ref[mp] - diag_bias

    def body(i, carry):
        j = 2 * i
        scores(j + 1, sb_ref)
        softmax_pv(j, sa_ref)
        scores(j + 2, sa_ref)
        softmax_pv(j + 1, sb_ref)
        return carry

    lax.fori_loop(0, nk // 2, body, 0)

    dl = dl_ref[...]
    lam = (jnp.exp(jnp.sum(dl[0:1] * dl[1:2], axis=1, keepdims=True))
           - jnp.exp(jnp.sum(dl[2:3] * dl[3:4], axis=1, keepdims=True)) + lam_init)
    a0 = acc_ref[0]
    a1 = acc_ref[1]
    o = a0[:, 0:A_V_DIM] / a0[:, A_V_DIM:] - lam * (a1[:, 0:A_V_DIM] / a1[:, A_V_DIM:])
    o = o * lax.rsqrt(jnp.mean(o * o, axis=-1, keepdims=True) + EPS)
    o_ref[...] = (o * (1.0 - lam_init)).astype(o_ref.dtype)


def _diff_attention(pb, diff_lambda, slopes_a, layer_idx, tq, tk):
    bsz, seq, _ = pb.shape
    assert seq % (2 * tk) == 0 and tk % tq == 0
    lam_init = 0.8 - 0.6 * math.exp(-0.3 * layer_idx)
    kern = functools.partial(_diff_attn_kernel, seq=seq, tq=tq, tk=tk, lam_init=lam_init)
    return pl.pallas_call(
        kern,
        grid=(bsz, A_HEADS, seq // tq),
        in_specs=[
            pl.BlockSpec(memory_space=pltpu.SMEM),
            pl.BlockSpec((None, tq, LANES), lambda b, h, i: (b, i, _PB_AQ + h)),
            pl.BlockSpec((None, seq, LANES), lambda b, h, i: (b, 0, _PB_AK + h)),
            pl.BlockSpec((None, seq, LANES), lambda b, h, i: (b, 0, _PB_AV + h)),
            pl.BlockSpec((4, A_QK_DIM), lambda b, h, i: (0, 0)),
        ],
        out_specs=pl.BlockSpec((None, tq, LANES), lambda b, h, i: (b, i, h)),
        out_shape=jax.ShapeDtypeStruct((bsz, seq, A_HEADS * A_V_DIM), jnp.bfloat16),
        scratch_shapes=[
            pltpu.VMEM((2, seq, LANES), jnp.bfloat16),
            pltpu.VMEM((seq, 2 * A_V_DIM), jnp.bfloat16),
            pltpu.VMEM((6, tq, LANES), jnp.bfloat16),
            pltpu.VMEM((2, tq, 1), jnp.float32),
            pltpu.VMEM((2, tq, 2 * A_V_DIM), jnp.float32),
            pltpu.VMEM((2, tq, tk), jnp.float32),
            pltpu.VMEM((2, tq, tk), jnp.float32),
        ],
        compiler_params=_cparams("parallel", "parallel", "arbitrary"),
        name="diff_attention",
    )(slopes_a, pb, pb, pb, diff_lambda)


def _retention_kernel(q_ref, k_ref, v_ref, g_ref, dec_ref, o_ref, of_ref, ob_ref, *, seq):
    head = pl.program_id(1)
    nc = seq // BLOCK
    f32 = jnp.float32

    lgv = _log_sigmoid(dec_ref[...])
    rr = lax.broadcasted_iota(jnp.int32, lgv.shape, 0)
    cc = lax.broadcasted_iota(jnp.int32, lgv.shape, 1)

    def pick(row):
        sel = jnp.where((rr == row) & (cc == head), lgv, 0.0)
        return jnp.sum(jnp.sum(sel, axis=1, keepdims=True), axis=0, keepdims=True)

    lg_f, lg_b = pick(0), pick(1)
    ri = lax.broadcasted_iota(jnp.int32, (BLOCK, BLOCK), 0).astype(f32)
    ci = lax.broadcasted_iota(jnp.int32, (BLOCK, BLOCK), 1).astype(f32)
    diff = ri - ci
    decay_f = jnp.where(diff >= 0, jnp.exp(lg_f * jnp.maximum(diff, 0.0)), 0.0)
    decay_b = jnp.where(diff <= 0, jnp.exp(lg_b * jnp.maximum(-diff, 0.0)), 0.0)
    idx = lax.broadcasted_iota(jnp.int32, (BLOCK, 1), 0).astype(f32)
    qdec_f = jnp.exp(lg_f * (idx + 1.0))
    kdec_f = jnp.exp(lg_f * (BLOCK - 1.0 - idx))
    qdec_b = jnp.exp(lg_b * (BLOCK - idx))
    kdec_b = jnp.exp(lg_b * idx)
    cdec_f = jnp.exp(lg_f * BLOCK)
    cdec_b = jnp.exp(lg_b * BLOCK)
    kscale = B_DIM ** -0.5

    def scores_and_kv(t0, backward):
        qb = q_ref[pl.ds(t0, BLOCK), :]
        kb = k_ref[pl.ds(t0, BLOCK), :].astype(f32) * kscale
        vb = v_ref[pl.ds(t0, BLOCK), :]
        kdec = kdec_b if backward else kdec_f
        return dict(t0=t0, backward=backward, qb=qb, vb=vb,
                    qk=_dot_nt(qb, kb.astype(qb.dtype)), kv=_dot_tn((kb * kdec).astype(vb.dtype), vb))

    def inner_product(g):
        s = g["qk"] * (decay_b if g["backward"] else decay_f)
        g["inner"] = _dot(s.astype(g["vb"].dtype), g["vb"])

    def with_state(g, state, out_ref):
        qdec, cdec = (qdec_b, cdec_b) if g["backward"] else (qdec_f, cdec_f)
        cross = _dot(g["qb"], state.astype(g["qb"].dtype)) * qdec
        out_ref[pl.ds(g["t0"], BLOCK), :] = g["inner"] + cross
        return state * cdec + g["kv"]

    def body(it, carry):
        rf, rb = carry
        steps = []
        for u in range(_SCAN_UNROLL):
            c = it * _SCAN_UNROLL + u
            steps.append(scores_and_kv(pl.multiple_of(c * BLOCK, BLOCK), False))
            steps.append(scores_and_kv(pl.multiple_of((nc - 1 - c) * BLOCK, BLOCK), True))
        for g in steps:
            inner_product(g)
        for g in steps:
            if g["backward"]:
                rb = with_state(g, rb, ob_ref)
            else:
                rf = with_state(g, rf, of_ref)
        return rf, rb

    zero = jnp.zeros((B_DIM, B_DIM), f32)
    lax.fori_loop(0, nc // _SCAN_UNROLL, body, (zero, zero))

    def fin(c, carry):
        t0 = pl.multiple_of(c * BLOCK, BLOCK)
        r = of_ref[pl.ds(t0, BLOCK), :] + ob_ref[pl.ds(t0, BLOCK), :]
        r = r * lax.rsqrt(jnp.mean(r * r, axis=-1, keepdims=True) + EPS)
        g = g_ref[pl.ds(t0, BLOCK), :].astype(f32)
        o_ref[pl.ds(t0, BLOCK), :] = (g * jax.nn.sigmoid(g) * r).astype(o_ref.dtype)
        return carry

    lax.fori_loop(0, nc, fin, 0, unroll=4)


def _retention(pb, dec_pad):
    bsz, seq, _ = pb.shape
    kern = functools.partial(_retention_kernel, seq=seq)

    def col(base):
        return pl.BlockSpec((None, seq, LANES), lambda b, h: (b, 0, base + h))

    return pl.pallas_call(
        kern,
        grid=(bsz, B_HEADS),
        in_specs=[col(_PB_BQ), col(_PB_BK), col(_PB_BV), col(_PB_BG),
                  pl.BlockSpec((8, LANES), lambda b, h: (0, 0))],
        out_specs=pl.BlockSpec((None, seq, LANES), lambda b, h: (b, 0, h)),
        out_shape=jax.ShapeDtypeStruct((bsz, seq, B_HEADS * B_DIM), jnp.bfloat16),
        scratch_shapes=[pltpu.VMEM((seq, B_DIM), jnp.float32), pltpu.VMEM((seq, B_DIM), jnp.float32)],
        compiler_params=_cparams("parallel", "parallel"),
        name="retention",
    )(pb, pb, pb, pb, dec_pad)


_ROWS_PER_DIR = 6


def _mlstm_kernel(cq_ref, ck_ref, gt_ref, v_ref, og_ref, wq_ref, wk_ref, gb_ref, o_ref,
                  qc_ref, kt_ref, vx_ref, rows_ref, of_ref, ob_ref, *, seq):
    nc = seq // BLOCK
    f32 = jnp.float32
    bf16 = jnp.bfloat16
    rowi = lax.broadcasted_iota(jnp.int32, (BLOCK, LANES), 0)
    coli = lax.broadcasted_iota(jnp.int32, (BLOCK, LANES), 1)
    lane_n = lax.broadcasted_iota(jnp.int32, (nc, LANES), 1)
    tril = rowi >= coli
    triu = rowi <= coli
    eye = rowi == coli
    tril_b = tril.astype(bf16)
    triu_b = triu.astype(bf16)
    ones_b = jnp.ones((BLOCK, LANES), bf16)
    head = pl.program_id(1)

    def cummax_lanes(x, backward):
        sh = 1
        while sh < LANES:
            if backward:
                x = jnp.maximum(x, jnp.where(lane_n < LANES - sh, pltpu.roll(x, LANES - sh, 1), _NEG_INF))
            else:
                x = jnp.maximum(x, jnp.where(lane_n >= sh, pltpu.roll(x, sh, 1), _NEG_INF))
            sh *= 2
        return x

    for d, (op, last) in enumerate(((triu_b, LANES - 1), (tril_b, 0))):
        lf = _log_sigmoid(gt_ref[2 * d + 1] + gb_ref[2 * d + 1, head])
        hi, lo = _split_hi_lo(lf)
        b = _dot(hi, op) + _dot(lo, op)
        a = gt_ref[2 * d] + gb_ref[2 * d, head] - b
        bl = jnp.broadcast_to(b[:, last:last + 1], (nc, LANES))
        ml = jnp.broadcast_to(jnp.max(bl + a, axis=1, keepdims=True), (nc, LANES))
        for r, val in enumerate((a, cummax_lanes(a, d == 1), lf, jnp.exp(bl + a - ml), bl, ml)):
            rows_ref[d * _ROWS_PER_DIR + r] = val

    def conv_silu(src_ref, w, c, t0):
        x = src_ref[pl.ds(t0, BLOCK), :]
        prev = src_ref[pl.ds(jnp.maximum(t0 - 1, 0), 1), :] * jnp.where(c > 0, 1.0, 0.0)
        nxt = src_ref[pl.ds(jnp.minimum(t0 + BLOCK, seq - 1), 1), :] * jnp.where(c < nc - 1, 1.0, 0.0)
        xd = jnp.where(rowi == 0, prev, pltpu.roll(x, 1, 0))
        xu = jnp.where(rowi == BLOCK - 1, nxt, pltpu.roll(x, BLOCK - 1, 0))
        y = xd * w[0:1] + x * w[1:2] + xu * w[2:3]
        return y * jax.nn.sigmoid(y)

    def prep(c, carry):
        t0 = pl.multiple_of(c * BLOCK, BLOCK)
        qc_ref[pl.ds(t0, BLOCK), :] = (conv_silu(cq_ref, wq_ref[...], c, t0) * (C_DIM ** -0.5)).astype(bf16)
        kt_ref[c] = conv_silu(ck_ref, wk_ref[...], c, t0).T.astype(bf16)
        vx_ref[pl.ds(t0, BLOCK), 0:C_DIM] = v_ref[pl.ds(t0, BLOCK), :]
        vx_ref[pl.ds(t0, BLOCK), C_DIM:2 * C_DIM] = ones_b
        return carry

    lax.fori_loop(0, nc, prep, 0, unroll=2)

    def load_rows(ch, backward):
        t0 = pl.multiple_of(ch * BLOCK, BLOCK)
        base = _ROWS_PER_DIR if backward else 0
        a, c, lf, w, bl, ml = (rows_ref[base + r, pl.ds(ch, 1), :] for r in range(_ROWS_PER_DIR))
        mask = triu if backward else tril
        kt = kt_ref[ch]
        return dict(t0=t0, backward=backward, bl=bl[:, 0:1], m_loc=ml[:, 0:1],
                    a_t=jnp.where(mask, a, _NEG_INF),
                    lf_hl=_split_hi_lo(jnp.where(mask, lf, 0.0)),
                    c_hl=_split_hi_lo(jnp.where(eye, c, 0.0)),
                    kw=(kt.astype(f32) * w).astype(bf16), kt=kt)

    def first_products(g):
        vx = vx_ref[pl.ds(g["t0"], BLOCK), :]
        g["qk"] = _dot(qc_ref[pl.ds(g["t0"], BLOCK), :], g["kt"])
        g["b_full"] = _dot(g["lf_hl"][0], ones_b) + _dot(g["lf_hl"][1], ones_b)
        g["c_full"] = _dot(g["c_hl"][0], ones_b) + _dot(g["c_hl"][1], ones_b)
        g["kv"] = _dot(g["kw"], vx)

    def decayed_scores(g):
        g["s0"] = (g["qk"] * jnp.exp(g["a_t"] - g["c_full"])).astype(bf16)

    def inner_products(g):
        g["p0"] = _dot(g["s0"], vx_ref[pl.ds(g["t0"], BLOCK), :])

    def with_state(g, state, out_ref):
        cx, m = state
        x = _dot(qc_ref[pl.ds(g["t0"], BLOCK), :], cx.astype(bf16))
        n = jnp.maximum(g["c_full"], m)
        e = jnp.exp(g["c_full"] - n)
        iw = jnp.exp(m - n)
        num = e * g["p0"][:, 0:C_DIM] + iw * x[:, 0:C_DIM]
        den = e * g["p0"][:, C_DIM:] + iw * x[:, C_DIM:]
        out_ref[pl.ds(g["t0"], BLOCK), :] = num / jnp.maximum(jnp.abs(den), jnp.exp(-(g["b_full"] + n)))
        m_new = jnp.maximum(g["bl"] + m, g["m_loc"])
        cw = jnp.exp(g["bl"] + m - m_new)
        f = jnp.exp(g["m_loc"] - m_new)
        return cw * cx + f * g["kv"], m_new

    def body(it, carry):
        sf, sb = carry
        steps = []
        for u in range(_SCAN_UNROLL):
            c = it * _SCAN_UNROLL + u
            steps.append(load_rows(c, False))
            steps.append(load_rows(nc - 1 - c, True))
        for g in steps:
            first_products(g)
        for g in steps:
            decayed_scores(g)
        for g in steps:
            inner_products(g)
        for g in steps:
            if g["backward"]:
                sb = with_state(g, sb, ob_ref)
            else:
                sf = with_state(g, sf, of_ref)
        return sf, sb

    init = (jnp.zeros((C_DIM, 2 * C_DIM), f32), jnp.zeros((1, 1), f32))
    lax.fori_loop(0, nc // _SCAN_UNROLL, body, (init, init))

    def fin(c, carry):
        t0 = pl.multiple_of(c * BLOCK, BLOCK)
        r = of_ref[pl.ds(t0, BLOCK), :] + ob_ref[pl.ds(t0, BLOCK), :]
        r = r * lax.rsqrt(jnp.mean(r * r, axis=-1, keepdims=True) + EPS)
        og = og_ref[pl.ds(t0, BLOCK), :].astype(f32)
        o_ref[pl.ds(t0, BLOCK), :] = (jax.nn.sigmoid(og) * r).astype(o_ref.dtype)
        return carry

    lax.fori_loop(0, nc, fin, 0, unroll=4)


def _mlstm(pb, pf, gates_t, wq_conv, wk_conv, gate_bias):
    bsz, seq, _ = pb.shape
    assert seq % (8 * BLOCK) == 0
    kern = functools.partial(_mlstm_kernel, seq=seq)

    def col(base, buffers=2):
        return pl.BlockSpec((None, seq, LANES), lambda b, h: (b, 0, base + h),
                            pipeline_mode=pl.Buffered(buffers))

    return pl.pallas_call(
        kern,
        grid=(bsz, C_HEADS),
        in_specs=[col(_PF_CQ, 1), col(_PF_CK, 1),
                  pl.BlockSpec((None, None, 4, seq // BLOCK, LANES), lambda b, h: (b, h, 0, 0, 0)),
                  col(_PB_CV), col(_PB_CO),
                  pl.BlockSpec((3, LANES), lambda b, h: (0, h)),
                  pl.BlockSpec((3, LANES), lambda b, h: (0, h)),
                  pl.BlockSpec(memory_space=pltpu.SMEM)],
        out_specs=pl.BlockSpec((None, seq, LANES), lambda b, h: (b, 0, h)),
        out_shape=jax.ShapeDtypeStruct((bsz, seq, C_HEADS * C_DIM), jnp.bfloat16),
        scratch_shapes=[pltpu.VMEM((seq, C_DIM), jnp.bfloat16),
                        pltpu.VMEM((seq // BLOCK, C_DIM, BLOCK), jnp.bfloat16),
                        pltpu.VMEM((seq, 2 * C_DIM), jnp.bfloat16),
                        pltpu.VMEM((2 * _ROWS_PER_DIR, seq // BLOCK, LANES), jnp.float32),
                        pltpu.VMEM((seq, C_DIM), jnp.float32), pltpu.VMEM((seq, C_DIM), jnp.float32)],
        compiler_params=_cparams("parallel", "parallel"),
        name="mlstm",
    )(pf, pf, gates_t, pb, pb, wq_conv, wk_conv, gate_bias)


def _window_attn_kernel(slopes_ref, sink_ref, q_ref, k_ref, v_ref, o_ref, kx_ref, vx_ref, *, seq):
    qi = pl.program_id(1)
    f32 = jnp.float32
    span = BLOCK + 2 * WINDOW
    group = D_HEADS // D_KV_HEADS

    @pl.when(qi == 0)
    def _():
        lane = lax.broadcasted_iota(jnp.int32, (seq, LANES), 1)
        for src, dst in ((k_ref, kx_ref), (v_ref, vx_ref)):
            x = src[...].astype(f32)
            x0 = jnp.where(lane < D_DIM, x, 0.0)
            x1 = jnp.where(lane >= D_DIM, x, 0.0)
            dst[0] = x0.astype(dst.dtype)
            dst[1] = pltpu.roll(x0, D_DIM, 1).astype(dst.dtype)
            dst[2] = pltpu.roll(x1, D_DIM, 1).astype(dst.dtype)
            dst[3] = x1.astype(dst.dtype)

    t0 = qi * BLOCK
    ws = pl.multiple_of(jnp.clip(t0 - WINDOW, 0, seq - span), BLOCK)
    rel = (lax.broadcasted_iota(jnp.int32, (BLOCK, span), 0)
           - lax.broadcasted_iota(jnp.int32, (BLOCK, span), 1)) + (t0 - ws)
    dist = jnp.abs(rel)
    valid = dist <= WINDOW
    distf = dist.astype(f32)
    scale = jnp.asarray(D_DIM ** -0.5, q_ref.dtype)

    heads = [(pair, half, (2 * pair) // group) for pair in range(D_HEADS // 2) for half in range(2)]
    scores = [_dot_nt(q_ref[:, pair * LANES:(pair + 1) * LANES] * scale,
                      kx_ref[2 * kv + half, pl.ds(ws, span), :]) for pair, half, kv in heads]
    probs = []
    for (pair, half, kv), s in zip(heads, scores):
        hd = 2 * pair + half
        s = jnp.where(valid, s - slopes_ref[hd] * distf, _NEG_INF)
        snk = sink_ref[hd]
        m = jnp.maximum(jnp.max(s, axis=1, keepdims=True), snk)
        p = jnp.exp(s - m)
        denom = jnp.sum(p, axis=1, keepdims=True) + jnp.exp(snk - m)
        probs.append((p / denom).astype(jnp.bfloat16))
    for pair in range(D_HEADS // 2):
        kv = (2 * pair) // group
        out = (_dot(probs[2 * pair], vx_ref[2 * kv, pl.ds(ws, span), :])
               + _dot(probs[2 * pair + 1], vx_ref[2 * kv + 1, pl.ds(ws, span), :]))
        o_ref[:, pair * LANES:(pair + 1) * LANES] = out.astype(o_ref.dtype)


def _window_attention(pb, attn_sink, slopes_d):
    bsz, seq, _ = pb.shape
    kern = functools.partial(_window_attn_kernel, seq=seq)
    width = D_HEADS * D_DIM
    return pl.pallas_call(
        kern,
        grid=(bsz, seq // BLOCK),
        in_specs=[
            pl.BlockSpec(memory_space=pltpu.SMEM),
            pl.BlockSpec(memory_space=pltpu.SMEM),
            pl.BlockSpec((None, BLOCK, width), lambda b, i: (b, i, _PB_DQ512)),
            pl.BlockSpec((None, seq, LANES), lambda b, i: (b, 0, _PB_DK)),
            pl.BlockSpec((None, seq, LANES), lambda b, i: (b, 0, _PB_DV)),
        ],
        out_specs=pl.BlockSpec((None, BLOCK, width), lambda b, i: (b, i, 0)),
        out_shape=jax.ShapeDtypeStruct((bsz, seq, width), jnp.bfloat16),
        scratch_shapes=[pltpu.VMEM((4, seq, LANES), jnp.bfloat16), pltpu.VMEM((4, seq, LANES), jnp.bfloat16)],
        compiler_params=_cparams("parallel", "arbitrary"),
        name="window_attention",
    )(slopes_d, attn_sink, pb, pb, pb)


def _out_proj_kernel(x_ref, ya_ref, yb_ref, yc_ref, yd_ref, w_ref, o_ref):
    acc = x_ref[...]
    for g, y_ref in enumerate((ya_ref, yb_ref, yc_ref, yd_ref)):
        acc = acc + _dot(y_ref[...], w_ref[g * 512:(g + 1) * 512, :])
    o_ref[...] = acc


def _out_proj(x2d, ys, w_out, tm):
    t, d = x2d.shape
    yspec = pl.BlockSpec((tm, 512), lambda i: (i, 0))
    return pl.pallas_call(
        _out_proj_kernel,
        grid=(t // tm,),
        in_specs=[pl.BlockSpec((tm, d), lambda i: (i, 0)), yspec, yspec, yspec, yspec,
                  pl.BlockSpec((d, d), lambda i: (0, 0))],
        out_specs=pl.BlockSpec((tm, d), lambda i: (i, 0)),
        out_shape=jax.ShapeDtypeStruct((t, d), jnp.float32),
        compiler_params=_cparams("parallel"),
        name="out_proj",
    )(x2d, *ys, w_out)


def _ffn_kernel(x_ref, xp_ref, xn_ref, nw_ref, wg_ref, wu_ref, cw_ref, wd_ref, fw_ref, o_ref,
                h_ref, acc_ref, *, seq, tm, final_norm):
    i = pl.program_id(0)
    j = pl.program_id(1)
    f32 = jnp.float32

    def norm(x):
        return x * lax.rsqrt(jnp.mean(x * x, axis=-1, keepdims=True) + EPS) * nw_ref[...]

    @pl.when(j == 0)
    def _():
        h_ref[0:tm, :] = norm(x_ref[...]).astype(h_ref.dtype)
        has_prev = ((i * tm) % seq != 0).astype(f32)
        has_next = (((i + 1) * tm) % seq != 0).astype(f32)
        halo = jnp.concatenate([norm(xp_ref[...]) * has_prev, norm(xn_ref[...]) * has_next], axis=0)
        h_ref[tm:tm + 16, :] = halo.astype(h_ref.dtype)
        acc_ref[...] = jnp.zeros(acc_ref.shape, f32)

    gate = _dot(h_ref[...], wg_ref[...])
    gm = gate[0:tm]
    rowi = lax.broadcasted_iota(jnp.int32, gm.shape, 0)
    g_dn = jnp.where(rowi == 0, gate[tm + 7:tm + 8], pltpu.roll(gm, 1, 0))
    g_up = jnp.where(rowi == tm - 1, gate[tm + 8:tm + 9], pltpu.roll(gm, tm - 1, 0))
    cw = cw_ref[...]
    g = g_dn * cw[0:1] + gm * cw[1:2] + g_up * cw[2:3]
    up = _dot(h_ref[0:tm, :], wu_ref[...])
    act = (g * jax.nn.sigmoid(g) * up).astype(wd_ref.dtype)
    acc_ref[...] += _dot(act, wd_ref[...])

    @pl.when(j == pl.num_programs(1) - 1)
    def _():
        y = x_ref[...] + acc_ref[...]
        if final_norm:
            y = y * lax.rsqrt(jnp.mean(y * y, axis=-1, keepdims=True) + EPS) * fw_ref[...]
        o_ref[...] = y


def _ffn(x2d, seq, nw, wg, wu, cw, wd, fw, final_norm, tm, tn):
    t, d = x2d.shape
    nff = wg.shape[1]
    rb = tm // 8
    last8 = t // 8 - 1
    kern = functools.partial(_ffn_kernel, seq=seq, tm=tm, final_norm=final_norm)
    return pl.pallas_call(
        kern,
        grid=(t // tm, nff // tn),
        in_specs=[
            pl.BlockSpec((tm, d), lambda i, j: (i, 0)),
            pl.BlockSpec((8, d), lambda i, j: (jnp.maximum(i * rb - 1, 0), 0)),
            pl.BlockSpec((8, d), lambda i, j: (jnp.minimum((i + 1) * rb, last8), 0)),
            pl.BlockSpec((1, d), lambda i, j: (0, 0)),
            pl.BlockSpec((d, tn), lambda i, j: (0, j)),
            pl.BlockSpec((d, tn), lambda i, j: (0, j)),
            pl.BlockSpec((3, tn), lambda i, j: (0, j)),
            pl.BlockSpec((tn, d), lambda i, j: (j, 0)),
            pl.BlockSpec((1, d), lambda i, j: (0, 0)),
        ],
        out_specs=pl.BlockSpec((tm, d), lambda i, j: (i, 0)),
        out_shape=jax.ShapeDtypeStruct((t, d), jnp.float32),
        scratch_shapes=[pltpu.VMEM((tm + 16, d), jnp.bfloat16), pltpu.VMEM((tm, d), jnp.float32)],
        compiler_params=_cparams("parallel", "arbitrary"),
        name="ffn",
    )(x2d, x2d, x2d, nw, wg, wu, cw, wd, fw)


def _prepare_layer(l, norm1_w, w_in, qk_conv_w, diff_lambda, ret_decay_logit, mlstm_gate_bias, attn_sink,
                   w_out, norm2_w, w_gate, w_up, ffn_conv_w, w_down):
    bf16 = jnp.bfloat16
    f32 = jnp.float32
    wi = w_in[l]
    w_pb = jnp.concatenate([wi[:, _OFF_A:_OFF_CQ], wi[:, _OFF_CV:_OFF_GATES], wi[:, _OFF_D:N_IN]], axis=1)
    wgates = jnp.pad(wi[:, _OFF_GATES:_OFF_D], ((0, 0), (0, LANES - N_GATES)))
    w_pf = jnp.concatenate([wi[:, _OFF_CQ:_OFF_CV], wgates], axis=1)
    gbias = mlstm_gate_bias[l].astype(f32)
    dec = jnp.zeros((8, LANES), f32).at[0:2, 0:B_HEADS].set(ret_decay_logit[l].astype(f32))
    return dict(
        norm1=norm1_w[l].reshape(1, D_MODEL).astype(f32),
        w_pb=w_pb.astype(bf16), w_pf=w_pf.astype(bf16),
        wq_conv=qk_conv_w[l][:, :C_HEADS * C_DIM].astype(f32),
        wk_conv=qk_conv_w[l][:, C_HEADS * C_DIM:].astype(f32),
        gbias=gbias, dec=dec,
        diff_lambda=diff_lambda[l].astype(f32), sink=attn_sink[l].astype(f32),
        w_out=w_out[l].astype(bf16),
        norm2=norm2_w[l].reshape(1, D_MODEL).astype(f32),
        w_gate=w_gate[l].astype(bf16), w_up=w_up[l].astype(bf16),
        ffn_conv=ffn_conv_w[l].astype(f32), w_down=w_down[l].astype(bf16),
    )


def _tile(n, pref):
    t = min(pref, n)
    while n % t:
        t //= 2
    return t


def _layer(x2d, bsz, seq, layer_idx, p, slopes, final_w, final_norm):
    t = bsz * seq
    tm_in = _tile(t, 1024)
    pb = _norm_matmul(x2d, p["norm1"], p["w_pb"], jnp.bfloat16, tm_in, 896).reshape(bsz, seq, PB_WIDTH)
    pf = _norm_matmul(x2d, p["norm1"], p["w_pf"], jnp.float32, tm_in, PF_WIDTH).reshape(bsz, seq, PF_WIDTH)
    gates_t = pf[:, :, _PF_GATES * LANES:_PF_GATES * LANES + N_GATES].reshape(bsz, seq, 4, C_HEADS)
    gates_t = jnp.transpose(gates_t, (0, 3, 2, 1)).reshape(bsz, C_HEADS, 4, seq // BLOCK, BLOCK)
    y_a = _diff_attention(pb, p["diff_lambda"], slopes[D_HEADS:], layer_idx, _tile(seq, 512), _tile(seq // 2, 512))
    y_b = _retention(pb, p["dec"])
    y_c = _mlstm(pb, pf, gates_t, p["wq_conv"], p["wk_conv"], p["gbias"])
    y_d = _window_attention(pb, p["sink"], slopes[:D_HEADS])
    ys = [y.reshape(t, 512) for y in (y_a, y_b, y_c, y_d)]
    x1 = _out_proj(x2d, ys, p["w_out"], _tile(t, 512))
    return _ffn(x1, seq, p["norm2"], p["w_gate"], p["w_up"], p["ffn_conv"], p["w_down"], final_w,
                final_norm, _tile(seq, 512), 512)


def _trunk(x, layers, slopes, final_w):
    bsz, seq, d = x.shape
    x2d = x.reshape(bsz * seq, d)
    for l, p in enumerate(layers):
        x2d = _layer(x2d, bsz, seq, l, p, slopes, final_w, l == len(layers) - 1)
    return x2d.reshape(bsz, seq, d)


def kernel(x_prompt, x_sample, norm1_w, w_in, qk_conv_w, diff_lambda, ret_decay_logit, mlstm_gate_bias,
           attn_sink, w_out, norm2_w, w_gate, w_up, ffn_conv_w, w_down, final_norm_w):
    layers = [_prepare_layer(l, norm1_w, w_in, qk_conv_w, diff_lambda, ret_decay_logit, mlstm_gate_bias,
                             attn_sink, w_out, norm2_w, w_gate, w_up, ffn_conv_w, w_down)
              for l in range(DEPTH)]
    n = A_HEADS + D_HEADS
    slopes = 2.0 ** (-8.0 * jnp.arange(1, n + 1, dtype=jnp.float32) / n)
    final_w = final_norm_w.reshape(1, D_MODEL).astype(jnp.float32)
    return (_trunk(x_prompt, layers, slopes, final_w), _trunk(x_sample, layers, slopes, final_w))
```

```python
import functools
import math

import jax
import jax.numpy as jnp
from jax import lax
from jax.experimental import pallas as pl
from jax.experimental.pallas import tpu as pltpu

D_MODEL = 2048
DEPTH = 2
A_HEADS = 4
A_QK_DIM = 64
A_V_DIM = 128
B_HEADS = 4
B_DIM = 128
C_HEADS = 4
C_DIM = 128
D_HEADS = 8
D_KV_HEADS = 2
D_DIM = 64
WINDOW = 128
BLOCK = 128
D_FF = 5632
EPS = 1e-6
LANES = 128

_OFF_A = 0
_OFF_B = 1536
_OFF_CQ = 3584
_OFF_CV = 4608
_OFF_GATES = 5632
_OFF_D = 5648
N_IN = 6416

_PB_AQ, _PB_AK, _PB_AV = 0, 4, 8
_PB_BQ, _PB_BK, _PB_BV, _PB_BG = 12, 16, 20, 24
_PB_CV, _PB_CO = 28, 32
_PB_DQ512 = 9
_PB_DK, _PB_DV = 40, 41
PB_WIDTH = 42 * LANES
_IN_PROJ_TN = 768
_PF_CQ, _PF_CK, _PF_GATES = 0, 4, 8
PF_WIDTH = 9 * LANES
N_GATES = 16

_VMEM_LIMIT = 56 * 1024 * 1024
_SCAN_UNROLL = 2
_NEG_INF = float("-inf")


def _cparams(*sem):
    return pltpu.CompilerParams(dimension_semantics=sem, vmem_limit_bytes=_VMEM_LIMIT)


def _dot(a, b):
    return jnp.dot(a, b, preferred_element_type=jnp.float32)


def _dot_nt(a, b):
    return lax.dot_general(a, b, (((1,), (1,)), ((), ())), preferred_element_type=jnp.float32)


def _dot_tn(a, b):
    return lax.dot_general(a, b, (((0,), (0,)), ((), ())), preferred_element_type=jnp.float32)


def _log_sigmoid(x):
    return -(jnp.maximum(-x, 0.0) + jnp.log1p(jnp.exp(-jnp.abs(x))))


def _split_hi_lo(x):
    hi = x.astype(jnp.bfloat16)
    lo = (x - hi.astype(jnp.float32)).astype(jnp.bfloat16)
    return hi, lo


def _in_proj_kernel(x_ref, nw_ref, wb_ref, wf_ref, ob_ref, of_ref, h_ref, *, nb):
    j = pl.program_id(1)

    @pl.when(j == 0)
    def _():
        x = x_ref[...]
        ms = jnp.mean(x * x, axis=-1, keepdims=True)
        h_ref[...] = (x * lax.rsqrt(ms + EPS) * nw_ref[...]).astype(h_ref.dtype)

    @pl.when(j < nb)
    def _():
        ob_ref[...] = _dot(h_ref[...], wb_ref[...]).astype(ob_ref.dtype)

    @pl.when(j == nb)
    def _():
        of_ref[...] = _dot(h_ref[...], wf_ref[...])


def _in_proj(x2d, nw, w_pb, w_pf, tm, tn):
    t, d = x2d.shape
    nb = PB_WIDTH // tn
    return pl.pallas_call(
        functools.partial(_in_proj_kernel, nb=nb),
        grid=(t // tm, nb + 1),
        in_specs=[
            pl.BlockSpec((tm, d), lambda i, j: (i, 0)),
            pl.BlockSpec((1, d), lambda i, j: (0, 0)),
            pl.BlockSpec((d, tn), lambda i, j: (0, jnp.minimum(j, nb - 1))),
            pl.BlockSpec((d, PF_WIDTH), lambda i, j: (0, 0)),
        ],
        out_specs=[pl.BlockSpec((tm, tn), lambda i, j: (i, jnp.minimum(j, nb - 1))),
                   pl.BlockSpec((tm, PF_WIDTH), lambda i, j: (i, 0))],
        out_shape=[jax.ShapeDtypeStruct((t, PB_WIDTH), jnp.bfloat16),
                   jax.ShapeDtypeStruct((t, PF_WIDTH), jnp.float32)],
        scratch_shapes=[pltpu.VMEM((tm, d), jnp.bfloat16)],
        compiler_params=_cparams("parallel", "arbitrary"),
        name="norm_in_proj",
    )(x2d, nw, w_pb, w_pf)


_LOG2E = 1.4426950408889634
_N_BIAS_LANES = 3
_V_ROWS = A_V_DIM + 16
_ATTN_TILES_PER_BODY = 4


def _diff_attn_kernel(slopes_ref, q_ref, k_ref, v_ref, dl_ref, o_ref,
                      kx_ref, vt_ref, qt_ref, m_ref, acc_ref, sa_ref, sb_ref, *, seq, tq, tk, lam_init):
    head = pl.program_id(1)
    qi = pl.program_id(2)
    f32 = jnp.float32
    bf16 = jnp.bfloat16
    sigma = slopes_ref[head] * _LOG2E
    nk = seq // tk
    prep_rows = tk

    @pl.when(qi == 0)
    def _():
        lane = lax.broadcasted_iota(jnp.int32, (prep_rows, LANES), 1)
        row = lax.broadcasted_iota(jnp.int32, (prep_rows, LANES), 0)
        ones_row = jnp.where(lax.broadcasted_iota(jnp.int32, (_V_ROWS - A_V_DIM, tk), 0) == 0, 1.0, 0.0)

        def prep(c, carry):
            r0 = pl.multiple_of(c * prep_rows, prep_rows)
            kk = k_ref[pl.ds(r0, prep_rows), :].astype(f32)
            col = sigma * (row + r0).astype(f32)
            c1 = col.astype(bf16).astype(f32)
            c2 = (col - c1).astype(bf16).astype(f32)
            c3 = col - c1 - c2
            bias = jnp.where(lane == A_QK_DIM, c1, jnp.where(lane == A_QK_DIM + 1, c2,
                             jnp.where(lane == A_QK_DIM + 2, c3, 0.0)))
            kx_ref[0, pl.ds(r0, prep_rows), :] = jnp.where(lane < A_QK_DIM, kk, bias).astype(bf16)
            kx_ref[1, pl.ds(r0, prep_rows), :] = jnp.where(
                lane < A_QK_DIM, pltpu.roll(kk, A_QK_DIM, 1), bias).astype(bf16)
            vt_ref[c, 0:A_V_DIM, :] = v_ref[pl.ds(r0, prep_rows), :].astype(f32).T.astype(bf16)
            vt_ref[c, A_V_DIM:_V_ROWS, :] = ones_row.astype(bf16)
            return carry

        lax.fori_loop(0, nk, prep, 0)

    qt = (q_ref[...].astype(f32) * (A_QK_DIM ** -0.5 * _LOG2E)).T
    qrow = lax.broadcasted_iota(jnp.int32, (LANES, tq), 0)
    is_bias = (qrow >= A_QK_DIM) & (qrow < A_QK_DIM + _N_BIAS_LANES)
    for mp in range(2):
        qm = qt if mp == 0 else pltpu.roll(qt, A_QK_DIM, 0)
        for var, sign in enumerate((1.0, 0.0, -1.0)):
            qt_ref[3 * mp + var] = jnp.where(qrow < A_QK_DIM, qm, jnp.where(is_bias, sign, 0.0)).astype(bf16)

    q0 = qi * tq
    rho = sigma * (lax.broadcasted_iota(jnp.int32, (1, tq), 1) + q0).astype(f32)
    m_ref[...] = jnp.full(m_ref.shape, _NEG_INF, f32)
    acc_ref[...] = jnp.zeros(acc_ref.shape, f32)

    kd = q0 // tk

    def key_tile(j):
        jm = jnp.minimum(j, nk - 1) - 1
        return jnp.where(j == 0, kd, jnp.where(jm < kd, jm, jm + 1))

    def scores(j, s_ref, mp):
        kb = key_tile(j)
        var = jnp.where(kb < kd, 0, jnp.where(kb > kd, 2, 1))
        k_start = pl.multiple_of(kb * tk, tk)
        s_ref[mp] = _dot(kx_ref[mp, pl.ds(k_start, tk), :], qt_ref[3 * mp + var])

    def softmax_pv(j, s_ref, mp):
        kb = key_tile(j)
        shift = jnp.where(kb < kd, -1.0, jnp.where(kb > kd, 1.0, 0.0)) * rho
        s = s_ref[mp]
        m_old = m_ref[mp]
        m_new = jnp.maximum(m_old, jnp.max(s, axis=0, keepdims=True) + shift)
        p = jnp.exp2(s - (m_new - shift))
        acc_ref[mp] = jnp.exp2(m_old - m_new) * acc_ref[mp] + _dot(vt_ref[kb], p.astype(bf16))
        m_ref[mp] = m_new

    scores(0, sa_ref, 0)
    scores(0, sa_ref, 1)
    rel = (lax.broadcasted_iota(jnp.int32, (tk, tq), 1) - lax.broadcasted_iota(jnp.int32, (tk, tq), 0)
           + (q0 - kd * tk)).astype(f32)
    diag_bias = sigma * jnp.abs(rel)
    for mp in range(2):
        sa_ref[mp] = sa_ref[mp] - diag_bias

    def body(i, carry):
        for u in range(_ATTN_TILES_PER_BODY):
            j = _ATTN_TILES_PER_BODY * i + u
            cur, nxt = (sa_ref, sb_ref) if u % 2 == 0 else (sb_ref, sa_ref)
            for mp in range(2):
                scores(j + 1, nxt, mp)
                softmax_pv(j, cur, mp)
        return carry

    lax.fori_loop(0, nk // _ATTN_TILES_PER_BODY, body, 0)

    dl = dl_ref[...]
    lam = (jnp.exp(jnp.sum(dl[0:1] * dl[1:2], axis=1, keepdims=True))
           - jnp.exp(jnp.sum(dl[2:3] * dl[3:4], axis=1, keepdims=True)) + lam_init)
    a0 = acc_ref[0]
    a1 = acc_ref[1]
    ot = (a0[0:A_V_DIM] / a0[A_V_DIM:A_V_DIM + 1] - lam * (a1[0:A_V_DIM] / a1[A_V_DIM:A_V_DIM + 1]))
    ot = ot * lax.rsqrt(jnp.mean(ot * ot, axis=0, keepdims=True) + EPS)
    o_ref[...] = (ot * (1.0 - lam_init)).T.astype(o_ref.dtype)


def _diff_attention(pb, diff_lambda, slopes_a, layer_idx, tq, tk):
    bsz, seq, _ = pb.shape
    assert seq % (_ATTN_TILES_PER_BODY * tk) == 0 and tk % tq == 0
    lam_init = 0.8 - 0.6 * math.exp(-0.3 * layer_idx)
    kern = functools.partial(_diff_attn_kernel, seq=seq, tq=tq, tk=tk, lam_init=lam_init)
    return pl.pallas_call(
        kern,
        grid=(bsz, A_HEADS, seq // tq),
        in_specs=[
            pl.BlockSpec(memory_space=pltpu.SMEM),
            pl.BlockSpec((None, tq, LANES), lambda b, h, i: (b, i, _PB_AQ + h)),
            pl.BlockSpec((None, seq, LANES), lambda b, h, i: (b, 0, _PB_AK + h)),
            pl.BlockSpec((None, seq, LANES), lambda b, h, i: (b, 0, _PB_AV + h)),
            pl.BlockSpec((4, A_QK_DIM), lambda b, h, i: (0, 0)),
        ],
        out_specs=pl.BlockSpec((None, tq, LANES), lambda b, h, i: (b, i, h)),
        out_shape=jax.ShapeDtypeStruct((bsz, seq, A_HEADS * A_V_DIM), jnp.bfloat16),
        scratch_shapes=[
            pltpu.VMEM((2, seq, LANES), jnp.bfloat16),
            pltpu.VMEM((seq // tk, _V_ROWS, tk), jnp.bfloat16),
            pltpu.VMEM((6, LANES, tq), jnp.bfloat16),
            pltpu.VMEM((2, 1, tq), jnp.float32),
            pltpu.VMEM((2, _V_ROWS, tq), jnp.float32),
            pltpu.VMEM((2, tk, tq), jnp.float32),
            pltpu.VMEM((2, tk, tq), jnp.float32),
        ],
        compiler_params=_cparams("parallel", "parallel", "arbitrary"),
        name="diff_attention",
    )(slopes_a, pb, pb, pb, diff_lambda)


def _retention_kernel(q_ref, k_ref, v_ref, g_ref, dec_ref, o_ref, of_ref, ob_ref, *, seq):
    head = pl.program_id(1)
    nc = seq // BLOCK
    f32 = jnp.float32

    lgv = _log_sigmoid(dec_ref[...])
    rr = lax.broadcasted_iota(jnp.int32, lgv.shape, 0)
    cc = lax.broadcasted_iota(jnp.int32, lgv.shape, 1)

    def pick(row):
        sel = jnp.where((rr == row) & (cc == head), lgv, 0.0)
        return jnp.sum(jnp.sum(sel, axis=1, keepdims=True), axis=0, keepdims=True)

    lg_f, lg_b = pick(0), pick(1)
    ri = lax.broadcasted_iota(jnp.int32, (BLOCK, BLOCK), 0).astype(f32)
    ci = lax.broadcasted_iota(jnp.int32, (BLOCK, BLOCK), 1).astype(f32)
    diff = ri - ci
    decay_f = jnp.where(diff >= 0, jnp.exp(lg_f * jnp.maximum(diff, 0.0)), 0.0)
    decay_b = jnp.where(diff <= 0, jnp.exp(lg_b * jnp.maximum(-diff, 0.0)), 0.0)
    idx = lax.broadcasted_iota(jnp.int32, (BLOCK, 1), 0).astype(f32)
    qdec_f = jnp.exp(lg_f * (idx + 1.0))
    kdec_f = jnp.exp(lg_f * (BLOCK - 1.0 - idx))
    qdec_b = jnp.exp(lg_b * (BLOCK - idx))
    kdec_b = jnp.exp(lg_b * idx)
    cdec_f = jnp.exp(lg_f * BLOCK)
    cdec_b = jnp.exp(lg_b * BLOCK)
    kscale = B_DIM ** -0.5

    def scores_and_kv(t0, backward):
        qb = q_ref[pl.ds(t0, BLOCK), :]
        kb = k_ref[pl.ds(t0, BLOCK), :].astype(f32) * kscale
        vb = v_ref[pl.ds(t0, BLOCK), :]
        kdec = kdec_b if backward else kdec_f
        return dict(t0=t0, backward=backward, qb=qb, vb=vb,
                    qk=_dot_nt(qb, kb.astype(qb.dtype)), kv=_dot_tn((kb * kdec).astype(vb.dtype), vb))

    def inner_product(g):
        s = g["qk"] * (decay_b if g["backward"] else decay_f)
        g["inner"] = _dot(s.astype(g["vb"].dtype), g["vb"])

    def with_state(g, state, out_ref):
        qdec, cdec = (qdec_b, cdec_b) if g["backward"] else (qdec_f, cdec_f)
        cross = _dot(g["qb"], state.astype(g["qb"].dtype)) * qdec
        out_ref[pl.ds(g["t0"], BLOCK), :] = g["inner"] + cross
        return state * cdec + g["kv"]

    def body(it, carry):
        rf, rb = carry
        steps = []
        for u in range(_SCAN_UNROLL):
            c = it * _SCAN_UNROLL + u
            steps.append(scores_and_kv(pl.multiple_of(c * BLOCK, BLOCK), False))
            steps.append(scores_and_kv(pl.multiple_of((nc - 1 - c) * BLOCK, BLOCK), True))
        for g in steps:
            inner_product(g)
        for g in steps:
            if g["backward"]:
                rb = with_state(g, rb, ob_ref)
            else:
                rf = with_state(g, rf, of_ref)
        return rf, rb

    zero = jnp.zeros((B_DIM, B_DIM), f32)
    lax.fori_loop(0, nc // _SCAN_UNROLL, body, (zero, zero))

    def fin(c, carry):
        t0 = pl.multiple_of(c * BLOCK, BLOCK)
        r = of_ref[pl.ds(t0, BLOCK), :] + ob_ref[pl.ds(t0, BLOCK), :]
        r = r * lax.rsqrt(jnp.mean(r * r, axis=-1, keepdims=True) + EPS)
        g = g_ref[pl.ds(t0, BLOCK), :].astype(f32)
        o_ref[pl.ds(t0, BLOCK), :] = (g * jax.nn.sigmoid(g) * r).astype(o_ref.dtype)
        return carry

    lax.fori_loop(0, nc, fin, 0, unroll=4)


def _retention(pb, dec_pad):
    bsz, seq, _ = pb.shape
    kern = functools.partial(_retention_kernel, seq=seq)

    def col(base):
        return pl.BlockSpec((None, seq, LANES), lambda b, h: (b, 0, base + h))

    return pl.pallas_call(
        kern,
        grid=(bsz, B_HEADS),
        in_specs=[col(_PB_BQ), col(_PB_BK), col(_PB_BV), col(_PB_BG),
                  pl.BlockSpec((8, LANES), lambda b, h: (0, 0))],
        out_specs=pl.BlockSpec((None, seq, LANES), lambda b, h: (b, 0, h)),
        out_shape=jax.ShapeDtypeStruct((bsz, seq, B_HEADS * B_DIM), jnp.bfloat16),
        scratch_shapes=[pltpu.VMEM((seq, B_DIM), jnp.float32), pltpu.VMEM((seq, B_DIM), jnp.float32)],
        compiler_params=_cparams("parallel", "parallel"),
        name="retention",
    )(pb, pb, pb, pb, dec_pad)


_ROWS_PER_DIR = 6


def _mlstm_kernel(cq_ref, ck_ref, gt_ref, v_ref, og_ref, wq_ref, wk_ref, gb_ref, o_ref,
                  qc_ref, kt_ref, vx_ref, rows_ref, of_ref, ob_ref, *, seq):
    nc = seq // BLOCK
    f32 = jnp.float32
    bf16 = jnp.bfloat16
    rowi = lax.broadcasted_iota(jnp.int32, (BLOCK, LANES), 0)
    coli = lax.broadcasted_iota(jnp.int32, (BLOCK, LANES), 1)
    lane_n = lax.broadcasted_iota(jnp.int32, (nc, LANES), 1)
    tril = rowi >= coli
    triu = rowi <= coli
    eye = rowi == coli
    tril_b = tril.astype(bf16)
    triu_b = triu.astype(bf16)
    ones_b = jnp.ones((BLOCK, LANES), bf16)
    head = pl.program_id(1)

    def cummax_lanes(x, backward):
        sh = 1
        while sh < LANES:
            if backward:
                x = jnp.maximum(x, jnp.where(lane_n < LANES - sh, pltpu.roll(x, LANES - sh, 1), _NEG_INF))
            else:
                x = jnp.maximum(x, jnp.where(lane_n >= sh, pltpu.roll(x, sh, 1), _NEG_INF))
            sh *= 2
        return x

    for d, (op, last) in enumerate(((triu_b, LANES - 1), (tril_b, 0))):
        lf = _log_sigmoid(gt_ref[2 * d + 1] + gb_ref[2 * d + 1, head])
        hi, lo = _split_hi_lo(lf)
        b = _dot(hi, op) + _dot(lo, op)
        a = gt_ref[2 * d] + gb_ref[2 * d, head] - b
        bl = jnp.broadcast_to(b[:, last:last + 1], (nc, LANES))
        ml = jnp.broadcast_to(jnp.max(bl + a, axis=1, keepdims=True), (nc, LANES))
        for r, val in enumerate((a, cummax_lanes(a, d == 1), lf, jnp.exp(bl + a - ml), bl, ml)):
            rows_ref[d * _ROWS_PER_DIR + r] = val

    def conv_silu(src_ref, w, c, t0):
        x = src_ref[pl.ds(t0, BLOCK), :]
        prev = src_ref[pl.ds(jnp.maximum(t0 - 1, 0), 1), :] * jnp.where(c > 0, 1.0, 0.0)
        nxt = src_ref[pl.ds(jnp.minimum(t0 + BLOCK, seq - 1), 1), :] * jnp.where(c < nc - 1, 1.0, 0.0)
        xd = jnp.where(rowi == 0, prev, pltpu.roll(x, 1, 0))
        xu = jnp.where(rowi == BLOCK - 1, nxt, pltpu.roll(x, BLOCK - 1, 0))
        y = xd * w[0:1] + x * w[1:2] + xu * w[2:3]
        return y * jax.nn.sigmoid(y)

    def prep(c, carry):
        t0 = pl.multiple_of(c * BLOCK, BLOCK)
        qc_ref[pl.ds(t0, BLOCK), :] = (conv_silu(cq_ref, wq_ref[...], c, t0) * (C_DIM ** -0.5)).astype(bf16)
        kt_ref[c] = conv_silu(ck_ref, wk_ref[...], c, t0).T.astype(bf16)
        vx_ref[pl.ds(t0, BLOCK), 0:C_DIM] = v_ref[pl.ds(t0, BLOCK), :]
        vx_ref[pl.ds(t0, BLOCK), C_DIM:2 * C_DIM] = ones_b
        return carry

    lax.fori_loop(0, nc, prep, 0, unroll=2)

    def load_rows(ch, backward):
        t0 = pl.multiple_of(ch * BLOCK, BLOCK)
        base = _ROWS_PER_DIR if backward else 0
        a, c, lf, w, bl, ml = (rows_ref[base + r, pl.ds(ch, 1), :] for r in range(_ROWS_PER_DIR))
        mask = triu if backward else tril
        kt = kt_ref[ch]
        return dict(t0=t0, backward=backward, bl=bl[:, 0:1], m_loc=ml[:, 0:1],
                    a_t=jnp.where(mask, a, _NEG_INF),
                    lf_hl=_split_hi_lo(jnp.where(mask, lf, 0.0)),
                    c_hl=_split_hi_lo(jnp.where(eye, c, 0.0)),
                    kw=(kt.astype(f32) * w).astype(bf16), kt=kt)

    def first_products(g):
        vx = vx_ref[pl.ds(g["t0"], BLOCK), :]
        g["qk"] = _dot(qc_ref[pl.ds(g["t0"], BLOCK), :], g["kt"])
        g["b_full"] = _dot(g["lf_hl"][0], ones_b) + _dot(g["lf_hl"][1], ones_b)
        g["c_full"] = _dot(g["c_hl"][0], ones_b) + _dot(g["c_hl"][1], ones_b)
        g["kv"] = _dot(g["kw"], vx)

    def decayed_scores(g):
        g["s0"] = (g["qk"] * jnp.exp(g["a_t"] - g["c_full"])).astype(bf16)

    def inner_products(g):
        g["p0"] = _dot(g["s0"], vx_ref[pl.ds(g["t0"], BLOCK), :])

    def with_state(g, state, out_ref):
        cx, m = state
        x = _dot(qc_ref[pl.ds(g["t0"], BLOCK), :], cx.astype(bf16))
        n = jnp.maximum(g["c_full"], m)
        e = jnp.exp(g["c_full"] - n)
        iw = jnp.exp(m - n)
        num = e * g["p0"][:, 0:C_DIM] + iw * x[:, 0:C_DIM]
        den = e * g["p0"][:, C_DIM:] + iw * x[:, C_DIM:]
        out_ref[pl.ds(g["t0"], BLOCK), :] = num / jnp.maximum(jnp.abs(den), jnp.exp(-(g["b_full"] + n)))
        m_new = jnp.maximum(g["bl"] + m, g["m_loc"])
        cw = jnp.exp(g["bl"] + m - m_new)
        f = jnp.exp(g["m_loc"] - m_new)
        return cw * cx + f * g["kv"], m_new

    def body(it, carry):
        sf, sb = carry
        steps = []
        for u in range(_SCAN_UNROLL):
            c = it * _SCAN_UNROLL + u
            steps.append(load_rows(c, False))
            steps.append(load_rows(nc - 1 - c, True))
        for g in steps:
            first_products(g)
        for g in steps:
            decayed_scores(g)
        for g in steps:
            inner_products(g)
        for g in steps:
            if g["backward"]:
                sb = with_state(g, sb, ob_ref)
            else:
                sf = with_state(g, sf, of_ref)
        return sf, sb

    init = (jnp.zeros((C_DIM, 2 * C_DIM), f32), jnp.zeros((1, 1), f32))
    lax.fori_loop(0, nc // _SCAN_UNROLL, body, (init, init))

    def fin(c, carry):
        t0 = pl.multiple_of(c * BLOCK, BLOCK)
        r = of_ref[pl.ds(t0, BLOCK), :] + ob_ref[pl.ds(t0, BLOCK), :]
        r = r * lax.rsqrt(jnp.mean(r * r, axis=-1, keepdims=True) + EPS)
        og = og_ref[pl.ds(t0, BLOCK), :].astype(f32)
        o_ref[pl.ds(t0, BLOCK), :] = (jax.nn.sigmoid(og) * r).astype(o_ref.dtype)
        return carry

    lax.fori_loop(0, nc, fin, 0, unroll=4)


def _mlstm(pb, pf, gates_t, wq_conv, wk_conv, gate_bias):
    bsz, seq, _ = pb.shape
    assert seq % (8 * BLOCK) == 0
    kern = functools.partial(_mlstm_kernel, seq=seq)

    def col(base, buffers=2):
        return pl.BlockSpec((None, seq, LANES), lambda b, h: (b, 0, base + h),
                            pipeline_mode=pl.Buffered(buffers))

    return pl.pallas_call(
        kern,
        grid=(bsz, C_HEADS),
        in_specs=[col(_PF_CQ, 1), col(_PF_CK, 1),
                  pl.BlockSpec((None, None, 4, seq // BLOCK, LANES), lambda b, h: (b, h, 0, 0, 0)),
                  col(_PB_CV), col(_PB_CO),
                  pl.BlockSpec((3, LANES), lambda b, h: (0, h)),
                  pl.BlockSpec((3, LANES), lambda b, h: (0, h)),
                  pl.BlockSpec(memory_space=pltpu.SMEM)],
        out_specs=pl.BlockSpec((None, seq, LANES), lambda b, h: (b, 0, h)),
        out_shape=jax.ShapeDtypeStruct((bsz, seq, C_HEADS * C_DIM), jnp.bfloat16),
        scratch_shapes=[pltpu.VMEM((seq, C_DIM), jnp.bfloat16),
                        pltpu.VMEM((seq // BLOCK, C_DIM, BLOCK), jnp.bfloat16),
                        pltpu.VMEM((seq, 2 * C_DIM), jnp.bfloat16),
                        pltpu.VMEM((2 * _ROWS_PER_DIR, seq // BLOCK, LANES), jnp.float32),
                        pltpu.VMEM((seq, C_DIM), jnp.float32), pltpu.VMEM((seq, C_DIM), jnp.float32)],
        compiler_params=_cparams("parallel", "parallel"),
        name="mlstm",
    )(pf, pf, gates_t, pb, pb, wq_conv, wk_conv, gate_bias)


def _window_attn_kernel(slopes_ref, sink_ref, q_ref, k_ref, v_ref, o_ref, kx_ref, vx_ref, *, seq):
    qi = pl.program_id(1)
    f32 = jnp.float32
    span = BLOCK + 2 * WINDOW
    group = D_HEADS // D_KV_HEADS

    @pl.when(qi == 0)
    def _():
        lane = lax.broadcasted_iota(jnp.int32, (seq, LANES), 1)
        for src, dst in ((k_ref, kx_ref), (v_ref, vx_ref)):
            x = src[...].astype(f32)
            x0 = jnp.where(lane < D_DIM, x, 0.0)
            x1 = jnp.where(lane >= D_DIM, x, 0.0)
            dst[0] = x0.astype(dst.dtype)
            dst[1] = pltpu.roll(x0, D_DIM, 1).astype(dst.dtype)
            dst[2] = pltpu.roll(x1, D_DIM, 1).astype(dst.dtype)
            dst[3] = x1.astype(dst.dtype)

    t0 = qi * BLOCK
    ws = pl.multiple_of(jnp.clip(t0 - WINDOW, 0, seq - span), BLOCK)
    rel = (lax.broadcasted_iota(jnp.int32, (BLOCK, span), 0)
           - lax.broadcasted_iota(jnp.int32, (BLOCK, span), 1)) + (t0 - ws)
    dist = jnp.abs(rel)
    valid = dist <= WINDOW
    distf = dist.astype(f32)
    scale = jnp.asarray(D_DIM ** -0.5, q_ref.dtype)

    heads = [(pair, half, (2 * pair) // group) for pair in range(D_HEADS // 2) for half in range(2)]
    scores = [_dot_nt(q_ref[:, pair * LANES:(pair + 1) * LANES] * scale,
                      kx_ref[2 * kv + half, pl.ds(ws, span), :]) for pair, half, kv in heads]
    probs = []
    for (pair, half, kv), s in zip(heads, scores):
        hd = 2 * pair + half
        s = jnp.where(valid, s - slopes_ref[hd] * distf, _NEG_INF)
        snk = sink_ref[hd]
        m = jnp.maximum(jnp.max(s, axis=1, keepdims=True), snk)
        p = jnp.exp(s - m)
        denom = jnp.sum(p, axis=1, keepdims=True) + jnp.exp(snk - m)
        probs.append((p / denom).astype(jnp.bfloat16))
    for pair in range(D_HEADS // 2):
        kv = (2 * pair) // group
        out = (_dot(probs[2 * pair], vx_ref[2 * kv, pl.ds(ws, span), :])
               + _dot(probs[2 * pair + 1], vx_ref[2 * kv + 1, pl.ds(ws, span), :]))
        o_ref[:, pair * LANES:(pair + 1) * LANES] = out.astype(o_ref.dtype)


def _window_attention(pb, attn_sink, slopes_d):
    bsz, seq, _ = pb.shape
    kern = functools.partial(_window_attn_kernel, seq=seq)
    width = D_HEADS * D_DIM
    return pl.pallas_call(
        kern,
        grid=(bsz, seq // BLOCK),
        in_specs=[
            pl.BlockSpec(memory_space=pltpu.SMEM),
            pl.BlockSpec(memory_space=pltpu.SMEM),
            pl.BlockSpec((None, BLOCK, width), lambda b, i: (b, i, _PB_DQ512)),
            pl.BlockSpec((None, seq, LANES), lambda b, i: (b, 0, _PB_DK)),
            pl.BlockSpec((None, seq, LANES), lambda b, i: (b, 0, _PB_DV)),
        ],
        out_specs=pl.BlockSpec((None, BLOCK, width), lambda b, i: (b, i, 0)),
        out_shape=jax.ShapeDtypeStruct((bsz, seq, width), jnp.bfloat16),
        scratch_shapes=[pltpu.VMEM((4, seq, LANES), jnp.bfloat16), pltpu.VMEM((4, seq, LANES), jnp.bfloat16)],
        compiler_params=_cparams("parallel", "arbitrary"),
        name="window_attention",
    )(slopes_d, attn_sink, pb, pb, pb)


def _out_proj_kernel(x_ref, ya_ref, yb_ref, yc_ref, yd_ref, w_ref, o_ref):
    acc = x_ref[...]
    for g, y_ref in enumerate((ya_ref, yb_ref, yc_ref, yd_ref)):
        acc = acc + _dot(y_ref[...], w_ref[g * 512:(g + 1) * 512, :])
    o_ref[...] = acc


def _out_proj(x2d, ys, w_out, tm):
    t, d = x2d.shape
    yspec = pl.BlockSpec((tm, 512), lambda i: (i, 0))
    return pl.pallas_call(
        _out_proj_kernel,
        grid=(t // tm,),
        in_specs=[pl.BlockSpec((tm, d), lambda i: (i, 0)), yspec, yspec, yspec, yspec,
                  pl.BlockSpec((d, d), lambda i: (0, 0))],
        out_specs=pl.BlockSpec((tm, d), lambda i: (i, 0)),
        out_shape=jax.ShapeDtypeStruct((t, d), jnp.float32),
        compiler_params=_cparams("parallel"),
        name="out_proj",
    )(x2d, *ys, w_out)


def _ffn_kernel(x_ref, xp_ref, xn_ref, nw_ref, wg_ref, wu_ref, cw_ref, wd_ref, fw_ref, o_ref,
                h_ref, acc_ref, *, seq, tm, final_norm):
    i = pl.program_id(0)
    j = pl.program_id(1)
    f32 = jnp.float32

    def norm(x):
        return x * lax.rsqrt(jnp.mean(x * x, axis=-1, keepdims=True) + EPS) * nw_ref[...]

    @pl.when(j == 0)
    def _():
        h_ref[0:tm, :] = norm(x_ref[...]).astype(h_ref.dtype)
        has_prev = ((i * tm) % seq != 0).astype(f32)
        has_next = (((i + 1) * tm) % seq != 0).astype(f32)
        halo = jnp.concatenate([norm(xp_ref[...]) * has_prev, norm(xn_ref[...]) * has_next], axis=0)
        h_ref[tm:tm + 16, :] = halo.astype(h_ref.dtype)
        acc_ref[...] = jnp.zeros(acc_ref.shape, f32)

    gate = _dot(h_ref[...], wg_ref[...])
    gm = gate[0:tm]
    rowi = lax.broadcasted_iota(jnp.int32, gm.shape, 0)
    g_dn = jnp.where(rowi == 0, gate[tm + 7:tm + 8], pltpu.roll(gm, 1, 0))
    g_up = jnp.where(rowi == tm - 1, gate[tm + 8:tm + 9], pltpu.roll(gm, tm - 1, 0))
    cw = cw_ref[...]
    g = g_dn * cw[0:1] + gm * cw[1:2] + g_up * cw[2:3]
    up = _dot(h_ref[0:tm, :], wu_ref[...])
    act = (g * jax.nn.sigmoid(g) * up).astype(wd_ref.dtype)
    acc_ref[...] += _dot(act, wd_ref[...])

    @pl.when(j == pl.num_programs(1) - 1)
    def _():
        y = x_ref[...] + acc_ref[...]
        if final_norm:
            y = y * lax.rsqrt(jnp.mean(y * y, axis=-1, keepdims=True) + EPS) * fw_ref[...]
        o_ref[...] = y


def _ffn(x2d, seq, nw, wg, wu, cw, wd, fw, final_norm, tm, tn):
    t, d = x2d.shape
    nff = wg.shape[1]
    rb = tm // 8
    last8 = t // 8 - 1
    kern = functools.partial(_ffn_kernel, seq=seq, tm=tm, final_norm=final_norm)
    return pl.pallas_call(
        kern,
        grid=(t // tm, nff // tn),
        in_specs=[
            pl.BlockSpec((tm, d), lambda i, j: (i, 0)),
            pl.BlockSpec((8, d), lambda i, j: (jnp.maximum(i * rb - 1, 0), 0)),
            pl.BlockSpec((8, d), lambda i, j: (jnp.minimum((i + 1) * rb, last8), 0)),
            pl.BlockSpec((1, d), lambda i, j: (0, 0)),
            pl.BlockSpec((d, tn), lambda i, j: (0, j)),
            pl.BlockSpec((d, tn), lambda i, j: (0, j)),
            pl.BlockSpec((3, tn), lambda i, j: (0, j)),
            pl.BlockSpec((tn, d), lambda i, j: (j, 0)),
            pl.BlockSpec((1, d), lambda i, j: (0, 0)),
        ],
        out_specs=pl.BlockSpec((tm, d), lambda i, j: (i, 0)),
        out_shape=jax.ShapeDtypeStruct((t, d), jnp.float32),
        scratch_shapes=[pltpu.VMEM((tm + 16, d), jnp.bfloat16), pltpu.VMEM((tm, d), jnp.float32)],
        compiler_params=_cparams("parallel", "arbitrary"),
        name="ffn",
    )(x2d, x2d, x2d, nw, wg, wu, cw, wd, fw)


def _prepare_layer(l, norm1_w, w_in, qk_conv_w, diff_lambda, ret_decay_logit, mlstm_gate_bias, attn_sink,
                   w_out, norm2_w, w_gate, w_up, ffn_conv_w, w_down):
    bf16 = jnp.bfloat16
    f32 = jnp.float32
    wi = w_in[l]
    w_pb = jnp.concatenate([wi[:, _OFF_A:_OFF_CQ], wi[:, _OFF_CV:_OFF_GATES], wi[:, _OFF_D:N_IN]], axis=1)
    wgates = jnp.pad(wi[:, _OFF_GATES:_OFF_D], ((0, 0), (0, LANES - N_GATES)))
    w_pf = jnp.concatenate([wi[:, _OFF_CQ:_OFF_CV], wgates], axis=1)
    gbias = mlstm_gate_bias[l].astype(f32)
    dec = jnp.zeros((8, LANES), f32).at[0:2, 0:B_HEADS].set(ret_decay_logit[l].astype(f32))
    return dict(
        norm1=norm1_w[l].reshape(1, D_MODEL).astype(f32),
        w_pb=w_pb.astype(bf16), w_pf=w_pf.astype(bf16),
        wq_conv=qk_conv_w[l][:, :C_HEADS * C_DIM].astype(f32),
        wk_conv=qk_conv_w[l][:, C_HEADS * C_DIM:].astype(f32),
        gbias=gbias, dec=dec,
        diff_lambda=diff_lambda[l].astype(f32), sink=attn_sink[l].astype(f32),
        w_out=w_out[l].astype(bf16),
        norm2=norm2_w[l].reshape(1, D_MODEL).astype(f32),
        w_gate=w_gate[l].astype(bf16), w_up=w_up[l].astype(bf16),
        ffn_conv=ffn_conv_w[l].astype(f32), w_down=w_down[l].astype(bf16),
    )


def _tile(n, pref):
    t = min(pref, n)
    while n % t:
        t //= 2
    return t


def _layer(x2d, bsz, seq, layer_idx, p, slopes, final_w, final_norm):
    t = bsz * seq
    tm_in = _tile(t, 1024)
    pb, pf = _in_proj(x2d, p["norm1"], p["w_pb"], p["w_pf"], tm_in, _IN_PROJ_TN)
    pb = pb.reshape(bsz, seq, PB_WIDTH)
    pf = pf.reshape(bsz, seq, PF_WIDTH)
    gates_t = pf[:, :, _PF_GATES * LANES:_PF_GATES * LANES + N_GATES].reshape(bsz, seq, 4, C_HEADS)
    gates_t = jnp.transpose(gates_t, (0, 3, 2, 1)).reshape(bsz, C_HEADS, 4, seq // BLOCK, BLOCK)
    y_a = _diff_attention(pb, p["diff_lambda"], slopes[D_HEADS:], layer_idx, _tile(seq, 512),
                          _tile(seq // _ATTN_TILES_PER_BODY, 512))
    y_b = _retention(pb, p["dec"])
    y_c = _mlstm(pb, pf, gates_t, p["wq_conv"], p["wk_conv"], p["gbias"])
    y_d = _window_attention(pb, p["sink"], slopes[:D_HEADS])
    ys = [y.reshape(t, 512) for y in (y_a, y_b, y_c, y_d)]
    x1 = _out_proj(x2d, ys, p["w_out"], _tile(t, 512))
    return _ffn(x1, seq, p["norm2"], p["w_gate"], p["w_up"], p["ffn_conv"], p["w_down"], final_w,
                final_norm, _tile(seq, 512), 512)


def _trunk(x, layers, slopes, final_w):
    bsz, seq, d = x.shape
    x2d = x.reshape(bsz * seq, d)
    for l, p in enumerate(layers):
        x2d = _layer(x2d, bsz, seq, l, p, slopes, final_w, l == len(layers) - 1)
    return x2d.reshape(bsz, seq, d)


def kernel(x_prompt, x_sample, norm1_w, w_in, qk_conv_w, diff_lambda, ret_decay_logit, mlstm_gate_bias,
           attn_sink, w_out, norm2_w, w_gate, w_up, ffn_conv_w, w_down, final_norm_w):
    layers = [_prepare_layer(l, norm1_w, w_in, qk_conv_w, diff_lambda, ret_decay_logit, mlstm_gate_bias,
                             attn_sink, w_out, norm2_w, w_gate, w_up, ffn_conv_w, w_down)
              for l in range(DEPTH)]
    n = A_HEADS + D_HEADS
    slopes = 2.0 ** (-8.0 * jnp.arange(1, n + 1, dtype=jnp.float32) / n)
    final_w = final_norm_w.reshape(1, D_MODEL).astype(jnp.float32)
    return (_trunk(x_prompt, layers, slopes, final_w), _trunk(x_sample, layers, slopes, final_w))
```

```python
import functools
import math

import jax
import jax.numpy as jnp
from jax import lax
from jax.experimental import pallas as pl
from jax.experimental.pallas import tpu as pltpu

D_MODEL = 2048
DEPTH = 2
A_HEADS = 4
A_QK_DIM = 64
A_V_DIM = 128
B_HEADS = 4
B_DIM = 128
C_HEADS = 4
C_DIM = 128
D_HEADS = 8
D_KV_HEADS = 2
D_DIM = 64
WINDOW = 128
BLOCK = 128
D_FF = 5632
EPS = 1e-6
LANES = 128

_OFF_A = 0
_OFF_B = 1536
_OFF_CQ = 3584
_OFF_CV = 4608
_OFF_GATES = 5632
_OFF_D = 5648
N_IN = 6416

_PB_AQ, _PB_AK, _PB_AV = 0, 4, 8
_PB_BQ, _PB_BK, _PB_BV, _PB_BG = 12, 16, 20, 24
_PB_CV, _PB_CO = 28, 32
_PB_DQ512 = 9
_PB_DK, _PB_DV = 40, 41
PB_WIDTH = 42 * LANES
_IN_PROJ_TN = 768
_PF_CQ, _PF_CK, _PF_GATES = 0, 4, 8
PF_WIDTH = 9 * LANES
N_GATES = 16

_VMEM_LIMIT = 56 * 1024 * 1024
_SCAN_UNROLL = 2
_NEG_INF = float("-inf")


def _cparams(*sem):
    return pltpu.CompilerParams(dimension_semantics=sem, vmem_limit_bytes=_VMEM_LIMIT)


def _dot(a, b):
    return jnp.dot(a, b, preferred_element_type=jnp.float32)


def _dot_nt(a, b):
    return lax.dot_general(a, b, (((1,), (1,)), ((), ())), preferred_element_type=jnp.float32)


def _dot_tn(a, b):
    return lax.dot_general(a, b, (((0,), (0,)), ((), ())), preferred_element_type=jnp.float32)


def _log_sigmoid(x):
    return -(jnp.maximum(-x, 0.0) + jnp.log1p(jnp.exp(-jnp.abs(x))))


def _split_hi_lo(x):
    hi = x.astype(jnp.bfloat16)
    lo = (x - hi.astype(jnp.float32)).astype(jnp.bfloat16)
    return hi, lo


def _in_proj_kernel(x_ref, nw_ref, wb_ref, wf_ref, ob_ref, of_ref, h_ref, *, nb):
    j = pl.program_id(1)

    @pl.when(j == 0)
    def _():
        x = x_ref[...]
        ms = jnp.mean(x * x, axis=-1, keepdims=True)
        h_ref[...] = (x * lax.rsqrt(ms + EPS) * nw_ref[...]).astype(h_ref.dtype)

    @pl.when(j < nb)
    def _():
        ob_ref[...] = _dot(h_ref[...], wb_ref[...]).astype(ob_ref.dtype)

    @pl.when(j == nb)
    def _():
        of_ref[...] = _dot(h_ref[...], wf_ref[...])


def _in_proj(x2d, nw, w_pb, w_pf, tm, tn):
    t, d = x2d.shape
    nb = PB_WIDTH // tn
    return pl.pallas_call(
        functools.partial(_in_proj_kernel, nb=nb),
        grid=(t // tm, nb + 1),
        in_specs=[
            pl.BlockSpec((tm, d), lambda i, j: (i, 0)),
            pl.BlockSpec((1, d), lambda i, j: (0, 0)),
            pl.BlockSpec((d, tn), lambda i, j: (0, jnp.minimum(j, nb - 1))),
            pl.BlockSpec((d, PF_WIDTH), lambda i, j: (0, 0)),
        ],
        out_specs=[pl.BlockSpec((tm, tn), lambda i, j: (i, jnp.minimum(j, nb - 1))),
                   pl.BlockSpec((tm, PF_WIDTH), lambda i, j: (i, 0))],
        out_shape=[jax.ShapeDtypeStruct((t, PB_WIDTH), jnp.bfloat16),
                   jax.ShapeDtypeStruct((t, PF_WIDTH), jnp.float32)],
        scratch_shapes=[pltpu.VMEM((tm, d), jnp.bfloat16)],
        compiler_params=_cparams("parallel", "arbitrary"),
        name="norm_in_proj",
    )(x2d, nw, w_pb, w_pf)


_LOG2E = 1.4426950408889634
_N_BIAS_LANES = 3
_V_ROWS = A_V_DIM + 16
_ATTN_TILES_PER_BODY = 4


def _diff_attn_kernel(slopes_ref, q_ref, k_ref, v_ref, dl_ref, o_ref,
                      kx_ref, vt_ref, qt_ref, m_ref, acc_ref, s_ref, cmax_ref, *, seq, tq, tk, lam_init):
    head = pl.program_id(1)
    qi = pl.program_id(2)
    f32 = jnp.float32
    bf16 = jnp.bfloat16
    sigma = slopes_ref[head] * _LOG2E
    nk = seq // tk
    prep_rows = tk

    @pl.when(qi == 0)
    def _():
        lane = lax.broadcasted_iota(jnp.int32, (prep_rows, LANES), 1)
        row = lax.broadcasted_iota(jnp.int32, (prep_rows, LANES), 0)
        ones_row = jnp.where(lax.broadcasted_iota(jnp.int32, (_V_ROWS - A_V_DIM, tk), 0) == 0, 1.0, 0.0)

        def prep(c, carry):
            r0 = pl.multiple_of(c * prep_rows, prep_rows)
            kk = k_ref[pl.ds(r0, prep_rows), :].astype(f32)
            col = sigma * (row + r0).astype(f32)
            c1 = col.astype(bf16).astype(f32)
            c2 = (col - c1).astype(bf16).astype(f32)
            c3 = col - c1 - c2
            bias = jnp.where(lane == A_QK_DIM, c1, jnp.where(lane == A_QK_DIM + 1, c2,
                             jnp.where(lane == A_QK_DIM + 2, c3, 0.0)))
            kx_ref[0, pl.ds(r0, prep_rows), :] = jnp.where(lane < A_QK_DIM, kk, bias).astype(bf16)
            kx_ref[1, pl.ds(r0, prep_rows), :] = jnp.where(
                lane < A_QK_DIM, pltpu.roll(kk, A_QK_DIM, 1), bias).astype(bf16)
            vt_ref[c, 0:A_V_DIM, :] = v_ref[pl.ds(r0, prep_rows), :].astype(f32).T.astype(bf16)
            vt_ref[c, A_V_DIM:_V_ROWS, :] = ones_row.astype(bf16)
            return carry

        lax.fori_loop(0, nk, prep, 0)

    qt = (q_ref[...].astype(f32) * (A_QK_DIM ** -0.5 * _LOG2E)).T
    qrow = lax.broadcasted_iota(jnp.int32, (LANES, tq), 0)
    is_bias = (qrow >= A_QK_DIM) & (qrow < A_QK_DIM + _N_BIAS_LANES)
    for mp in range(2):
        qm = qt if mp == 0 else pltpu.roll(qt, A_QK_DIM, 0)
        for var, sign in enumerate((1.0, 0.0, -1.0)):
            qt_ref[3 * mp + var] = jnp.where(qrow < A_QK_DIM, qm, jnp.where(is_bias, sign, 0.0)).astype(bf16)

    q0 = qi * tq
    rho = sigma * (lax.broadcasted_iota(jnp.int32, (1, tq), 1) + q0).astype(f32)
    m_ref[...] = jnp.full(m_ref.shape, _NEG_INF, f32)
    acc_ref[...] = jnp.zeros(acc_ref.shape, f32)

    kd = q0 // tk

    def key_tile(j):
        jm = jnp.minimum(j, nk - 1) - 1
        return jnp.where(j == 0, kd, jnp.where(jm < kd, jm, jm + 1))

    def scores(j, slot, mp):
        kb = key_tile(j)
        var = jnp.where(kb < kd, 0, jnp.where(kb > kd, 2, 1))
        k_start = pl.multiple_of(kb * tk, tk)
        s = _dot(kx_ref[mp, pl.ds(k_start, tk), :], qt_ref[3 * mp + var])
        s_ref[slot, mp] = s
        cmax_ref[slot, mp] = jnp.max(s, axis=0, keepdims=True)

    def softmax_pv(j, slot, mp):
        kb = key_tile(j)
        shift = jnp.where(kb < kd, -1.0, jnp.where(kb > kd, 1.0, 0.0)) * rho
        m_old = m_ref[mp]
        m_new = jnp.maximum(m_old, cmax_ref[slot, mp] + shift)
        p = jnp.exp2(s_ref[slot, mp] - (m_new - shift))
        acc_ref[mp] = jnp.exp2(m_old - m_new) * acc_ref[mp] + _dot(vt_ref[kb], p.astype(bf16))
        m_ref[mp] = m_new

    rel = (lax.broadcasted_iota(jnp.int32, (tk, tq), 1) - lax.broadcasted_iota(jnp.int32, (tk, tq), 0)
           + (q0 - kd * tk)).astype(f32)
    diag_bias = sigma * jnp.abs(rel)
    for mp in range(2):
        scores(0, 0, mp)
        s = s_ref[0, mp] - diag_bias
        s_ref[0, mp] = s
        cmax_ref[0, mp] = jnp.max(s, axis=0, keepdims=True)

    def body(i, carry):
        for u in range(_ATTN_TILES_PER_BODY):
            j = _ATTN_TILES_PER_BODY * i + u
            for mp in range(2):
                scores(j + 1, (u + 1) % _ATTN_TILES_PER_BODY, mp)
                softmax_pv(j, u, mp)
        return carry

    lax.fori_loop(0, nk // _ATTN_TILES_PER_BODY, body, 0)

    dl = dl_ref[...]
    lam = (jnp.exp(jnp.sum(dl[0:1] * dl[1:2], axis=1, keepdims=True))
           - jnp.exp(jnp.sum(dl[2:3] * dl[3:4], axis=1, keepdims=True)) + lam_init)
    a0 = acc_ref[0]
    a1 = acc_ref[1]
    ot = (a0[0:A_V_DIM] / a0[A_V_DIM:A_V_DIM + 1] - lam * (a1[0:A_V_DIM] / a1[A_V_DIM:A_V_DIM + 1]))
    ot = ot * lax.rsqrt(jnp.mean(ot * ot, axis=0, keepdims=True) + EPS)
    o_ref[...] = (ot * (1.0 - lam_init)).T.astype(o_ref.dtype)


def _diff_attention(pb, diff_lambda, slopes_a, layer_idx, tq, tk):
    bsz, seq, _ = pb.shape
    assert seq % (_ATTN_TILES_PER_BODY * tk) == 0 and tk % tq == 0
    lam_init = 0.8 - 0.6 * math.exp(-0.3 * layer_idx)
    kern = functools.partial(_diff_attn_kernel, seq=seq, tq=tq, tk=tk, lam_init=lam_init)
    return pl.pallas_call(
        kern,
        grid=(bsz, A_HEADS, seq // tq),
        in_specs=[
            pl.BlockSpec(memory_space=pltpu.SMEM),
            pl.BlockSpec((None, tq, LANES), lambda b, h, i: (b, i, _PB_AQ + h)),
            pl.BlockSpec((None, seq, LANES), lambda b, h, i: (b, 0, _PB_AK + h)),
            pl.BlockSpec((None, seq, LANES), lambda b, h, i: (b, 0, _PB_AV + h)),
            pl.BlockSpec((4, A_QK_DIM), lambda b, h, i: (0, 0)),
        ],
        out_specs=pl.BlockSpec((None, tq, LANES), lambda b, h, i: (b, i, h)),
        out_shape=jax.ShapeDtypeStruct((bsz, seq, A_HEADS * A_V_DIM), jnp.bfloat16),
        scratch_shapes=[
            pltpu.VMEM((2, seq, LANES), jnp.bfloat16),
            pltpu.VMEM((seq // tk, _V_ROWS, tk), jnp.bfloat16),
            pltpu.VMEM((6, LANES, tq), jnp.bfloat16),
            pltpu.VMEM((2, 1, tq), jnp.float32),
            pltpu.VMEM((2, _V_ROWS, tq), jnp.float32),
            pltpu.VMEM((_ATTN_TILES_PER_BODY, 2, tk, tq), jnp.float32),
            pltpu.VMEM((_ATTN_TILES_PER_BODY, 2, 1, tq), jnp.float32),
        ],
        compiler_params=_cparams("parallel", "parallel", "arbitrary"),
        name="diff_attention",
    )(slopes_a, pb, pb, pb, diff_lambda)


def _retention_kernel(q_ref, k_ref, v_ref, g_ref, dec_ref, o_ref, of_ref, ob_ref, *, seq):
    head = pl.program_id(1)
    nc = seq // BLOCK
    f32 = jnp.float32

    lgv = _log_sigmoid(dec_ref[...])
    rr = lax.broadcasted_iota(jnp.int32, lgv.shape, 0)
    cc = lax.broadcasted_iota(jnp.int32, lgv.shape, 1)

    def pick(row):
        sel = jnp.where((rr == row) & (cc == head), lgv, 0.0)
        return jnp.sum(jnp.sum(sel, axis=1, keepdims=True), axis=0, keepdims=True)

    lg_f, lg_b = pick(0), pick(1)
    ri = lax.broadcasted_iota(jnp.int32, (BLOCK, BLOCK), 0).astype(f32)
    ci = lax.broadcasted_iota(jnp.int32, (BLOCK, BLOCK), 1).astype(f32)
    diff = ri - ci
    decay_f = jnp.where(diff >= 0, jnp.exp(lg_f * jnp.maximum(diff, 0.0)), 0.0)
    decay_b = jnp.where(diff <= 0, jnp.exp(lg_b * jnp.maximum(-diff, 0.0)), 0.0)
    idx = lax.broadcasted_iota(jnp.int32, (BLOCK, 1), 0).astype(f32)
    qdec_f = jnp.exp(lg_f * (idx + 1.0))
    kdec_f = jnp.exp(lg_f * (BLOCK - 1.0 - idx))
    qdec_b = jnp.exp(lg_b * (BLOCK - idx))
    kdec_b = jnp.exp(lg_b * idx)
    cdec_f = jnp.exp(lg_f * BLOCK)
    cdec_b = jnp.exp(lg_b * BLOCK)
    kscale = B_DIM ** -0.5

    def scores_and_kv(t0, backward):
        qb = q_ref[pl.ds(t0, BLOCK), :]
        kb = k_ref[pl.ds(t0, BLOCK), :].astype(f32) * kscale
        vb = v_ref[pl.ds(t0, BLOCK), :]
        kdec = kdec_b if backward else kdec_f
        return dict(t0=t0, backward=backward, qb=qb, vb=vb,
                    qk=_dot_nt(qb, kb.astype(qb.dtype)), kv=_dot_tn((kb * kdec).astype(vb.dtype), vb))

    def inner_product(g):
        s = g["qk"] * (decay_b if g["backward"] else decay_f)
        g["inner"] = _dot(s.astype(g["vb"].dtype), g["vb"])

    def with_state(g, state, out_ref):
        qdec, cdec = (qdec_b, cdec_b) if g["backward"] else (qdec_f, cdec_f)
        cross = _dot(g["qb"], state.astype(g["qb"].dtype)) * qdec
        out_ref[pl.ds(g["t0"], BLOCK), :] = g["inner"] + cross
        return state * cdec + g["kv"]

    def body(it, carry):
        rf, rb = carry
        steps = []
        for u in range(_SCAN_UNROLL):
            c = it * _SCAN_UNROLL + u
            steps.append(scores_and_kv(pl.multiple_of(c * BLOCK, BLOCK), False))
            steps.append(scores_and_kv(pl.multiple_of((nc - 1 - c) * BLOCK, BLOCK), True))
        for g in steps:
            inner_product(g)
        for g in steps:
            if g["backward"]:
                rb = with_state(g, rb, ob_ref)
            else:
                rf = with_state(g, rf, of_ref)
        return rf, rb

    zero = jnp.zeros((B_DIM, B_DIM), f32)
    lax.fori_loop(0, nc // _SCAN_UNROLL, body, (zero, zero))

    def fin(c, carry):
        t0 = pl.multiple_of(c * BLOCK, BLOCK)
        r = of_ref[pl.ds(t0, BLOCK), :] + ob_ref[pl.ds(t0, BLOCK), :]
        r = r * lax.rsqrt(jnp.mean(r * r, axis=-1, keepdims=True) + EPS)
        g = g_ref[pl.ds(t0, BLOCK), :].astype(f32)
        o_ref[pl.ds(t0, BLOCK), :] = (g * jax.nn.sigmoid(g) * r).astype(o_ref.dtype)
        return carry

    lax.fori_loop(0, nc, fin, 0, unroll=4)


def _retention(pb, dec_pad):
    bsz, seq, _ = pb.shape
    kern = functools.partial(_retention_kernel, seq=seq)

    def col(base):
        return pl.BlockSpec((None, seq, LANES), lambda b, h: (b, 0, base + h))

    return pl.pallas_call(
        kern,
        grid=(bsz, B_HEADS),
        in_specs=[col(_PB_BQ), col(_PB_BK), col(_PB_BV), col(_PB_BG),
                  pl.BlockSpec((8, LANES), lambda b, h: (0, 0))],
        out_specs=pl.BlockSpec((None, seq, LANES), lambda b, h: (b, 0, h)),
        out_shape=jax.ShapeDtypeStruct((bsz, seq, B_HEADS * B_DIM), jnp.bfloat16),
        scratch_shapes=[pltpu.VMEM((seq, B_DIM), jnp.float32), pltpu.VMEM((seq, B_DIM), jnp.float32)],
        compiler_params=_cparams("parallel", "parallel"),
        name="retention",
    )(pb, pb, pb, pb, dec_pad)


_ROWS_PER_DIR = 6


def _mlstm_kernel(cq_ref, ck_ref, gt_ref, v_ref, og_ref, wq_ref, wk_ref, gb_ref, o_ref,
                  qc_ref, kt_ref, vx_ref, rows_ref, of_ref, ob_ref, *, seq):
    nc = seq // BLOCK
    f32 = jnp.float32
    bf16 = jnp.bfloat16
    rowi = lax.broadcasted_iota(jnp.int32, (BLOCK, LANES), 0)
    coli = lax.broadcasted_iota(jnp.int32, (BLOCK, LANES), 1)
    lane_n = lax.broadcasted_iota(jnp.int32, (nc, LANES), 1)
    tril = rowi >= coli
    triu = rowi <= coli
    tril_b = tril.astype(bf16)
    triu_b = triu.astype(bf16)
    ones_b = jnp.ones((BLOCK, LANES), bf16)
    head = pl.program_id(1)

    def cummax_lanes(x, backward):
        sh = 1
        while sh < LANES:
            if backward:
                x = jnp.maximum(x, jnp.where(lane_n < LANES - sh, pltpu.roll(x, LANES - sh, 1), _NEG_INF))
            else:
                x = jnp.maximum(x, jnp.where(lane_n >= sh, pltpu.roll(x, sh, 1), _NEG_INF))
            sh *= 2
        return x

    for d, (op, last) in enumerate(((triu_b, LANES - 1), (tril_b, 0))):
        lf = _log_sigmoid(gt_ref[2 * d + 1] + gb_ref[2 * d + 1, head])
        hi, lo = _split_hi_lo(lf)
        b = _dot(hi, op) + _dot(lo, op)
        a = gt_ref[2 * d] + gb_ref[2 * d, head] - b
        bl = jnp.broadcast_to(b[:, last:last + 1], (nc, LANES))
        ml = jnp.broadcast_to(jnp.max(bl + a, axis=1, keepdims=True), (nc, LANES))
        for r, val in enumerate((a, cummax_lanes(a, d == 1), b, jnp.exp(bl + a - ml), bl, ml)):
            rows_ref[d * _ROWS_PER_DIR + r] = val

    def conv_silu(src_ref, w, c, t0):
        x = src_ref[pl.ds(t0, BLOCK), :]
        prev = src_ref[pl.ds(jnp.maximum(t0 - 1, 0), 1), :] * jnp.where(c > 0, 1.0, 0.0)
        nxt = src_ref[pl.ds(jnp.minimum(t0 + BLOCK, seq - 1), 1), :] * jnp.where(c < nc - 1, 1.0, 0.0)
        xd = jnp.where(rowi == 0, prev, pltpu.roll(x, 1, 0))
        xu = jnp.where(rowi == BLOCK - 1, nxt, pltpu.roll(x, BLOCK - 1, 0))
        y = xd * w[0:1] + x * w[1:2] + xu * w[2:3]
        return y * jax.nn.sigmoid(y)

    def prep(c, carry):
        t0 = pl.multiple_of(c * BLOCK, BLOCK)
        qc_ref[pl.ds(t0, BLOCK), :] = (conv_silu(cq_ref, wq_ref[...], c, t0) * (C_DIM ** -0.5)).astype(bf16)
        kt_ref[c] = conv_silu(ck_ref, wk_ref[...], c, t0).T.astype(bf16)
        vx_ref[pl.ds(t0, BLOCK), 0:C_DIM] = v_ref[pl.ds(t0, BLOCK), :]
        vx_ref[pl.ds(t0, BLOCK), C_DIM:2 * C_DIM] = ones_b
        return carry

    lax.fori_loop(0, nc, prep, 0, unroll=2)

    def load_rows(ch, backward):
        t0 = pl.multiple_of(ch * BLOCK, BLOCK)
        base = _ROWS_PER_DIR if backward else 0
        a, c, b, w, bl, ml = (rows_ref[base + r, pl.ds(ch, 1), :] for r in range(_ROWS_PER_DIR))
        kt = kt_ref[ch]
        return dict(t0=t0, backward=backward, bl=bl[:, 0:1], m_loc=ml[:, 0:1],
                    a_t=jnp.where(triu if backward else tril, a, _NEG_INF),
                    b_full=jnp.broadcast_to(b, (BLOCK, LANES)).T,
                    c_full=jnp.broadcast_to(c, (BLOCK, LANES)).T,
                    kw=(kt.astype(f32) * w).astype(bf16), kt=kt)

    def first_products(g):
        vx = vx_ref[pl.ds(g["t0"], BLOCK), :]
        g["qk"] = _dot(qc_ref[pl.ds(g["t0"], BLOCK), :], g["kt"])
        g["kv"] = _dot(g["kw"], vx)

    def decayed_scores(g):
        g["s0"] = (g["qk"] * jnp.exp(g["a_t"] - g["c_full"])).astype(bf16)

    def inner_products(g):
        g["p0"] = _dot(g["s0"], vx_ref[pl.ds(g["t0"], BLOCK), :])

    def with_state(g, state, out_ref):
        cx, m = state
        x = _dot(qc_ref[pl.ds(g["t0"], BLOCK), :], cx.astype(bf16))
        n = jnp.maximum(g["c_full"], m)
        e = jnp.exp(g["c_full"] - n)
        iw = jnp.exp(m - n)
        num = e * g["p0"][:, 0:C_DIM] + iw * x[:, 0:C_DIM]
        den = e * g["p0"][:, C_DIM:] + iw * x[:, C_DIM:]
        out_ref[pl.ds(g["t0"], BLOCK), :] = num / jnp.maximum(jnp.abs(den), jnp.exp(-(g["b_full"] + n)))
        m_new = jnp.maximum(g["bl"] + m, g["m_loc"])
        cw = jnp.exp(g["bl"] + m - m_new)
        f = jnp.exp(g["m_loc"] - m_new)
        return cw * cx + f * g["kv"], m_new

    def body(it, carry):
        sf, sb = carry
        steps = []
        for u in range(_SCAN_UNROLL):
            c = it * _SCAN_UNROLL + u
            steps.append(load_rows(c, False))
            steps.append(load_rows(nc - 1 - c, True))
        for g in steps:
            first_products(g)
        for g in steps:
            decayed_scores(g)
        for g in steps:
            inner_products(g)
        for g in steps:
            if g["backward"]:
                sb = with_state(g, sb, ob_ref)
            else:
                sf = with_state(g, sf, of_ref)
        return sf, sb

    init = (jnp.zeros((C_DIM, 2 * C_DIM), f32), jnp.zeros((1, 1), f32))
    lax.fori_loop(0, nc // _SCAN_UNROLL, body, (init, init))

    def fin(c, carry):
        t0 = pl.multiple_of(c * BLOCK, BLOCK)
        r = of_ref[pl.ds(t0, BLOCK), :] + ob_ref[pl.ds(t0, BLOCK), :]
        r = r * lax.rsqrt(jnp.mean(r * r, axis=-1, keepdims=True) + EPS)
        og = og_ref[pl.ds(t0, BLOCK), :].astype(f32)
        o_ref[pl.ds(t0, BLOCK), :] = (jax.nn.sigmoid(og) * r).astype(o_ref.dtype)
        return carry

    lax.fori_loop(0, nc, fin, 0, unroll=4)


def _mlstm(pb, pf, gates_t, wq_conv, wk_conv, gate_bias):
    bsz, seq, _ = pb.shape
    assert seq % (8 * BLOCK) == 0
    kern = functools.partial(_mlstm_kernel, seq=seq)

    def col(base, buffers=2):
        return pl.BlockSpec((None, seq, LANES), lambda b, h: (b, 0, base + h),
                            pipeline_mode=pl.Buffered(buffers))

    return pl.pallas_call(
        kern,
        grid=(bsz, C_HEADS),
        in_specs=[col(_PF_CQ, 1), col(_PF_CK, 1),
                  pl.BlockSpec((None, None, 4, seq // BLOCK, LANES), lambda b, h: (b, h, 0, 0, 0)),
                  col(_PB_CV), col(_PB_CO),
                  pl.BlockSpec((3, LANES), lambda b, h: (0, h)),
                  pl.BlockSpec((3, LANES), lambda b, h: (0, h)),
                  pl.BlockSpec(memory_space=pltpu.SMEM)],
        out_specs=pl.BlockSpec((None, seq, LANES), lambda b, h: (b, 0, h)),
        out_shape=jax.ShapeDtypeStruct((bsz, seq, C_HEADS * C_DIM), jnp.bfloat16),
        scratch_shapes=[pltpu.VMEM((seq, C_DIM), jnp.bfloat16),
                        pltpu.VMEM((seq // BLOCK, C_DIM, BLOCK), jnp.bfloat16),
                        pltpu.VMEM((seq, 2 * C_DIM), jnp.bfloat16),
                        pltpu.VMEM((2 * _ROWS_PER_DIR, seq // BLOCK, LANES), jnp.float32),
                        pltpu.VMEM((seq, C_DIM), jnp.float32), pltpu.VMEM((seq, C_DIM), jnp.float32)],
        compiler_params=_cparams("parallel", "parallel"),
        name="mlstm",
    )(pf, pf, gates_t, pb, pb, wq_conv, wk_conv, gate_bias)


def _window_attn_kernel(slopes_ref, sink_ref, q_ref, k_ref, v_ref, o_ref, kx_ref, vx_ref, *, seq):
    qi = pl.program_id(1)
    f32 = jnp.float32
    span = BLOCK + 2 * WINDOW
    group = D_HEADS // D_KV_HEADS

    @pl.when(qi == 0)
    def _():
        lane = lax.broadcasted_iota(jnp.int32, (seq, LANES), 1)
        for src, dst in ((k_ref, kx_ref), (v_ref, vx_ref)):
            x = src[...].astype(f32)
            x0 = jnp.where(lane < D_DIM, x, 0.0)
            x1 = jnp.where(lane >= D_DIM, x, 0.0)
            dst[0] = x0.astype(dst.dtype)
            dst[1] = pltpu.roll(x0, D_DIM, 1).astype(dst.dtype)
            dst[2] = pltpu.roll(x1, D_DIM, 1).astype(dst.dtype)
            dst[3] = x1.astype(dst.dtype)

    t0 = qi * BLOCK
    ws = pl.multiple_of(jnp.clip(t0 - WINDOW, 0, seq - span), BLOCK)
    rel = (lax.broadcasted_iota(jnp.int32, (BLOCK, span), 0)
           - lax.broadcasted_iota(jnp.int32, (BLOCK, span), 1)) + (t0 - ws)
    dist = jnp.abs(rel)
    valid = dist <= WINDOW
    distf = dist.astype(f32)
    scale = jnp.asarray(D_DIM ** -0.5, q_ref.dtype)

    heads = [(pair, half, (2 * pair) // group) for pair in range(D_HEADS // 2) for half in range(2)]
    scores = [_dot_nt(q_ref[:, pair * LANES:(pair + 1) * LANES] * scale,
                      kx_ref[2 * kv + half, pl.ds(ws, span), :]) for pair, half, kv in heads]
    probs = []
    for (pair, half, kv), s in zip(heads, scores):
        hd = 2 * pair + half
        s = jnp.where(valid, s - slopes_ref[hd] * distf, _NEG_INF)
        snk = sink_ref[hd]
        m = jnp.maximum(jnp.max(s, axis=1, keepdims=True), snk)
        p = jnp.exp(s - m)
        denom = jnp.sum(p, axis=1, keepdims=True) + jnp.exp(snk - m)
        probs.append((p / denom).astype(jnp.bfloat16))
    for pair in range(D_HEADS // 2):
        kv = (2 * pair) // group
        out = (_dot(probs[2 * pair], vx_ref[2 * kv, pl.ds(ws, span), :])
               + _dot(probs[2 * pair + 1], vx_ref[2 * kv + 1, pl.ds(ws, span), :]))
        o_ref[:, pair * LANES:(pair + 1) * LANES] = out.astype(o_ref.dtype)


def _window_attention(pb, attn_sink, slopes_d):
    bsz, seq, _ = pb.shape
    kern = functools.partial(_window_attn_kernel, seq=seq)
    width = D_HEADS * D_DIM
    return pl.pallas_call(
        kern,
        grid=(bsz, seq // BLOCK),
        in_specs=[
            pl.BlockSpec(memory_space=pltpu.SMEM),
            pl.BlockSpec(memory_space=pltpu.SMEM),
            pl.BlockSpec((None, BLOCK, width), lambda b, i: (b, i, _PB_DQ512)),
            pl.BlockSpec((None, seq, LANES), lambda b, i: (b, 0, _PB_DK)),
            pl.BlockSpec((None, seq, LANES), lambda b, i: (b, 0, _PB_DV)),
        ],
        out_specs=pl.BlockSpec((None, BLOCK, width), lambda b, i: (b, i, 0)),
        out_shape=jax.ShapeDtypeStruct((bsz, seq, width), jnp.bfloat16),
        scratch_shapes=[pltpu.VMEM((4, seq, LANES), jnp.bfloat16), pltpu.VMEM((4, seq, LANES), jnp.bfloat16)],
        compiler_params=_cparams("parallel", "arbitrary"),
        name="window_attention",
    )(slopes_d, attn_sink, pb, pb, pb)


def _out_proj_kernel(x_ref, ya_ref, yb_ref, yc_ref, yd_ref, w_ref, o_ref):
    acc = x_ref[...]
    for g, y_ref in enumerate((ya_ref, yb_ref, yc_ref, yd_ref)):
        acc = acc + _dot(y_ref[...], w_ref[g * 512:(g + 1) * 512, :])
    o_ref[...] = acc


def _out_proj(x2d, ys, w_out, tm):
    t, d = x2d.shape
    yspec = pl.BlockSpec((tm, 512), lambda i: (i, 0))
    return pl.pallas_call(
        _out_proj_kernel,
        grid=(t // tm,),
        in_specs=[pl.BlockSpec((tm, d), lambda i: (i, 0)), yspec, yspec, yspec, yspec,
                  pl.BlockSpec((d, d), lambda i: (0, 0))],
        out_specs=pl.BlockSpec((tm, d), lambda i: (i, 0)),
        out_shape=jax.ShapeDtypeStruct((t, d), jnp.float32),
        compiler_params=_cparams("parallel"),
        name="out_proj",
    )(x2d, *ys, w_out)


def _ffn_kernel(x_ref, xp_ref, xn_ref, nw_ref, wg_ref, wu_ref, cw_ref, wd_ref, fw_ref, o_ref,
                h_ref, acc_ref, *, seq, tm, final_norm):
    i = pl.program_id(0)
    j = pl.program_id(1)
    f32 = jnp.float32

    def norm(x):
        return x * lax.rsqrt(jnp.mean(x * x, axis=-1, keepdims=True) + EPS) * nw_ref[...]

    @pl.when(j == 0)
    def _():
        h_ref[0:tm, :] = norm(x_ref[...]).astype(h_ref.dtype)
        has_prev = ((i * tm) % seq != 0).astype(f32)
        has_next = (((i + 1) * tm) % seq != 0).astype(f32)
        halo = jnp.concatenate([norm(xp_ref[...]) * has_prev, norm(xn_ref[...]) * has_next], axis=0)
        h_ref[tm:tm + 16, :] = halo.astype(h_ref.dtype)
        acc_ref[...] = jnp.zeros(acc_ref.shape, f32)

    gate = _dot(h_ref[...], wg_ref[...])
    gm = gate[0:tm]
    rowi = lax.broadcasted_iota(jnp.int32, gm.shape, 0)
    g_dn = jnp.where(rowi == 0, gate[tm + 7:tm + 8], pltpu.roll(gm, 1, 0))
    g_up = jnp.where(rowi == tm - 1, gate[tm + 8:tm + 9], pltpu.roll(gm, tm - 1, 0))
    cw = cw_ref[...]
    g = g_dn * cw[0:1] + gm * cw[1:2] + g_up * cw[2:3]
    up = _dot(h_ref[0:tm, :], wu_ref[...])
    act = (g * jax.nn.sigmoid(g) * up).astype(wd_ref.dtype)
    acc_ref[...] += _dot(act, wd_ref[...])

    @pl.when(j == pl.num_programs(1) - 1)
    def _():
        y = x_ref[...] + acc_ref[...]
        if final_norm:
            y = y * lax.rsqrt(jnp.mean(y * y, axis=-1, keepdims=True) + EPS) * fw_ref[...]
        o_ref[...] = y


def _ffn(x2d, seq, nw, wg, wu, cw, wd, fw, final_norm, tm, tn):
    t, d = x2d.shape
    nff = wg.shape[1]
    rb = tm // 8
    last8 = t // 8 - 1
    kern = functools.partial(_ffn_kernel, seq=seq, tm=tm, final_norm=final_norm)
    return pl.pallas_call(
        kern,
        grid=(t // tm, nff // tn),
        in_specs=[
            pl.BlockSpec((tm, d), lambda i, j: (i, 0)),
            pl.BlockSpec((8, d), lambda i, j: (jnp.maximum(i * rb - 1, 0), 0)),
            pl.BlockSpec((8, d), lambda i, j: (jnp.minimum((i + 1) * rb, last8), 0)),
            pl.BlockSpec((1, d), lambda i, j: (0, 0)),
            pl.BlockSpec((d, tn), lambda i, j: (0, j)),
            pl.BlockSpec((d, tn), lambda i, j: (0, j)),
            pl.BlockSpec((3, tn), lambda i, j: (0, j)),
            pl.BlockSpec((tn, d), lambda i, j: (j, 0)),
            pl.BlockSpec((1, d), lambda i, j: (0, 0)),
        ],
        out_specs=pl.BlockSpec((tm, d), lambda i, j: (i, 0)),
        out_shape=jax.ShapeDtypeStruct((t, d), jnp.float32),
        scratch_shapes=[pltpu.VMEM((tm + 16, d), jnp.bfloat16), pltpu.VMEM((tm, d), jnp.float32)],
        compiler_params=_cparams("parallel", "arbitrary"),
        name="ffn",
    )(x2d, x2d, x2d, nw, wg, wu, cw, wd, fw)


def _prepare_layer(l, norm1_w, w_in, qk_conv_w, diff_lambda, ret_decay_logit, mlstm_gate_bias, attn_sink,
                   w_out, norm2_w, w_gate, w_up, ffn_conv_w, w_down):
    bf16 = jnp.bfloat16
    f32 = jnp.float32
    wi = w_in[l]
    w_pb = jnp.concatenate([wi[:, _OFF_A:_OFF_CQ], wi[:, _OFF_CV:_OFF_GATES], wi[:, _OFF_D:N_IN]], axis=1)
    wgates = jnp.pad(wi[:, _OFF_GATES:_OFF_D], ((0, 0), (0, LANES - N_GATES)))
    w_pf = jnp.concatenate([wi[:, _OFF_CQ:_OFF_CV], wgates], axis=1)
    gbias = mlstm_gate_bias[l].astype(f32)
    dec = jnp.zeros((8, LANES), f32).at[0:2, 0:B_HEADS].set(ret_decay_logit[l].astype(f32))
    return dict(
        norm1=norm1_w[l].reshape(1, D_MODEL).astype(f32),
        w_pb=w_pb.astype(bf16), w_pf=w_pf.astype(bf16),
        wq_conv=qk_conv_w[l][:, :C_HEADS * C_DIM].astype(f32),
        wk_conv=qk_conv_w[l][:, C_HEADS * C_DIM:].astype(f32),
        gbias=gbias, dec=dec,
        diff_lambda=diff_lambda[l].astype(f32), sink=attn_sink[l].astype(f32),
        w_out=w_out[l].astype(bf16),
        norm2=norm2_w[l].reshape(1, D_MODEL).astype(f32),
        w_gate=w_gate[l].astype(bf16), w_up=w_up[l].astype(bf16),
        ffn_conv=ffn_conv_w[l].astype(f32), w_down=w_down[l].astype(bf16),
    )


def _tile(n, pref):
    t = min(pref, n)
    while n % t:
        t //= 2
    return t


def _layer(x2d, bsz, seq, layer_idx, p, slopes, final_w, final_norm):
    t = bsz * seq
    tm_in = _tile(t, 1024)
    pb, pf = _in_proj(x2d, p["norm1"], p["w_pb"], p["w_pf"], tm_in, _IN_PROJ_TN)
    pb = pb.reshape(bsz, seq, PB_WIDTH)
    pf = pf.reshape(bsz, seq, PF_WIDTH)
    gates_t = pf[:, :, _PF_GATES * LANES:_PF_GATES * LANES + N_GATES].reshape(bsz, seq, 4, C_HEADS)
    gates_t = jnp.transpose(gates_t, (0, 3, 2, 1)).reshape(bsz, C_HEADS, 4, seq // BLOCK, BLOCK)
    y_a = _diff_attention(pb, p["diff_lambda"], slopes[D_HEADS:], layer_idx, _tile(seq, 512),
                          _tile(seq // _ATTN_TILES_PER_BODY, 512))
    y_b = _retention(pb, p["dec"])
    y_c = _mlstm(pb, pf, gates_t, p["wq_conv"], p["wk_conv"], p["gbias"])
    y_d = _window_attention(pb, p["sink"], slopes[:D_HEADS])
    ys = [y.reshape(t, 512) for y in (y_a, y_b, y_c, y_d)]
    x1 = _out_proj(x2d, ys, p["w_out"], _tile(t, 512))
    return _ffn(x1, seq, p["norm2"], p["w_gate"], p["w_up"], p["ffn_conv"], p["w_down"], final_w,
                final_norm, _tile(seq, 512), 512)


def _trunk(x, layers, slopes, final_w):
    bsz, seq, d = x.shape
    x2d = x.reshape(bsz * seq, d)
    for l, p in enumerate(layers):
        x2d = _layer(x2d, bsz, seq, l, p, slopes, final_w, l == len(layers) - 1)
    return x2d.reshape(bsz, seq, d)


def kernel(x_prompt, x_sample, norm1_w, w_in, qk_conv_w, diff_lambda, ret_decay_logit, mlstm_gate_bias,
           attn_sink, w_out, norm2_w, w_gate, w_up, ffn_conv_w, w_down, final_norm_w):
    layers = [_prepare_layer(l, norm1_w, w_in, qk_conv_w, diff_lambda, ret_decay_logit, mlstm_gate_bias,
                             attn_sink, w_out, norm2_w, w_gate, w_up, ffn_conv_w, w_down)
              for l in range(DEPTH)]
    n = A_HEADS + D_HEADS
    slopes = 2.0 ** (-8.0 * jnp.arange(1, n + 1, dtype=jnp.float32) / n)
    final_w = final_norm_w.reshape(1, D_MODEL).astype(jnp.float32)
    return (_trunk(x_prompt, layers, slopes, final_w), _trunk(x_sample, layers, slopes, final_w))
```

```python
import functools
import math

import jax
import jax.numpy as jnp
from jax import lax
from jax.experimental import pallas as pl
from jax.experimental.pallas import tpu as pltpu

D_MODEL = 2048
DEPTH = 2
A_HEADS = 4
A_QK_DIM = 64
A_V_DIM = 128
B_HEADS = 4
B_DIM = 128
C_HEADS = 4
C_DIM = 128
D_HEADS = 8
D_KV_HEADS = 2
D_DIM = 64
WINDOW = 128
BLOCK = 128
D_FF = 5632
EPS = 1e-6
LANES = 128

_OFF_A = 0
_OFF_B = 1536
_OFF_CQ = 3584
_OFF_CV = 4608
_OFF_GATES = 5632
_OFF_D = 5648
N_IN = 6416

_PB_AQ, _PB_AK, _PB_AV = 0, 4, 8
_PB_BQ, _PB_BK, _PB_BV, _PB_BG = 12, 16, 20, 24
_PB_CV, _PB_CO = 28, 32
_PB_DQ512 = 9
_PB_DK, _PB_DV = 40, 41
PB_WIDTH = 42 * LANES
_IN_PROJ_TN = 768
_PF_CQ, _PF_CK, _PF_GATES = 0, 4, 8
PF_WIDTH = 9 * LANES
N_GATES = 16

_VMEM_LIMIT = 56 * 1024 * 1024
_SCAN_UNROLL = 4
_NEG_INF = float("-inf")


def _cparams(*sem):
    return pltpu.CompilerParams(dimension_semantics=sem, vmem_limit_bytes=_VMEM_LIMIT)


def _dot(a, b):
    return jnp.dot(a, b, preferred_element_type=jnp.float32)


def _dot_nt(a, b):
    return lax.dot_general(a, b, (((1,), (1,)), ((), ())), preferred_element_type=jnp.float32)


def _dot_tn(a, b):
    return lax.dot_general(a, b, (((0,), (0,)), ((), ())), preferred_element_type=jnp.float32)


def _log_sigmoid(x):
    return -(jnp.maximum(-x, 0.0) + jnp.log1p(jnp.exp(-jnp.abs(x))))


def _split_hi_lo(x):
    hi = x.astype(jnp.bfloat16)
    lo = (x - hi.astype(jnp.float32)).astype(jnp.bfloat16)
    return hi, lo


def _in_proj_kernel(x_ref, nw_ref, wb_ref, wf_ref, ob_ref, of_ref, h_ref, *, nb):
    j = pl.program_id(1)

    @pl.when(j == 0)
    def _():
        x = x_ref[...]
        ms = jnp.mean(x * x, axis=-1, keepdims=True)
        h_ref[...] = (x * lax.rsqrt(ms + EPS) * nw_ref[...]).astype(h_ref.dtype)

    @pl.when(j < nb)
    def _():
        ob_ref[...] = _dot(h_ref[...], wb_ref[...]).astype(ob_ref.dtype)

    @pl.when(j == nb)
    def _():
        of_ref[...] = _dot(h_ref[...], wf_ref[...])


def _in_proj(x2d, nw, w_pb, w_pf, tm, tn):
    t, d = x2d.shape
    nb = PB_WIDTH // tn
    return pl.pallas_call(
        functools.partial(_in_proj_kernel, nb=nb),
        grid=(t // tm, nb + 1),
        in_specs=[
            pl.BlockSpec((tm, d), lambda i, j: (i, 0)),
            pl.BlockSpec((1, d), lambda i, j: (0, 0)),
            pl.BlockSpec((d, tn), lambda i, j: (0, jnp.minimum(j, nb - 1))),
            pl.BlockSpec((d, PF_WIDTH), lambda i, j: (0, 0)),
        ],
        out_specs=[pl.BlockSpec((tm, tn), lambda i, j: (i, jnp.minimum(j, nb - 1))),
                   pl.BlockSpec((tm, PF_WIDTH), lambda i, j: (i, 0))],
        out_shape=[jax.ShapeDtypeStruct((t, PB_WIDTH), jnp.bfloat16),
                   jax.ShapeDtypeStruct((t, PF_WIDTH), jnp.float32)],
        scratch_shapes=[pltpu.VMEM((tm, d), jnp.bfloat16)],
        compiler_params=_cparams("parallel", "arbitrary"),
        name="norm_in_proj",
    )(x2d, nw, w_pb, w_pf)


_LOG2E = 1.4426950408889634
_N_BIAS_LANES = 3
_V_ROWS = A_V_DIM + 16
_ATTN_TILES_PER_BODY = 4


def _diff_attn_kernel(slopes_ref, q_ref, k_ref, v_ref, dl_ref, o_ref,
                      kx_ref, vt_ref, qt_ref, m_ref, acc_ref, s_ref, cmax_ref, *, seq, tq, tk, lam_init):
    head = pl.program_id(1)
    qi = pl.program_id(2)
    f32 = jnp.float32
    bf16 = jnp.bfloat16
    sigma = slopes_ref[head] * _LOG2E
    nk = seq // tk
    prep_rows = tk

    @pl.when(qi == 0)
    def _():
        lane = lax.broadcasted_iota(jnp.int32, (prep_rows, LANES), 1)
        row = lax.broadcasted_iota(jnp.int32, (prep_rows, LANES), 0)
        ones_row = jnp.where(lax.broadcasted_iota(jnp.int32, (_V_ROWS - A_V_DIM, tk), 0) == 0, 1.0, 0.0)

        def prep(c, carry):
            r0 = pl.multiple_of(c * prep_rows, prep_rows)
            kk = k_ref[pl.ds(r0, prep_rows), :].astype(f32)
            col = sigma * (row + r0).astype(f32)
            c1 = col.astype(bf16).astype(f32)
            c2 = (col - c1).astype(bf16).astype(f32)
            c3 = col - c1 - c2
            bias = jnp.where(lane == A_QK_DIM, c1, jnp.where(lane == A_QK_DIM + 1, c2,
                             jnp.where(lane == A_QK_DIM + 2, c3, 0.0)))
            kx_ref[0, pl.ds(r0, prep_rows), :] = jnp.where(lane < A_QK_DIM, kk, bias).astype(bf16)
            kx_ref[1, pl.ds(r0, prep_rows), :] = jnp.where(
                lane < A_QK_DIM, pltpu.roll(kk, A_QK_DIM, 1), bias).astype(bf16)
            vt_ref[c, 0:A_V_DIM, :] = v_ref[pl.ds(r0, prep_rows), :].astype(f32).T.astype(bf16)
            vt_ref[c, A_V_DIM:_V_ROWS, :] = ones_row.astype(bf16)
            return carry

        lax.fori_loop(0, nk, prep, 0)

    qt = (q_ref[...].astype(f32) * (A_QK_DIM ** -0.5 * _LOG2E)).T
    qrow = lax.broadcasted_iota(jnp.int32, (LANES, tq), 0)
    is_bias = (qrow >= A_QK_DIM) & (qrow < A_QK_DIM + _N_BIAS_LANES)
    for mp in range(2):
        qm = qt if mp == 0 else pltpu.roll(qt, A_QK_DIM, 0)
        for var, sign in enumerate((1.0, 0.0, -1.0)):
            qt_ref[3 * mp + var] = jnp.where(qrow < A_QK_DIM, qm, jnp.where(is_bias, sign, 0.0)).astype(bf16)

    q0 = qi * tq
    rho = sigma * (lax.broadcasted_iota(jnp.int32, (1, tq), 1) + q0).astype(f32)
    m_ref[...] = jnp.full(m_ref.shape, _NEG_INF, f32)
    acc_ref[...] = jnp.zeros(acc_ref.shape, f32)

    kd = q0 // tk

    def key_tile(j):
        jm = jnp.minimum(j, nk - 1) - 1
        return jnp.where(j == 0, kd, jnp.where(jm < kd, jm, jm + 1))

    def scores(j, slot, mp):
        kb = key_tile(j)
        var = jnp.where(kb < kd, 0, jnp.where(kb > kd, 2, 1))
        k_start = pl.multiple_of(kb * tk, tk)
        s = _dot(kx_ref[mp, pl.ds(k_start, tk), :], qt_ref[3 * mp + var])
        s_ref[slot, mp] = s
        cmax_ref[slot, mp] = jnp.max(s, axis=0, keepdims=True)

    def softmax_pv(j, slot, mp):
        kb = key_tile(j)
        shift = jnp.where(kb < kd, -1.0, jnp.where(kb > kd, 1.0, 0.0)) * rho
        m_old = m_ref[mp]
        m_new = jnp.maximum(m_old, cmax_ref[slot, mp] + shift)
        p = jnp.exp2(s_ref[slot, mp] - (m_new - shift))
        acc_ref[mp] = jnp.exp2(m_old - m_new) * acc_ref[mp] + _dot(vt_ref[kb], p.astype(bf16))
        m_ref[mp] = m_new

    rel = (lax.broadcasted_iota(jnp.int32, (tk, tq), 1) - lax.broadcasted_iota(jnp.int32, (tk, tq), 0)
           + (q0 - kd * tk)).astype(f32)
    diag_bias = sigma * jnp.abs(rel)
    for mp in range(2):
        scores(0, 0, mp)
        s = s_ref[0, mp] - diag_bias
        s_ref[0, mp] = s
        cmax_ref[0, mp] = jnp.max(s, axis=0, keepdims=True)

    def body(i, carry):
        for u in range(_ATTN_TILES_PER_BODY):
            j = _ATTN_TILES_PER_BODY * i + u
            for mp in range(2):
                scores(j + 1, (u + 1) % _ATTN_TILES_PER_BODY, mp)
                softmax_pv(j, u, mp)
        return carry

    lax.fori_loop(0, nk // _ATTN_TILES_PER_BODY, body, 0)

    dl = dl_ref[...]
    lam = (jnp.exp(jnp.sum(dl[0:1] * dl[1:2], axis=1, keepdims=True))
           - jnp.exp(jnp.sum(dl[2:3] * dl[3:4], axis=1, keepdims=True)) + lam_init)
    a0 = acc_ref[0]
    a1 = acc_ref[1]
    ot = (a0[0:A_V_DIM] / a0[A_V_DIM:A_V_DIM + 1] - lam * (a1[0:A_V_DIM] / a1[A_V_DIM:A_V_DIM + 1]))
    ot = ot * lax.rsqrt(jnp.mean(ot * ot, axis=0, keepdims=True) + EPS)
    o_ref[...] = (ot * (1.0 - lam_init)).T.astype(o_ref.dtype)


def _diff_attention(pb, diff_lambda, slopes_a, layer_idx, tq, tk):
    bsz, seq, _ = pb.shape
    assert seq % (_ATTN_TILES_PER_BODY * tk) == 0 and tk % tq == 0
    lam_init = 0.8 - 0.6 * math.exp(-0.3 * layer_idx)
    kern = functools.partial(_diff_attn_kernel, seq=seq, tq=tq, tk=tk, lam_init=lam_init)
    return pl.pallas_call(
        kern,
        grid=(bsz, A_HEADS, seq // tq),
        in_specs=[
            pl.BlockSpec(memory_space=pltpu.SMEM),
            pl.BlockSpec((None, tq, LANES), lambda b, h, i: (b, i, _PB_AQ + h)),
            pl.BlockSpec((None, seq, LANES), lambda b, h, i: (b, 0, _PB_AK + h)),
            pl.BlockSpec((None, seq, LANES), lambda b, h, i: (b, 0, _PB_AV + h)),
            pl.BlockSpec((4, A_QK_DIM), lambda b, h, i: (0, 0)),
        ],
        out_specs=pl.BlockSpec((None, tq, LANES), lambda b, h, i: (b, i, h)),
        out_shape=jax.ShapeDtypeStruct((bsz, seq, A_HEADS * A_V_DIM), jnp.bfloat16),
        scratch_shapes=[
            pltpu.VMEM((2, seq, LANES), jnp.bfloat16),
            pltpu.VMEM((seq // tk, _V_ROWS, tk), jnp.bfloat16),
            pltpu.VMEM((6, LANES, tq), jnp.bfloat16),
            pltpu.VMEM((2, 1, tq), jnp.float32),
            pltpu.VMEM((2, _V_ROWS, tq), jnp.float32),
            pltpu.VMEM((_ATTN_TILES_PER_BODY, 2, tk, tq), jnp.float32),
            pltpu.VMEM((_ATTN_TILES_PER_BODY, 2, 1, tq), jnp.float32),
        ],
        compiler_params=_cparams("parallel", "parallel", "arbitrary"),
        name="diff_attention",
    )(slopes_a, pb, pb, pb, diff_lambda)


def _retention_kernel(q_ref, k_ref, v_ref, g_ref, dec_ref, o_ref, of_ref, ob_ref, *, seq):
    head = pl.program_id(1)
    nc = seq // BLOCK
    f32 = jnp.float32

    lgv = _log_sigmoid(dec_ref[...])
    rr = lax.broadcasted_iota(jnp.int32, lgv.shape, 0)
    cc = lax.broadcasted_iota(jnp.int32, lgv.shape, 1)

    def pick(row):
        sel = jnp.where((rr == row) & (cc == head), lgv, 0.0)
        return jnp.sum(jnp.sum(sel, axis=1, keepdims=True), axis=0, keepdims=True)

    lg_f, lg_b = pick(0), pick(1)
    ri = lax.broadcasted_iota(jnp.int32, (BLOCK, BLOCK), 0).astype(f32)
    ci = lax.broadcasted_iota(jnp.int32, (BLOCK, BLOCK), 1).astype(f32)
    diff = ri - ci
    decay_f = jnp.where(diff >= 0, jnp.exp(lg_f * jnp.maximum(diff, 0.0)), 0.0)
    decay_b = jnp.where(diff <= 0, jnp.exp(lg_b * jnp.maximum(-diff, 0.0)), 0.0)
    idx = lax.broadcasted_iota(jnp.int32, (BLOCK, 1), 0).astype(f32)
    qdec_f = jnp.exp(lg_f * (idx + 1.0))
    kdec_f = jnp.exp(lg_f * (BLOCK - 1.0 - idx))
    qdec_b = jnp.exp(lg_b * (BLOCK - idx))
    kdec_b = jnp.exp(lg_b * idx)
    cdec_f = jnp.exp(lg_f * BLOCK)
    cdec_b = jnp.exp(lg_b * BLOCK)
    kscale = B_DIM ** -0.5

    def scores_and_kv(t0, backward):
        qb = q_ref[pl.ds(t0, BLOCK), :]
        kb = k_ref[pl.ds(t0, BLOCK), :].astype(f32) * kscale
        vb = v_ref[pl.ds(t0, BLOCK), :]
        kdec = kdec_b if backward else kdec_f
        return dict(t0=t0, backward=backward, qb=qb, vb=vb,
                    qk=_dot_nt(qb, kb.astype(qb.dtype)), kv=_dot_tn((kb * kdec).astype(vb.dtype), vb))

    def inner_product(g):
        s = g["qk"] * (decay_b if g["backward"] else decay_f)
        g["inner"] = _dot(s.astype(g["vb"].dtype), g["vb"])

    def with_state(g, state, out_ref):
        qdec, cdec = (qdec_b, cdec_b) if g["backward"] else (qdec_f, cdec_f)
        cross = _dot(g["qb"], state.astype(g["qb"].dtype)) * qdec
        out_ref[pl.ds(g["t0"], BLOCK), :] = g["inner"] + cross
        return state * cdec + g["kv"]

    def body(it, carry):
        rf, rb = carry
        steps = []
        for u in range(_SCAN_UNROLL):
            c = it * _SCAN_UNROLL + u
            steps.append(scores_and_kv(pl.multiple_of(c * BLOCK, BLOCK), False))
            steps.append(scores_and_kv(pl.multiple_of((nc - 1 - c) * BLOCK, BLOCK), True))
        for g in steps:
            inner_product(g)
        for g in steps:
            if g["backward"]:
                rb = with_state(g, rb, ob_ref)
            else:
                rf = with_state(g, rf, of_ref)
        return rf, rb

    zero = jnp.zeros((B_DIM, B_DIM), f32)
    lax.fori_loop(0, nc // _SCAN_UNROLL, body, (zero, zero))

    def fin(c, carry):
        t0 = pl.multiple_of(c * BLOCK, BLOCK)
        r = of_ref[pl.ds(t0, BLOCK), :] + ob_ref[pl.ds(t0, BLOCK), :]
        r = r * lax.rsqrt(jnp.mean(r * r, axis=-1, keepdims=True) + EPS)
        g = g_ref[pl.ds(t0, BLOCK), :].astype(f32)
        o_ref[pl.ds(t0, BLOCK), :] = (g * jax.nn.sigmoid(g) * r).astype(o_ref.dtype)
        return carry

    lax.fori_loop(0, nc, fin, 0, unroll=4)


def _retention(pb, dec_pad):
    bsz, seq, _ = pb.shape
    kern = functools.partial(_retention_kernel, seq=seq)

    def col(base):
        return pl.BlockSpec((None, seq, LANES), lambda b, h: (b, 0, base + h))

    return pl.pallas_call(
        kern,
        grid=(bsz, B_HEADS),
        in_specs=[col(_PB_BQ), col(_PB_BK), col(_PB_BV), col(_PB_BG),
                  pl.BlockSpec((8, LANES), lambda b, h: (0, 0))],
        out_specs=pl.BlockSpec((None, seq, LANES), lambda b, h: (b, 0, h)),
        out_shape=jax.ShapeDtypeStruct((bsz, seq, B_HEADS * B_DIM), jnp.bfloat16),
        scratch_shapes=[pltpu.VMEM((seq, B_DIM), jnp.float32), pltpu.VMEM((seq, B_DIM), jnp.float32)],
        compiler_params=_cparams("parallel", "parallel"),
        name="retention",
    )(pb, pb, pb, pb, dec_pad)


_ROWS_PER_DIR = 6


def _mlstm_kernel(cq_ref, ck_ref, gt_ref, v_ref, og_ref, wq_ref, wk_ref, gb_ref, o_ref,
                  qc_ref, kt_ref, vx_ref, rows_ref, of_ref, ob_ref, *, seq):
    nc = seq // BLOCK
    f32 = jnp.float32
    bf16 = jnp.bfloat16
    rowi = lax.broadcasted_iota(jnp.int32, (BLOCK, LANES), 0)
    coli = lax.broadcasted_iota(jnp.int32, (BLOCK, LANES), 1)
    lane_n = lax.broadcasted_iota(jnp.int32, (nc, LANES), 1)
    tril = rowi >= coli
    triu = rowi <= coli
    tril_b = tril.astype(bf16)
    triu_b = triu.astype(bf16)
    ones_b = jnp.ones((BLOCK, LANES), bf16)
    head = pl.program_id(1)

    def cummax_lanes(x, backward):
        sh = 1
        while sh < LANES:
            if backward:
                x = jnp.maximum(x, jnp.where(lane_n < LANES - sh, pltpu.roll(x, LANES - sh, 1), _NEG_INF))
            else:
                x = jnp.maximum(x, jnp.where(lane_n >= sh, pltpu.roll(x, sh, 1), _NEG_INF))
            sh *= 2
        return x

    for d, (op, last) in enumerate(((triu_b, LANES - 1), (tril_b, 0))):
        lf = _log_sigmoid(gt_ref[2 * d + 1] + gb_ref[2 * d + 1, head])
        hi, lo = _split_hi_lo(lf)
        b = _dot(hi, op) + _dot(lo, op)
        a = gt_ref[2 * d] + gb_ref[2 * d, head] - b
        bl = jnp.broadcast_to(b[:, last:last + 1], (nc, LANES))
        ml = jnp.broadcast_to(jnp.max(bl + a, axis=1, keepdims=True), (nc, LANES))
        for r, val in enumerate((a, cummax_lanes(a, d == 1), b, jnp.exp(bl + a - ml), bl, ml)):
            rows_ref[d * _ROWS_PER_DIR + r] = val

    def conv_silu(src_ref, w, c, t0):
        x = src_ref[pl.ds(t0, BLOCK), :]
        prev = src_ref[pl.ds(jnp.maximum(t0 - 1, 0), 1), :] * jnp.where(c > 0, 1.0, 0.0)
        nxt = src_ref[pl.ds(jnp.minimum(t0 + BLOCK, seq - 1), 1), :] * jnp.where(c < nc - 1, 1.0, 0.0)
        xd = jnp.where(rowi == 0, prev, pltpu.roll(x, 1, 0))
        xu = jnp.where(rowi == BLOCK - 1, nxt, pltpu.roll(x, BLOCK - 1, 0))
        y = xd * w[0:1] + x * w[1:2] + xu * w[2:3]
        return y * jax.nn.sigmoid(y)

    def prep(c, carry):
        t0 = pl.multiple_of(c * BLOCK, BLOCK)
        qc_ref[pl.ds(t0, BLOCK), :] = (conv_silu(cq_ref, wq_ref[...], c, t0) * (C_DIM ** -0.5)).astype(bf16)
        kt_ref[c] = conv_silu(ck_ref, wk_ref[...], c, t0).T.astype(bf16)
        vx_ref[pl.ds(t0, BLOCK), 0:C_DIM] = v_ref[pl.ds(t0, BLOCK), :]
        vx_ref[pl.ds(t0, BLOCK), C_DIM:2 * C_DIM] = ones_b
        return carry

    lax.fori_loop(0, nc, prep, 0, unroll=2)

    def load_rows(ch, backward):
        t0 = pl.multiple_of(ch * BLOCK, BLOCK)
        base = _ROWS_PER_DIR if backward else 0
        a, c, b, w, bl, ml = (rows_ref[base + r, pl.ds(ch, 1), :] for r in range(_ROWS_PER_DIR))
        kt = kt_ref[ch]
        return dict(t0=t0, backward=backward, bl=bl[:, 0:1], m_loc=ml[:, 0:1],
                    a_t=jnp.where(triu if backward else tril, a, _NEG_INF),
                    b_full=jnp.broadcast_to(b, (BLOCK, LANES)).T,
                    c_full=jnp.broadcast_to(c, (BLOCK, LANES)).T,
                    kw=(kt.astype(f32) * w).astype(bf16), kt=kt)

    def first_products(g):
        vx = vx_ref[pl.ds(g["t0"], BLOCK), :]
        g["qk"] = _dot(qc_ref[pl.ds(g["t0"], BLOCK), :], g["kt"])
        g["kv"] = _dot(g["kw"], vx)

    def decayed_scores(g):
        g["s0"] = (g["qk"] * jnp.exp(g["a_t"] - g["c_full"])).astype(bf16)

    def inner_products(g):
        g["p0"] = _dot(g["s0"], vx_ref[pl.ds(g["t0"], BLOCK), :])

    def with_state(g, state, out_ref):
        cx, m = state
        x = _dot(qc_ref[pl.ds(g["t0"], BLOCK), :], cx.astype(bf16))
        n = jnp.maximum(g["c_full"], m)
        e = jnp.exp(g["c_full"] - n)
        iw = jnp.exp(m - n)
        num = e * g["p0"][:, 0:C_DIM] + iw * x[:, 0:C_DIM]
        den = e * g["p0"][:, C_DIM:] + iw * x[:, C_DIM:]
        out_ref[pl.ds(g["t0"], BLOCK), :] = num / jnp.maximum(jnp.abs(den), jnp.exp(-(g["b_full"] + n)))
        m_new = jnp.maximum(g["bl"] + m, g["m_loc"])
        cw = jnp.exp(g["bl"] + m - m_new)
        f = jnp.exp(g["m_loc"] - m_new)
        return cw * cx + f * g["kv"], m_new

    def body(it, carry):
        sf, sb = carry
        steps = []
        for u in range(_SCAN_UNROLL):
            c = it * _SCAN_UNROLL + u
            steps.append(load_rows(c, False))
            steps.append(load_rows(nc - 1 - c, True))
        for g in steps:
            first_products(g)
        for g in steps:
            decayed_scores(g)
        for g in steps:
            inner_products(g)
        for g in steps:
            if g["backward"]:
                sb = with_state(g, sb, ob_ref)
            else:
                sf = with_state(g, sf, of_ref)
        return sf, sb

    init = (jnp.zeros((C_DIM, 2 * C_DIM), f32), jnp.zeros((1, 1), f32))
    lax.fori_loop(0, nc // _SCAN_UNROLL, body, (init, init))

    def fin(c, carry):
        t0 = pl.multiple_of(c * BLOCK, BLOCK)
        r = of_ref[pl.ds(t0, BLOCK), :] + ob_ref[pl.ds(t0, BLOCK), :]
        r = r * lax.rsqrt(jnp.mean(r * r, axis=-1, keepdims=True) + EPS)
        og = og_ref[pl.ds(t0, BLOCK), :].astype(f32)
        o_ref[pl.ds(t0, BLOCK), :] = (jax.nn.sigmoid(og) * r).astype(o_ref.dtype)
        return carry

    lax.fori_loop(0, nc, fin, 0, unroll=4)


def _mlstm(pb, pf, gates_t, wq_conv, wk_conv, gate_bias):
    bsz, seq, _ = pb.shape
    assert seq % (8 * BLOCK) == 0
    kern = functools.partial(_mlstm_kernel, seq=seq)

    def col(base, buffers=2):
        return pl.BlockSpec((None, seq, LANES), lambda b, h: (b, 0, base + h),
                            pipeline_mode=pl.Buffered(buffers))

    return pl.pallas_call(
        kern,
        grid=(bsz, C_HEADS),
        in_specs=[col(_PF_CQ, 1), col(_PF_CK, 1),
                  pl.BlockSpec((None, None, 4, seq // BLOCK, LANES), lambda b, h: (b, h, 0, 0, 0)),
                  col(_PB_CV), col(_PB_CO),
                  pl.BlockSpec((3, LANES), lambda b, h: (0, h)),
                  pl.BlockSpec((3, LANES), lambda b, h: (0, h)),
                  pl.BlockSpec(memory_space=pltpu.SMEM)],
        out_specs=pl.BlockSpec((None, seq, LANES), lambda b, h: (b, 0, h)),
        out_shape=jax.ShapeDtypeStruct((bsz, seq, C_HEADS * C_DIM), jnp.bfloat16),
        scratch_shapes=[pltpu.VMEM((seq, C_DIM), jnp.bfloat16),
                        pltpu.VMEM((seq // BLOCK, C_DIM, BLOCK), jnp.bfloat16),
                        pltpu.VMEM((seq, 2 * C_DIM), jnp.bfloat16),
                        pltpu.VMEM((2 * _ROWS_PER_DIR, seq // BLOCK, LANES), jnp.float32),
                        pltpu.VMEM((seq, C_DIM), jnp.float32), pltpu.VMEM((seq, C_DIM), jnp.float32)],
        compiler_params=_cparams("parallel", "parallel"),
        name="mlstm",
    )(pf, pf, gates_t, pb, pb, wq_conv, wk_conv, gate_bias)


def _window_attn_kernel(slopes_ref, sink_ref, q_ref, k_ref, v_ref, o_ref, kx_ref, vx_ref, *, seq):
    qi = pl.program_id(1)
    f32 = jnp.float32
    span = BLOCK + 2 * WINDOW
    group = D_HEADS // D_KV_HEADS

    @pl.when(qi == 0)
    def _():
        lane = lax.broadcasted_iota(jnp.int32, (seq, LANES), 1)
        ones = jnp.ones((seq, LANES), vx_ref.dtype)
        for src, dst in ((k_ref, kx_ref), (v_ref, vx_ref)):
            x = src[...].astype(f32)
            x0 = jnp.where(lane < D_DIM, x, 0.0)
            x1 = jnp.where(lane >= D_DIM, x, 0.0)
            for i, xi in enumerate((x0, pltpu.roll(x0, D_DIM, 1), pltpu.roll(x1, D_DIM, 1), x1)):
                dst[i, :, 0:LANES] = xi.astype(dst.dtype)
        for i in range(4):
            vx_ref[i, :, LANES:2 * LANES] = ones

    t0 = qi * BLOCK
    ws = pl.multiple_of(jnp.clip(t0 - WINDOW, 0, seq - span), BLOCK)
    rel = (lax.broadcasted_iota(jnp.int32, (BLOCK, span), 0)
           - lax.broadcasted_iota(jnp.int32, (BLOCK, span), 1)) + (t0 - ws)
    dist = jnp.abs(rel)
    neg_dist = jnp.where(dist <= WINDOW, dist.astype(f32) * (-_LOG2E), _NEG_INF)

    n_pairs = D_HEADS // 2
    lookahead = 2
    scores = {}

    def score_pair(pair):
        kv = (2 * pair) // group
        qp = (q_ref[:, pair * LANES:(pair + 1) * LANES].astype(f32) * (D_DIM ** -0.5 * _LOG2E)).astype(q_ref.dtype)
        for half in range(2):
            scores[2 * pair + half] = _dot_nt(qp, kx_ref[2 * kv + half, pl.ds(ws, span), :])

    def attend(hd, kv_slot):
        s = scores[hd] + slopes_ref[hd] * neg_dist
        snk = sink_ref[hd] * _LOG2E
        m = jnp.maximum(jnp.max(s, axis=1, keepdims=True), snk)
        o = _dot(jnp.exp2(s - m).astype(jnp.bfloat16), vx_ref[kv_slot, pl.ds(ws, span), :])
        return o[:, 0:LANES] / (o[:, LANES:] + jnp.exp2(snk - m))

    for pair in range(lookahead):
        score_pair(pair)
    for pair in range(n_pairs):
        kv = (2 * pair) // group
        out = attend(2 * pair, 2 * kv) + attend(2 * pair + 1, 2 * kv + 1)
        o_ref[:, pair * LANES:(pair + 1) * LANES] = out.astype(o_ref.dtype)
        if pair + lookahead < n_pairs:
            score_pair(pair + lookahead)


def _window_attention(pb, attn_sink, slopes_d):
    bsz, seq, _ = pb.shape
    kern = functools.partial(_window_attn_kernel, seq=seq)
    width = D_HEADS * D_DIM
    return pl.pallas_call(
        kern,
        grid=(bsz, seq // BLOCK),
        in_specs=[
            pl.BlockSpec(memory_space=pltpu.SMEM),
            pl.BlockSpec(memory_space=pltpu.SMEM),
            pl.BlockSpec((None, BLOCK, width), lambda b, i: (b, i, _PB_DQ512)),
            pl.BlockSpec((None, seq, LANES), lambda b, i: (b, 0, _PB_DK)),
            pl.BlockSpec((None, seq, LANES), lambda b, i: (b, 0, _PB_DV)),
        ],
        out_specs=pl.BlockSpec((None, BLOCK, width), lambda b, i: (b, i, 0)),
        out_shape=jax.ShapeDtypeStruct((bsz, seq, width), jnp.bfloat16),
        scratch_shapes=[pltpu.VMEM((4, seq, LANES), jnp.bfloat16), pltpu.VMEM((4, seq, 2 * LANES), jnp.bfloat16)],
        compiler_params=_cparams("parallel", "arbitrary"),
        name="window_attention",
    )(slopes_d, attn_sink, pb, pb, pb)


def _out_proj_kernel(x_ref, ya_ref, yb_ref, yc_ref, yd_ref, w_ref, o_ref):
    acc = x_ref[...]
    for g, y_ref in enumerate((ya_ref, yb_ref, yc_ref, yd_ref)):
        acc = acc + _dot(y_ref[...], w_ref[g * 512:(g + 1) * 512, :])
    o_ref[...] = acc


def _out_proj(x2d, ys, w_out, tm):
    t, d = x2d.shape
    yspec = pl.BlockSpec((tm, 512), lambda i: (i, 0))
    return pl.pallas_call(
        _out_proj_kernel,
        grid=(t // tm,),
        in_specs=[pl.BlockSpec((tm, d), lambda i: (i, 0)), yspec, yspec, yspec, yspec,
                  pl.BlockSpec((d, d), lambda i: (0, 0))],
        out_specs=pl.BlockSpec((tm, d), lambda i: (i, 0)),
        out_shape=jax.ShapeDtypeStruct((t, d), jnp.float32),
        compiler_params=_cparams("parallel"),
        name="out_proj",
    )(x2d, *ys, w_out)


def _ffn_kernel(x_ref, xp_ref, xn_ref, nw_ref, wg_ref, wu_ref, cw_ref, wd_ref, fw_ref, o_ref,
                h_ref, *, seq, tm, final_norm):
    i = pl.program_id(0)
    j = pl.program_id(1)
    f32 = jnp.float32

    def norm(x):
        return x * lax.rsqrt(jnp.mean(x * x, axis=-1, keepdims=True) + EPS) * nw_ref[...]

    @pl.when(j == 0)
    def _():
        h_ref[0:tm, :] = norm(x_ref[...]).astype(h_ref.dtype)
        has_prev = ((i * tm) % seq != 0).astype(f32)
        has_next = (((i + 1) * tm) % seq != 0).astype(f32)
        halo = jnp.concatenate([norm(xp_ref[...]) * has_prev, norm(xn_ref[...]) * has_next], axis=0)
        h_ref[tm:tm + 16, :] = halo.astype(h_ref.dtype)
        o_ref[...] = jnp.zeros(o_ref.shape, f32)

    gate = _dot(h_ref[...], wg_ref[...])
    gm = gate[0:tm]
    rowi = lax.broadcasted_iota(jnp.int32, gm.shape, 0)
    g_dn = jnp.where(rowi == 0, gate[tm + 7:tm + 8], pltpu.roll(gm, 1, 0))
    g_up = jnp.where(rowi == tm - 1, gate[tm + 8:tm + 9], pltpu.roll(gm, tm - 1, 0))
    cw = cw_ref[...]
    g = g_dn * cw[0:1] + gm * cw[1:2] + g_up * cw[2:3]
    up = _dot(h_ref[0:tm, :], wu_ref[...])
    act = (g * jax.nn.sigmoid(g) * up).astype(wd_ref.dtype)
    o_ref[...] += _dot(act, wd_ref[...])

    @pl.when(j == pl.num_programs(1) - 1)
    def _():
        y = x_ref[...] + o_ref[...]
        if final_norm:
            y = y * lax.rsqrt(jnp.mean(y * y, axis=-1, keepdims=True) + EPS) * fw_ref[...]
        o_ref[...] = y


def _ffn(x2d, seq, nw, wg, wu, cw, wd, fw, final_norm, tm, tn):
    t, d = x2d.shape
    nff = wg.shape[1]
    rb = tm // 8
    last8 = t // 8 - 1
    kern = functools.partial(_ffn_kernel, seq=seq, tm=tm, final_norm=final_norm)
    return pl.pallas_call(
        kern,
        grid=(t // tm, nff // tn),
        in_specs=[
            pl.BlockSpec((tm, d), lambda i, j: (i, 0), pipeline_mode=pl.Buffered(1)),
            pl.BlockSpec((8, d), lambda i, j: (jnp.maximum(i * rb - 1, 0), 0)),
            pl.BlockSpec((8, d), lambda i, j: (jnp.minimum((i + 1) * rb, last8), 0)),
            pl.BlockSpec((1, d), lambda i, j: (0, 0)),
            pl.BlockSpec((d, tn), lambda i, j: (0, j)),
            pl.BlockSpec((d, tn), lambda i, j: (0, j)),
            pl.BlockSpec((3, tn), lambda i, j: (0, j)),
            pl.BlockSpec((tn, d), lambda i, j: (j, 0)),
            pl.BlockSpec((1, d), lambda i, j: (0, 0)),
        ],
        out_specs=pl.BlockSpec((tm, d), lambda i, j: (i, 0)),
        out_shape=jax.ShapeDtypeStruct((t, d), jnp.float32),
        scratch_shapes=[pltpu.VMEM((tm + 16, d), jnp.bfloat16)],
        compiler_params=_cparams("parallel", "arbitrary"),
        name="ffn",
    )(x2d, x2d, x2d, nw, wg, wu, cw, wd, fw)


def _prepare_layer(l, norm1_w, w_in, qk_conv_w, diff_lambda, ret_decay_logit, mlstm_gate_bias, attn_sink,
                   w_out, norm2_w, w_gate, w_up, ffn_conv_w, w_down):
    bf16 = jnp.bfloat16
    f32 = jnp.float32
    wi = w_in[l]
    w_pb = jnp.concatenate([wi[:, _OFF_A:_OFF_CQ], wi[:, _OFF_CV:_OFF_GATES], wi[:, _OFF_D:N_IN]], axis=1)
    wgates = jnp.pad(wi[:, _OFF_GATES:_OFF_D], ((0, 0), (0, LANES - N_GATES)))
    w_pf = jnp.concatenate([wi[:, _OFF_CQ:_OFF_CV], wgates], axis=1)
    gbias = mlstm_gate_bias[l].astype(f32)
    dec = jnp.zeros((8, LANES), f32).at[0:2, 0:B_HEADS].set(ret_decay_logit[l].astype(f32))
    return dict(
        norm1=norm1_w[l].reshape(1, D_MODEL).astype(f32),
        w_pb=w_pb.astype(bf16), w_pf=w_pf.astype(bf16),
        wq_conv=qk_conv_w[l][:, :C_HEADS * C_DIM].astype(f32),
        wk_conv=qk_conv_w[l][:, C_HEADS * C_DIM:].astype(f32),
        gbias=gbias, dec=dec,
        diff_lambda=diff_lambda[l].astype(f32), sink=attn_sink[l].astype(f32),
        w_out=w_out[l].astype(bf16),
        norm2=norm2_w[l].reshape(1, D_MODEL).astype(f32),
        w_gate=w_gate[l].astype(bf16), w_up=w_up[l].astype(bf16),
        ffn_conv=ffn_conv_w[l].astype(f32), w_down=w_down[l].astype(bf16),
    )


def _tile(n, pref):
    t = min(pref, n)
    while n % t:
        t //= 2
    return t


def _layer(x2d, bsz, seq, layer_idx, p, slopes, final_w, final_norm):
    t = bsz * seq
    tm_in = _tile(t, 1024)
    pb, pf = _in_proj(x2d, p["norm1"], p["w_pb"], p["w_pf"], tm_in, _IN_PROJ_TN)
    pb = pb.reshape(bsz, seq, PB_WIDTH)
    pf = pf.reshape(bsz, seq, PF_WIDTH)
    gates_t = pf[:, :, _PF_GATES * LANES:_PF_GATES * LANES + N_GATES].reshape(bsz, seq, 4, C_HEADS)
    gates_t = jnp.transpose(gates_t, (0, 3, 2, 1)).reshape(bsz, C_HEADS, 4, seq // BLOCK, BLOCK)
    y_a = _diff_attention(pb, p["diff_lambda"], slopes[D_HEADS:], layer_idx, _tile(seq, 512),
                          _tile(seq // _ATTN_TILES_PER_BODY, 512))
    y_b = _retention(pb, p["dec"])
    y_c = _mlstm(pb, pf, gates_t, p["wq_conv"], p["wk_conv"], p["gbias"])
    y_d = _window_attention(pb, p["sink"], slopes[:D_HEADS])
    ys = [y.reshape(t, 512) for y in (y_a, y_b, y_c, y_d)]
    x1 = _out_proj(x2d, ys, p["w_out"], _tile(t, 512))
    return _ffn(x1, seq, p["norm2"], p["w_gate"], p["w_up"], p["ffn_conv"], p["w_down"], final_w,
                final_norm, _tile(seq, 1024), 512)


def _trunk(x, layers, slopes, final_w):
    bsz, seq, d = x.shape
    x2d = x.reshape(bsz * seq, d)
    for l, p in enumerate(layers):
        x2d = _layer(x2d, bsz, seq, l, p, slopes, final_w, l == len(layers) - 1)
    return x2d.reshape(bsz, seq, d)


def kernel(x_prompt, x_sample, norm1_w, w_in, qk_conv_w, diff_lambda, ret_decay_logit, mlstm_gate_bias,
           attn_sink, w_out, norm2_w, w_gate, w_up, ffn_conv_w, w_down, final_norm_w):
    layers = [_prepare_layer(l, norm1_w, w_in, qk_conv_w, diff_lambda, ret_decay_logit, mlstm_gate_bias,
                             attn_sink, w_out, norm2_w, w_gate, w_up, ffn_conv_w, w_down)
              for l in range(DEPTH)]
    n = A_HEADS + D_HEADS
    slopes = 2.0 ** (-8.0 * jnp.arange(1, n + 1, dtype=jnp.float32) / n)
    final_w = final_norm_w.reshape(1, D_MODEL).astype(jnp.float32)
    return (_trunk(x_prompt, layers, slopes, final_w), _trunk(x_sample, layers, slopes, final_w))
```

```python
import functools
import math

import jax
import jax.numpy as jnp
from jax import lax
from jax.experimental import pallas as pl
from jax.experimental.pallas import tpu as pltpu

D_MODEL = 2048
DEPTH = 2
A_HEADS = 4
A_QK_DIM = 64
A_V_DIM = 128
B_HEADS = 4
B_DIM = 128
C_HEADS = 4
C_DIM = 128
D_HEADS = 8
D_KV_HEADS = 2
D_DIM = 64
WINDOW = 128
BLOCK = 128
D_FF = 5632
EPS = 1e-6
LANES = 128

_OFF_A = 0
_OFF_B = 1536
_OFF_CQ = 3584
_OFF_CV = 4608
_OFF_GATES = 5632
_OFF_D = 5648
N_IN = 6416

_PB_AQ, _PB_AK, _PB_AV = 0, 4, 8
_PB_BQ, _PB_BK, _PB_BV, _PB_BG = 12, 16, 20, 24
_PB_CV, _PB_CO = 28, 32
_PB_DQ512 = 9
_PB_DK, _PB_DV = 40, 41
PB_WIDTH = 42 * LANES
_IN_PROJ_TN = 768
_PF_CQ, _PF_CK, _PF_GATES = 0, 4, 8
PF_WIDTH = 9 * LANES
N_GATES = 16

_VMEM_LIMIT = 56 * 1024 * 1024
_SCAN_UNROLL = 4
_NEG_INF = float("-inf")


def _cparams(*sem):
    return pltpu.CompilerParams(dimension_semantics=sem, vmem_limit_bytes=_VMEM_LIMIT)


def _dot(a, b):
    return jnp.dot(a, b, preferred_element_type=jnp.float32)


def _dot_nt(a, b):
    return lax.dot_general(a, b, (((1,), (1,)), ((), ())), preferred_element_type=jnp.float32)


def _dot_tn(a, b):
    return lax.dot_general(a, b, (((0,), (0,)), ((), ())), preferred_element_type=jnp.float32)


def _log_sigmoid(x):
    return -(jnp.maximum(-x, 0.0) + jnp.log1p(jnp.exp(-jnp.abs(x))))


def _split_hi_lo(x):
    hi = x.astype(jnp.bfloat16)
    lo = (x - hi.astype(jnp.float32)).astype(jnp.bfloat16)
    return hi, lo


def _in_proj_kernel(x_ref, nw_ref, wb_ref, wf_ref, ob_ref, of_ref, h_ref, *, nb):
    j = pl.program_id(1)

    @pl.when(j == 0)
    def _():
        x = x_ref[...]
        ms = jnp.mean(x * x, axis=-1, keepdims=True)
        h_ref[...] = (x * lax.rsqrt(ms + EPS) * nw_ref[...]).astype(h_ref.dtype)

    @pl.when(j < nb)
    def _():
        ob_ref[...] = _dot(h_ref[...], wb_ref[...]).astype(ob_ref.dtype)

    @pl.when(j == nb)
    def _():
        of_ref[...] = _dot(h_ref[...], wf_ref[...])


def _in_proj(x2d, nw, w_pb, w_pf, tm, tn):
    t, d = x2d.shape
    nb = PB_WIDTH // tn
    return pl.pallas_call(
        functools.partial(_in_proj_kernel, nb=nb),
        grid=(t // tm, nb + 1),
        in_specs=[
            pl.BlockSpec((tm, d), lambda i, j: (i, 0)),
            pl.BlockSpec((1, d), lambda i, j: (0, 0)),
            pl.BlockSpec((d, tn), lambda i, j: (0, jnp.minimum(j, nb - 1))),
            pl.BlockSpec((d, PF_WIDTH), lambda i, j: (0, 0)),
        ],
        out_specs=[pl.BlockSpec((tm, tn), lambda i, j: (i, jnp.minimum(j, nb - 1))),
                   pl.BlockSpec((tm, PF_WIDTH), lambda i, j: (i, 0))],
        out_shape=[jax.ShapeDtypeStruct((t, PB_WIDTH), jnp.bfloat16),
                   jax.ShapeDtypeStruct((t, PF_WIDTH), jnp.float32)],
        scratch_shapes=[pltpu.VMEM((tm, d), jnp.bfloat16)],
        compiler_params=_cparams("parallel", "arbitrary"),
        name="norm_in_proj",
    )(x2d, nw, w_pb, w_pf)


_LOG2E = 1.4426950408889634
_N_BIAS_LANES = 3
_V_ROWS = A_V_DIM + 16
_ATTN_TILES_PER_BODY = 4
_ATTN_TILE = 512


def _diff_attn_kernel(slopes_ref, q_ref, k_ref, v_ref, dl_ref, o_ref,
                      kx_ref, vt_ref, qt_ref, m_ref, acc_ref, s_ref, cmax_ref, bias_ref, *, seq, tq, tk,
                      lam_init):
    head = pl.program_id(1)
    qi = pl.program_id(2)
    f32 = jnp.float32
    bf16 = jnp.bfloat16
    sigma = slopes_ref[head] * _LOG2E
    nk = seq // tk
    prep_rows = tk

    @pl.when(qi == 0)
    def _():
        lane = lax.broadcasted_iota(jnp.int32, (prep_rows, LANES), 1)
        row = lax.broadcasted_iota(jnp.int32, (prep_rows, LANES), 0)
        ones_row = jnp.where(lax.broadcasted_iota(jnp.int32, (_V_ROWS - A_V_DIM, tk), 0) == 0, 1.0, 0.0)

        def prep(c, carry):
            r0 = pl.multiple_of(c * prep_rows, prep_rows)
            kk = k_ref[pl.ds(r0, prep_rows), :].astype(f32)
            col = sigma * (row + r0).astype(f32)
            c1 = col.astype(bf16).astype(f32)
            c2 = (col - c1).astype(bf16).astype(f32)
            c3 = col - c1 - c2
            bias = jnp.where(lane == A_QK_DIM, c1, jnp.where(lane == A_QK_DIM + 1, c2,
                             jnp.where(lane == A_QK_DIM + 2, c3, 0.0)))
            kx_ref[0, pl.ds(r0, prep_rows), :] = jnp.where(lane < A_QK_DIM, kk, bias).astype(bf16)
            kx_ref[1, pl.ds(r0, prep_rows), :] = jnp.where(
                lane < A_QK_DIM, pltpu.roll(kk, A_QK_DIM, 1), bias).astype(bf16)
            vt_ref[c, 0:A_V_DIM, :] = v_ref[pl.ds(r0, prep_rows), :].astype(f32).T.astype(bf16)
            vt_ref[c, A_V_DIM:_V_ROWS, :] = ones_row.astype(bf16)
            return carry

        lax.fori_loop(0, nk, prep, 0)
        rel = (lax.broadcasted_iota(jnp.int32, (tk, tq), 1)
               - lax.broadcasted_iota(jnp.int32, (tk, tq), 0)).astype(f32)
        bias_ref[...] = sigma * jnp.abs(rel)

    qt = (q_ref[...].astype(f32) * (A_QK_DIM ** -0.5 * _LOG2E)).T
    qrow = lax.broadcasted_iota(jnp.int32, (LANES, tq), 0)
    is_bias = (qrow >= A_QK_DIM) & (qrow < A_QK_DIM + _N_BIAS_LANES)
    for mp in range(2):
        qm = qt if mp == 0 else pltpu.roll(qt, A_QK_DIM, 0)
        for var, sign in enumerate((1.0, 0.0, -1.0)):
            qt_ref[3 * mp + var] = jnp.where(qrow < A_QK_DIM, qm, jnp.where(is_bias, sign, 0.0)).astype(bf16)

    q0 = qi * tq
    rho = sigma * (lax.broadcasted_iota(jnp.int32, (1, tq), 1) + q0).astype(f32)
    m_ref[...] = jnp.full(m_ref.shape, _NEG_INF, f32)
    acc_ref[...] = jnp.zeros(acc_ref.shape, f32)

    kd = q0 // tk

    def key_tile(j):
        jm = jnp.minimum(j, nk - 1) - 1
        return jnp.where(j == 0, kd, jnp.where(jm < kd, jm, jm + 1))

    def scores(j, slot, mp):
        kb = key_tile(j)
        var = jnp.where(kb < kd, 0, jnp.where(kb > kd, 2, 1))
        k_start = pl.multiple_of(kb * tk, tk)
        s = _dot(kx_ref[mp, pl.ds(k_start, tk), :], qt_ref[3 * mp + var])
        s_ref[slot, mp] = s
        cmax_ref[slot, mp] = jnp.max(s, axis=0, keepdims=True)

    def softmax_pv(j, slot, mp):
        kb = key_tile(j)
        shift = jnp.where(kb < kd, -1.0, jnp.where(kb > kd, 1.0, 0.0)) * rho
        m_old = m_ref[mp]
        m_new = jnp.maximum(m_old, cmax_ref[slot, mp] + shift)
        p = jnp.exp2(s_ref[slot, mp] - (m_new - shift))
        acc_ref[mp] = jnp.exp2(m_old - m_new) * acc_ref[mp] + _dot(vt_ref[kb], p.astype(bf16))
        m_ref[mp] = m_new

    k_diag = pl.multiple_of(kd * tk, tk)
    for mp in range(2):
        s = _dot(kx_ref[mp, pl.ds(k_diag, tk), :], qt_ref[3 * mp + 1]) - bias_ref[...]
        s_ref[0, mp] = s
        cmax_ref[0, mp] = jnp.max(s, axis=0, keepdims=True)

    def body(i, carry):
        for u in range(_ATTN_TILES_PER_BODY):
            j = _ATTN_TILES_PER_BODY * i + u
            for mp in range(2):
                scores(j + 1, (u + 1) % _ATTN_TILES_PER_BODY, mp)
                softmax_pv(j, u, mp)
        return carry

    lax.fori_loop(0, nk // _ATTN_TILES_PER_BODY, body, 0)

    dl = dl_ref[...]
    lam = (jnp.exp(jnp.sum(dl[0:1] * dl[1:2], axis=1, keepdims=True))
           - jnp.exp(jnp.sum(dl[2:3] * dl[3:4], axis=1, keepdims=True)) + lam_init)
    a0 = acc_ref[0]
    a1 = acc_ref[1]
    ot = (a0[0:A_V_DIM] / a0[A_V_DIM:A_V_DIM + 1] - lam * (a1[0:A_V_DIM] / a1[A_V_DIM:A_V_DIM + 1]))
    ot = ot * lax.rsqrt(jnp.mean(ot * ot, axis=0, keepdims=True) + EPS)
    o_ref[...] = (ot * (1.0 - lam_init)).T.astype(o_ref.dtype)


def _diff_attention(pb, diff_lambda, slopes_a, layer_idx, tq, tk):
    bsz, seq, _ = pb.shape
    assert seq % (_ATTN_TILES_PER_BODY * tk) == 0 and tk == tq
    lam_init = 0.8 - 0.6 * math.exp(-0.3 * layer_idx)
    kern = functools.partial(_diff_attn_kernel, seq=seq, tq=tq, tk=tk, lam_init=lam_init)
    return pl.pallas_call(
        kern,
        grid=(bsz, A_HEADS, seq // tq),
        in_specs=[
            pl.BlockSpec(memory_space=pltpu.SMEM),
            pl.BlockSpec((None, tq, LANES), lambda b, h, i: (b, i, _PB_AQ + h)),
            pl.BlockSpec((None, seq, LANES), lambda b, h, i: (b, 0, _PB_AK + h)),
            pl.BlockSpec((None, seq, LANES), lambda b, h, i: (b, 0, _PB_AV + h)),
            pl.BlockSpec((4, A_QK_DIM), lambda b, h, i: (0, 0)),
        ],
        out_specs=pl.BlockSpec((None, tq, LANES), lambda b, h, i: (b, i, h)),
        out_shape=jax.ShapeDtypeStruct((bsz, seq, A_HEADS * A_V_DIM), jnp.bfloat16),
        scratch_shapes=[
            pltpu.VMEM((2, seq, LANES), jnp.bfloat16),
            pltpu.VMEM((seq // tk, _V_ROWS, tk), jnp.bfloat16),
            pltpu.VMEM((6, LANES, tq), jnp.bfloat16),
            pltpu.VMEM((2, 1, tq), jnp.float32),
            pltpu.VMEM((2, _V_ROWS, tq), jnp.float32),
            pltpu.VMEM((_ATTN_TILES_PER_BODY, 2, tk, tq), jnp.float32),
            pltpu.VMEM((_ATTN_TILES_PER_BODY, 2, 1, tq), jnp.float32),
            pltpu.VMEM((tk, tq), jnp.float32),
        ],
        compiler_params=_cparams("parallel", "parallel", "arbitrary"),
        name="diff_attention",
    )(slopes_a, pb, pb, pb, diff_lambda)


def _retention_kernel(q_ref, k_ref, v_ref, g_ref, dec_ref, o_ref, of_ref, ob_ref, *, seq):
    head = pl.program_id(1)
    nc = seq // BLOCK
    f32 = jnp.float32

    lgv = _log_sigmoid(dec_ref[...])
    rr = lax.broadcasted_iota(jnp.int32, lgv.shape, 0)
    cc = lax.broadcasted_iota(jnp.int32, lgv.shape, 1)

    def pick(row):
        sel = jnp.where((rr == row) & (cc == head), lgv, 0.0)
        return jnp.sum(jnp.sum(sel, axis=1, keepdims=True), axis=0, keepdims=True)

    lg_f, lg_b = pick(0), pick(1)
    ri = lax.broadcasted_iota(jnp.int32, (BLOCK, BLOCK), 0).astype(f32)
    ci = lax.broadcasted_iota(jnp.int32, (BLOCK, BLOCK), 1).astype(f32)
    diff = ri - ci
    decay_f = jnp.where(diff >= 0, jnp.exp(lg_f * jnp.maximum(diff, 0.0)), 0.0)
    decay_b = jnp.where(diff <= 0, jnp.exp(lg_b * jnp.maximum(-diff, 0.0)), 0.0)
    idx = lax.broadcasted_iota(jnp.int32, (BLOCK, 1), 0).astype(f32)
    qdec_f = jnp.exp(lg_f * (idx + 1.0))
    kdec_f = jnp.exp(lg_f * (BLOCK - 1.0 - idx))
    qdec_b = jnp.exp(lg_b * (BLOCK - idx))
    kdec_b = jnp.exp(lg_b * idx)
    cdec_f = jnp.exp(lg_f * BLOCK)
    cdec_b = jnp.exp(lg_b * BLOCK)
    kscale = B_DIM ** -0.5

    def scores_and_kv(t0, backward):
        qb = q_ref[pl.ds(t0, BLOCK), :]
        kb = k_ref[pl.ds(t0, BLOCK), :].astype(f32) * kscale
        vb = v_ref[pl.ds(t0, BLOCK), :]
        kdec = kdec_b if backward else kdec_f
        return dict(t0=t0, backward=backward, qb=qb, vb=vb,
                    qk=_dot_nt(qb, kb.astype(qb.dtype)), kv=_dot_tn((kb * kdec).astype(vb.dtype), vb))

    def inner_product(g):
        s = g["qk"] * (decay_b if g["backward"] else decay_f)
        g["inner"] = _dot(s.astype(g["vb"].dtype), g["vb"])

    def with_state(g, state, out_ref):
        qdec, cdec = (qdec_b, cdec_b) if g["backward"] else (qdec_f, cdec_f)
        cross = _dot(g["qb"], state.astype(g["qb"].dtype)) * qdec
        out_ref[pl.ds(g["t0"], BLOCK), :] = g["inner"] + cross
        return state * cdec + g["kv"]

    def body(it, carry):
        rf, rb = carry
        steps = []
        for u in range(_SCAN_UNROLL):
            c = it * _SCAN_UNROLL + u
            steps.append(scores_and_kv(pl.multiple_of(c * BLOCK, BLOCK), False))
            steps.append(scores_and_kv(pl.multiple_of((nc - 1 - c) * BLOCK, BLOCK), True))
        for g in steps:
            inner_product(g)
        for g in steps:
            if g["backward"]:
                rb = with_state(g, rb, ob_ref)
            else:
                rf = with_state(g, rf, of_ref)
        return rf, rb

    zero = jnp.zeros((B_DIM, B_DIM), f32)
    lax.fori_loop(0, nc // _SCAN_UNROLL, body, (zero, zero))

    def fin(c, carry):
        t0 = pl.multiple_of(c * BLOCK, BLOCK)
        r = of_ref[pl.ds(t0, BLOCK), :] + ob_ref[pl.ds(t0, BLOCK), :]
        r = r * lax.rsqrt(jnp.mean(r * r, axis=-1, keepdims=True) + EPS)
        g = g_ref[pl.ds(t0, BLOCK), :].astype(f32)
        o_ref[pl.ds(t0, BLOCK), :] = (g * jax.nn.sigmoid(g) * r).astype(o_ref.dtype)
        return carry

    lax.fori_loop(0, nc, fin, 0, unroll=4)


def _retention(pb, dec_pad):
    bsz, seq, _ = pb.shape
    kern = functools.partial(_retention_kernel, seq=seq)

    def col(base):
        return pl.BlockSpec((None, seq, LANES), lambda b, h: (b, 0, base + h))

    return pl.pallas_call(
        kern,
        grid=(bsz, B_HEADS),
        in_specs=[col(_PB_BQ), col(_PB_BK), col(_PB_BV), col(_PB_BG),
                  pl.BlockSpec((8, LANES), lambda b, h: (0, 0))],
        out_specs=pl.BlockSpec((None, seq, LANES), lambda b, h: (b, 0, h)),
        out_shape=jax.ShapeDtypeStruct((bsz, seq, B_HEADS * B_DIM), jnp.bfloat16),
        scratch_shapes=[pltpu.VMEM((seq, B_DIM), jnp.float32), pltpu.VMEM((seq, B_DIM), jnp.float32)],
        compiler_params=_cparams("parallel", "parallel"),
        name="retention",
    )(pb, pb, pb, pb, dec_pad)


_ROWS_PER_DIR = 6


def _mlstm_kernel(cq_ref, ck_ref, gt_ref, v_ref, og_ref, wq_ref, wk_ref, gb_ref, o_ref,
                  qc_ref, kt_ref, vx_ref, rows_ref, of_ref, ob_ref, *, seq):
    nc = seq // BLOCK
    f32 = jnp.float32
    bf16 = jnp.bfloat16
    rowi = lax.broadcasted_iota(jnp.int32, (BLOCK, LANES), 0)
    coli = lax.broadcasted_iota(jnp.int32, (BLOCK, LANES), 1)
    lane_n = lax.broadcasted_iota(jnp.int32, (nc, LANES), 1)
    tril = rowi >= coli
    triu = rowi <= coli
    tril_b = tril.astype(bf16)
    triu_b = triu.astype(bf16)
    ones_b = jnp.ones((BLOCK, LANES), bf16)
    head = pl.program_id(1)

    def cummax_lanes(x, backward):
        sh = 1
        while sh < LANES:
            if backward:
                x = jnp.maximum(x, jnp.where(lane_n < LANES - sh, pltpu.roll(x, LANES - sh, 1), _NEG_INF))
            else:
                x = jnp.maximum(x, jnp.where(lane_n >= sh, pltpu.roll(x, sh, 1), _NEG_INF))
            sh *= 2
        return x

    for d, (op, last) in enumerate(((triu_b, LANES - 1), (tril_b, 0))):
        lf = _log_sigmoid(gt_ref[2 * d + 1] + gb_ref[2 * d + 1, head])
        hi, lo = _split_hi_lo(lf)
        b = _dot(hi, op) + _dot(lo, op)
        a = gt_ref[2 * d] + gb_ref[2 * d, head] - b
        bl = jnp.broadcast_to(b[:, last:last + 1], (nc, LANES))
        ml = jnp.broadcast_to(jnp.max(bl + a, axis=1, keepdims=True), (nc, LANES))
        for r, val in enumerate((a, cummax_lanes(a, d == 1), b, jnp.exp(bl + a - ml), bl, ml)):
            rows_ref[d * _ROWS_PER_DIR + r] = val

    def conv_silu(src_ref, w, c, t0):
        x = src_ref[pl.ds(t0, BLOCK), :]
        prev = src_ref[pl.ds(jnp.maximum(t0 - 1, 0), 1), :] * jnp.where(c > 0, 1.0, 0.0)
        nxt = src_ref[pl.ds(jnp.minimum(t0 + BLOCK, seq - 1), 1), :] * jnp.where(c < nc - 1, 1.0, 0.0)
        xd = jnp.where(rowi == 0, prev, pltpu.roll(x, 1, 0))
        xu = jnp.where(rowi == BLOCK - 1, nxt, pltpu.roll(x, BLOCK - 1, 0))
        y = xd * w[0:1] + x * w[1:2] + xu * w[2:3]
        return y * jax.nn.sigmoid(y)

    def prep(c, carry):
        t0 = pl.multiple_of(c * BLOCK, BLOCK)
        qc_ref[pl.ds(t0, BLOCK), :] = (conv_silu(cq_ref, wq_ref[...], c, t0) * (C_DIM ** -0.5)).astype(bf16)
        kt_ref[c] = conv_silu(ck_ref, wk_ref[...], c, t0).T.astype(bf16)
        vx_ref[pl.ds(t0, BLOCK), 0:C_DIM] = v_ref[pl.ds(t0, BLOCK), :]
        vx_ref[pl.ds(t0, BLOCK), C_DIM:2 * C_DIM] = ones_b
        return carry

    lax.fori_loop(0, nc, prep, 0, unroll=2)

    def load_rows(ch, backward):
        t0 = pl.multiple_of(ch * BLOCK, BLOCK)
        base = _ROWS_PER_DIR if backward else 0
        a, c, b, w, bl, ml = (rows_ref[base + r, pl.ds(ch, 1), :] for r in range(_ROWS_PER_DIR))
        kt = kt_ref[ch]
        return dict(t0=t0, backward=backward, bl=bl[:, 0:1], m_loc=ml[:, 0:1],
                    a_t=jnp.where(triu if backward else tril, a, _NEG_INF),
                    b_full=jnp.broadcast_to(b, (BLOCK, LANES)).T,
                    c_full=jnp.broadcast_to(c, (BLOCK, LANES)).T,
                    kw=(kt.astype(f32) * w).astype(bf16), kt=kt)

    def first_products(g):
        vx = vx_ref[pl.ds(g["t0"], BLOCK), :]
        g["qk"] = _dot(qc_ref[pl.ds(g["t0"], BLOCK), :], g["kt"])
        g["kv"] = _dot(g["kw"], vx)

    def decayed_scores(g):
        g["s0"] = (g["qk"] * jnp.exp(g["a_t"] - g["c_full"])).astype(bf16)

    def inner_products(g):
        g["p0"] = _dot(g["s0"], vx_ref[pl.ds(g["t0"], BLOCK), :])

    def with_state(g, state, out_ref):
        cx, m = state
        x = _dot(qc_ref[pl.ds(g["t0"], BLOCK), :], cx.astype(bf16))
        n = jnp.maximum(g["c_full"], m)
        e = jnp.exp(g["c_full"] - n)
        iw = jnp.exp(m - n)
        num = e * g["p0"][:, 0:C_DIM] + iw * x[:, 0:C_DIM]
        den = e * g["p0"][:, C_DIM:] + iw * x[:, C_DIM:]
        out_ref[pl.ds(g["t0"], BLOCK), :] = num / jnp.maximum(jnp.abs(den), jnp.exp(-(g["b_full"] + n)))
        m_new = jnp.maximum(g["bl"] + m, g["m_loc"])
        cw = jnp.exp(g["bl"] + m - m_new)
        f = jnp.exp(g["m_loc"] - m_new)
        return cw * cx + f * g["kv"], m_new

    def body(it, carry):
        sf, sb = carry
        steps = []
        for u in range(_SCAN_UNROLL):
            c = it * _SCAN_UNROLL + u
            steps.append(load_rows(c, False))
            steps.append(load_rows(nc - 1 - c, True))
        for g in steps:
            first_products(g)
        for g in steps:
            decayed_scores(g)
        for g in steps:
            inner_products(g)
        for g in steps:
            if g["backward"]:
                sb = with_state(g, sb, ob_ref)
            else:
                sf = with_state(g, sf, of_ref)
        return sf, sb

    init = (jnp.zeros((C_DIM, 2 * C_DIM), f32), jnp.zeros((1, 1), f32))
    lax.fori_loop(0, nc // _SCAN_UNROLL, body, (init, init))

    def fin(c, carry):
        t0 = pl.multiple_of(c * BLOCK, BLOCK)
        r = of_ref[pl.ds(t0, BLOCK), :] + ob_ref[pl.ds(t0, BLOCK), :]
        r = r * lax.rsqrt(jnp.mean(r * r, axis=-1, keepdims=True) + EPS)
        og = og_ref[pl.ds(t0, BLOCK), :].astype(f32)
        o_ref[pl.ds(t0, BLOCK), :] = (jax.nn.sigmoid(og) * r).astype(o_ref.dtype)
        return carry

    lax.fori_loop(0, nc, fin, 0, unroll=4)


def _mlstm(pb, pf, gates_t, wq_conv, wk_conv, gate_bias):
    bsz, seq, _ = pb.shape
    assert seq % (8 * BLOCK) == 0
    kern = functools.partial(_mlstm_kernel, seq=seq)

    def col(base, buffers=2):
        return pl.BlockSpec((None, seq, LANES), lambda b, h: (b, 0, base + h),
                            pipeline_mode=pl.Buffered(buffers))

    return pl.pallas_call(
        kern,
        grid=(bsz, C_HEADS),
        in_specs=[col(_PF_CQ, 1), col(_PF_CK, 1),
                  pl.BlockSpec((None, None, 4, seq // BLOCK, LANES), lambda b, h: (b, h, 0, 0, 0)),
                  col(_PB_CV), col(_PB_CO),
                  pl.BlockSpec((3, LANES), lambda b, h: (0, h)),
                  pl.BlockSpec((3, LANES), lambda b, h: (0, h)),
                  pl.BlockSpec(memory_space=pltpu.SMEM)],
        out_specs=pl.BlockSpec((None, seq, LANES), lambda b, h: (b, 0, h)),
        out_shape=jax.ShapeDtypeStruct((bsz, seq, C_HEADS * C_DIM), jnp.bfloat16),
        scratch_shapes=[pltpu.VMEM((seq, C_DIM), jnp.bfloat16),
                        pltpu.VMEM((seq // BLOCK, C_DIM, BLOCK), jnp.bfloat16),
                        pltpu.VMEM((seq, 2 * C_DIM), jnp.bfloat16),
                        pltpu.VMEM((2 * _ROWS_PER_DIR, seq // BLOCK, LANES), jnp.float32),
                        pltpu.VMEM((seq, C_DIM), jnp.float32), pltpu.VMEM((seq, C_DIM), jnp.float32)],
        compiler_params=_cparams("parallel", "parallel"),
        name="mlstm",
    )(pf, pf, gates_t, pb, pb, wq_conv, wk_conv, gate_bias)


def _window_attn_kernel(slopes_ref, sink_ref, q_ref, k_ref, v_ref, o_ref, kx_ref, vx_ref, *, seq):
    qi = pl.program_id(1)
    f32 = jnp.float32
    span = BLOCK + 2 * WINDOW
    group = D_HEADS // D_KV_HEADS

    @pl.when(qi == 0)
    def _():
        lane = lax.broadcasted_iota(jnp.int32, (seq, LANES), 1)
        ones = jnp.ones((seq, LANES), vx_ref.dtype)
        for src, dst in ((k_ref, kx_ref), (v_ref, vx_ref)):
            x = src[...].astype(f32)
            x0 = jnp.where(lane < D_DIM, x, 0.0)
            x1 = jnp.where(lane >= D_DIM, x, 0.0)
            for i, xi in enumerate((x0, pltpu.roll(x0, D_DIM, 1), pltpu.roll(x1, D_DIM, 1), x1)):
                dst[i, :, 0:LANES] = xi.astype(dst.dtype)
        for i in range(4):
            vx_ref[i, :, LANES:2 * LANES] = ones

    t0 = qi * BLOCK
    ws = pl.multiple_of(jnp.clip(t0 - WINDOW, 0, seq - span), BLOCK)
    rel = (lax.broadcasted_iota(jnp.int32, (BLOCK, span), 0)
           - lax.broadcasted_iota(jnp.int32, (BLOCK, span), 1)) + (t0 - ws)
    dist = jnp.abs(rel)
    neg_dist = jnp.where(dist <= WINDOW, dist.astype(f32) * (-_LOG2E), _NEG_INF)

    n_pairs = D_HEADS // 2
    lookahead = 2
    scores = {}

    def score_pair(pair):
        kv = (2 * pair) // group
        qp = (q_ref[:, pair * LANES:(pair + 1) * LANES].astype(f32) * (D_DIM ** -0.5 * _LOG2E)).astype(q_ref.dtype)
        for half in range(2):
            scores[2 * pair + half] = _dot_nt(qp, kx_ref[2 * kv + half, pl.ds(ws, span), :])

    def attend(hd, kv_slot):
        s = scores[hd] + slopes_ref[hd] * neg_dist
        snk = sink_ref[hd] * _LOG2E
        m = jnp.maximum(jnp.max(s, axis=1, keepdims=True), snk)
        o = _dot(jnp.exp2(s - m).astype(jnp.bfloat16), vx_ref[kv_slot, pl.ds(ws, span), :])
        return o[:, 0:LANES] / (o[:, LANES:] + jnp.exp2(snk - m))

    for pair in range(lookahead):
        score_pair(pair)
    for pair in range(n_pairs):
        kv = (2 * pair) // group
        out = attend(2 * pair, 2 * kv) + attend(2 * pair + 1, 2 * kv + 1)
        o_ref[:, pair * LANES:(pair + 1) * LANES] = out.astype(o_ref.dtype)
        if pair + lookahead < n_pairs:
            score_pair(pair + lookahead)


def _window_attention(pb, attn_sink, slopes_d):
    bsz, seq, _ = pb.shape
    kern = functools.partial(_window_attn_kernel, seq=seq)
    width = D_HEADS * D_DIM
    return pl.pallas_call(
        kern,
        grid=(bsz, seq // BLOCK),
        in_specs=[
            pl.BlockSpec(memory_space=pltpu.SMEM),
            pl.BlockSpec(memory_space=pltpu.SMEM),
            pl.BlockSpec((None, BLOCK, width), lambda b, i: (b, i, _PB_DQ512)),
            pl.BlockSpec((None, seq, LANES), lambda b, i: (b, 0, _PB_DK)),
            pl.BlockSpec((None, seq, LANES), lambda b, i: (b, 0, _PB_DV)),
        ],
        out_specs=pl.BlockSpec((None, BLOCK, width), lambda b, i: (b, i, 0)),
        out_shape=jax.ShapeDtypeStruct((bsz, seq, width), jnp.bfloat16),
        scratch_shapes=[pltpu.VMEM((4, seq, LANES), jnp.bfloat16), pltpu.VMEM((4, seq, 2 * LANES), jnp.bfloat16)],
        compiler_params=_cparams("parallel", "arbitrary"),
        name="window_attention",
    )(slopes_d, attn_sink, pb, pb, pb)


def _out_proj_kernel(x_ref, ya_ref, yb_ref, yc_ref, yd_ref, w_ref, o_ref):
    acc = x_ref[...]
    for g, y_ref in enumerate((ya_ref, yb_ref, yc_ref, yd_ref)):
        acc = acc + _dot(y_ref[...], w_ref[g * 512:(g + 1) * 512, :])
    o_ref[...] = acc


def _out_proj(x2d, ys, w_out, tm):
    t, d = x2d.shape
    yspec = pl.BlockSpec((tm, 512), lambda i: (i, 0))
    return pl.pallas_call(
        _out_proj_kernel,
        grid=(t // tm,),
        in_specs=[pl.BlockSpec((tm, d), lambda i: (i, 0)), yspec, yspec, yspec, yspec,
                  pl.BlockSpec((d, d), lambda i: (0, 0))],
        out_specs=pl.BlockSpec((tm, d), lambda i: (i, 0)),
        out_shape=jax.ShapeDtypeStruct((t, d), jnp.float32),
        compiler_params=_cparams("parallel"),
        name="out_proj",
    )(x2d, *ys, w_out)


def _ffn_kernel(x_ref, xp_ref, xn_ref, nw_ref, wg_ref, wu_ref, cw_ref, wd_ref, fw_ref, o_ref,
                h_ref, acc_ref, *, seq, tm, final_norm):
    i = pl.program_id(0)
    j = pl.program_id(1)
    f32 = jnp.float32

    def norm(x):
        return x * lax.rsqrt(jnp.mean(x * x, axis=-1, keepdims=True) + EPS) * nw_ref[...]

    @pl.when(j == 0)
    def _():
        h_ref[0:tm, :] = norm(x_ref[...]).astype(h_ref.dtype)
        has_prev = ((i * tm) % seq != 0).astype(f32)
        has_next = (((i + 1) * tm) % seq != 0).astype(f32)
        halo = jnp.concatenate([norm(xp_ref[...]) * has_prev, norm(xn_ref[...]) * has_next], axis=0)
        h_ref[tm:tm + 16, :] = halo.astype(h_ref.dtype)
        acc_ref[...] = jnp.zeros(acc_ref.shape, f32)

    gate = _dot(h_ref[...], wg_ref[...])
    gm = gate[0:tm]
    rowi = lax.broadcasted_iota(jnp.int32, gm.shape, 0)
    g_dn = jnp.where(rowi == 0, gate[tm + 7:tm + 8], pltpu.roll(gm, 1, 0))
    g_up = jnp.where(rowi == tm - 1, gate[tm + 8:tm + 9], pltpu.roll(gm, tm - 1, 0))
    cw = cw_ref[...]
    g = g_dn * cw[0:1] + gm * cw[1:2] + g_up * cw[2:3]
    up = _dot(h_ref[0:tm, :], wu_ref[...])
    act = (g * jax.nn.sigmoid(g) * up).astype(wd_ref.dtype)
    acc_ref[...] += _dot(act, wd_ref[...])

    @pl.when(j == pl.num_programs(1) - 1)
    def _():
        y = x_ref[...] + acc_ref[...]
        if final_norm:
            y = y * lax.rsqrt(jnp.mean(y * y, axis=-1, keepdims=True) + EPS) * fw_ref[...]
        o_ref[...] = y


def _ffn(x2d, seq, nw, wg, wu, cw, wd, fw, final_norm, tm, tn):
    t, d = x2d.shape
    nff = wg.shape[1]
    rb = tm // 8
    last8 = t // 8 - 1
    kern = functools.partial(_ffn_kernel, seq=seq, tm=tm, final_norm=final_norm)
    return pl.pallas_call(
        kern,
        grid=(t // tm, nff // tn),
        in_specs=[
            pl.BlockSpec((tm, d), lambda i, j: (i, 0)),
            pl.BlockSpec((8, d), lambda i, j: (jnp.maximum(i * rb - 1, 0), 0)),
            pl.BlockSpec((8, d), lambda i, j: (jnp.minimum((i + 1) * rb, last8), 0)),
            pl.BlockSpec((1, d), lambda i, j: (0, 0)),
            pl.BlockSpec((d, tn), lambda i, j: (0, j)),
            pl.BlockSpec((d, tn), lambda i, j: (0, j)),
            pl.BlockSpec((3, tn), lambda i, j: (0, j)),
            pl.BlockSpec((tn, d), lambda i, j: (j, 0)),
            pl.BlockSpec((1, d), lambda i, j: (0, 0)),
        ],
        out_specs=pl.BlockSpec((tm, d), lambda i, j: (i, 0)),
        out_shape=jax.ShapeDtypeStruct((t, d), jnp.float32),
        scratch_shapes=[pltpu.VMEM((tm + 16, d), jnp.bfloat16), pltpu.VMEM((tm, d), jnp.float32)],
        compiler_params=_cparams("parallel", "arbitrary"),
        name="ffn",
    )(x2d, x2d, x2d, nw, wg, wu, cw, wd, fw)


def _prepare_layer(l, norm1_w, w_in, qk_conv_w, diff_lambda, ret_decay_logit, mlstm_gate_bias, attn_sink,
                   w_out, norm2_w, w_gate, w_up, ffn_conv_w, w_down):
    bf16 = jnp.bfloat16
    f32 = jnp.float32
    wi = w_in[l]
    w_pb = jnp.concatenate([wi[:, _OFF_A:_OFF_CQ], wi[:, _OFF_CV:_OFF_GATES], wi[:, _OFF_D:N_IN]], axis=1)
    wgates = jnp.pad(wi[:, _OFF_GATES:_OFF_D], ((0, 0), (0, LANES - N_GATES)))
    w_pf = jnp.concatenate([wi[:, _OFF_CQ:_OFF_CV], wgates], axis=1)
    gbias = mlstm_gate_bias[l].astype(f32)
    dec = jnp.zeros((8, LANES), f32).at[0:2, 0:B_HEADS].set(ret_decay_logit[l].astype(f32))
    return dict(
        norm1=norm1_w[l].reshape(1, D_MODEL).astype(f32),
        w_pb=w_pb.astype(bf16), w_pf=w_pf.astype(bf16),
        wq_conv=qk_conv_w[l][:, :C_HEADS * C_DIM].astype(f32),
        wk_conv=qk_conv_w[l][:, C_HEADS * C_DIM:].astype(f32),
        gbias=gbias, dec=dec,
        diff_lambda=diff_lambda[l].astype(f32), sink=attn_sink[l].astype(f32),
        w_out=w_out[l].astype(bf16),
        norm2=norm2_w[l].reshape(1, D_MODEL).astype(f32),
        w_gate=w_gate[l].astype(bf16), w_up=w_up[l].astype(bf16),
        ffn_conv=ffn_conv_w[l].astype(f32), w_down=w_down[l].astype(bf16),
    )


def _tile(n, pref):
    t = min(pref, n)
    while n % t:
        t //= 2
    return t


def _layer(x2d, bsz, seq, layer_idx, p, slopes, final_w, final_norm):
    t = bsz * seq
    tm_in = _tile(t, 1024)
    pb, pf = _in_proj(x2d, p["norm1"], p["w_pb"], p["w_pf"], tm_in, _IN_PROJ_TN)
    pb = pb.reshape(bsz, seq, PB_WIDTH)
    pf = pf.reshape(bsz, seq, PF_WIDTH)
    gates_t = pf[:, :, _PF_GATES * LANES:_PF_GATES * LANES + N_GATES].reshape(bsz, seq, 4, C_HEADS)
    gates_t = jnp.transpose(gates_t, (0, 3, 2, 1)).reshape(bsz, C_HEADS, 4, seq // BLOCK, BLOCK)
    y_a = _diff_attention(pb, p["diff_lambda"], slopes[D_HEADS:], layer_idx,
                          _tile(seq // _ATTN_TILES_PER_BODY, _ATTN_TILE),
                          _tile(seq // _ATTN_TILES_PER_BODY, _ATTN_TILE))
    y_b = _retention(pb, p["dec"])
    y_c = _mlstm(pb, pf, gates_t, p["wq_conv"], p["wk_conv"], p["gbias"])
    y_d = _window_attention(pb, p["sink"], slopes[:D_HEADS])
    ys = [y.reshape(t, 512) for y in (y_a, y_b, y_c, y_d)]
    x1 = _out_proj(x2d, ys, p["w_out"], _tile(t, 512))
    return _ffn(x1, seq, p["norm2"], p["w_gate"], p["w_up"], p["ffn_conv"], p["w_down"], final_w,
                final_norm, _tile(seq, 512), 512)


def _trunk(x, layers, slopes, final_w):
    bsz, seq, d = x.shape
    x2d = x.reshape(bsz * seq, d)
    for l, p in enumerate(layers):
        x2d = _layer(x2d, bsz, seq, l, p, slopes, final_w, l == len(layers) - 1)
    return x2d.reshape(bsz, seq, d)


def kernel(x_prompt, x_sample, norm1_w, w_in, qk_conv_w, diff_lambda, ret_decay_logit, mlstm_gate_bias,
           attn_sink, w_out, norm2_w, w_gate, w_up, ffn_conv_w, w_down, final_norm_w):
    layers = [_prepare_layer(l, norm1_w, w_in, qk_conv_w, diff_lambda, ret_decay_logit, mlstm_gate_bias,
                             attn_sink, w_out, norm2_w, w_gate, w_up, ffn_conv_w, w_down)
              for l in range(DEPTH)]
    n = A_HEADS + D_HEADS
    slopes = 2.0 ** (-8.0 * jnp.arange(1, n + 1, dtype=jnp.float32) / n)
    final_w = final_norm_w.reshape(1, D_MODEL).astype(jnp.float32)
    return (_trunk(x_prompt, layers, slopes, final_w), _trunk(x_sample, layers, slopes, final_w))
```

```python
import functools
import math

import jax
import jax.numpy as jnp
from jax import lax
from jax.experimental import pallas as pl
from jax.experimental.pallas import tpu as pltpu

D_MODEL = 2048
DEPTH = 2
A_HEADS = 4
A_QK_DIM = 64
A_V_DIM = 128
B_HEADS = 4
B_DIM = 128
C_HEADS = 4
C_DIM = 128
D_HEADS = 8
D_KV_HEADS = 2
D_DIM = 64
WINDOW = 128
BLOCK = 128
D_FF = 5632
EPS = 1e-6
LANES = 128

_OFF_A = 0
_OFF_B = 1536
_OFF_CQ = 3584
_OFF_CV = 4608
_OFF_GATES = 5632
_OFF_D = 5648
N_IN = 6416

_PB_AQ, _PB_AK, _PB_AV = 0, 4, 8
_PB_BQ, _PB_BK, _PB_BV, _PB_BG = 12, 16, 20, 24
_PB_CV, _PB_CO = 28, 32
_PB_DQ512 = 9
_PB_DK, _PB_DV = 40, 41
PB_WIDTH = 42 * LANES
_IN_PROJ_TN = 768
_PF_CQ, _PF_CK, _PF_GATES = 0, 4, 8
PF_WIDTH = 9 * LANES
N_GATES = 16

_VMEM_LIMIT = 56 * 1024 * 1024
_SCAN_UNROLL = 4
_NEG_INF = float("-inf")


def _cparams(*sem):
    return pltpu.CompilerParams(dimension_semantics=sem, vmem_limit_bytes=_VMEM_LIMIT)


def _dot(a, b):
    return jnp.dot(a, b, preferred_element_type=jnp.float32)


def _dot_nt(a, b):
    return lax.dot_general(a, b, (((1,), (1,)), ((), ())), preferred_element_type=jnp.float32)


def _dot_tn(a, b):
    return lax.dot_general(a, b, (((0,), (0,)), ((), ())), preferred_element_type=jnp.float32)


def _log_sigmoid(x):
    return -(jnp.maximum(-x, 0.0) + jnp.log1p(jnp.exp(-jnp.abs(x))))


def _split_hi_lo(x):
    hi = x.astype(jnp.bfloat16)
    lo = (x - hi.astype(jnp.float32)).astype(jnp.bfloat16)
    return hi, lo


def _in_proj_kernel(x_ref, nw_ref, wb_ref, wf_ref, ob_ref, of_ref, h_ref, *, nb):
    j = pl.program_id(1)

    @pl.when(j == 0)
    def _():
        x = x_ref[...]
        ms = jnp.mean(x * x, axis=-1, keepdims=True)
        h_ref[...] = (x * lax.rsqrt(ms + EPS) * nw_ref[...]).astype(h_ref.dtype)

    @pl.when(j < nb)
    def _():
        ob_ref[...] = _dot(h_ref[...], wb_ref[...]).astype(ob_ref.dtype)

    @pl.when(j == nb)
    def _():
        of_ref[...] = _dot(h_ref[...], wf_ref[...])


def _in_proj(x2d, nw, w_pb, w_pf, tm, tn):
    t, d = x2d.shape
    nb = PB_WIDTH // tn
    return pl.pallas_call(
        functools.partial(_in_proj_kernel, nb=nb),
        grid=(t // tm, nb + 1),
        in_specs=[
            pl.BlockSpec((tm, d), lambda i, j: (i, 0)),
            pl.BlockSpec((1, d), lambda i, j: (0, 0)),
            pl.BlockSpec((d, tn), lambda i, j: (0, jnp.minimum(j, nb - 1))),
            pl.BlockSpec((d, PF_WIDTH), lambda i, j: (0, 0)),
        ],
        out_specs=[pl.BlockSpec((tm, tn), lambda i, j: (i, jnp.minimum(j, nb - 1))),
                   pl.BlockSpec((tm, PF_WIDTH), lambda i, j: (i, 0))],
        out_shape=[jax.ShapeDtypeStruct((t, PB_WIDTH), jnp.bfloat16),
                   jax.ShapeDtypeStruct((t, PF_WIDTH), jnp.float32)],
        scratch_shapes=[pltpu.VMEM((tm, d), jnp.bfloat16)],
        compiler_params=_cparams("parallel", "arbitrary"),
        name="norm_in_proj",
    )(x2d, nw, w_pb, w_pf)


_LOG2E = 1.4426950408889634
_N_BIAS_LANES = 3
_V_ROWS = A_V_DIM + 16
_ATTN_TILES_PER_BODY = 4
_ATTN_TILE = 512


def _diff_attn_kernel(slopes_ref, q_ref, qn_ref, k_ref, v_ref, dl_ref, o_ref,
                      kx_ref, vt_ref, qt_ref, m_ref, acc_ref, s_ref, cmax_ref, bias_ref, *, seq, tq, tk,
                      lam_init):
    head = pl.program_id(1)
    qi = pl.program_id(2)
    f32 = jnp.float32
    bf16 = jnp.bfloat16
    sigma = slopes_ref[head] * _LOG2E
    nk = seq // tk
    prep_rows = tk

    @pl.when(qi == 0)
    def _():
        lane = lax.broadcasted_iota(jnp.int32, (prep_rows, LANES), 1)
        row = lax.broadcasted_iota(jnp.int32, (prep_rows, LANES), 0)
        ones_row = jnp.where(lax.broadcasted_iota(jnp.int32, (_V_ROWS - A_V_DIM, tk), 0) == 0, 1.0, 0.0)

        def prep(c, carry):
            r0 = pl.multiple_of(c * prep_rows, prep_rows)
            kk = k_ref[pl.ds(r0, prep_rows), :].astype(f32)
            col = sigma * (row + r0).astype(f32)
            c1 = col.astype(bf16).astype(f32)
            c2 = (col - c1).astype(bf16).astype(f32)
            c3 = col - c1 - c2
            bias = jnp.where(lane == A_QK_DIM, c1, jnp.where(lane == A_QK_DIM + 1, c2,
                             jnp.where(lane == A_QK_DIM + 2, c3, 0.0)))
            kx_ref[0, pl.ds(r0, prep_rows), :] = jnp.where(lane < A_QK_DIM, kk, bias).astype(bf16)
            kx_ref[1, pl.ds(r0, prep_rows), :] = jnp.where(
                lane < A_QK_DIM, pltpu.roll(kk, A_QK_DIM, 1), bias).astype(bf16)
            vt_ref[c, 0:A_V_DIM, :] = v_ref[pl.ds(r0, prep_rows), :].astype(f32).T.astype(bf16)
            vt_ref[c, A_V_DIM:_V_ROWS, :] = ones_row.astype(bf16)
            return carry

        lax.fori_loop(0, nk, prep, 0)
        rel = (lax.broadcasted_iota(jnp.int32, (tk, tq), 1)
               - lax.broadcasted_iota(jnp.int32, (tk, tq), 0)).astype(f32)
        bias_ref[...] = sigma * jnp.abs(rel)

    qrow = lax.broadcasted_iota(jnp.int32, (LANES, tq), 0)
    bias_rows = lax.broadcasted_iota(jnp.int32, (16, tq), 0) < _N_BIAS_LANES

    def build_q_variants(src_ref, base_slot):
        qt = (src_ref[...].astype(f32) * (A_QK_DIM ** -0.5 * _LOG2E)).T
        for mp in range(2):
            qm = qt if mp == 0 else pltpu.roll(qt, A_QK_DIM, 0)
            base = jnp.where(qrow < A_QK_DIM, qm, 0.0).astype(bf16)
            for var, sign in enumerate((1.0, 0.0, -1.0)):
                qt_ref[base_slot + 3 * mp + var] = base
                qt_ref[base_slot + 3 * mp + var, A_QK_DIM:A_QK_DIM + 16, :] = (
                    jnp.where(bias_rows, sign, 0.0).astype(bf16))

    def diagonal_scores(tile, base_slot, mp):
        k_start = pl.multiple_of(tile * tk, tk)
        s = _dot(kx_ref[mp, pl.ds(k_start, tk), :], qt_ref[base_slot + 3 * mp + 1]) - bias_ref[...]
        s_ref[0, mp] = s
        cmax_ref[0, mp] = jnp.max(s, axis=0, keepdims=True)

    qslot = (qi % 2) * 6
    qslot_next = 6 - qslot

    @pl.when(qi == 0)
    def _():
        build_q_variants(q_ref, 0)
        for mp in range(2):
            diagonal_scores(0, 0, mp)

    q0 = qi * tq
    rho = sigma * (lax.broadcasted_iota(jnp.int32, (1, tq), 1) + q0).astype(f32)
    m_ref[...] = jnp.full(m_ref.shape, _NEG_INF, f32)
    acc_ref[...] = jnp.zeros(acc_ref.shape, f32)

    kd = q0 // tk

    def key_tile(j):
        jm = jnp.minimum(j, nk - 1) - 1
        return jnp.where(j == 0, kd, jnp.where(jm < kd, jm, jm + 1))

    def scores(j, slot, mp):
        kb = key_tile(j)
        var = jnp.where(kb < kd, 0, jnp.where(kb > kd, 2, 1))
        k_start = pl.multiple_of(kb * tk, tk)
        s = _dot(kx_ref[mp, pl.ds(k_start, tk), :], qt_ref[qslot + 3 * mp + var])
        s_ref[slot, mp] = s
        cmax_ref[slot, mp] = jnp.max(s, axis=0, keepdims=True)

    def softmax_pv(j, slot, mp):
        kb = key_tile(j)
        shift = jnp.where(kb < kd, -1.0, jnp.where(kb > kd, 1.0, 0.0)) * rho
        m_old = m_ref[mp]
        m_new = jnp.maximum(m_old, cmax_ref[slot, mp] + shift)
        p = jnp.exp2(s_ref[slot, mp] - (m_new - shift))
        acc_ref[mp] = jnp.exp2(m_old - m_new) * acc_ref[mp] + _dot(vt_ref[kb], p.astype(bf16))
        m_ref[mp] = m_new

    def body(i, carry):
        for u in range(_ATTN_TILES_PER_BODY):
            j = _ATTN_TILES_PER_BODY * i + u
            for mp in range(2):
                scores(j + 1, (u + 1) % _ATTN_TILES_PER_BODY, mp)
                softmax_pv(j, u, mp)
        return carry

    n_bodies = nk // _ATTN_TILES_PER_BODY
    lax.fori_loop(0, n_bodies - 1, body, 0)

    build_q_variants(qn_ref, qslot_next)
    kd_next = jnp.minimum(qi + 1, seq // tq - 1)
    for u in range(_ATTN_TILES_PER_BODY):
        j = _ATTN_TILES_PER_BODY * (n_bodies - 1) + u
        for mp in range(2):
            if u + 1 < _ATTN_TILES_PER_BODY:
                scores(j + 1, u + 1, mp)
            else:
                diagonal_scores(kd_next, qslot_next, mp)
            softmax_pv(j, u, mp)

    dl = dl_ref[...]
    lam = (jnp.exp(jnp.sum(dl[0:1] * dl[1:2], axis=1, keepdims=True))
           - jnp.exp(jnp.sum(dl[2:3] * dl[3:4], axis=1, keepdims=True)) + lam_init)
    a0 = acc_ref[0]
    a1 = acc_ref[1]
    ot = (a0[0:A_V_DIM] / a0[A_V_DIM:A_V_DIM + 1] - lam * (a1[0:A_V_DIM] / a1[A_V_DIM:A_V_DIM + 1]))
    ot = ot * lax.rsqrt(jnp.mean(ot * ot, axis=0, keepdims=True) + EPS)
    o_ref[...] = (ot * (1.0 - lam_init)).T.astype(o_ref.dtype)


def _diff_attention(pb, diff_lambda, slopes_a, layer_idx, tq, tk):
    bsz, seq, _ = pb.shape
    assert seq % (_ATTN_TILES_PER_BODY * tk) == 0 and tk == tq
    lam_init = 0.8 - 0.6 * math.exp(-0.3 * layer_idx)
    kern = functools.partial(_diff_attn_kernel, seq=seq, tq=tq, tk=tk, lam_init=lam_init)
    return pl.pallas_call(
        kern,
        grid=(bsz, A_HEADS, seq // tq),
        in_specs=[
            pl.BlockSpec(memory_space=pltpu.SMEM),
            pl.BlockSpec((None, tq, LANES), lambda b, h, i: (b, i, _PB_AQ + h)),
            pl.BlockSpec((None, tq, LANES), lambda b, h, i: (b, jnp.minimum(i + 1, seq // tq - 1), _PB_AQ + h)),
            pl.BlockSpec((None, seq, LANES), lambda b, h, i: (b, 0, _PB_AK + h)),
            pl.BlockSpec((None, seq, LANES), lambda b, h, i: (b, 0, _PB_AV + h)),
            pl.BlockSpec((4, A_QK_DIM), lambda b, h, i: (0, 0)),
        ],
        out_specs=pl.BlockSpec((None, tq, LANES), lambda b, h, i: (b, i, h)),
        out_shape=jax.ShapeDtypeStruct((bsz, seq, A_HEADS * A_V_DIM), jnp.bfloat16),
        scratch_shapes=[
            pltpu.VMEM((2, seq, LANES), jnp.bfloat16),
            pltpu.VMEM((seq // tk, _V_ROWS, tk), jnp.bfloat16),
            pltpu.VMEM((12, LANES, tq), jnp.bfloat16),
            pltpu.VMEM((2, 1, tq), jnp.float32),
            pltpu.VMEM((2, _V_ROWS, tq), jnp.float32),
            pltpu.VMEM((_ATTN_TILES_PER_BODY, 2, tk, tq), jnp.float32),
            pltpu.VMEM((_ATTN_TILES_PER_BODY, 2, 1, tq), jnp.float32),
            pltpu.VMEM((tk, tq), jnp.float32),
        ],
        compiler_params=_cparams("parallel", "parallel", "arbitrary"),
        name="diff_attention",
    )(slopes_a, pb, pb, pb, pb, diff_lambda)


def _retention_kernel(q_ref, k_ref, v_ref, g_ref, dec_ref, o_ref, of_ref, ob_ref, *, seq):
    head = pl.program_id(1)
    nc = seq // BLOCK
    f32 = jnp.float32

    lgv = _log_sigmoid(dec_ref[...])
    rr = lax.broadcasted_iota(jnp.int32, lgv.shape, 0)
    cc = lax.broadcasted_iota(jnp.int32, lgv.shape, 1)

    def pick(row):
        sel = jnp.where((rr == row) & (cc == head), lgv, 0.0)
        return jnp.sum(jnp.sum(sel, axis=1, keepdims=True), axis=0, keepdims=True)

    lg_f, lg_b = pick(0), pick(1)
    ri = lax.broadcasted_iota(jnp.int32, (BLOCK, BLOCK), 0).astype(f32)
    ci = lax.broadcasted_iota(jnp.int32, (BLOCK, BLOCK), 1).astype(f32)
    diff = ri - ci
    decay_f = jnp.where(diff >= 0, jnp.exp(lg_f * jnp.maximum(diff, 0.0)), 0.0)
    decay_b = jnp.where(diff <= 0, jnp.exp(lg_b * jnp.maximum(-diff, 0.0)), 0.0)
    idx = lax.broadcasted_iota(jnp.int32, (BLOCK, 1), 0).astype(f32)
    qdec_f = jnp.exp(lg_f * (idx + 1.0))
    kdec_f = jnp.exp(lg_f * (BLOCK - 1.0 - idx))
    qdec_b = jnp.exp(lg_b * (BLOCK - idx))
    kdec_b = jnp.exp(lg_b * idx)
    cdec_f = jnp.exp(lg_f * BLOCK)
    cdec_b = jnp.exp(lg_b * BLOCK)
    kscale = B_DIM ** -0.5

    def scores_and_kv(t0, backward):
        qb = q_ref[pl.ds(t0, BLOCK), :]
        kb = k_ref[pl.ds(t0, BLOCK), :].astype(f32) * kscale
        vb = v_ref[pl.ds(t0, BLOCK), :]
        kdec = kdec_b if backward else kdec_f
        return dict(t0=t0, backward=backward, qb=qb, vb=vb,
                    qk=_dot_nt(qb, kb.astype(qb.dtype)), kv=_dot_tn((kb * kdec).astype(vb.dtype), vb))

    def inner_product(g):
        s = g["qk"] * (decay_b if g["backward"] else decay_f)
        g["inner"] = _dot(s.astype(g["vb"].dtype), g["vb"])

    def with_state(g, state, out_ref):
        qdec, cdec = (qdec_b, cdec_b) if g["backward"] else (qdec_f, cdec_f)
        cross = _dot(g["qb"], state.astype(g["qb"].dtype)) * qdec
        out_ref[pl.ds(g["t0"], BLOCK), :] = g["inner"] + cross
        return state * cdec + g["kv"]

    def body(it, carry):
        rf, rb = carry
        steps = []
        for u in range(_SCAN_UNROLL):
            c = it * _SCAN_UNROLL + u
            steps.append(scores_and_kv(pl.multiple_of(c * BLOCK, BLOCK), False))
            steps.append(scores_and_kv(pl.multiple_of((nc - 1 - c) * BLOCK, BLOCK), True))
        for g in steps:
            inner_product(g)
        for g in steps:
            if g["backward"]:
                rb = with_state(g, rb, ob_ref)
            else:
                rf = with_state(g, rf, of_ref)
        return rf, rb

    zero = jnp.zeros((B_DIM, B_DIM), f32)
    lax.fori_loop(0, nc // _SCAN_UNROLL, body, (zero, zero))

    def fin(c, carry):
        t0 = pl.multiple_of(c * BLOCK, BLOCK)
        r = of_ref[pl.ds(t0, BLOCK), :] + ob_ref[pl.ds(t0, BLOCK), :]
        r = r * lax.rsqrt(jnp.mean(r * r, axis=-1, keepdims=True) + EPS)
        g = g_ref[pl.ds(t0, BLOCK), :].astype(f32)
        o_ref[pl.ds(t0, BLOCK), :] = (g * jax.nn.sigmoid(g) * r).astype(o_ref.dtype)
        return carry

    lax.fori_loop(0, nc, fin, 0, unroll=4)


def _retention(pb, dec_pad):
    bsz, seq, _ = pb.shape
    kern = functools.partial(_retention_kernel, seq=seq)

    def col(base):
        return pl.BlockSpec((None, seq, LANES), lambda b, h: (b, 0, base + h))

    return pl.pallas_call(
        kern,
        grid=(bsz, B_HEADS),
        in_specs=[col(_PB_BQ), col(_PB_BK), col(_PB_BV), col(_PB_BG),
                  pl.BlockSpec((8, LANES), lambda b, h: (0, 0))],
        out_specs=pl.BlockSpec((None, seq, LANES), lambda b, h: (b, 0, h)),
        out_shape=jax.ShapeDtypeStruct((bsz, seq, B_HEADS * B_DIM), jnp.bfloat16),
        scratch_shapes=[pltpu.VMEM((seq, B_DIM), jnp.float32), pltpu.VMEM((seq, B_DIM), jnp.float32)],
        compiler_params=_cparams("parallel", "parallel"),
        name="retention",
    )(pb, pb, pb, pb, dec_pad)


_ROWS_PER_DIR = 6


def _mlstm_kernel(cq_ref, ck_ref, gt_ref, v_ref, og_ref, wq_ref, wk_ref, gb_ref, o_ref,
                  qc_ref, kt_ref, vx_ref, rows_ref, of_ref, ob_ref, *, seq):
    nc = seq // BLOCK
    f32 = jnp.float32
    bf16 = jnp.bfloat16
    rowi = lax.broadcasted_iota(jnp.int32, (BLOCK, LANES), 0)
    coli = lax.broadcasted_iota(jnp.int32, (BLOCK, LANES), 1)
    lane_n = lax.broadcasted_iota(jnp.int32, (nc, LANES), 1)
    tril = rowi >= coli
    triu = rowi <= coli
    tril_b = tril.astype(bf16)
    triu_b = triu.astype(bf16)
    ones_b = jnp.ones((BLOCK, LANES), bf16)
    head = pl.program_id(1)

    def cummax_lanes(x, backward):
        sh = 1
        while sh < LANES:
            if backward:
                x = jnp.maximum(x, jnp.where(lane_n < LANES - sh, pltpu.roll(x, LANES - sh, 1), _NEG_INF))
            else:
                x = jnp.maximum(x, jnp.where(lane_n >= sh, pltpu.roll(x, sh, 1), _NEG_INF))
            sh *= 2
        return x

    for d, (op, last) in enumerate(((triu_b, LANES - 1), (tril_b, 0))):
        lf = _log_sigmoid(gt_ref[2 * d + 1] + gb_ref[2 * d + 1, head]) * _LOG2E
        hi, lo = _split_hi_lo(lf)
        b = _dot(hi, op) + _dot(lo, op)
        a = (gt_ref[2 * d] + gb_ref[2 * d, head]) * _LOG2E - b
        bl = jnp.broadcast_to(b[:, last:last + 1], (nc, LANES))
        ml = jnp.broadcast_to(jnp.max(bl + a, axis=1, keepdims=True), (nc, LANES))
        for r, val in enumerate((a, cummax_lanes(a, d == 1), b, jnp.exp2(bl + a - ml), bl, ml)):
            rows_ref[d * _ROWS_PER_DIR + r] = val

    def conv_silu(src_ref, w, c, t0):
        x = src_ref[pl.ds(t0, BLOCK), :]
        prev = src_ref[pl.ds(jnp.maximum(t0 - 1, 0), 1), :] * jnp.where(c > 0, 1.0, 0.0)
        nxt = src_ref[pl.ds(jnp.minimum(t0 + BLOCK, seq - 1), 1), :] * jnp.where(c < nc - 1, 1.0, 0.0)
        xd = jnp.where(rowi == 0, prev, pltpu.roll(x, 1, 0))
        xu = jnp.where(rowi == BLOCK - 1, nxt, pltpu.roll(x, BLOCK - 1, 0))
        y = xd * w[0:1] + x * w[1:2] + xu * w[2:3]
        return y * jax.nn.sigmoid(y)

    def prep(c, carry):
        t0 = pl.multiple_of(c * BLOCK, BLOCK)
        qc_ref[pl.ds(t0, BLOCK), :] = (conv_silu(cq_ref, wq_ref[...], c, t0) * (C_DIM ** -0.5)).astype(bf16)
        kt_ref[c] = conv_silu(ck_ref, wk_ref[...], c, t0).T.astype(bf16)
        vx_ref[pl.ds(t0, BLOCK), 0:C_DIM] = v_ref[pl.ds(t0, BLOCK), :]
        vx_ref[pl.ds(t0, BLOCK), C_DIM:2 * C_DIM] = ones_b
        return carry

    lax.fori_loop(0, nc, prep, 0, unroll=2)

    def load_rows(ch, backward):
        t0 = pl.multiple_of(ch * BLOCK, BLOCK)
        base = _ROWS_PER_DIR if backward else 0
        a, c, b, w, bl, ml = (rows_ref[base + r, pl.ds(ch, 1), :] for r in range(_ROWS_PER_DIR))
        kt = kt_ref[ch]
        return dict(t0=t0, backward=backward, bl=bl[:, 0:1], m_loc=ml[:, 0:1],
                    a_t=jnp.where(triu if backward else tril, a, _NEG_INF),
                    b_full=jnp.broadcast_to(b, (BLOCK, LANES)).T,
                    c_full=jnp.broadcast_to(c, (BLOCK, LANES)).T,
                    kw=(kt.astype(f32) * w).astype(bf16), kt=kt)

    def first_products(g):
        vx = vx_ref[pl.ds(g["t0"], BLOCK), :]
        g["qk"] = _dot(qc_ref[pl.ds(g["t0"], BLOCK), :], g["kt"])
        g["kv"] = _dot(g["kw"], vx)

    def decayed_scores(g):
        g["s0"] = (g["qk"] * jnp.exp2(g["a_t"] - g["c_full"])).astype(bf16)

    def inner_products(g):
        g["p0"] = _dot(g["s0"], vx_ref[pl.ds(g["t0"], BLOCK), :])

    def with_state(g, state, out_ref):
        cx, m = state
        x = _dot(qc_ref[pl.ds(g["t0"], BLOCK), :], cx.astype(bf16))
        n = jnp.maximum(g["c_full"], m)
        e = jnp.exp2(g["c_full"] - n)
        iw = jnp.exp2(m - n)
        num = e * g["p0"][:, 0:C_DIM] + iw * x[:, 0:C_DIM]
        den = e * g["p0"][:, C_DIM:] + iw * x[:, C_DIM:]
        out_ref[pl.ds(g["t0"], BLOCK), :] = num / jnp.maximum(jnp.abs(den), jnp.exp2(-(g["b_full"] + n)))
        m_new = jnp.maximum(g["bl"] + m, g["m_loc"])
        cw = jnp.exp2(g["bl"] + m - m_new)
        f = jnp.exp2(g["m_loc"] - m_new)
        return cw * cx + f * g["kv"], m_new

    def body(it, carry):
        sf, sb = carry
        steps = []
        for u in range(_SCAN_UNROLL):
            c = it * _SCAN_UNROLL + u
            steps.append(load_rows(c, False))
            steps.append(load_rows(nc - 1 - c, True))
        for g in steps:
            first_products(g)
        for g in steps:
            decayed_scores(g)
        for g in steps:
            inner_products(g)
        for g in steps:
            if g["backward"]:
                sb = with_state(g, sb, ob_ref)
            else:
                sf = with_state(g, sf, of_ref)
        return sf, sb

    init = (jnp.zeros((C_DIM, 2 * C_DIM), f32), jnp.zeros((1, 1), f32))
    lax.fori_loop(0, nc // _SCAN_UNROLL, body, (init, init))

    def fin(c, carry):
        t0 = pl.multiple_of(c * BLOCK, BLOCK)
        r = of_ref[pl.ds(t0, BLOCK), :] + ob_ref[pl.ds(t0, BLOCK), :]
        r = r * lax.rsqrt(jnp.mean(r * r, axis=-1, keepdims=True) + EPS)
        og = og_ref[pl.ds(t0, BLOCK), :].astype(f32)
        o_ref[pl.ds(t0, BLOCK), :] = (jax.nn.sigmoid(og) * r).astype(o_ref.dtype)
        return carry

    lax.fori_loop(0, nc, fin, 0, unroll=4)


def _mlstm(pb, pf, gates_t, wq_conv, wk_conv, gate_bias):
    bsz, seq, _ = pb.shape
    assert seq % (8 * BLOCK) == 0
    kern = functools.partial(_mlstm_kernel, seq=seq)

    def col(base, buffers=2):
        return pl.BlockSpec((None, seq, LANES), lambda b, h: (b, 0, base + h),
                            pipeline_mode=pl.Buffered(buffers))

    return pl.pallas_call(
        kern,
        grid=(bsz, C_HEADS),
        in_specs=[col(_PF_CQ, 1), col(_PF_CK, 1),
                  pl.BlockSpec((None, None, 4, seq // BLOCK, LANES), lambda b, h: (b, h, 0, 0, 0)),
                  col(_PB_CV), col(_PB_CO),
                  pl.BlockSpec((3, LANES), lambda b, h: (0, h)),
                  pl.BlockSpec((3, LANES), lambda b, h: (0, h)),
                  pl.BlockSpec(memory_space=pltpu.SMEM)],
        out_specs=pl.BlockSpec((None, seq, LANES), lambda b, h: (b, 0, h)),
        out_shape=jax.ShapeDtypeStruct((bsz, seq, C_HEADS * C_DIM), jnp.bfloat16),
        scratch_shapes=[pltpu.VMEM((seq, C_DIM), jnp.bfloat16),
                        pltpu.VMEM((seq // BLOCK, C_DIM, BLOCK), jnp.bfloat16),
                        pltpu.VMEM((seq, 2 * C_DIM), jnp.bfloat16),
                        pltpu.VMEM((2 * _ROWS_PER_DIR, seq // BLOCK, LANES), jnp.float32),
                        pltpu.VMEM((seq, C_DIM), jnp.float32), pltpu.VMEM((seq, C_DIM), jnp.float32)],
        compiler_params=_cparams("parallel", "parallel"),
        name="mlstm",
    )(pf, pf, gates_t, pb, pb, wq_conv, wk_conv, gate_bias)


def _window_attn_kernel(slopes_ref, sink_ref, q_ref, k_ref, v_ref, o_ref, kx_ref, vx_ref, *, seq):
    qi = pl.program_id(1)
    f32 = jnp.float32
    span = BLOCK + 2 * WINDOW
    group = D_HEADS // D_KV_HEADS

    @pl.when(qi == 0)
    def _():
        lane = lax.broadcasted_iota(jnp.int32, (seq, LANES), 1)
        ones = jnp.ones((seq, LANES), vx_ref.dtype)
        for src, dst in ((k_ref, kx_ref), (v_ref, vx_ref)):
            x = src[...].astype(f32)
            x0 = jnp.where(lane < D_DIM, x, 0.0)
            x1 = jnp.where(lane >= D_DIM, x, 0.0)
            for i, xi in enumerate((x0, pltpu.roll(x0, D_DIM, 1), pltpu.roll(x1, D_DIM, 1), x1)):
                dst[i, :, 0:LANES] = xi.astype(dst.dtype)
        for i in range(4):
            vx_ref[i, :, LANES:2 * LANES] = ones

    t0 = qi * BLOCK
    ws = pl.multiple_of(jnp.clip(t0 - WINDOW, 0, seq - span), BLOCK)
    rel = (lax.broadcasted_iota(jnp.int32, (BLOCK, span), 0)
           - lax.broadcasted_iota(jnp.int32, (BLOCK, span), 1)) + (t0 - ws)
    dist = jnp.abs(rel)
    neg_dist = jnp.where(dist <= WINDOW, dist.astype(f32) * (-_LOG2E), _NEG_INF)

    n_pairs = D_HEADS // 2
    lookahead = 2
    scores = {}

    def score_pair(pair):
        kv = (2 * pair) // group
        qp = (q_ref[:, pair * LANES:(pair + 1) * LANES].astype(f32) * (D_DIM ** -0.5 * _LOG2E)).astype(q_ref.dtype)
        for half in range(2):
            scores[2 * pair + half] = _dot_nt(qp, kx_ref[2 * kv + half, pl.ds(ws, span), :])

    def attend(hd, kv_slot):
        s = scores[hd] + slopes_ref[hd] * neg_dist
        snk = sink_ref[hd] * _LOG2E
        m = jnp.maximum(jnp.max(s, axis=1, keepdims=True), snk)
        o = _dot(jnp.exp2(s - m).astype(jnp.bfloat16), vx_ref[kv_slot, pl.ds(ws, span), :])
        return o[:, 0:LANES] / (o[:, LANES:] + jnp.exp2(snk - m))

    for pair in range(lookahead):
        score_pair(pair)
    for pair in range(n_pairs):
        kv = (2 * pair) // group
        out = attend(2 * pair, 2 * kv) + attend(2 * pair + 1, 2 * kv + 1)
        o_ref[:, pair * LANES:(pair + 1) * LANES] = out.astype(o_ref.dtype)
        if pair + lookahead < n_pairs:
            score_pair(pair + lookahead)


def _window_attention(pb, attn_sink, slopes_d):
    bsz, seq, _ = pb.shape
    kern = functools.partial(_window_attn_kernel, seq=seq)
    width = D_HEADS * D_DIM
    return pl.pallas_call(
        kern,
        grid=(bsz, seq // BLOCK),
        in_specs=[
            pl.BlockSpec(memory_space=pltpu.SMEM),
            pl.BlockSpec(memory_space=pltpu.SMEM),
            pl.BlockSpec((None, BLOCK, width), lambda b, i: (b, i, _PB_DQ512)),
            pl.BlockSpec((None, seq, LANES), lambda b, i: (b, 0, _PB_DK)),
            pl.BlockSpec((None, seq, LANES), lambda b, i: (b, 0, _PB_DV)),
        ],
        out_specs=pl.BlockSpec((None, BLOCK, width), lambda b, i: (b, i, 0)),
        out_shape=jax.ShapeDtypeStruct((bsz, seq, width), jnp.bfloat16),
        scratch_shapes=[pltpu.VMEM((4, seq, LANES), jnp.bfloat16), pltpu.VMEM((4, seq, 2 * LANES), jnp.bfloat16)],
        compiler_params=_cparams("parallel", "arbitrary"),
        name="window_attention",
    )(slopes_d, attn_sink, pb, pb, pb)


def _out_proj_kernel(x_ref, ya_ref, yb_ref, yc_ref, yd_ref, w_ref, o_ref):
    acc = x_ref[...]
    for g, y_ref in enumerate((ya_ref, yb_ref, yc_ref, yd_ref)):
        acc = acc + _dot(y_ref[...], w_ref[g * 512:(g + 1) * 512, :])
    o_ref[...] = acc


def _out_proj(x2d, ys, w_out, tm):
    t, d = x2d.shape
    yspec = pl.BlockSpec((tm, 512), lambda i: (i, 0))
    return pl.pallas_call(
        _out_proj_kernel,
        grid=(t // tm,),
        in_specs=[pl.BlockSpec((tm, d), lambda i: (i, 0)), yspec, yspec, yspec, yspec,
                  pl.BlockSpec((d, d), lambda i: (0, 0))],
        out_specs=pl.BlockSpec((tm, d), lambda i: (i, 0)),
        out_shape=jax.ShapeDtypeStruct((t, d), jnp.float32),
        compiler_params=_cparams("parallel"),
        name="out_proj",
    )(x2d, *ys, w_out)


def _ffn_kernel(x_ref, xp_ref, xn_ref, nw_ref, wg_ref, wu_ref, cw_ref, wd_ref, fw_ref, o_ref,
                h_ref, acc_ref, *, seq, tm, final_norm):
    i = pl.program_id(0)
    j = pl.program_id(1)
    f32 = jnp.float32

    def norm(x):
        return x * lax.rsqrt(jnp.mean(x * x, axis=-1, keepdims=True) + EPS) * nw_ref[...]

    @pl.when(j == 0)
    def _():
        h_ref[0:tm, :] = norm(x_ref[...]).astype(h_ref.dtype)
        has_prev = ((i * tm) % seq != 0).astype(f32)
        has_next = (((i + 1) * tm) % seq != 0).astype(f32)
        halo = jnp.concatenate([norm(xp_ref[...]) * has_prev, norm(xn_ref[...]) * has_next], axis=0)
        h_ref[tm:tm + 16, :] = halo.astype(h_ref.dtype)
        acc_ref[...] = jnp.zeros(acc_ref.shape, f32)

    gate = _dot(h_ref[...], wg_ref[...])
    gm = gate[0:tm]
    rowi = lax.broadcasted_iota(jnp.int32, gm.shape, 0)
    g_dn = jnp.where(rowi == 0, gate[tm + 7:tm + 8], pltpu.roll(gm, 1, 0))
    g_up = jnp.where(rowi == tm - 1, gate[tm + 8:tm + 9], pltpu.roll(gm, tm - 1, 0))
    cw = cw_ref[...]
    g = g_dn * cw[0:1] + gm * cw[1:2] + g_up * cw[2:3]
    up = _dot(h_ref[0:tm, :], wu_ref[...])
    act = (g * jax.nn.sigmoid(g) * up).astype(wd_ref.dtype)
    acc_ref[...] += _dot(act, wd_ref[...])

    @pl.when(j == pl.num_programs(1) - 1)
    def _():
        y = x_ref[...] + acc_ref[...]
        if final_norm:
            y = y * lax.rsqrt(jnp.mean(y * y, axis=-1, keepdims=True) + EPS) * fw_ref[...]
        o_ref[...] = y


def _ffn(x2d, seq, nw, wg, wu, cw, wd, fw, final_norm, tm, tn):
    t, d = x2d.shape
    nff = wg.shape[1]
    rb = tm // 8
    last8 = t // 8 - 1
    kern = functools.partial(_ffn_kernel, seq=seq, tm=tm, final_norm=final_norm)
    return pl.pallas_call(
        kern,
        grid=(t // tm, nff // tn),
        in_specs=[
            pl.BlockSpec((tm, d), lambda i, j: (i, 0)),
            pl.BlockSpec((8, d), lambda i, j: (jnp.maximum(i * rb - 1, 0), 0)),
            pl.BlockSpec((8, d), lambda i, j: (jnp.minimum((i + 1) * rb, last8), 0)),
            pl.BlockSpec((1, d), lambda i, j: (0, 0)),
            pl.BlockSpec((d, tn), lambda i, j: (0, j)),
            pl.BlockSpec((d, tn), lambda i, j: (0, j)),
            pl.BlockSpec((3, tn), lambda i, j: (0, j)),
            pl.BlockSpec((tn, d), lambda i, j: (j, 0)),
            pl.BlockSpec((1, d), lambda i, j: (0, 0)),
        ],
        out_specs=pl.BlockSpec((tm, d), lambda i, j: (i, 0)),
        out_shape=jax.ShapeDtypeStruct((t, d), jnp.float32),
        scratch_shapes=[pltpu.VMEM((tm + 16, d), jnp.bfloat16), pltpu.VMEM((tm, d), jnp.float32)],
        compiler_params=_cparams("parallel", "arbitrary"),
        name="ffn",
    )(x2d, x2d, x2d, nw, wg, wu, cw, wd, fw)


def _prepare_layer(l, norm1_w, w_in, qk_conv_w, diff_lambda, ret_decay_logit, mlstm_gate_bias, attn_sink,
                   w_out, norm2_w, w_gate, w_up, ffn_conv_w, w_down):
    bf16 = jnp.bfloat16
    f32 = jnp.float32
    wi = w_in[l]
    w_pb = jnp.concatenate([wi[:, _OFF_A:_OFF_CQ], wi[:, _OFF_CV:_OFF_GATES], wi[:, _OFF_D:N_IN]], axis=1)
    wgates = jnp.pad(wi[:, _OFF_GATES:_OFF_D], ((0, 0), (0, LANES - N_GATES)))
    w_pf = jnp.concatenate([wi[:, _OFF_CQ:_OFF_CV], wgates], axis=1)
    gbias = mlstm_gate_bias[l].astype(f32)
    dec = jnp.zeros((8, LANES), f32).at[0:2, 0:B_HEADS].set(ret_decay_logit[l].astype(f32))
    return dict(
        norm1=norm1_w[l].reshape(1, D_MODEL).astype(f32),
        w_pb=w_pb.astype(bf16), w_pf=w_pf.astype(bf16),
        wq_conv=qk_conv_w[l][:, :C_HEADS * C_DIM].astype(f32),
        wk_conv=qk_conv_w[l][:, C_HEADS * C_DIM:].astype(f32),
        gbias=gbias, dec=dec,
        diff_lambda=diff_lambda[l].astype(f32), sink=attn_sink[l].astype(f32),
        w_out=w_out[l].astype(bf16),
        norm2=norm2_w[l].reshape(1, D_MODEL).astype(f32),
        w_gate=w_gate[l].astype(bf16), w_up=w_up[l].astype(bf16),
        ffn_conv=ffn_conv_w[l].astype(f32), w_down=w_down[l].astype(bf16),
    )


def _tile(n, pref):
    t = min(pref, n)
    while n % t:
        t //= 2
    return t


def _layer(x2d, bsz, seq, layer_idx, p, slopes, final_w, final_norm):
    t = bsz * seq
    tm_in = _tile(t, 1024)
    pb, pf = _in_proj(x2d, p["norm1"], p["w_pb"], p["w_pf"], tm_in, _IN_PROJ_TN)
    pb = pb.reshape(bsz, seq, PB_WIDTH)
    pf = pf.reshape(bsz, seq, PF_WIDTH)
    gates_t = pf[:, :, _PF_GATES * LANES:_PF_GATES * LANES + N_GATES].reshape(bsz, seq, 4, C_HEADS)
    gates_t = jnp.transpose(gates_t, (0, 3, 2, 1)).reshape(bsz, C_HEADS, 4, seq // BLOCK, BLOCK)
    y_a = _diff_attention(pb, p["diff_lambda"], slopes[D_HEADS:], layer_idx,
                          _tile(seq // _ATTN_TILES_PER_BODY, _ATTN_TILE),
                          _tile(seq // _ATTN_TILES_PER_BODY, _ATTN_TILE))
    y_b = _retention(pb, p["dec"])
    y_c = _mlstm(pb, pf, gates_t, p["wq_conv"], p["wk_conv"], p["gbias"])
    y_d = _window_attention(pb, p["sink"], slopes[:D_HEADS])
    ys = [y.reshape(t, 512) for y in (y_a, y_b, y_c, y_d)]
    x1 = _out_proj(x2d, ys, p["w_out"], _tile(t, 512))
    return _ffn(x1, seq, p["norm2"], p["w_gate"], p["w_up"], p["ffn_conv"], p["w_down"], final_w,
                final_norm, _tile(seq, 512), 512)


def _trunk(x, layers, slopes, final_w):
    bsz, seq, d = x.shape
    x2d = x.reshape(bsz * seq, d)
    for l, p in enumerate(layers):
        x2d = _layer(x2d, bsz, seq, l, p, slopes, final_w, l == len(layers) - 1)
    return x2d.reshape(bsz, seq, d)


def kernel(x_prompt, x_sample, norm1_w, w_in, qk_conv_w, diff_lambda, ret_decay_logit, mlstm_gate_bias,
           attn_sink, w_out, norm2_w, w_gate, w_up, ffn_conv_w, w_down, final_norm_w):
    layers = [_prepare_layer(l, norm1_w, w_in, qk_conv_w, diff_lambda, ret_decay_logit, mlstm_gate_bias,
                             attn_sink, w_out, norm2_w, w_gate, w_up, ffn_conv_w, w_down)
              for l in range(DEPTH)]
    n = A_HEADS + D_HEADS
    slopes = 2.0 ** (-8.0 * jnp.arange(1, n + 1, dtype=jnp.float32) / n)
    final_w = final_norm_w.reshape(1, D_MODEL).astype(jnp.float32)
    return (_trunk(x_prompt, layers, slopes, final_w), _trunk(x_sample, layers, slopes, final_w))
```

```python
import functools
import math

import jax
import jax.numpy as jnp
from jax import lax
from jax.experimental import pallas as pl
from jax.experimental.pallas import tpu as pltpu

D_MODEL = 2048
DEPTH = 2
A_HEADS = 4
A_QK_DIM = 64
A_V_DIM = 128
B_HEADS = 4
B_DIM = 128
C_HEADS = 4
C_DIM = 128
D_HEADS = 8
D_KV_HEADS = 2
D_DIM = 64
WINDOW = 128
BLOCK = 128
D_FF = 5632
EPS = 1e-6
LANES = 128

_OFF_A = 0
_OFF_B = 1536
_OFF_CQ = 3584
_OFF_CV = 4608
_OFF_GATES = 5632
_OFF_D = 5648
N_IN = 6416

_PB_AQ, _PB_AK, _PB_AV = 0, 4, 8
_PB_BQ, _PB_BK, _PB_BV, _PB_BG = 12, 16, 20, 24
_PB_CV, _PB_CO = 28, 32
_PB_DQ512 = 9
_PB_DK, _PB_DV = 40, 41
PB_WIDTH = 42 * LANES
_IN_PROJ_TN = 768
_PF_CQ, _PF_CK, _PF_GATES = 0, 4, 8
PF_WIDTH = 9 * LANES
N_GATES = 16

_VMEM_LIMIT = 56 * 1024 * 1024
_SCAN_UNROLL = 4
_NEG_INF = float("-inf")


def _cparams(*sem):
    return pltpu.CompilerParams(dimension_semantics=sem, vmem_limit_bytes=_VMEM_LIMIT)


def _dot(a, b):
    return jnp.dot(a, b, preferred_element_type=jnp.float32)


def _dot_nt(a, b):
    return lax.dot_general(a, b, (((1,), (1,)), ((), ())), preferred_element_type=jnp.float32)


def _dot_tn(a, b):
    return lax.dot_general(a, b, (((0,), (0,)), ((), ())), preferred_element_type=jnp.float32)


def _log_sigmoid(x):
    return -(jnp.maximum(-x, 0.0) + jnp.log1p(jnp.exp(-jnp.abs(x))))


def _split_hi_lo(x):
    hi = x.astype(jnp.bfloat16)
    lo = (x - hi.astype(jnp.float32)).astype(jnp.bfloat16)
    return hi, lo


def _in_proj_kernel(x_ref, nw_ref, wb_ref, wf_ref, ob_ref, of_ref, h_ref, *, nb):
    j = pl.program_id(1)

    @pl.when(j == 0)
    def _():
        x = x_ref[...]
        ms = jnp.mean(x * x, axis=-1, keepdims=True)
        h_ref[...] = (x * lax.rsqrt(ms + EPS) * nw_ref[...]).astype(h_ref.dtype)

    @pl.when(j < nb)
    def _():
        ob_ref[...] = _dot(h_ref[...], wb_ref[...]).astype(ob_ref.dtype)

    @pl.when(j == nb)
    def _():
        of_ref[...] = _dot(h_ref[...], wf_ref[...])


def _in_proj(x2d, nw, w_pb, w_pf, tm, tn):
    t, d = x2d.shape
    nb = PB_WIDTH // tn
    return pl.pallas_call(
        functools.partial(_in_proj_kernel, nb=nb),
        grid=(t // tm, nb + 1),
        in_specs=[
            pl.BlockSpec((tm, d), lambda i, j: (i, 0)),
            pl.BlockSpec((1, d), lambda i, j: (0, 0)),
            pl.BlockSpec((d, tn), lambda i, j: (0, jnp.minimum(j, nb - 1))),
            pl.BlockSpec((d, PF_WIDTH), lambda i, j: (0, 0)),
        ],
        out_specs=[pl.BlockSpec((tm, tn), lambda i, j: (i, jnp.minimum(j, nb - 1))),
                   pl.BlockSpec((tm, PF_WIDTH), lambda i, j: (i, 0))],
        out_shape=[jax.ShapeDtypeStruct((t, PB_WIDTH), jnp.bfloat16),
                   jax.ShapeDtypeStruct((t, PF_WIDTH), jnp.float32)],
        scratch_shapes=[pltpu.VMEM((tm, d), jnp.bfloat16)],
        compiler_params=_cparams("parallel", "arbitrary"),
        name="norm_in_proj",
    )(x2d, nw, w_pb, w_pf)


_LOG2E = 1.4426950408889634
_N_BIAS_LANES = 3
_V_ROWS = A_V_DIM + 16
_ATTN_TILES_PER_BODY = 4
_ATTN_TILE = 512


def _diff_attn_kernel(slopes_ref, q_ref, qn_ref, k_ref, v_ref, dl_ref, o_ref,
                      kx_ref, vt_ref, qt_ref, m_ref, acc_ref, s_ref, cmax_ref, bias_ref, *, seq, tq, tk,
                      lam_init):
    head = pl.program_id(1)
    qi = pl.program_id(2)
    f32 = jnp.float32
    bf16 = jnp.bfloat16
    sigma = slopes_ref[head] * _LOG2E
    nk = seq // tk
    prep_rows = tk

    @pl.when(qi == 0)
    def _():
        lane = lax.broadcasted_iota(jnp.int32, (prep_rows, LANES), 1)
        row = lax.broadcasted_iota(jnp.int32, (prep_rows, LANES), 0)
        ones_row = jnp.where(lax.broadcasted_iota(jnp.int32, (_V_ROWS - A_V_DIM, tk), 0) == 0, 1.0, 0.0)

        def prep(c, carry):
            r0 = pl.multiple_of(c * prep_rows, prep_rows)
            kk = k_ref[pl.ds(r0, prep_rows), :].astype(f32)
            col = sigma * (row + r0).astype(f32)
            c1 = col.astype(bf16).astype(f32)
            c2 = (col - c1).astype(bf16).astype(f32)
            c3 = col - c1 - c2
            bias = jnp.where(lane == A_QK_DIM, c1, jnp.where(lane == A_QK_DIM + 1, c2,
                             jnp.where(lane == A_QK_DIM + 2, c3, 0.0)))
            kx_ref[0, pl.ds(r0, prep_rows), :] = jnp.where(lane < A_QK_DIM, kk, bias).astype(bf16)
            kx_ref[1, pl.ds(r0, prep_rows), :] = jnp.where(
                lane < A_QK_DIM, pltpu.roll(kk, A_QK_DIM, 1), bias).astype(bf16)
            vt_ref[c, 0:A_V_DIM, :] = v_ref[pl.ds(r0, prep_rows), :].astype(f32).T.astype(bf16)
            vt_ref[c, A_V_DIM:_V_ROWS, :] = ones_row.astype(bf16)
            return carry

        lax.fori_loop(0, nk, prep, 0)
        rel = (lax.broadcasted_iota(jnp.int32, (tk, tq), 1)
               - lax.broadcasted_iota(jnp.int32, (tk, tq), 0)).astype(f32)
        bias_ref[...] = sigma * jnp.abs(rel)

    qrow = lax.broadcasted_iota(jnp.int32, (LANES, tq), 0)
    bias_rows = lax.broadcasted_iota(jnp.int32, (16, tq), 0) < _N_BIAS_LANES

    def build_q_variants(src_ref, base_slot):
        qt = (src_ref[...].astype(f32) * (A_QK_DIM ** -0.5 * _LOG2E)).T
        for mp in range(2):
            qm = qt if mp == 0 else pltpu.roll(qt, A_QK_DIM, 0)
            base = jnp.where(qrow < A_QK_DIM, qm, 0.0).astype(bf16)
            for var, sign in enumerate((1.0, 0.0, -1.0)):
                qt_ref[base_slot + 3 * mp + var] = base
                qt_ref[base_slot + 3 * mp + var, A_QK_DIM:A_QK_DIM + 16, :] = (
                    jnp.where(bias_rows, sign, 0.0).astype(bf16))

    def diagonal_scores(tile, base_slot, mp):
        k_start = pl.multiple_of(tile * tk, tk)
        s = _dot(kx_ref[mp, pl.ds(k_start, tk), :], qt_ref[base_slot + 3 * mp + 1]) - bias_ref[...]
        s_ref[0, mp] = s
        cmax_ref[0, mp] = jnp.max(s, axis=0, keepdims=True)

    qslot = (qi % 2) * 6
    qslot_next = 6 - qslot

    @pl.when(qi == 0)
    def _():
        build_q_variants(q_ref, 0)
        for mp in range(2):
            diagonal_scores(0, 0, mp)

    q0 = qi * tq
    rho = sigma * (lax.broadcasted_iota(jnp.int32, (1, tq), 1) + q0).astype(f32)
    m_ref[...] = jnp.full(m_ref.shape, _NEG_INF, f32)
    acc_ref[...] = jnp.zeros(acc_ref.shape, f32)

    kd = q0 // tk

    def key_tile(j):
        jm = jnp.minimum(j, nk - 1) - 1
        return jnp.where(j == 0, kd, jnp.where(jm < kd, jm, jm + 1))

    def scores(j, slot, mp):
        kb = key_tile(j)
        var = jnp.where(kb < kd, 0, jnp.where(kb > kd, 2, 1))
        k_start = pl.multiple_of(kb * tk, tk)
        s = _dot(kx_ref[mp, pl.ds(k_start, tk), :], qt_ref[qslot + 3 * mp + var])
        s_ref[slot, mp] = s
        cmax_ref[slot, mp] = jnp.max(s, axis=0, keepdims=True)

    def softmax_pv(j, slot, mp):
        kb = key_tile(j)
        shift = jnp.where(kb < kd, -1.0, jnp.where(kb > kd, 1.0, 0.0)) * rho
        m_old = m_ref[mp]
        m_new = jnp.maximum(m_old, cmax_ref[slot, mp] + shift)
        p = jnp.exp2(s_ref[slot, mp] - (m_new - shift))
        acc_ref[mp] = jnp.exp2(m_old - m_new) * acc_ref[mp] + _dot(vt_ref[kb], p.astype(bf16))
        m_ref[mp] = m_new

    def body(i, carry):
        for u in range(_ATTN_TILES_PER_BODY):
            j = _ATTN_TILES_PER_BODY * i + u
            for mp in range(2):
                scores(j + 1, (u + 1) % _ATTN_TILES_PER_BODY, mp)
                softmax_pv(j, u, mp)
        return carry

    n_bodies = nk // _ATTN_TILES_PER_BODY
    lax.fori_loop(0, n_bodies - 1, body, 0)

    build_q_variants(qn_ref, qslot_next)
    kd_next = jnp.minimum(qi + 1, seq // tq - 1)
    for u in range(_ATTN_TILES_PER_BODY):
        j = _ATTN_TILES_PER_BODY * (n_bodies - 1) + u
        for mp in range(2):
            if u + 1 < _ATTN_TILES_PER_BODY:
                scores(j + 1, u + 1, mp)
            else:
                diagonal_scores(kd_next, qslot_next, mp)
            softmax_pv(j, u, mp)

    dl = dl_ref[...]
    lam = (jnp.exp(jnp.sum(dl[0:1] * dl[1:2], axis=1, keepdims=True))
           - jnp.exp(jnp.sum(dl[2:3] * dl[3:4], axis=1, keepdims=True)) + lam_init)
    a0 = acc_ref[0]
    a1 = acc_ref[1]
    ot = (a0[0:A_V_DIM] / a0[A_V_DIM:A_V_DIM + 1] - lam * (a1[0:A_V_DIM] / a1[A_V_DIM:A_V_DIM + 1]))
    ot = ot * lax.rsqrt(jnp.mean(ot * ot, axis=0, keepdims=True) + EPS)
    o_ref[...] = (ot * (1.0 - lam_init)).T.astype(o_ref.dtype)


def _diff_attention(pb, diff_lambda, slopes_a, layer_idx, tq, tk):
    bsz, seq, _ = pb.shape
    assert seq % (_ATTN_TILES_PER_BODY * tk) == 0 and tk == tq
    lam_init = 0.8 - 0.6 * math.exp(-0.3 * layer_idx)
    kern = functools.partial(_diff_attn_kernel, seq=seq, tq=tq, tk=tk, lam_init=lam_init)
    return pl.pallas_call(
        kern,
        grid=(bsz, A_HEADS, seq // tq),
        in_specs=[
            pl.BlockSpec(memory_space=pltpu.SMEM),
            pl.BlockSpec((None, tq, LANES), lambda b, h, i: (b, i, _PB_AQ + h)),
            pl.BlockSpec((None, tq, LANES), lambda b, h, i: (b, jnp.minimum(i + 1, seq // tq - 1), _PB_AQ + h)),
            pl.BlockSpec((None, seq, LANES), lambda b, h, i: (b, 0, _PB_AK + h)),
            pl.BlockSpec((None, seq, LANES), lambda b, h, i: (b, 0, _PB_AV + h)),
            pl.BlockSpec((4, A_QK_DIM), lambda b, h, i: (0, 0)),
        ],
        out_specs=pl.BlockSpec((None, tq, LANES), lambda b, h, i: (b, i, h)),
        out_shape=jax.ShapeDtypeStruct((bsz, seq, A_HEADS * A_V_DIM), jnp.bfloat16),
        scratch_shapes=[
            pltpu.VMEM((2, seq, LANES), jnp.bfloat16),
            pltpu.VMEM((seq // tk, _V_ROWS, tk), jnp.bfloat16),
            pltpu.VMEM((12, LANES, tq), jnp.bfloat16),
            pltpu.VMEM((2, 1, tq), jnp.float32),
            pltpu.VMEM((2, _V_ROWS, tq), jnp.float32),
            pltpu.VMEM((_ATTN_TILES_PER_BODY, 2, tk, tq), jnp.float32),
            pltpu.VMEM((_ATTN_TILES_PER_BODY, 2, 1, tq), jnp.float32),
            pltpu.VMEM((tk, tq), jnp.float32),
        ],
        compiler_params=_cparams("parallel", "parallel", "arbitrary"),
        name="diff_attention",
    )(slopes_a, pb, pb, pb, pb, diff_lambda)


def _retention_kernel(q_ref, k_ref, v_ref, g_ref, dec_ref, o_ref, of_ref, ob_ref, *, seq):
    head = pl.program_id(1)
    nc = seq // BLOCK
    f32 = jnp.float32

    lgv = _log_sigmoid(dec_ref[...])
    rr = lax.broadcasted_iota(jnp.int32, lgv.shape, 0)
    cc = lax.broadcasted_iota(jnp.int32, lgv.shape, 1)

    def pick(row):
        sel = jnp.where((rr == row) & (cc == head), lgv, 0.0)
        return jnp.sum(jnp.sum(sel, axis=1, keepdims=True), axis=0, keepdims=True)

    lg_f, lg_b = pick(0), pick(1)
    ri = lax.broadcasted_iota(jnp.int32, (BLOCK, BLOCK), 0).astype(f32)
    ci = lax.broadcasted_iota(jnp.int32, (BLOCK, BLOCK), 1).astype(f32)
    diff = ri - ci
    decay_f = jnp.where(diff >= 0, jnp.exp(lg_f * jnp.maximum(diff, 0.0)), 0.0)
    decay_b = jnp.where(diff <= 0, jnp.exp(lg_b * jnp.maximum(-diff, 0.0)), 0.0)
    idx = lax.broadcasted_iota(jnp.int32, (BLOCK, 1), 0).astype(f32)
    qdec_f = jnp.exp(lg_f * (idx + 1.0))
    kdec_f = jnp.exp(lg_f * (BLOCK - 1.0 - idx))
    qdec_b = jnp.exp(lg_b * (BLOCK - idx))
    kdec_b = jnp.exp(lg_b * idx)
    cdec_f = jnp.exp(lg_f * BLOCK)
    cdec_b = jnp.exp(lg_b * BLOCK)
    kscale = B_DIM ** -0.5

    def scores_and_kv(t0, backward):
        qb = q_ref[pl.ds(t0, BLOCK), :]
        kb = k_ref[pl.ds(t0, BLOCK), :].astype(f32) * kscale
        vb = v_ref[pl.ds(t0, BLOCK), :]
        kdec = kdec_b if backward else kdec_f
        return dict(t0=t0, backward=backward, qb=qb, vb=vb,
                    qk=_dot_nt(qb, kb.astype(qb.dtype)), kv=_dot_tn((kb * kdec).astype(vb.dtype), vb))

    def inner_product(g):
        s = g["qk"] * (decay_b if g["backward"] else decay_f)
        g["inner"] = _dot(s.astype(g["vb"].dtype), g["vb"])

    def with_state(g, state, out_ref):
        qdec, cdec = (qdec_b, cdec_b) if g["backward"] else (qdec_f, cdec_f)
        cross = _dot(g["qb"], state.astype(g["qb"].dtype)) * qdec
        out_ref[pl.ds(g["t0"], BLOCK), :] = g["inner"] + cross
        return state * cdec + g["kv"]

    def body(it, carry):
        rf, rb = carry
        steps = []
        for u in range(_SCAN_UNROLL):
            c = it * _SCAN_UNROLL + u
            steps.append(scores_and_kv(pl.multiple_of(c * BLOCK, BLOCK), False))
            steps.append(scores_and_kv(pl.multiple_of((nc - 1 - c) * BLOCK, BLOCK), True))
        for g in steps:
            inner_product(g)
        for g in steps:
            if g["backward"]:
                rb = with_state(g, rb, ob_ref)
            else:
                rf = with_state(g, rf, of_ref)
        return rf, rb

    zero = jnp.zeros((B_DIM, B_DIM), f32)
    lax.fori_loop(0, nc // _SCAN_UNROLL, body, (zero, zero))

    def fin(c, carry):
        t0 = pl.multiple_of(c * BLOCK, BLOCK)
        r = of_ref[pl.ds(t0, BLOCK), :] + ob_ref[pl.ds(t0, BLOCK), :]
        r = r * lax.rsqrt(jnp.mean(r * r, axis=-1, keepdims=True) + EPS)
        g = g_ref[pl.ds(t0, BLOCK), :].astype(f32)
        o_ref[pl.ds(t0, BLOCK), :] = (g * jax.nn.sigmoid(g) * r).astype(o_ref.dtype)
        return carry

    lax.fori_loop(0, nc, fin, 0, unroll=4)


def _retention(pb, dec_pad):
    bsz, seq, _ = pb.shape
    kern = functools.partial(_retention_kernel, seq=seq)

    def col(base):
        return pl.BlockSpec((None, seq, LANES), lambda b, h: (b, 0, base + h))

    return pl.pallas_call(
        kern,
        grid=(bsz, B_HEADS),
        in_specs=[col(_PB_BQ), col(_PB_BK), col(_PB_BV), col(_PB_BG),
                  pl.BlockSpec((8, LANES), lambda b, h: (0, 0))],
        out_specs=pl.BlockSpec((None, seq, LANES), lambda b, h: (b, 0, h)),
        out_shape=jax.ShapeDtypeStruct((bsz, seq, B_HEADS * B_DIM), jnp.bfloat16),
        scratch_shapes=[pltpu.VMEM((seq, B_DIM), jnp.float32), pltpu.VMEM((seq, B_DIM), jnp.float32)],
        compiler_params=_cparams("parallel", "parallel"),
        name="retention",
    )(pb, pb, pb, pb, dec_pad)


_ROWS_PER_DIR = 6


def _mlstm_kernel(cq_ref, ck_ref, gt_ref, v_ref, og_ref, wq_ref, wk_ref, gb_ref, o_ref,
                  qc_ref, kt_ref, vx_ref, rows_ref, of_ref, ob_ref, *, seq):
    nc = seq // BLOCK
    f32 = jnp.float32
    bf16 = jnp.bfloat16
    rowi = lax.broadcasted_iota(jnp.int32, (BLOCK, LANES), 0)
    coli = lax.broadcasted_iota(jnp.int32, (BLOCK, LANES), 1)
    lane_n = lax.broadcasted_iota(jnp.int32, (nc, LANES), 1)
    tril = rowi >= coli
    triu = rowi <= coli
    tril_b = tril.astype(bf16)
    triu_b = triu.astype(bf16)
    ones_b = jnp.ones((BLOCK, LANES), bf16)
    head = pl.program_id(1)

    def cummax_lanes(x, backward):
        sh = 1
        while sh < LANES:
            if backward:
                x = jnp.maximum(x, jnp.where(lane_n < LANES - sh, pltpu.roll(x, LANES - sh, 1), _NEG_INF))
            else:
                x = jnp.maximum(x, jnp.where(lane_n >= sh, pltpu.roll(x, sh, 1), _NEG_INF))
            sh *= 2
        return x

    for d, (op, last) in enumerate(((triu_b, LANES - 1), (tril_b, 0))):
        lf = _log_sigmoid(gt_ref[2 * d + 1] + gb_ref[2 * d + 1, head]) * _LOG2E
        hi, lo = _split_hi_lo(lf)
        b = _dot(hi, op) + _dot(lo, op)
        a = (gt_ref[2 * d] + gb_ref[2 * d, head]) * _LOG2E - b
        bl = jnp.broadcast_to(b[:, last:last + 1], (nc, LANES))
        ml = jnp.broadcast_to(jnp.max(bl + a, axis=1, keepdims=True), (nc, LANES))
        for r, val in enumerate((a, cummax_lanes(a, d == 1), b, jnp.exp2(bl + a - ml), bl, ml)):
            rows_ref[d * _ROWS_PER_DIR + r] = val

    def conv_silu(src_ref, w, c, t0):
        x = src_ref[pl.ds(t0, BLOCK), :]
        prev = src_ref[pl.ds(jnp.maximum(t0 - 1, 0), 1), :] * jnp.where(c > 0, 1.0, 0.0)
        nxt = src_ref[pl.ds(jnp.minimum(t0 + BLOCK, seq - 1), 1), :] * jnp.where(c < nc - 1, 1.0, 0.0)
        xd = jnp.where(rowi == 0, prev, pltpu.roll(x, 1, 0))
        xu = jnp.where(rowi == BLOCK - 1, nxt, pltpu.roll(x, BLOCK - 1, 0))
        y = xd * w[0:1] + x * w[1:2] + xu * w[2:3]
        return y * jax.nn.sigmoid(y)

    def prep(c, carry):
        t0 = pl.multiple_of(c * BLOCK, BLOCK)
        qc_ref[pl.ds(t0, BLOCK), :] = (conv_silu(cq_ref, wq_ref[...], c, t0) * (C_DIM ** -0.5)).astype(bf16)
        kt_ref[c] = conv_silu(ck_ref, wk_ref[...], c, t0).T.astype(bf16)
        vx_ref[pl.ds(t0, BLOCK), 0:C_DIM] = v_ref[pl.ds(t0, BLOCK), :]
        vx_ref[pl.ds(t0, BLOCK), C_DIM:2 * C_DIM] = ones_b
        return carry

    lax.fori_loop(0, nc, prep, 0, unroll=2)

    def load_rows(ch, backward):
        t0 = pl.multiple_of(ch * BLOCK, BLOCK)
        base = _ROWS_PER_DIR if backward else 0
        a, c, b, w, bl, ml = (rows_ref[base + r, pl.ds(ch, 1), :] for r in range(_ROWS_PER_DIR))
        kt = kt_ref[ch]
        return dict(t0=t0, backward=backward, bl=bl[:, 0:1], m_loc=ml[:, 0:1],
                    a_t=jnp.where(triu if backward else tril, a, _NEG_INF),
                    b_full=jnp.broadcast_to(b, (BLOCK, LANES)).T,
                    c_full=jnp.broadcast_to(c, (BLOCK, LANES)).T,
                    kw=(kt.astype(f32) * w).astype(bf16), kt=kt)

    def first_products(g):
        vx = vx_ref[pl.ds(g["t0"], BLOCK), :]
        g["qk"] = _dot(qc_ref[pl.ds(g["t0"], BLOCK), :], g["kt"])
        g["kv"] = _dot(g["kw"], vx)

    def decayed_scores(g):
        g["s0"] = (g["qk"] * jnp.exp2(g["a_t"] - g["c_full"])).astype(bf16)

    def inner_products(g):
        g["p0"] = _dot(g["s0"], vx_ref[pl.ds(g["t0"], BLOCK), :])

    def with_state(g, state, out_ref):
        cx, m = state
        x = _dot(qc_ref[pl.ds(g["t0"], BLOCK), :], cx.astype(bf16))
        n = jnp.maximum(g["c_full"], m)
        e = jnp.exp2(g["c_full"] - n)
        iw = jnp.exp2(m - n)
        num = e * g["p0"][:, 0:C_DIM] + iw * x[:, 0:C_DIM]
        den = e * g["p0"][:, C_DIM:] + iw * x[:, C_DIM:]
        out_ref[pl.ds(g["t0"], BLOCK), :] = num / jnp.maximum(jnp.abs(den), jnp.exp2(-(g["b_full"] + n)))
        m_new = jnp.maximum(g["bl"] + m, g["m_loc"])
        cw = jnp.exp2(g["bl"] + m - m_new)
        f = jnp.exp2(g["m_loc"] - m_new)
        return cw * cx + f * g["kv"], m_new

    def body(it, carry):
        sf, sb = carry
        steps = []
        for u in range(_SCAN_UNROLL):
            c = it * _SCAN_UNROLL + u
            steps.append(load_rows(c, False))
            steps.append(load_rows(nc - 1 - c, True))
        for g in steps:
            first_products(g)
        for g in steps:
            decayed_scores(g)
        for g in steps:
            inner_products(g)
        for g in steps:
            if g["backward"]:
                sb = with_state(g, sb, ob_ref)
            else:
                sf = with_state(g, sf, of_ref)
        return sf, sb

    init = (jnp.zeros((C_DIM, 2 * C_DIM), f32), jnp.zeros((1, 1), f32))
    lax.fori_loop(0, nc // _SCAN_UNROLL, body, (init, init))

    def fin(c, carry):
        t0 = pl.multiple_of(c * BLOCK, BLOCK)
        r = of_ref[pl.ds(t0, BLOCK), :] + ob_ref[pl.ds(t0, BLOCK), :]
        r = r * lax.rsqrt(jnp.mean(r * r, axis=-1, keepdims=True) + EPS)
        og = og_ref[pl.ds(t0, BLOCK), :].astype(f32)
        o_ref[pl.ds(t0, BLOCK), :] = (jax.nn.sigmoid(og) * r).astype(o_ref.dtype)
        return carry

    lax.fori_loop(0, nc, fin, 0, unroll=4)


def _mlstm(pb, pf, gates_t, wq_conv, wk_conv, gate_bias):
    bsz, seq, _ = pb.shape
    assert seq % (8 * BLOCK) == 0
    kern = functools.partial(_mlstm_kernel, seq=seq)

    def col(base):
        return pl.BlockSpec((None, seq, LANES), lambda b, h: (b, 0, base + h))

    return pl.pallas_call(
        kern,
        grid=(bsz, C_HEADS),
        in_specs=[col(_PF_CQ), col(_PF_CK),
                  pl.BlockSpec((None, None, 4, seq // BLOCK, LANES), lambda b, h: (b, h, 0, 0, 0)),
                  col(_PB_CV), col(_PB_CO),
                  pl.BlockSpec((3, LANES), lambda b, h: (0, h)),
                  pl.BlockSpec((3, LANES), lambda b, h: (0, h)),
                  pl.BlockSpec(memory_space=pltpu.SMEM)],
        out_specs=pl.BlockSpec((None, seq, LANES), lambda b, h: (b, 0, h)),
        out_shape=jax.ShapeDtypeStruct((bsz, seq, C_HEADS * C_DIM), jnp.bfloat16),
        scratch_shapes=[pltpu.VMEM((seq, C_DIM), jnp.bfloat16),
                        pltpu.VMEM((seq // BLOCK, C_DIM, BLOCK), jnp.bfloat16),
                        pltpu.VMEM((seq, 2 * C_DIM), jnp.bfloat16),
                        pltpu.VMEM((2 * _ROWS_PER_DIR, seq // BLOCK, LANES), jnp.float32),
                        pltpu.VMEM((seq, C_DIM), jnp.float32), pltpu.VMEM((seq, C_DIM), jnp.float32)],
        compiler_params=_cparams("parallel", "parallel"),
        name="mlstm",
    )(pf, pf, gates_t, pb, pb, wq_conv, wk_conv, gate_bias)


def _window_attn_kernel(slopes_ref, sink_ref, q_ref, k_ref, v_ref, o_ref, kx_ref, vx_ref, *, seq):
    qi = pl.program_id(1)
    f32 = jnp.float32
    span = BLOCK + 2 * WINDOW
    group = D_HEADS // D_KV_HEADS

    @pl.when(qi == 0)
    def _():
        lane = lax.broadcasted_iota(jnp.int32, (seq, LANES), 1)
        ones = jnp.ones((seq, LANES), vx_ref.dtype)
        for src, dst in ((k_ref, kx_ref), (v_ref, vx_ref)):
            x = src[...].astype(f32)
            x0 = jnp.where(lane < D_DIM, x, 0.0)
            x1 = jnp.where(lane >= D_DIM, x, 0.0)
            for i, xi in enumerate((x0, pltpu.roll(x0, D_DIM, 1), pltpu.roll(x1, D_DIM, 1), x1)):
                dst[i, :, 0:LANES] = xi.astype(dst.dtype)
        for i in range(4):
            vx_ref[i, :, LANES:2 * LANES] = ones

    t0 = qi * BLOCK
    ws = pl.multiple_of(jnp.clip(t0 - WINDOW, 0, seq - span), BLOCK)
    rel = (lax.broadcasted_iota(jnp.int32, (BLOCK, span), 0)
           - lax.broadcasted_iota(jnp.int32, (BLOCK, span), 1)) + (t0 - ws)
    dist = jnp.abs(rel)
    neg_dist = jnp.where(dist <= WINDOW, dist.astype(f32) * (-_LOG2E), _NEG_INF)

    n_pairs = D_HEADS // 2
    lookahead = 2
    scores = {}

    def score_pair(pair):
        kv = (2 * pair) // group
        qp = (q_ref[:, pair * LANES:(pair + 1) * LANES].astype(f32) * (D_DIM ** -0.5 * _LOG2E)).astype(q_ref.dtype)
        for half in range(2):
            scores[2 * pair + half] = _dot_nt(qp, kx_ref[2 * kv + half, pl.ds(ws, span), :])

    def attend(hd, kv_slot):
        s = scores[hd] + slopes_ref[hd] * neg_dist
        snk = sink_ref[hd] * _LOG2E
        m = jnp.maximum(jnp.max(s, axis=1, keepdims=True), snk)
        o = _dot(jnp.exp2(s - m).astype(jnp.bfloat16), vx_ref[kv_slot, pl.ds(ws, span), :])
        return o[:, 0:LANES] / (o[:, LANES:] + jnp.exp2(snk - m))

    for pair in range(lookahead):
        score_pair(pair)
    for pair in range(n_pairs):
        kv = (2 * pair) // group
        out = attend(2 * pair, 2 * kv) + attend(2 * pair + 1, 2 * kv + 1)
        o_ref[:, pair * LANES:(pair + 1) * LANES] = out.astype(o_ref.dtype)
        if pair + lookahead < n_pairs:
            score_pair(pair + lookahead)


def _window_attention(pb, attn_sink, slopes_d):
    bsz, seq, _ = pb.shape
    kern = functools.partial(_window_attn_kernel, seq=seq)
    width = D_HEADS * D_DIM
    return pl.pallas_call(
        kern,
        grid=(bsz, seq // BLOCK),
        in_specs=[
            pl.BlockSpec(memory_space=pltpu.SMEM),
            pl.BlockSpec(memory_space=pltpu.SMEM),
            pl.BlockSpec((None, BLOCK, width), lambda b, i: (b, i, _PB_DQ512)),
            pl.BlockSpec((None, seq, LANES), lambda b, i: (b, 0, _PB_DK)),
            pl.BlockSpec((None, seq, LANES), lambda b, i: (b, 0, _PB_DV)),
        ],
        out_specs=pl.BlockSpec((None, BLOCK, width), lambda b, i: (b, i, 0)),
        out_shape=jax.ShapeDtypeStruct((bsz, seq, width), jnp.bfloat16),
        scratch_shapes=[pltpu.VMEM((4, seq, LANES), jnp.bfloat16), pltpu.VMEM((4, seq, 2 * LANES), jnp.bfloat16)],
        compiler_params=_cparams("parallel", "arbitrary"),
        name="window_attention",
    )(slopes_d, attn_sink, pb, pb, pb)


def _out_proj_kernel(x_ref, ya_ref, yb_ref, yc_ref, yd_ref, w_ref, o_ref):
    acc = x_ref[...]
    for g, y_ref in enumerate((ya_ref, yb_ref, yc_ref, yd_ref)):
        acc = acc + _dot(y_ref[...], w_ref[g * 512:(g + 1) * 512, :])
    o_ref[...] = acc


def _out_proj(x2d, ys, w_out, tm):
    t, d = x2d.shape
    yspec = pl.BlockSpec((tm, 512), lambda i: (i, 0))
    return pl.pallas_call(
        _out_proj_kernel,
        grid=(t // tm,),
        in_specs=[pl.BlockSpec((tm, d), lambda i: (i, 0)), yspec, yspec, yspec, yspec,
                  pl.BlockSpec((d, d), lambda i: (0, 0))],
        out_specs=pl.BlockSpec((tm, d), lambda i: (i, 0)),
        out_shape=jax.ShapeDtypeStruct((t, d), jnp.float32),
        compiler_params=_cparams("parallel"),
        name="out_proj",
    )(x2d, *ys, w_out)


def _ffn_kernel(x_ref, xp_ref, xn_ref, nw_ref, wg_ref, wu_ref, cw_ref, wd_ref, fw_ref, o_ref,
                h_ref, acc_ref, *, seq, tm, final_norm):
    i = pl.program_id(0)
    j = pl.program_id(1)
    f32 = jnp.float32

    def norm(x):
        return x * lax.rsqrt(jnp.mean(x * x, axis=-1, keepdims=True) + EPS) * nw_ref[...]

    @pl.when(j == 0)
    def _():
        h_ref[0:tm, :] = norm(x_ref[...]).astype(h_ref.dtype)
        has_prev = ((i * tm) % seq != 0).astype(f32)
        has_next = (((i + 1) * tm) % seq != 0).astype(f32)
        halo = jnp.concatenate([norm(xp_ref[...]) * has_prev, norm(xn_ref[...]) * has_next], axis=0)
        h_ref[tm:tm + 16, :] = halo.astype(h_ref.dtype)
        acc_ref[...] = jnp.zeros(acc_ref.shape, f32)

    gate = _dot(h_ref[...], wg_ref[...])
    gm = gate[0:tm]
    rowi = lax.broadcasted_iota(jnp.int32, gm.shape, 0)
    g_dn = jnp.where(rowi == 0, gate[tm + 7:tm + 8], pltpu.roll(gm, 1, 0))
    g_up = jnp.where(rowi == tm - 1, gate[tm + 8:tm + 9], pltpu.roll(gm, tm - 1, 0))
    cw = cw_ref[...]
    g = g_dn * cw[0:1] + gm * cw[1:2] + g_up * cw[2:3]
    up = _dot(h_ref[0:tm, :], wu_ref[...])
    act = (g * jax.nn.sigmoid(g) * up).astype(wd_ref.dtype)
    acc_ref[...] += _dot(act, wd_ref[...])

    @pl.when(j == pl.num_programs(1) - 1)
    def _():
        y = x_ref[...] + acc_ref[...]
        if final_norm:
            y = y * lax.rsqrt(jnp.mean(y * y, axis=-1, keepdims=True) + EPS) * fw_ref[...]
        o_ref[...] = y


def _ffn(x2d, seq, nw, wg, wu, cw, wd, fw, final_norm, tm, tn):
    t, d = x2d.shape
    nff = wg.shape[1]
    rb = tm // 8
    last8 = t // 8 - 1
    kern = functools.partial(_ffn_kernel, seq=seq, tm=tm, final_norm=final_norm)
    return pl.pallas_call(
        kern,
        grid=(t // tm, nff // tn),
        in_specs=[
            pl.BlockSpec((tm, d), lambda i, j: (i, 0)),
            pl.BlockSpec((8, d), lambda i, j: (jnp.maximum(i * rb - 1, 0), 0)),
            pl.BlockSpec((8, d), lambda i, j: (jnp.minimum((i + 1) * rb, last8), 0)),
            pl.BlockSpec((1, d), lambda i, j: (0, 0)),
            pl.BlockSpec((d, tn), lambda i, j: (0, j)),
            pl.BlockSpec((d, tn), lambda i, j: (0, j)),
            pl.BlockSpec((3, tn), lambda i, j: (0, j)),
            pl.BlockSpec((tn, d), lambda i, j: (j, 0)),
            pl.BlockSpec((1, d), lambda i, j: (0, 0)),
        ],
        out_specs=pl.BlockSpec((tm, d), lambda i, j: (i, 0)),
        out_shape=jax.ShapeDtypeStruct((t, d), jnp.float32),
        scratch_shapes=[pltpu.VMEM((tm + 16, d), jnp.bfloat16), pltpu.VMEM((tm, d), jnp.float32)],
        compiler_params=_cparams("parallel", "arbitrary"),
        name="ffn",
    )(x2d, x2d, x2d, nw, wg, wu, cw, wd, fw)


def _prepare_layer(l, norm1_w, w_in, qk_conv_w, diff_lambda, ret_decay_logit, mlstm_gate_bias, attn_sink,
                   w_out, norm2_w, w_gate, w_up, ffn_conv_w, w_down):
    bf16 = jnp.bfloat16
    f32 = jnp.float32
    wi = w_in[l]
    w_pb = jnp.concatenate([wi[:, _OFF_A:_OFF_CQ], wi[:, _OFF_CV:_OFF_GATES], wi[:, _OFF_D:N_IN]], axis=1)
    wgates = jnp.pad(wi[:, _OFF_GATES:_OFF_D], ((0, 0), (0, LANES - N_GATES)))
    w_pf = jnp.concatenate([wi[:, _OFF_CQ:_OFF_CV], wgates], axis=1)
    gbias = mlstm_gate_bias[l].astype(f32)
    dec = jnp.zeros((8, LANES), f32).at[0:2, 0:B_HEADS].set(ret_decay_logit[l].astype(f32))
    return dict(
        norm1=norm1_w[l].reshape(1, D_MODEL).astype(f32),
        w_pb=w_pb.astype(bf16), w_pf=w_pf.astype(bf16),
        wq_conv=qk_conv_w[l][:, :C_HEADS * C_DIM].astype(f32),
        wk_conv=qk_conv_w[l][:, C_HEADS * C_DIM:].astype(f32),
        gbias=gbias, dec=dec,
        diff_lambda=diff_lambda[l].astype(f32), sink=attn_sink[l].astype(f32),
        w_out=w_out[l].astype(bf16),
        norm2=norm2_w[l].reshape(1, D_MODEL).astype(f32),
        w_gate=w_gate[l].astype(bf16), w_up=w_up[l].astype(bf16),
        ffn_conv=ffn_conv_w[l].astype(f32), w_down=w_down[l].astype(bf16),
    )


def _tile(n, pref):
    t = min(pref, n)
    while n % t:
        t //= 2
    return t


def _layer(x2d, bsz, seq, layer_idx, p, slopes, final_w, final_norm):
    t = bsz * seq
    tm_in = _tile(t, 1024)
    pb, pf = _in_proj(x2d, p["norm1"], p["w_pb"], p["w_pf"], tm_in, _IN_PROJ_TN)
    pb = pb.reshape(bsz, seq, PB_WIDTH)
    pf = pf.reshape(bsz, seq, PF_WIDTH)
    gates_t = pf[:, :, _PF_GATES * LANES:_PF_GATES * LANES + N_GATES].reshape(bsz, seq, 4, C_HEADS)
    gates_t = jnp.transpose(gates_t, (0, 3, 2, 1)).reshape(bsz, C_HEADS, 4, seq // BLOCK, BLOCK)
    y_a = _diff_attention(pb, p["diff_lambda"], slopes[D_HEADS:], layer_idx,
                          _tile(seq // _ATTN_TILES_PER_BODY, _ATTN_TILE),
                          _tile(seq // _ATTN_TILES_PER_BODY, _ATTN_TILE))
    y_b = _retention(pb, p["dec"])
    y_c = _mlstm(pb, pf, gates_t, p["wq_conv"], p["wk_conv"], p["gbias"])
    y_d = _window_attention(pb, p["sink"], slopes[:D_HEADS])
    ys = [y.reshape(t, 512) for y in (y_a, y_b, y_c, y_d)]
    x1 = _out_proj(x2d, ys, p["w_out"], _tile(t, 512))
    return _ffn(x1, seq, p["norm2"], p["w_gate"], p["w_up"], p["ffn_conv"], p["w_down"], final_w,
                final_norm, _tile(seq, 512), 512)


def _trunk(x, layers, slopes, final_w):
    bsz, seq, d = x.shape
    x2d = x.reshape(bsz * seq, d)
    for l, p in enumerate(layers):
        x2d = _layer(x2d, bsz, seq, l, p, slopes, final_w, l == len(layers) - 1)
    return x2d.reshape(bsz, seq, d)


def kernel(x_prompt, x_sample, norm1_w, w_in, qk_conv_w, diff_lambda, ret_decay_logit, mlstm_gate_bias,
           attn_sink, w_out, norm2_w, w_gate, w_up, ffn_conv_w, w_down, final_norm_w):
    layers = [_prepare_layer(l, norm1_w, w_in, qk_conv_w, diff_lambda, ret_decay_logit, mlstm_gate_bias,
                             attn_sink, w_out, norm2_w, w_gate, w_up, ffn_conv_w, w_down)
              for l in range(DEPTH)]
    n = A_HEADS + D_HEADS
    slopes = 2.0 ** (-8.0 * jnp.arange(1, n + 1, dtype=jnp.float32) / n)
    final_w = final_norm_w.reshape(1, D_MODEL).astype(jnp.float32)
    return (_trunk(x_prompt, layers, slopes, final_w), _trunk(x_sample, layers, slopes, final_w))
```

```python
import functools
import math

import jax
import jax.numpy as jnp
from jax import lax
from jax.experimental import pallas as pl
from jax.experimental.pallas import tpu as pltpu

D_MODEL = 2048
DEPTH = 2
A_HEADS = 4
A_QK_DIM = 64
A_V_DIM = 128
B_HEADS = 4
B_DIM = 128
C_HEADS = 4
C_DIM = 128
D_HEADS = 8
D_KV_HEADS = 2
D_DIM = 64
WINDOW = 128
BLOCK = 128
D_FF = 5632
EPS = 1e-6
LANES = 128

_OFF_A = 0
_OFF_B = 1536
_OFF_CQ = 3584
_OFF_CV = 4608
_OFF_GATES = 5632
_OFF_D = 5648
N_IN = 6416

_PB_AQ, _PB_AK, _PB_AV = 0, 4, 8
_PB_BQ, _PB_BK, _PB_BV, _PB_BG = 12, 16, 20, 24
_PB_CV, _PB_CO = 28, 32
_PB_DQ512 = 9
_PB_DK, _PB_DV = 40, 41
PB_WIDTH = 42 * LANES
_IN_PROJ_TN = 768
_PF_CQ, _PF_CK, _PF_GATES = 0, 4, 8
PF_WIDTH = 9 * LANES
N_GATES = 16

_VMEM_LIMIT = 56 * 1024 * 1024
_FFN_VMEM_LIMIT = 61 * 1024 * 1024
_SCAN_UNROLL = 4
_NEG_INF = float("-inf")


def _cparams(*sem, vmem_limit=_VMEM_LIMIT):
    return pltpu.CompilerParams(dimension_semantics=sem, vmem_limit_bytes=vmem_limit)


def _dot(a, b):
    return jnp.dot(a, b, preferred_element_type=jnp.float32)


def _dot_nt(a, b):
    return lax.dot_general(a, b, (((1,), (1,)), ((), ())), preferred_element_type=jnp.float32)


def _dot_tn(a, b):
    return lax.dot_general(a, b, (((0,), (0,)), ((), ())), preferred_element_type=jnp.float32)


def _log_sigmoid(x):
    return -(jnp.maximum(-x, 0.0) + jnp.log1p(jnp.exp(-jnp.abs(x))))


def _split_hi_lo(x):
    hi = x.astype(jnp.bfloat16)
    lo = (x - hi.astype(jnp.float32)).astype(jnp.bfloat16)
    return hi, lo


def _in_proj_kernel(x_ref, nw_ref, wb_ref, wf_ref, ob_ref, of_ref, h_ref, *, nb):
    j = pl.program_id(1)

    @pl.when(j == 0)
    def _():
        x = x_ref[...]
        ms = jnp.mean(x * x, axis=-1, keepdims=True)
        h_ref[...] = (x * lax.rsqrt(ms + EPS) * nw_ref[...]).astype(h_ref.dtype)

    @pl.when(j < nb)
    def _():
        ob_ref[...] = _dot(h_ref[...], wb_ref[...]).astype(ob_ref.dtype)

    @pl.when(j == nb)
    def _():
        of_ref[...] = _dot(h_ref[...], wf_ref[...])


def _in_proj(x2d, nw, w_pb, w_pf, tm, tn):
    t, d = x2d.shape
    nb = PB_WIDTH // tn
    return pl.pallas_call(
        functools.partial(_in_proj_kernel, nb=nb),
        grid=(t // tm, nb + 1),
        in_specs=[
            pl.BlockSpec((tm, d), lambda i, j: (i, 0)),
            pl.BlockSpec((1, d), lambda i, j: (0, 0)),
            pl.BlockSpec((d, tn), lambda i, j: (0, jnp.minimum(j, nb - 1))),
            pl.BlockSpec((d, PF_WIDTH), lambda i, j: (0, 0)),
        ],
        out_specs=[pl.BlockSpec((tm, tn), lambda i, j: (i, jnp.minimum(j, nb - 1))),
                   pl.BlockSpec((tm, PF_WIDTH), lambda i, j: (i, 0))],
        out_shape=[jax.ShapeDtypeStruct((t, PB_WIDTH), jnp.bfloat16),
                   jax.ShapeDtypeStruct((t, PF_WIDTH), jnp.float32)],
        scratch_shapes=[pltpu.VMEM((tm, d), jnp.bfloat16)],
        compiler_params=_cparams("parallel", "arbitrary"),
        name="norm_in_proj",
    )(x2d, nw, w_pb, w_pf)


_LOG2E = 1.4426950408889634
_N_BIAS_LANES = 3
_V_ROWS = A_V_DIM + 16
_ATTN_TILES_PER_BODY = 4
_ATTN_TILE = 512


def _diff_attn_kernel(slopes_ref, q_ref, qn_ref, k_ref, v_ref, dl_ref, o_ref,
                      kx_ref, vt_ref, qt_ref, m_ref, acc_ref, s_ref, cmax_ref, bias_ref, *, seq, tq, tk,
                      lam_init):
    head = pl.program_id(1)
    qi = pl.program_id(2)
    f32 = jnp.float32
    bf16 = jnp.bfloat16
    sigma = slopes_ref[head] * _LOG2E
    nk = seq // tk
    prep_rows = tk

    @pl.when(qi == 0)
    def _():
        lane = lax.broadcasted_iota(jnp.int32, (prep_rows, LANES), 1)
        row = lax.broadcasted_iota(jnp.int32, (prep_rows, LANES), 0)
        ones_row = jnp.where(lax.broadcasted_iota(jnp.int32, (_V_ROWS - A_V_DIM, tk), 0) == 0, 1.0, 0.0)

        def prep(c, carry):
            r0 = pl.multiple_of(c * prep_rows, prep_rows)
            kk = k_ref[pl.ds(r0, prep_rows), :].astype(f32)
            col = sigma * (row + r0).astype(f32)
            c1 = col.astype(bf16).astype(f32)
            c2 = (col - c1).astype(bf16).astype(f32)
            c3 = col - c1 - c2
            bias = jnp.where(lane == A_QK_DIM, c1, jnp.where(lane == A_QK_DIM + 1, c2,
                             jnp.where(lane == A_QK_DIM + 2, c3, 0.0)))
            kx_ref[0, pl.ds(r0, prep_rows), :] = jnp.where(lane < A_QK_DIM, kk, bias).astype(bf16)
            kx_ref[1, pl.ds(r0, prep_rows), :] = jnp.where(
                lane < A_QK_DIM, pltpu.roll(kk, A_QK_DIM, 1), bias).astype(bf16)
            vt_ref[c, 0:A_V_DIM, :] = v_ref[pl.ds(r0, prep_rows), :].astype(f32).T.astype(bf16)
            vt_ref[c, A_V_DIM:_V_ROWS, :] = ones_row.astype(bf16)
            return carry

        lax.fori_loop(0, nk, prep, 0)
        rel = (lax.broadcasted_iota(jnp.int32, (tk, tq), 1)
               - lax.broadcasted_iota(jnp.int32, (tk, tq), 0)).astype(f32)
        bias_ref[...] = sigma * jnp.abs(rel)

    qrow = lax.broadcasted_iota(jnp.int32, (LANES, tq), 0)
    bias_rows = lax.broadcasted_iota(jnp.int32, (16, tq), 0) < _N_BIAS_LANES

    def build_q_variants(src_ref, base_slot):
        qt = (src_ref[...].astype(f32) * (A_QK_DIM ** -0.5 * _LOG2E)).T
        for mp in range(2):
            qm = qt if mp == 0 else pltpu.roll(qt, A_QK_DIM, 0)
            base = jnp.where(qrow < A_QK_DIM, qm, 0.0).astype(bf16)
            for var, sign in enumerate((1.0, 0.0, -1.0)):
                qt_ref[base_slot + 3 * mp + var] = base
                qt_ref[base_slot + 3 * mp + var, A_QK_DIM:A_QK_DIM + 16, :] = (
                    jnp.where(bias_rows, sign, 0.0).astype(bf16))

    def diagonal_scores(tile, base_slot, mp):
        k_start = pl.multiple_of(tile * tk, tk)
        s = _dot(kx_ref[mp, pl.ds(k_start, tk), :], qt_ref[base_slot + 3 * mp + 1]) - bias_ref[...]
        s_ref[0, mp] = s
        cmax_ref[0, mp] = jnp.max(s, axis=0, keepdims=True)

    qslot = (qi % 2) * 6
    qslot_next = 6 - qslot

    @pl.when(qi == 0)
    def _():
        build_q_variants(q_ref, 0)
        for mp in range(2):
            diagonal_scores(0, 0, mp)

    q0 = qi * tq
    rho = sigma * (lax.broadcasted_iota(jnp.int32, (1, tq), 1) + q0).astype(f32)
    m_ref[...] = jnp.full(m_ref.shape, _NEG_INF, f32)
    acc_ref[...] = jnp.zeros(acc_ref.shape, f32)

    kd = q0 // tk

    def key_tile(j):
        jm = jnp.minimum(j, nk - 1) - 1
        return jnp.where(j == 0, kd, jnp.where(jm < kd, jm, jm + 1))

    def scores(j, slot, mp):
        kb = key_tile(j)
        var = jnp.where(kb < kd, 0, jnp.where(kb > kd, 2, 1))
        k_start = pl.multiple_of(kb * tk, tk)
        s = _dot(kx_ref[mp, pl.ds(k_start, tk), :], qt_ref[qslot + 3 * mp + var])
        s_ref[slot, mp] = s
        cmax_ref[slot, mp] = jnp.max(s, axis=0, keepdims=True)

    def softmax_pv(j, slot, mp):
        kb = key_tile(j)
        shift = jnp.where(kb < kd, -1.0, jnp.where(kb > kd, 1.0, 0.0)) * rho
        m_old = m_ref[mp]
        m_new = jnp.maximum(m_old, cmax_ref[slot, mp] + shift)
        p = jnp.exp2(s_ref[slot, mp] - (m_new - shift))
        acc_ref[mp] = jnp.exp2(m_old - m_new) * acc_ref[mp] + _dot(vt_ref[kb], p.astype(bf16))
        m_ref[mp] = m_new

    def body(i, carry):
        for u in range(_ATTN_TILES_PER_BODY):
            j = _ATTN_TILES_PER_BODY * i + u
            for mp in range(2):
                scores(j + 1, (u + 1) % _ATTN_TILES_PER_BODY, mp)
                softmax_pv(j, u, mp)
        return carry

    n_bodies = nk // _ATTN_TILES_PER_BODY
    lax.fori_loop(0, n_bodies - 1, body, 0)

    build_q_variants(qn_ref, qslot_next)
    kd_next = jnp.minimum(qi + 1, seq // tq - 1)
    for u in range(_ATTN_TILES_PER_BODY):
        j = _ATTN_TILES_PER_BODY * (n_bodies - 1) + u
        for mp in range(2):
            if u + 1 < _ATTN_TILES_PER_BODY:
                scores(j + 1, u + 1, mp)
            else:
                diagonal_scores(kd_next, qslot_next, mp)
            softmax_pv(j, u, mp)

    dl = dl_ref[...]
    lam = (jnp.exp(jnp.sum(dl[0:1] * dl[1:2], axis=1, keepdims=True))
           - jnp.exp(jnp.sum(dl[2:3] * dl[3:4], axis=1, keepdims=True)) + lam_init)
    a0 = acc_ref[0]
    a1 = acc_ref[1]
    ot = (a0[0:A_V_DIM] / a0[A_V_DIM:A_V_DIM + 1] - lam * (a1[0:A_V_DIM] / a1[A_V_DIM:A_V_DIM + 1]))
    ot = ot * lax.rsqrt(jnp.mean(ot * ot, axis=0, keepdims=True) + EPS)
    o_ref[...] = (ot * (1.0 - lam_init)).T.astype(o_ref.dtype)


def _diff_attention(pb, diff_lambda, slopes_a, layer_idx, tq, tk):
    bsz, seq, _ = pb.shape
    assert seq % (_ATTN_TILES_PER_BODY * tk) == 0 and tk == tq
    lam_init = 0.8 - 0.6 * math.exp(-0.3 * layer_idx)
    kern = functools.partial(_diff_attn_kernel, seq=seq, tq=tq, tk=tk, lam_init=lam_init)
    return pl.pallas_call(
        kern,
        grid=(bsz, A_HEADS, seq // tq),
        in_specs=[
            pl.BlockSpec(memory_space=pltpu.SMEM),
            pl.BlockSpec((None, tq, LANES), lambda b, h, i: (b, i, _PB_AQ + h)),
            pl.BlockSpec((None, tq, LANES), lambda b, h, i: (b, jnp.minimum(i + 1, seq // tq - 1), _PB_AQ + h)),
            pl.BlockSpec((None, seq, LANES), lambda b, h, i: (b, 0, _PB_AK + h)),
            pl.BlockSpec((None, seq, LANES), lambda b, h, i: (b, 0, _PB_AV + h)),
            pl.BlockSpec((4, A_QK_DIM), lambda b, h, i: (0, 0)),
        ],
        out_specs=pl.BlockSpec((None, tq, LANES), lambda b, h, i: (b, i, h)),
        out_shape=jax.ShapeDtypeStruct((bsz, seq, A_HEADS * A_V_DIM), jnp.bfloat16),
        scratch_shapes=[
            pltpu.VMEM((2, seq, LANES), jnp.bfloat16),
            pltpu.VMEM((seq // tk, _V_ROWS, tk), jnp.bfloat16),
            pltpu.VMEM((12, LANES, tq), jnp.bfloat16),
            pltpu.VMEM((2, 1, tq), jnp.float32),
            pltpu.VMEM((2, _V_ROWS, tq), jnp.float32),
            pltpu.VMEM((_ATTN_TILES_PER_BODY, 2, tk, tq), jnp.float32),
            pltpu.VMEM((_ATTN_TILES_PER_BODY, 2, 1, tq), jnp.float32),
            pltpu.VMEM((tk, tq), jnp.float32),
        ],
        compiler_params=_cparams("parallel", "parallel", "arbitrary"),
        name="diff_attention",
    )(slopes_a, pb, pb, pb, pb, diff_lambda)


def _retention_kernel(q_ref, k_ref, v_ref, g_ref, dec_ref, o_ref, of_ref, ob_ref, *, seq):
    head = pl.program_id(1)
    nc = seq // BLOCK
    f32 = jnp.float32

    lgv = _log_sigmoid(dec_ref[...])
    rr = lax.broadcasted_iota(jnp.int32, lgv.shape, 0)
    cc = lax.broadcasted_iota(jnp.int32, lgv.shape, 1)

    def pick(row):
        sel = jnp.where((rr == row) & (cc == head), lgv, 0.0)
        return jnp.sum(jnp.sum(sel, axis=1, keepdims=True), axis=0, keepdims=True)

    lg_f, lg_b = pick(0), pick(1)
    ri = lax.broadcasted_iota(jnp.int32, (BLOCK, BLOCK), 0).astype(f32)
    ci = lax.broadcasted_iota(jnp.int32, (BLOCK, BLOCK), 1).astype(f32)
    diff = ri - ci
    decay_f = jnp.where(diff >= 0, jnp.exp(lg_f * jnp.maximum(diff, 0.0)), 0.0)
    decay_b = jnp.where(diff <= 0, jnp.exp(lg_b * jnp.maximum(-diff, 0.0)), 0.0)
    idx = lax.broadcasted_iota(jnp.int32, (BLOCK, 1), 0).astype(f32)
    qdec_f = jnp.exp(lg_f * (idx + 1.0))
    kdec_f = jnp.exp(lg_f * (BLOCK - 1.0 - idx))
    qdec_b = jnp.exp(lg_b * (BLOCK - idx))
    kdec_b = jnp.exp(lg_b * idx)
    cdec_f = jnp.exp(lg_f * BLOCK)
    cdec_b = jnp.exp(lg_b * BLOCK)
    kscale = B_DIM ** -0.5

    def scores_and_kv(t0, backward):
        qb = q_ref[pl.ds(t0, BLOCK), :]
        kb = k_ref[pl.ds(t0, BLOCK), :].astype(f32) * kscale
        vb = v_ref[pl.ds(t0, BLOCK), :]
        kdec = kdec_b if backward else kdec_f
        return dict(t0=t0, backward=backward, qb=qb, vb=vb,
                    qk=_dot_nt(qb, kb.astype(qb.dtype)), kv=_dot_tn((kb * kdec).astype(vb.dtype), vb))

    def inner_product(g):
        s = g["qk"] * (decay_b if g["backward"] else decay_f)
        g["inner"] = _dot(s.astype(g["vb"].dtype), g["vb"])

    def with_state(g, state, out_ref):
        qdec, cdec = (qdec_b, cdec_b) if g["backward"] else (qdec_f, cdec_f)
        cross = _dot(g["qb"], state.astype(g["qb"].dtype)) * qdec
        out_ref[pl.ds(g["t0"], BLOCK), :] = g["inner"] + cross
        return state * cdec + g["kv"]

    def body(it, carry):
        rf, rb = carry
        steps = []
        for u in range(_SCAN_UNROLL):
            c = it * _SCAN_UNROLL + u
            steps.append(scores_and_kv(pl.multiple_of(c * BLOCK, BLOCK), False))
            steps.append(scores_and_kv(pl.multiple_of((nc - 1 - c) * BLOCK, BLOCK), True))
        for g in steps:
            inner_product(g)
        for g in steps:
            if g["backward"]:
                rb = with_state(g, rb, ob_ref)
            else:
                rf = with_state(g, rf, of_ref)
        return rf, rb

    zero = jnp.zeros((B_DIM, B_DIM), f32)
    lax.fori_loop(0, nc // _SCAN_UNROLL, body, (zero, zero))

    def fin(c, carry):
        t0 = pl.multiple_of(c * BLOCK, BLOCK)
        r = of_ref[pl.ds(t0, BLOCK), :] + ob_ref[pl.ds(t0, BLOCK), :]
        r = r * lax.rsqrt(jnp.mean(r * r, axis=-1, keepdims=True) + EPS)
        g = g_ref[pl.ds(t0, BLOCK), :].astype(f32)
        o_ref[pl.ds(t0, BLOCK), :] = (g * jax.nn.sigmoid(g) * r).astype(o_ref.dtype)
        return carry

    lax.fori_loop(0, nc, fin, 0, unroll=4)


def _retention(pb, dec_pad):
    bsz, seq, _ = pb.shape
    kern = functools.partial(_retention_kernel, seq=seq)

    def col(base):
        return pl.BlockSpec((None, seq, LANES), lambda b, h: (b, 0, base + h))

    return pl.pallas_call(
        kern,
        grid=(bsz, B_HEADS),
        in_specs=[col(_PB_BQ), col(_PB_BK), col(_PB_BV), col(_PB_BG),
                  pl.BlockSpec((8, LANES), lambda b, h: (0, 0))],
        out_specs=pl.BlockSpec((None, seq, LANES), lambda b, h: (b, 0, h)),
        out_shape=jax.ShapeDtypeStruct((bsz, seq, B_HEADS * B_DIM), jnp.bfloat16),
        scratch_shapes=[pltpu.VMEM((seq, B_DIM), jnp.float32), pltpu.VMEM((seq, B_DIM), jnp.float32)],
        compiler_params=_cparams("parallel", "parallel"),
        name="retention",
    )(pb, pb, pb, pb, dec_pad)


_ROWS_PER_DIR = 6


def _mlstm_kernel(cq_ref, ck_ref, gt_ref, v_ref, og_ref, wq_ref, wk_ref, gb_ref, o_ref,
                  qc_ref, kt_ref, vx_ref, rows_ref, of_ref, ob_ref, *, seq):
    nc = seq // BLOCK
    f32 = jnp.float32
    bf16 = jnp.bfloat16
    rowi = lax.broadcasted_iota(jnp.int32, (BLOCK, LANES), 0)
    coli = lax.broadcasted_iota(jnp.int32, (BLOCK, LANES), 1)
    lane_n = lax.broadcasted_iota(jnp.int32, (nc, LANES), 1)
    tril = rowi >= coli
    triu = rowi <= coli
    tril_b = tril.astype(bf16)
    triu_b = triu.astype(bf16)
    ones_b = jnp.ones((BLOCK, LANES), bf16)
    head = pl.program_id(1)

    def cummax_lanes(x, backward):
        sh = 1
        while sh < LANES:
            if backward:
                x = jnp.maximum(x, jnp.where(lane_n < LANES - sh, pltpu.roll(x, LANES - sh, 1), _NEG_INF))
            else:
                x = jnp.maximum(x, jnp.where(lane_n >= sh, pltpu.roll(x, sh, 1), _NEG_INF))
            sh *= 2
        return x

    for d, (op, last) in enumerate(((triu_b, LANES - 1), (tril_b, 0))):
        lf = _log_sigmoid(gt_ref[2 * d + 1] + gb_ref[2 * d + 1, head]) * _LOG2E
        hi, lo = _split_hi_lo(lf)
        b = _dot(hi, op) + _dot(lo, op)
        a = (gt_ref[2 * d] + gb_ref[2 * d, head]) * _LOG2E - b
        bl = jnp.broadcast_to(b[:, last:last + 1], (nc, LANES))
        ml = jnp.broadcast_to(jnp.max(bl + a, axis=1, keepdims=True), (nc, LANES))
        for r, val in enumerate((a, cummax_lanes(a, d == 1), b, jnp.exp2(bl + a - ml), bl, ml)):
            rows_ref[d * _ROWS_PER_DIR + r] = val

    def conv_silu(src_ref, w, c, t0):
        x = src_ref[pl.ds(t0, BLOCK), :]
        prev = src_ref[pl.ds(jnp.maximum(t0 - 1, 0), 1), :] * jnp.where(c > 0, 1.0, 0.0)
        nxt = src_ref[pl.ds(jnp.minimum(t0 + BLOCK, seq - 1), 1), :] * jnp.where(c < nc - 1, 1.0, 0.0)
        xd = jnp.where(rowi == 0, prev, pltpu.roll(x, 1, 0))
        xu = jnp.where(rowi == BLOCK - 1, nxt, pltpu.roll(x, BLOCK - 1, 0))
        y = xd * w[0:1] + x * w[1:2] + xu * w[2:3]
        return y * jax.nn.sigmoid(y)

    def prep(c, carry):
        t0 = pl.multiple_of(c * BLOCK, BLOCK)
        qc_ref[pl.ds(t0, BLOCK), :] = (conv_silu(cq_ref, wq_ref[...], c, t0) * (C_DIM ** -0.5)).astype(bf16)
        kt_ref[c] = conv_silu(ck_ref, wk_ref[...], c, t0).T.astype(bf16)
        vx_ref[pl.ds(t0, BLOCK), 0:C_DIM] = v_ref[pl.ds(t0, BLOCK), :]
        vx_ref[pl.ds(t0, BLOCK), C_DIM:2 * C_DIM] = ones_b
        return carry

    lax.fori_loop(0, nc, prep, 0, unroll=2)

    def load_rows(ch, backward):
        t0 = pl.multiple_of(ch * BLOCK, BLOCK)
        base = _ROWS_PER_DIR if backward else 0
        a, c, b, w, bl, ml = (rows_ref[base + r, pl.ds(ch, 1), :] for r in range(_ROWS_PER_DIR))
        kt = kt_ref[ch]
        return dict(t0=t0, backward=backward, bl=bl[:, 0:1], m_loc=ml[:, 0:1],
                    a_t=jnp.where(triu if backward else tril, a, _NEG_INF),
                    b_full=jnp.broadcast_to(b, (BLOCK, LANES)).T,
                    c_full=jnp.broadcast_to(c, (BLOCK, LANES)).T,
                    kw=(kt.astype(f32) * w).astype(bf16), kt=kt)

    def first_products(g):
        vx = vx_ref[pl.ds(g["t0"], BLOCK), :]
        g["qk"] = _dot(qc_ref[pl.ds(g["t0"], BLOCK), :], g["kt"])
        g["kv"] = _dot(g["kw"], vx)

    def decayed_scores(g):
        g["s0"] = (g["qk"] * jnp.exp2(g["a_t"] - g["c_full"])).astype(bf16)

    def inner_products(g):
        g["p0"] = _dot(g["s0"], vx_ref[pl.ds(g["t0"], BLOCK), :])

    def with_state(g, state, out_ref):
        cx, m = state
        x = _dot(qc_ref[pl.ds(g["t0"], BLOCK), :], cx.astype(bf16))
        n = jnp.maximum(g["c_full"], m)
        e = jnp.exp2(g["c_full"] - n)
        iw = jnp.exp2(m - n)
        num = e * g["p0"][:, 0:C_DIM] + iw * x[:, 0:C_DIM]
        den = e * g["p0"][:, C_DIM:] + iw * x[:, C_DIM:]
        out_ref[pl.ds(g["t0"], BLOCK), :] = num / jnp.maximum(jnp.abs(den), jnp.exp2(-(g["b_full"] + n)))
        m_new = jnp.maximum(g["bl"] + m, g["m_loc"])
        cw = jnp.exp2(g["bl"] + m - m_new)
        f = jnp.exp2(g["m_loc"] - m_new)
        return cw * cx + f * g["kv"], m_new

    def body(it, carry):
        sf, sb = carry
        steps = []
        for u in range(_SCAN_UNROLL):
            c = it * _SCAN_UNROLL + u
            steps.append(load_rows(c, False))
            steps.append(load_rows(nc - 1 - c, True))
        for g in steps:
            first_products(g)
        for g in steps:
            decayed_scores(g)
        for g in steps:
            inner_products(g)
        for g in steps:
            if g["backward"]:
                sb = with_state(g, sb, ob_ref)
            else:
                sf = with_state(g, sf, of_ref)
        return sf, sb

    init = (jnp.zeros((C_DIM, 2 * C_DIM), f32), jnp.zeros((1, 1), f32))
    lax.fori_loop(0, nc // _SCAN_UNROLL, body, (init, init))

    def fin(c, carry):
        t0 = pl.multiple_of(c * BLOCK, BLOCK)
        r = of_ref[pl.ds(t0, BLOCK), :] + ob_ref[pl.ds(t0, BLOCK), :]
        r = r * lax.rsqrt(jnp.mean(r * r, axis=-1, keepdims=True) + EPS)
        og = og_ref[pl.ds(t0, BLOCK), :].astype(f32)
        o_ref[pl.ds(t0, BLOCK), :] = (jax.nn.sigmoid(og) * r).astype(o_ref.dtype)
        return carry

    lax.fori_loop(0, nc, fin, 0, unroll=4)


def _mlstm(pb, pf, gates_t, wq_conv, wk_conv, gate_bias):
    bsz, seq, _ = pb.shape
    assert seq % (8 * BLOCK) == 0
    kern = functools.partial(_mlstm_kernel, seq=seq)

    def col(base):
        return pl.BlockSpec((None, seq, LANES), lambda b, h: (b, 0, base + h))

    return pl.pallas_call(
        kern,
        grid=(bsz, C_HEADS),
        in_specs=[col(_PF_CQ), col(_PF_CK),
                  pl.BlockSpec((None, None, 4, seq // BLOCK, LANES), lambda b, h: (b, h, 0, 0, 0)),
                  col(_PB_CV), col(_PB_CO),
                  pl.BlockSpec((3, LANES), lambda b, h: (0, h)),
                  pl.BlockSpec((3, LANES), lambda b, h: (0, h)),
                  pl.BlockSpec(memory_space=pltpu.SMEM)],
        out_specs=pl.BlockSpec((None, seq, LANES), lambda b, h: (b, 0, h)),
        out_shape=jax.ShapeDtypeStruct((bsz, seq, C_HEADS * C_DIM), jnp.bfloat16),
        scratch_shapes=[pltpu.VMEM((seq, C_DIM), jnp.bfloat16),
                        pltpu.VMEM((seq // BLOCK, C_DIM, BLOCK), jnp.bfloat16),
                        pltpu.VMEM((seq, 2 * C_DIM), jnp.bfloat16),
                        pltpu.VMEM((2 * _ROWS_PER_DIR, seq // BLOCK, LANES), jnp.float32),
                        pltpu.VMEM((seq, C_DIM), jnp.float32), pltpu.VMEM((seq, C_DIM), jnp.float32)],
        compiler_params=_cparams("parallel", "parallel"),
        name="mlstm",
    )(pf, pf, gates_t, pb, pb, wq_conv, wk_conv, gate_bias)


def _window_attn_kernel(slopes_ref, sink_ref, q_ref, k_ref, v_ref, o_ref, kx_ref, vx_ref, *, seq):
    qi = pl.program_id(1)
    f32 = jnp.float32
    span = BLOCK + 2 * WINDOW
    group = D_HEADS // D_KV_HEADS

    @pl.when(qi == 0)
    def _():
        lane = lax.broadcasted_iota(jnp.int32, (seq, LANES), 1)
        ones = jnp.ones((seq, LANES), vx_ref.dtype)
        for src, dst in ((k_ref, kx_ref), (v_ref, vx_ref)):
            x = src[...].astype(f32)
            x0 = jnp.where(lane < D_DIM, x, 0.0)
            x1 = jnp.where(lane >= D_DIM, x, 0.0)
            for i, xi in enumerate((x0, pltpu.roll(x0, D_DIM, 1), pltpu.roll(x1, D_DIM, 1), x1)):
                dst[i, :, 0:LANES] = xi.astype(dst.dtype)
        for i in range(4):
            vx_ref[i, :, LANES:2 * LANES] = ones

    t0 = qi * BLOCK
    ws = pl.multiple_of(jnp.clip(t0 - WINDOW, 0, seq - span), BLOCK)
    rel = (lax.broadcasted_iota(jnp.int32, (BLOCK, span), 0)
           - lax.broadcasted_iota(jnp.int32, (BLOCK, span), 1)) + (t0 - ws)
    dist = jnp.abs(rel)
    neg_dist = jnp.where(dist <= WINDOW, dist.astype(f32) * (-_LOG2E), _NEG_INF)

    n_pairs = D_HEADS // 2
    lookahead = 2
    scores = {}

    def score_pair(pair):
        kv = (2 * pair) // group
        qp = (q_ref[:, pair * LANES:(pair + 1) * LANES].astype(f32) * (D_DIM ** -0.5 * _LOG2E)).astype(q_ref.dtype)
        for half in range(2):
            scores[2 * pair + half] = _dot_nt(qp, kx_ref[2 * kv + half, pl.ds(ws, span), :])

    def attend(hd, kv_slot):
        s = scores[hd] + slopes_ref[hd] * neg_dist
        snk = sink_ref[hd] * _LOG2E
        m = jnp.maximum(jnp.max(s, axis=1, keepdims=True), snk)
        o = _dot(jnp.exp2(s - m).astype(jnp.bfloat16), vx_ref[kv_slot, pl.ds(ws, span), :])
        return o[:, 0:LANES] / (o[:, LANES:] + jnp.exp2(snk - m))

    for pair in range(lookahead):
        score_pair(pair)
    for pair in range(n_pairs):
        kv = (2 * pair) // group
        out = attend(2 * pair, 2 * kv) + attend(2 * pair + 1, 2 * kv + 1)
        o_ref[:, pair * LANES:(pair + 1) * LANES] = out.astype(o_ref.dtype)
        if pair + lookahead < n_pairs:
            score_pair(pair + lookahead)


def _window_attention(pb, attn_sink, slopes_d):
    bsz, seq, _ = pb.shape
    kern = functools.partial(_window_attn_kernel, seq=seq)
    width = D_HEADS * D_DIM
    return pl.pallas_call(
        kern,
        grid=(bsz, seq // BLOCK),
        in_specs=[
            pl.BlockSpec(memory_space=pltpu.SMEM),
            pl.BlockSpec(memory_space=pltpu.SMEM),
            pl.BlockSpec((None, BLOCK, width), lambda b, i: (b, i, _PB_DQ512)),
            pl.BlockSpec((None, seq, LANES), lambda b, i: (b, 0, _PB_DK)),
            pl.BlockSpec((None, seq, LANES), lambda b, i: (b, 0, _PB_DV)),
        ],
        out_specs=pl.BlockSpec((None, BLOCK, width), lambda b, i: (b, i, 0)),
        out_shape=jax.ShapeDtypeStruct((bsz, seq, width), jnp.bfloat16),
        scratch_shapes=[pltpu.VMEM((4, seq, LANES), jnp.bfloat16), pltpu.VMEM((4, seq, 2 * LANES), jnp.bfloat16)],
        compiler_params=_cparams("parallel", "arbitrary"),
        name="window_attention",
    )(slopes_d, attn_sink, pb, pb, pb)


def _out_proj_kernel(x_ref, ya_ref, yb_ref, yc_ref, yd_ref, w_ref, o_ref):
    acc = x_ref[...]
    for g, y_ref in enumerate((ya_ref, yb_ref, yc_ref, yd_ref)):
        acc = acc + _dot(y_ref[...], w_ref[g * 512:(g + 1) * 512, :])
    o_ref[...] = acc


def _out_proj(x2d, ys, w_out, tm):
    t, d = x2d.shape
    yspec = pl.BlockSpec((tm, 512), lambda i: (i, 0))
    return pl.pallas_call(
        _out_proj_kernel,
        grid=(t // tm,),
        in_specs=[pl.BlockSpec((tm, d), lambda i: (i, 0)), yspec, yspec, yspec, yspec,
                  pl.BlockSpec((d, d), lambda i: (0, 0))],
        out_specs=pl.BlockSpec((tm, d), lambda i: (i, 0)),
        out_shape=jax.ShapeDtypeStruct((t, d), jnp.float32),
        compiler_params=_cparams("parallel"),
        name="out_proj",
    )(x2d, *ys, w_out)


def _ffn_kernel(x_ref, xp_ref, xn_ref, nw_ref, wg_ref, wu_ref, cw_ref, wd_ref, fw_ref, o_ref,
                h_ref, *, seq, tm, final_norm):
    i = pl.program_id(0)
    j = pl.program_id(1)
    f32 = jnp.float32

    def norm(x):
        return x * lax.rsqrt(jnp.mean(x * x, axis=-1, keepdims=True) + EPS) * nw_ref[...]

    @pl.when(j == 0)
    def _():
        h_ref[0:tm, :] = norm(x_ref[...]).astype(h_ref.dtype)
        has_prev = ((i * tm) % seq != 0).astype(f32)
        has_next = (((i + 1) * tm) % seq != 0).astype(f32)
        halo = jnp.concatenate([norm(xp_ref[...]) * has_prev, norm(xn_ref[...]) * has_next], axis=0)
        h_ref[tm:tm + 16, :] = halo.astype(h_ref.dtype)
        o_ref[...] = jnp.zeros(o_ref.shape, f32)

    gate = _dot(h_ref[...], wg_ref[...])
    gm = gate[0:tm]
    rowi = lax.broadcasted_iota(jnp.int32, gm.shape, 0)
    g_dn = jnp.where(rowi == 0, gate[tm + 7:tm + 8], pltpu.roll(gm, 1, 0))
    g_up = jnp.where(rowi == tm - 1, gate[tm + 8:tm + 9], pltpu.roll(gm, tm - 1, 0))
    cw = cw_ref[...]
    g = g_dn * cw[0:1] + gm * cw[1:2] + g_up * cw[2:3]
    up = _dot(h_ref[0:tm, :], wu_ref[...])
    act = (g * jax.nn.sigmoid(g) * up).astype(wd_ref.dtype)
    o_ref[...] += _dot(act, wd_ref[...])

    @pl.when(j == pl.num_programs(1) - 1)
    def _():
        y = x_ref[...] + o_ref[...]
        if final_norm:
            y = y * lax.rsqrt(jnp.mean(y * y, axis=-1, keepdims=True) + EPS) * fw_ref[...]
        o_ref[...] = y


def _ffn(x2d, seq, nw, wg, wu, cw, wd, fw, final_norm, tm, tn):
    t, d = x2d.shape
    nff = wg.shape[1]
    rb = tm // 8
    last8 = t // 8 - 1
    kern = functools.partial(_ffn_kernel, seq=seq, tm=tm, final_norm=final_norm)
    return pl.pallas_call(
        kern,
        grid=(t // tm, nff // tn),
        in_specs=[
            pl.BlockSpec((tm, d), lambda i, j: (i, 0)),
            pl.BlockSpec((8, d), lambda i, j: (jnp.maximum(i * rb - 1, 0), 0)),
            pl.BlockSpec((8, d), lambda i, j: (jnp.minimum((i + 1) * rb, last8), 0)),
            pl.BlockSpec((1, d), lambda i, j: (0, 0)),
            pl.BlockSpec((d, tn), lambda i, j: (0, j)),
            pl.BlockSpec((d, tn), lambda i, j: (0, j)),
            pl.BlockSpec((3, tn), lambda i, j: (0, j)),
            pl.BlockSpec((tn, d), lambda i, j: (j, 0)),
            pl.BlockSpec((1, d), lambda i, j: (0, 0)),
        ],
        out_specs=pl.BlockSpec((tm, d), lambda i, j: (i, 0)),
        out_shape=jax.ShapeDtypeStruct((t, d), jnp.float32),
        scratch_shapes=[pltpu.VMEM((tm + 16, d), jnp.bfloat16)],
        compiler_params=_cparams("parallel", "arbitrary", vmem_limit=_FFN_VMEM_LIMIT),
        name="ffn",
    )(x2d, x2d, x2d, nw, wg, wu, cw, wd, fw)


def _prepare_layer(l, norm1_w, w_in, qk_conv_w, diff_lambda, ret_decay_logit, mlstm_gate_bias, attn_sink,
                   w_out, norm2_w, w_gate, w_up, ffn_conv_w, w_down):
    bf16 = jnp.bfloat16
    f32 = jnp.float32
    wi = w_in[l]
    w_pb = jnp.concatenate([wi[:, _OFF_A:_OFF_CQ], wi[:, _OFF_CV:_OFF_GATES], wi[:, _OFF_D:N_IN]], axis=1)
    wgates = jnp.pad(wi[:, _OFF_GATES:_OFF_D], ((0, 0), (0, LANES - N_GATES)))
    w_pf = jnp.concatenate([wi[:, _OFF_CQ:_OFF_CV], wgates], axis=1)
    gbias = mlstm_gate_bias[l].astype(f32)
    dec = jnp.zeros((8, LANES), f32).at[0:2, 0:B_HEADS].set(ret_decay_logit[l].astype(f32))
    return dict(
        norm1=norm1_w[l].reshape(1, D_MODEL).astype(f32),
        w_pb=w_pb.astype(bf16), w_pf=w_pf.astype(bf16),
        wq_conv=qk_conv_w[l][:, :C_HEADS * C_DIM].astype(f32),
        wk_conv=qk_conv_w[l][:, C_HEADS * C_DIM:].astype(f32),
        gbias=gbias, dec=dec,
        diff_lambda=diff_lambda[l].astype(f32), sink=attn_sink[l].astype(f32),
        w_out=w_out[l].astype(bf16),
        norm2=norm2_w[l].reshape(1, D_MODEL).astype(f32),
        w_gate=w_gate[l].astype(bf16), w_up=w_up[l].astype(bf16),
        ffn_conv=ffn_conv_w[l].astype(f32), w_down=w_down[l].astype(bf16),
    )


def _tile(n, pref):
    t = min(pref, n)
    while n % t:
        t //= 2
    return t


def _layer(x2d, bsz, seq, layer_idx, p, slopes, final_w, final_norm):
    t = bsz * seq
    tm_in = _tile(t, 1024)
    pb, pf = _in_proj(x2d, p["norm1"], p["w_pb"], p["w_pf"], tm_in, _IN_PROJ_TN)
    pb = pb.reshape(bsz, seq, PB_WIDTH)
    pf = pf.reshape(bsz, seq, PF_WIDTH)
    gates_t = pf[:, :, _PF_GATES * LANES:_PF_GATES * LANES + N_GATES].reshape(bsz, seq, 4, C_HEADS)
    gates_t = jnp.transpose(gates_t, (0, 3, 2, 1)).reshape(bsz, C_HEADS, 4, seq // BLOCK, BLOCK)
    y_a = _diff_attention(pb, p["diff_lambda"], slopes[D_HEADS:], layer_idx,
                          _tile(seq // _ATTN_TILES_PER_BODY, _ATTN_TILE),
                          _tile(seq // _ATTN_TILES_PER_BODY, _ATTN_TILE))
    y_b = _retention(pb, p["dec"])
    y_c = _mlstm(pb, pf, gates_t, p["wq_conv"], p["wk_conv"], p["gbias"])
    y_d = _window_attention(pb, p["sink"], slopes[:D_HEADS])
    ys = [y.reshape(t, 512) for y in (y_a, y_b, y_c, y_d)]
    x1 = _out_proj(x2d, ys, p["w_out"], _tile(t, 512))
    return _ffn(x1, seq, p["norm2"], p["w_gate"], p["w_up"], p["ffn_conv"], p["w_down"], final_w,
                final_norm, _tile(seq, 1024), 512)


def _trunk(x, layers, slopes, final_w):
    bsz, seq, d = x.shape
    x2d = x.reshape(bsz * seq, d)
    for l, p in enumerate(layers):
        x2d = _layer(x2d, bsz, seq, l, p, slopes, final_w, l == len(layers) - 1)
    return x2d.reshape(bsz, seq, d)


def kernel(x_prompt, x_sample, norm1_w, w_in, qk_conv_w, diff_lambda, ret_decay_logit, mlstm_gate_bias,
           attn_sink, w_out, norm2_w, w_gate, w_up, ffn_conv_w, w_down, final_norm_w):
    layers = [_prepare_layer(l, norm1_w, w_in, qk_conv_w, diff_lambda, ret_decay_logit, mlstm_gate_bias,
                             attn_sink, w_out, norm2_w, w_gate, w_up, ffn_conv_w, w_down)
              for l in range(DEPTH)]
    n = A_HEADS + D_HEADS
    slopes = 2.0 ** (-8.0 * jnp.arange(1, n + 1, dtype=jnp.float32) / n)
    final_w = final_norm_w.reshape(1, D_MODEL).astype(jnp.float32)
    return (_trunk(x_prompt, layers, slopes, final_w), _trunk(x_sample, layers, slopes, final_w))
```

```python
import functools
import math

import jax
import jax.numpy as jnp
from jax import lax
from jax.experimental import pallas as pl
from jax.experimental.pallas import tpu as pltpu

D_MODEL = 2048
DEPTH = 2
A_HEADS = 4
A_QK_DIM = 64
A_V_DIM = 128
B_HEADS = 4
B_DIM = 128
C_HEADS = 4
C_DIM = 128
D_HEADS = 8
D_KV_HEADS = 2
D_DIM = 64
WINDOW = 128
BLOCK = 128
D_FF = 5632
EPS = 1e-6
LANES = 128

_OFF_A = 0
_OFF_B = 1536
_OFF_CQ = 3584
_OFF_CV = 4608
_OFF_GATES = 5632
_OFF_D = 5648
N_IN = 6416

_PB_AQ, _PB_AK, _PB_AV = 0, 4, 8
_PB_BQ, _PB_BK, _PB_BV, _PB_BG = 12, 16, 20, 24
_PB_CV, _PB_CO = 28, 32
_PB_DQ512 = 9
_PB_DK, _PB_DV = 40, 41
PB_WIDTH = 42 * LANES
_IN_PROJ_TN = 1792
_PF_CQ, _PF_CK, _PF_GATES = 0, 4, 8
PF_WIDTH = 9 * LANES
N_GATES = 16

_VMEM_LIMIT = 56 * 1024 * 1024
_BIG_VMEM_LIMIT = 61 * 1024 * 1024
_SCAN_UNROLL = 4
_NEG_INF = float("-inf")


def _cparams(*sem, vmem_limit=_VMEM_LIMIT):
    return pltpu.CompilerParams(dimension_semantics=sem, vmem_limit_bytes=vmem_limit)


def _dot(a, b):
    return jnp.dot(a, b, preferred_element_type=jnp.float32)


def _dot_nt(a, b):
    return lax.dot_general(a, b, (((1,), (1,)), ((), ())), preferred_element_type=jnp.float32)


def _dot_tn(a, b):
    return lax.dot_general(a, b, (((0,), (0,)), ((), ())), preferred_element_type=jnp.float32)


def _log_sigmoid(x):
    return -(jnp.maximum(-x, 0.0) + jnp.log1p(jnp.exp(-jnp.abs(x))))


def _split_hi_lo(x):
    hi = x.astype(jnp.bfloat16)
    lo = (x - hi.astype(jnp.float32)).astype(jnp.bfloat16)
    return hi, lo


def _in_proj_kernel(x_ref, nw_ref, wb_ref, wf_ref, ob_ref, of_ref, h_ref, *, nb):
    j = pl.program_id(1)

    @pl.when(j == 0)
    def _():
        x = x_ref[...]
        ms = jnp.mean(x * x, axis=-1, keepdims=True)
        h_ref[...] = (x * lax.rsqrt(ms + EPS) * nw_ref[...]).astype(h_ref.dtype)

    @pl.when(j < nb)
    def _():
        ob_ref[...] = _dot(h_ref[...], wb_ref[...]).astype(ob_ref.dtype)

    @pl.when(j == nb)
    def _():
        of_ref[...] = _dot(h_ref[...], wf_ref[...])


def _in_proj(x2d, nw, w_pb, w_pf, tm, tn):
    t, d = x2d.shape
    nb = PB_WIDTH // tn
    return pl.pallas_call(
        functools.partial(_in_proj_kernel, nb=nb),
        grid=(t // tm, nb + 1),
        in_specs=[
            pl.BlockSpec((tm, d), lambda i, j: (i, 0)),
            pl.BlockSpec((1, d), lambda i, j: (0, 0)),
            pl.BlockSpec((d, tn), lambda i, j: (0, jnp.minimum(j, nb - 1))),
            pl.BlockSpec((d, PF_WIDTH), lambda i, j: (0, 0), pipeline_mode=pl.Buffered(1)),
        ],
        out_specs=[pl.BlockSpec((tm, tn), lambda i, j: (i, jnp.minimum(j, nb - 1))),
                   pl.BlockSpec((tm, PF_WIDTH), lambda i, j: (i, 0))],
        out_shape=[jax.ShapeDtypeStruct((t, PB_WIDTH), jnp.bfloat16),
                   jax.ShapeDtypeStruct((t, PF_WIDTH), jnp.float32)],
        scratch_shapes=[pltpu.VMEM((tm, d), jnp.bfloat16)],
        compiler_params=_cparams("parallel", "arbitrary", vmem_limit=_BIG_VMEM_LIMIT),
        name="norm_in_proj",
    )(x2d, nw, w_pb, w_pf)


_LOG2E = 1.4426950408889634
_N_BIAS_LANES = 3
_V_ROWS = A_V_DIM + 16
_ATTN_TILES_PER_BODY = 4
_ATTN_TILE = 512


def _diff_attn_kernel(slopes_ref, q_ref, qn_ref, k_ref, v_ref, dl_ref, o_ref,
                      kx_ref, vt_ref, qt_ref, m_ref, acc_ref, s_ref, cmax_ref, bias_ref, *, seq, tq, tk,
                      lam_init):
    head = pl.program_id(1)
    qi = pl.program_id(2)
    f32 = jnp.float32
    bf16 = jnp.bfloat16
    sigma = slopes_ref[head] * _LOG2E
    nk = seq // tk
    prep_rows = tk

    @pl.when(qi == 0)
    def _():
        lane = lax.broadcasted_iota(jnp.int32, (prep_rows, LANES), 1)
        row = lax.broadcasted_iota(jnp.int32, (prep_rows, LANES), 0)
        ones_row = jnp.where(lax.broadcasted_iota(jnp.int32, (_V_ROWS - A_V_DIM, tk), 0) == 0, 1.0, 0.0)

        def prep(c, carry):
            r0 = pl.multiple_of(c * prep_rows, prep_rows)
            kk = k_ref[pl.ds(r0, prep_rows), :].astype(f32)
            col = sigma * (row + r0).astype(f32)
            c1 = col.astype(bf16).astype(f32)
            c2 = (col - c1).astype(bf16).astype(f32)
            c3 = col - c1 - c2
            bias = jnp.where(lane == A_QK_DIM, c1, jnp.where(lane == A_QK_DIM + 1, c2,
                             jnp.where(lane == A_QK_DIM + 2, c3, 0.0)))
            kx_ref[0, pl.ds(r0, prep_rows), :] = jnp.where(lane < A_QK_DIM, kk, bias).astype(bf16)
            kx_ref[1, pl.ds(r0, prep_rows), :] = jnp.where(
                lane < A_QK_DIM, pltpu.roll(kk, A_QK_DIM, 1), bias).astype(bf16)
            vt_ref[c, 0:A_V_DIM, :] = v_ref[pl.ds(r0, prep_rows), :].astype(f32).T.astype(bf16)
            vt_ref[c, A_V_DIM:_V_ROWS, :] = ones_row.astype(bf16)
            return carry

        lax.fori_loop(0, nk, prep, 0)
        rel = (lax.broadcasted_iota(jnp.int32, (tk, tq), 1)
               - lax.broadcasted_iota(jnp.int32, (tk, tq), 0)).astype(f32)
        bias_ref[...] = sigma * jnp.abs(rel)

    qrow = lax.broadcasted_iota(jnp.int32, (LANES, tq), 0)
    bias_rows = lax.broadcasted_iota(jnp.int32, (16, tq), 0) < _N_BIAS_LANES

    def build_q_variants(src_ref, base_slot):
        qt = (src_ref[...].astype(f32) * (A_QK_DIM ** -0.5 * _LOG2E)).T
        for mp in range(2):
            qm = qt if mp == 0 else pltpu.roll(qt, A_QK_DIM, 0)
            base = jnp.where(qrow < A_QK_DIM, qm, 0.0).astype(bf16)
            for var, sign in enumerate((1.0, 0.0, -1.0)):
                qt_ref[base_slot + 3 * mp + var] = base
                qt_ref[base_slot + 3 * mp + var, A_QK_DIM:A_QK_DIM + 16, :] = (
                    jnp.where(bias_rows, sign, 0.0).astype(bf16))

    def diagonal_scores(tile, base_slot, mp):
        k_start = pl.multiple_of(tile * tk, tk)
        s = _dot(kx_ref[mp, pl.ds(k_start, tk), :], qt_ref[base_slot + 3 * mp + 1]) - bias_ref[...]
        s_ref[0, mp] = s
        cmax_ref[0, mp] = jnp.max(s, axis=0, keepdims=True)

    qslot = (qi % 2) * 6
    qslot_next = 6 - qslot

    @pl.when(qi == 0)
    def _():
        build_q_variants(q_ref, 0)
        for mp in range(2):
            diagonal_scores(0, 0, mp)

    q0 = qi * tq
    rho = sigma * (lax.broadcasted_iota(jnp.int32, (1, tq), 1) + q0).astype(f32)
    m_ref[...] = jnp.full(m_ref.shape, _NEG_INF, f32)
    acc_ref[...] = jnp.zeros(acc_ref.shape, f32)

    kd = q0 // tk

    def key_tile(j):
        jm = jnp.minimum(j, nk - 1) - 1
        return jnp.where(j == 0, kd, jnp.where(jm < kd, jm, jm + 1))

    def scores(j, slot, mp):
        kb = key_tile(j)
        var = jnp.where(kb < kd, 0, jnp.where(kb > kd, 2, 1))
        k_start = pl.multiple_of(kb * tk, tk)
        s = _dot(kx_ref[mp, pl.ds(k_start, tk), :], qt_ref[qslot + 3 * mp + var])
        s_ref[slot, mp] = s
        cmax_ref[slot, mp] = jnp.max(s, axis=0, keepdims=True)

    def softmax_pv(j, slot, mp):
        kb = key_tile(j)
        shift = jnp.where(kb < kd, -1.0, jnp.where(kb > kd, 1.0, 0.0)) * rho
        m_old = m_ref[mp]
        m_new = jnp.maximum(m_old, cmax_ref[slot, mp] + shift)
        p = jnp.exp2(s_ref[slot, mp] - (m_new - shift))
        acc_ref[mp] = jnp.exp2(m_old - m_new) * acc_ref[mp] + _dot(vt_ref[kb], p.astype(bf16))
        m_ref[mp] = m_new

    def body(i, carry):
        for u in range(_ATTN_TILES_PER_BODY):
            j = _ATTN_TILES_PER_BODY * i + u
            for mp in range(2):
                scores(j + 1, (u + 1) % _ATTN_TILES_PER_BODY, mp)
                softmax_pv(j, u, mp)
        return carry

    n_bodies = nk // _ATTN_TILES_PER_BODY
    lax.fori_loop(0, n_bodies - 1, body, 0)

    build_q_variants(qn_ref, qslot_next)
    kd_next = jnp.minimum(qi + 1, seq // tq - 1)
    for u in range(_ATTN_TILES_PER_BODY):
        j = _ATTN_TILES_PER_BODY * (n_bodies - 1) + u
        for mp in range(2):
            if u + 1 < _ATTN_TILES_PER_BODY:
                scores(j + 1, u + 1, mp)
            else:
                diagonal_scores(kd_next, qslot_next, mp)
            softmax_pv(j, u, mp)

    dl = dl_ref[...]
    lam = (jnp.exp(jnp.sum(dl[0:1] * dl[1:2], axis=1, keepdims=True))
           - jnp.exp(jnp.sum(dl[2:3] * dl[3:4], axis=1, keepdims=True)) + lam_init)
    a0 = acc_ref[0]
    a1 = acc_ref[1]
    ot = (a0[0:A_V_DIM] / a0[A_V_DIM:A_V_DIM + 1] - lam * (a1[0:A_V_DIM] / a1[A_V_DIM:A_V_DIM + 1]))
    ot = ot * lax.rsqrt(jnp.mean(ot * ot, axis=0, keepdims=True) + EPS)
    o_ref[...] = (ot * (1.0 - lam_init)).T.astype(o_ref.dtype)


def _diff_attention(pb, diff_lambda, slopes_a, layer_idx, tq, tk):
    bsz, seq, _ = pb.shape
    assert seq % (_ATTN_TILES_PER_BODY * tk) == 0 and tk == tq
    lam_init = 0.8 - 0.6 * math.exp(-0.3 * layer_idx)
    kern = functools.partial(_diff_attn_kernel, seq=seq, tq=tq, tk=tk, lam_init=lam_init)
    return pl.pallas_call(
        kern,
        grid=(bsz, A_HEADS, seq // tq),
        in_specs=[
            pl.BlockSpec(memory_space=pltpu.SMEM),
            pl.BlockSpec((None, tq, LANES), lambda b, h, i: (b, i, _PB_AQ + h)),
            pl.BlockSpec((None, tq, LANES), lambda b, h, i: (b, jnp.minimum(i + 1, seq // tq - 1), _PB_AQ + h)),
            pl.BlockSpec((None, seq, LANES), lambda b, h, i: (b, 0, _PB_AK + h)),
            pl.BlockSpec((None, seq, LANES), lambda b, h, i: (b, 0, _PB_AV + h)),
            pl.BlockSpec((4, A_QK_DIM), lambda b, h, i: (0, 0)),
        ],
        out_specs=pl.BlockSpec((None, tq, LANES), lambda b, h, i: (b, i, h)),
        out_shape=jax.ShapeDtypeStruct((bsz, seq, A_HEADS * A_V_DIM), jnp.bfloat16),
        scratch_shapes=[
            pltpu.VMEM((2, seq, LANES), jnp.bfloat16),
            pltpu.VMEM((seq // tk, _V_ROWS, tk), jnp.bfloat16),
            pltpu.VMEM((12, LANES, tq), jnp.bfloat16),
            pltpu.VMEM((2, 1, tq), jnp.float32),
            pltpu.VMEM((2, _V_ROWS, tq), jnp.float32),
            pltpu.VMEM((_ATTN_TILES_PER_BODY, 2, tk, tq), jnp.float32),
            pltpu.VMEM((_ATTN_TILES_PER_BODY, 2, 1, tq), jnp.float32),
            pltpu.VMEM((tk, tq), jnp.float32),
        ],
        compiler_params=_cparams("parallel", "parallel", "arbitrary"),
        name="diff_attention",
    )(slopes_a, pb, pb, pb, pb, diff_lambda)


def _retention_kernel(q_ref, k_ref, v_ref, g_ref, dec_ref, o_ref, of_ref, ob_ref, *, seq):
    head = pl.program_id(1)
    nc = seq // BLOCK
    f32 = jnp.float32

    lgv = _log_sigmoid(dec_ref[...])
    rr = lax.broadcasted_iota(jnp.int32, lgv.shape, 0)
    cc = lax.broadcasted_iota(jnp.int32, lgv.shape, 1)

    def pick(row):
        sel = jnp.where((rr == row) & (cc == head), lgv, 0.0)
        return jnp.sum(jnp.sum(sel, axis=1, keepdims=True), axis=0, keepdims=True)

    lg_f, lg_b = pick(0), pick(1)
    ri = lax.broadcasted_iota(jnp.int32, (BLOCK, BLOCK), 0).astype(f32)
    ci = lax.broadcasted_iota(jnp.int32, (BLOCK, BLOCK), 1).astype(f32)
    diff = ri - ci
    decay_f = jnp.where(diff >= 0, jnp.exp(lg_f * jnp.maximum(diff, 0.0)), 0.0)
    decay_b = jnp.where(diff <= 0, jnp.exp(lg_b * jnp.maximum(-diff, 0.0)), 0.0)
    idx = lax.broadcasted_iota(jnp.int32, (BLOCK, 1), 0).astype(f32)
    qdec_f = jnp.exp(lg_f * (idx + 1.0))
    kdec_f = jnp.exp(lg_f * (BLOCK - 1.0 - idx))
    qdec_b = jnp.exp(lg_b * (BLOCK - idx))
    kdec_b = jnp.exp(lg_b * idx)
    cdec_f = jnp.exp(lg_f * BLOCK)
    cdec_b = jnp.exp(lg_b * BLOCK)
    kscale = B_DIM ** -0.5

    def scores_and_kv(t0, backward):
        qb = q_ref[pl.ds(t0, BLOCK), :]
        kb = k_ref[pl.ds(t0, BLOCK), :].astype(f32) * kscale
        vb = v_ref[pl.ds(t0, BLOCK), :]
        kdec = kdec_b if backward else kdec_f
        return dict(t0=t0, backward=backward, qb=qb, vb=vb,
                    qk=_dot_nt(qb, kb.astype(qb.dtype)), kv=_dot_tn((kb * kdec).astype(vb.dtype), vb))

    def inner_product(g):
        s = g["qk"] * (decay_b if g["backward"] else decay_f)
        g["inner"] = _dot(s.astype(g["vb"].dtype), g["vb"])

    def with_state(g, state, out_ref):
        qdec, cdec = (qdec_b, cdec_b) if g["backward"] else (qdec_f, cdec_f)
        cross = _dot(g["qb"], state.astype(g["qb"].dtype)) * qdec
        out_ref[pl.ds(g["t0"], BLOCK), :] = g["inner"] + cross
        return state * cdec + g["kv"]

    def body(it, carry):
        rf, rb = carry
        steps = []
        for u in range(_SCAN_UNROLL):
            c = it * _SCAN_UNROLL + u
            steps.append(scores_and_kv(pl.multiple_of(c * BLOCK, BLOCK), False))
            steps.append(scores_and_kv(pl.multiple_of((nc - 1 - c) * BLOCK, BLOCK), True))
        for g in steps:
            inner_product(g)
        for g in steps:
            if g["backward"]:
                rb = with_state(g, rb, ob_ref)
            else:
                rf = with_state(g, rf, of_ref)
        return rf, rb

    zero = jnp.zeros((B_DIM, B_DIM), f32)
    lax.fori_loop(0, nc // _SCAN_UNROLL, body, (zero, zero))

    def fin(c, carry):
        t0 = pl.multiple_of(c * BLOCK, BLOCK)
        r = of_ref[pl.ds(t0, BLOCK), :] + ob_ref[pl.ds(t0, BLOCK), :]
        r = r * lax.rsqrt(jnp.mean(r * r, axis=-1, keepdims=True) + EPS)
        g = g_ref[pl.ds(t0, BLOCK), :].astype(f32)
        o_ref[pl.ds(t0, BLOCK), :] = (g * jax.nn.sigmoid(g) * r).astype(o_ref.dtype)
        return carry

    lax.fori_loop(0, nc, fin, 0, unroll=4)


def _retention(pb, dec_pad):
    bsz, seq, _ = pb.shape
    kern = functools.partial(_retention_kernel, seq=seq)

    def col(base):
        return pl.BlockSpec((None, seq, LANES), lambda b, h: (b, 0, base + h))

    return pl.pallas_call(
        kern,
        grid=(bsz, B_HEADS),
        in_specs=[col(_PB_BQ), col(_PB_BK), col(_PB_BV), col(_PB_BG),
                  pl.BlockSpec((8, LANES), lambda b, h: (0, 0))],
        out_specs=pl.BlockSpec((None, seq, LANES), lambda b, h: (b, 0, h)),
        out_shape=jax.ShapeDtypeStruct((bsz, seq, B_HEADS * B_DIM), jnp.bfloat16),
        scratch_shapes=[pltpu.VMEM((seq, B_DIM), jnp.float32), pltpu.VMEM((seq, B_DIM), jnp.float32)],
        compiler_params=_cparams("parallel", "parallel"),
        name="retention",
    )(pb, pb, pb, pb, dec_pad)


_ROWS_PER_DIR = 6


def _mlstm_kernel(cq_ref, ck_ref, gt_ref, v_ref, og_ref, wq_ref, wk_ref, gb_ref, o_ref,
                  qc_ref, kt_ref, vx_ref, rows_ref, of_ref, ob_ref, *, seq):
    nc = seq // BLOCK
    f32 = jnp.float32
    bf16 = jnp.bfloat16
    rowi = lax.broadcasted_iota(jnp.int32, (BLOCK, LANES), 0)
    coli = lax.broadcasted_iota(jnp.int32, (BLOCK, LANES), 1)
    lane_n = lax.broadcasted_iota(jnp.int32, (nc, LANES), 1)
    tril = rowi >= coli
    triu = rowi <= coli
    tril_b = tril.astype(bf16)
    triu_b = triu.astype(bf16)
    ones_b = jnp.ones((BLOCK, LANES), bf16)
    head = pl.program_id(1)

    def cummax_lanes(x, backward):
        sh = 1
        while sh < LANES:
            if backward:
                x = jnp.maximum(x, jnp.where(lane_n < LANES - sh, pltpu.roll(x, LANES - sh, 1), _NEG_INF))
            else:
                x = jnp.maximum(x, jnp.where(lane_n >= sh, pltpu.roll(x, sh, 1), _NEG_INF))
            sh *= 2
        return x

    for d, (op, last) in enumerate(((triu_b, LANES - 1), (tril_b, 0))):
        lf = _log_sigmoid(gt_ref[2 * d + 1] + gb_ref[2 * d + 1, head]) * _LOG2E
        hi, lo = _split_hi_lo(lf)
        b = _dot(hi, op) + _dot(lo, op)
        a = (gt_ref[2 * d] + gb_ref[2 * d, head]) * _LOG2E - b
        bl = jnp.broadcast_to(b[:, last:last + 1], (nc, LANES))
        ml = jnp.broadcast_to(jnp.max(bl + a, axis=1, keepdims=True), (nc, LANES))
        for r, val in enumerate((a, cummax_lanes(a, d == 1), b, jnp.exp2(bl + a - ml), bl, ml)):
            rows_ref[d * _ROWS_PER_DIR + r] = val

    def conv_silu(src_ref, w, c, t0):
        x = src_ref[pl.ds(t0, BLOCK), :]
        prev = src_ref[pl.ds(jnp.maximum(t0 - 1, 0), 1), :] * jnp.where(c > 0, 1.0, 0.0)
        nxt = src_ref[pl.ds(jnp.minimum(t0 + BLOCK, seq - 1), 1), :] * jnp.where(c < nc - 1, 1.0, 0.0)
        xd = jnp.where(rowi == 0, prev, pltpu.roll(x, 1, 0))
        xu = jnp.where(rowi == BLOCK - 1, nxt, pltpu.roll(x, BLOCK - 1, 0))
        y = xd * w[0:1] + x * w[1:2] + xu * w[2:3]
        return y * jax.nn.sigmoid(y)

    def prep(c, carry):
        t0 = pl.multiple_of(c * BLOCK, BLOCK)
        qc_ref[pl.ds(t0, BLOCK), :] = (conv_silu(cq_ref, wq_ref[...], c, t0) * (C_DIM ** -0.5)).astype(bf16)
        kt_ref[c] = conv_silu(ck_ref, wk_ref[...], c, t0).T.astype(bf16)
        vx_ref[pl.ds(t0, BLOCK), 0:C_DIM] = v_ref[pl.ds(t0, BLOCK), :]
        vx_ref[pl.ds(t0, BLOCK), C_DIM:2 * C_DIM] = ones_b
        return carry

    lax.fori_loop(0, nc, prep, 0, unroll=2)

    def load_rows(ch, backward):
        t0 = pl.multiple_of(ch * BLOCK, BLOCK)
        base = _ROWS_PER_DIR if backward else 0
        a, c, b, w, bl, ml = (rows_ref[base + r, pl.ds(ch, 1), :] for r in range(_ROWS_PER_DIR))
        kt = kt_ref[ch]
        return dict(t0=t0, backward=backward, bl=bl[:, 0:1], m_loc=ml[:, 0:1],
                    a_t=jnp.where(triu if backward else tril, a, _NEG_INF),
                    b_full=jnp.broadcast_to(b, (BLOCK, LANES)).T,
                    c_full=jnp.broadcast_to(c, (BLOCK, LANES)).T,
                    kw=(kt.astype(f32) * w).astype(bf16), kt=kt)

    def first_products(g):
        vx = vx_ref[pl.ds(g["t0"], BLOCK), :]
        g["qk"] = _dot(qc_ref[pl.ds(g["t0"], BLOCK), :], g["kt"])
        g["kv"] = _dot(g["kw"], vx)

    def decayed_scores(g):
        g["s0"] = (g["qk"] * jnp.exp2(g["a_t"] - g["c_full"])).astype(bf16)

    def inner_products(g):
        g["p0"] = _dot(g["s0"], vx_ref[pl.ds(g["t0"], BLOCK), :])

    def with_state(g, state, out_ref):
        cx, m = state
        x = _dot(qc_ref[pl.ds(g["t0"], BLOCK), :], cx.astype(bf16))
        n = jnp.maximum(g["c_full"], m)
        e = jnp.exp2(g["c_full"] - n)
        iw = jnp.exp2(m - n)
        num = e * g["p0"][:, 0:C_DIM] + iw * x[:, 0:C_DIM]
        den = e * g["p0"][:, C_DIM:] + iw * x[:, C_DIM:]
        out_ref[pl.ds(g["t0"], BLOCK), :] = num / jnp.maximum(jnp.abs(den), jnp.exp2(-(g["b_full"] + n)))
        m_new = jnp.maximum(g["bl"] + m, g["m_loc"])
        cw = jnp.exp2(g["bl"] + m - m_new)
        f = jnp.exp2(g["m_loc"] - m_new)
        return cw * cx + f * g["kv"], m_new

    def body(it, carry):
        sf, sb = carry
        steps = []
        for u in range(_SCAN_UNROLL):
            c = it * _SCAN_UNROLL + u
            steps.append(load_rows(c, False))
            steps.append(load_rows(nc - 1 - c, True))
        for g in steps:
            first_products(g)
        for g in steps:
            decayed_scores(g)
        for g in steps:
            inner_products(g)
        for g in steps:
            if g["backward"]:
                sb = with_state(g, sb, ob_ref)
            else:
                sf = with_state(g, sf, of_ref)
        return sf, sb

    init = (jnp.zeros((C_DIM, 2 * C_DIM), f32), jnp.zeros((1, 1), f32))
    lax.fori_loop(0, nc // _SCAN_UNROLL, body, (init, init))

    def fin(c, carry):
        t0 = pl.multiple_of(c * BLOCK, BLOCK)
        r = of_ref[pl.ds(t0, BLOCK), :] + ob_ref[pl.ds(t0, BLOCK), :]
        r = r * lax.rsqrt(jnp.mean(r * r, axis=-1, keepdims=True) + EPS)
        og = og_ref[pl.ds(t0, BLOCK), :].astype(f32)
        o_ref[pl.ds(t0, BLOCK), :] = (jax.nn.sigmoid(og) * r).astype(o_ref.dtype)
        return carry

    lax.fori_loop(0, nc, fin, 0, unroll=4)


def _mlstm(pb, pf, gates_t, wq_conv, wk_conv, gate_bias):
    bsz, seq, _ = pb.shape
    assert seq % (8 * BLOCK) == 0
    kern = functools.partial(_mlstm_kernel, seq=seq)

    def col(base):
        return pl.BlockSpec((None, seq, LANES), lambda b, h: (b, 0, base + h))

    return pl.pallas_call(
        kern,
        grid=(bsz, C_HEADS),
        in_specs=[col(_PF_CQ), col(_PF_CK),
                  pl.BlockSpec((None, None, 4, seq // BLOCK, LANES), lambda b, h: (b, h, 0, 0, 0)),
                  col(_PB_CV), col(_PB_CO),
                  pl.BlockSpec((3, LANES), lambda b, h: (0, h)),
                  pl.BlockSpec((3, LANES), lambda b, h: (0, h)),
                  pl.BlockSpec(memory_space=pltpu.SMEM)],
        out_specs=pl.BlockSpec((None, seq, LANES), lambda b, h: (b, 0, h)),
        out_shape=jax.ShapeDtypeStruct((bsz, seq, C_HEADS * C_DIM), jnp.bfloat16),
        scratch_shapes=[pltpu.VMEM((seq, C_DIM), jnp.bfloat16),
                        pltpu.VMEM((seq // BLOCK, C_DIM, BLOCK), jnp.bfloat16),
                        pltpu.VMEM((seq, 2 * C_DIM), jnp.bfloat16),
                        pltpu.VMEM((2 * _ROWS_PER_DIR, seq // BLOCK, LANES), jnp.float32),
                        pltpu.VMEM((seq, C_DIM), jnp.float32), pltpu.VMEM((seq, C_DIM), jnp.float32)],
        compiler_params=_cparams("parallel", "parallel"),
        name="mlstm",
    )(pf, pf, gates_t, pb, pb, wq_conv, wk_conv, gate_bias)


def _window_attn_kernel(slopes_ref, sink_ref, q_ref, k_ref, v_ref, o_ref, kx_ref, vx_ref, *, seq):
    qi = pl.program_id(1)
    f32 = jnp.float32
    span = BLOCK + 2 * WINDOW
    group = D_HEADS // D_KV_HEADS

    @pl.when(qi == 0)
    def _():
        lane = lax.broadcasted_iota(jnp.int32, (seq, LANES), 1)
        ones = jnp.ones((seq, LANES), vx_ref.dtype)
        for src, dst in ((k_ref, kx_ref), (v_ref, vx_ref)):
            x = src[...].astype(f32)
            x0 = jnp.where(lane < D_DIM, x, 0.0)
            x1 = jnp.where(lane >= D_DIM, x, 0.0)
            for i, xi in enumerate((x0, pltpu.roll(x0, D_DIM, 1), pltpu.roll(x1, D_DIM, 1), x1)):
                dst[i, :, 0:LANES] = xi.astype(dst.dtype)
        for i in range(4):
            vx_ref[i, :, LANES:2 * LANES] = ones

    t0 = qi * BLOCK
    ws = pl.multiple_of(jnp.clip(t0 - WINDOW, 0, seq - span), BLOCK)
    rel = (lax.broadcasted_iota(jnp.int32, (BLOCK, span), 0)
           - lax.broadcasted_iota(jnp.int32, (BLOCK, span), 1)) + (t0 - ws)
    dist = jnp.abs(rel)
    neg_dist = jnp.where(dist <= WINDOW, dist.astype(f32) * (-_LOG2E), _NEG_INF)

    n_pairs = D_HEADS // 2
    lookahead = 2
    scores = {}

    def score_pair(pair):
        kv = (2 * pair) // group
        qp = (q_ref[:, pair * LANES:(pair + 1) * LANES].astype(f32) * (D_DIM ** -0.5 * _LOG2E)).astype(q_ref.dtype)
        for half in range(2):
            scores[2 * pair + half] = _dot_nt(qp, kx_ref[2 * kv + half, pl.ds(ws, span), :])

    def attend(hd, kv_slot):
        s = scores[hd] + slopes_ref[hd] * neg_dist
        snk = sink_ref[hd] * _LOG2E
        m = jnp.maximum(jnp.max(s, axis=1, keepdims=True), snk)
        o = _dot(jnp.exp2(s - m).astype(jnp.bfloat16), vx_ref[kv_slot, pl.ds(ws, span), :])
        return o[:, 0:LANES] / (o[:, LANES:] + jnp.exp2(snk - m))

    for pair in range(lookahead):
        score_pair(pair)
    for pair in range(n_pairs):
        kv = (2 * pair) // group
        out = attend(2 * pair, 2 * kv) + attend(2 * pair + 1, 2 * kv + 1)
        o_ref[:, pair * LANES:(pair + 1) * LANES] = out.astype(o_ref.dtype)
        if pair + lookahead < n_pairs:
            score_pair(pair + lookahead)


def _window_attention(pb, attn_sink, slopes_d):
    bsz, seq, _ = pb.shape
    kern = functools.partial(_window_attn_kernel, seq=seq)
    width = D_HEADS * D_DIM
    return pl.pallas_call(
        kern,
        grid=(bsz, seq // BLOCK),
        in_specs=[
            pl.BlockSpec(memory_space=pltpu.SMEM),
            pl.BlockSpec(memory_space=pltpu.SMEM),
            pl.BlockSpec((None, BLOCK, width), lambda b, i: (b, i, _PB_DQ512)),
            pl.BlockSpec((None, seq, LANES), lambda b, i: (b, 0, _PB_DK)),
            pl.BlockSpec((None, seq, LANES), lambda b, i: (b, 0, _PB_DV)),
        ],
        out_specs=pl.BlockSpec((None, BLOCK, width), lambda b, i: (b, i, 0)),
        out_shape=jax.ShapeDtypeStruct((bsz, seq, width), jnp.bfloat16),
        scratch_shapes=[pltpu.VMEM((4, seq, LANES), jnp.bfloat16), pltpu.VMEM((4, seq, 2 * LANES), jnp.bfloat16)],
        compiler_params=_cparams("parallel", "arbitrary"),
        name="window_attention",
    )(slopes_d, attn_sink, pb, pb, pb)


def _out_proj_kernel(x_ref, ya_ref, yb_ref, yc_ref, yd_ref, w_ref, o_ref):
    acc = x_ref[...]
    for g, y_ref in enumerate((ya_ref, yb_ref, yc_ref, yd_ref)):
        acc = acc + _dot(y_ref[...], w_ref[g * 512:(g + 1) * 512, :])
    o_ref[...] = acc


def _out_proj(x2d, ys, w_out, tm):
    t, d = x2d.shape
    yspec = pl.BlockSpec((tm, 512), lambda i: (i, 0))
    return pl.pallas_call(
        _out_proj_kernel,
        grid=(t // tm,),
        in_specs=[pl.BlockSpec((tm, d), lambda i: (i, 0)), yspec, yspec, yspec, yspec,
                  pl.BlockSpec((d, d), lambda i: (0, 0), pipeline_mode=pl.Buffered(1))],
        out_specs=pl.BlockSpec((tm, d), lambda i: (i, 0)),
        out_shape=jax.ShapeDtypeStruct((t, d), jnp.float32),
        compiler_params=_cparams("parallel", vmem_limit=_BIG_VMEM_LIMIT),
        name="out_proj",
    )(x2d, *ys, w_out)


def _ffn_kernel(x_ref, xp_ref, xn_ref, nw_ref, wg_ref, wu_ref, cw_ref, wd_ref, fw_ref, o_ref,
                h_ref, *, seq, tm, final_norm):
    i = pl.program_id(0)
    j = pl.program_id(1)
    f32 = jnp.float32

    def norm(x):
        return x * lax.rsqrt(jnp.mean(x * x, axis=-1, keepdims=True) + EPS) * nw_ref[...]

    @pl.when(j == 0)
    def _():
        h_ref[0:tm, :] = norm(x_ref[...]).astype(h_ref.dtype)
        has_prev = ((i * tm) % seq != 0).astype(f32)
        has_next = (((i + 1) * tm) % seq != 0).astype(f32)
        halo = jnp.concatenate([norm(xp_ref[...]) * has_prev, norm(xn_ref[...]) * has_next], axis=0)
        h_ref[tm:tm + 16, :] = halo.astype(h_ref.dtype)
        o_ref[...] = jnp.zeros(o_ref.shape, f32)

    gate = _dot(h_ref[...], wg_ref[...])
    gm = gate[0:tm]
    rowi = lax.broadcasted_iota(jnp.int32, gm.shape, 0)
    g_dn = jnp.where(rowi == 0, gate[tm + 7:tm + 8], pltpu.roll(gm, 1, 0))
    g_up = jnp.where(rowi == tm - 1, gate[tm + 8:tm + 9], pltpu.roll(gm, tm - 1, 0))
    cw = cw_ref[...]
    g = g_dn * cw[0:1] + gm * cw[1:2] + g_up * cw[2:3]
    up = _dot(h_ref[0:tm, :], wu_ref[...])
    act = (g * jax.nn.sigmoid(g) * up).astype(wd_ref.dtype)
    o_ref[...] += _dot(act, wd_ref[...])

    @pl.when(j == pl.num_programs(1) - 1)
    def _():
        y = x_ref[...] + o_ref[...]
        if final_norm:
            y = y * lax.rsqrt(jnp.mean(y * y, axis=-1, keepdims=True) + EPS) * fw_ref[...]
        o_ref[...] = y


def _ffn(x2d, seq, nw, wg, wu, cw, wd, fw, final_norm, tm, tn):
    t, d = x2d.shape
    nff = wg.shape[1]
    rb = tm // 8
    last8 = t // 8 - 1
    kern = functools.partial(_ffn_kernel, seq=seq, tm=tm, final_norm=final_norm)
    return pl.pallas_call(
        kern,
        grid=(t // tm, nff // tn),
        in_specs=[
            pl.BlockSpec((tm, d), lambda i, j: (i, 0)),
            pl.BlockSpec((8, d), lambda i, j: (jnp.maximum(i * rb - 1, 0), 0)),
            pl.BlockSpec((8, d), lambda i, j: (jnp.minimum((i + 1) * rb, last8), 0)),
            pl.BlockSpec((1, d), lambda i, j: (0, 0)),
            pl.BlockSpec((d, tn), lambda i, j: (0, j)),
            pl.BlockSpec((d, tn), lambda i, j: (0, j)),
            pl.BlockSpec((3, tn), lambda i, j: (0, j)),
            pl.BlockSpec((tn, d), lambda i, j: (j, 0)),
            pl.BlockSpec((1, d), lambda i, j: (0, 0)),
        ],
        out_specs=pl.BlockSpec((tm, d), lambda i, j: (i, 0)),
        out_shape=jax.ShapeDtypeStruct((t, d), jnp.float32),
        scratch_shapes=[pltpu.VMEM((tm + 16, d), jnp.bfloat16)],
        compiler_params=_cparams("parallel", "arbitrary", vmem_limit=_BIG_VMEM_LIMIT),
        name="ffn",
    )(x2d, x2d, x2d, nw, wg, wu, cw, wd, fw)


def _prepare_layer(l, norm1_w, w_in, qk_conv_w, diff_lambda, ret_decay_logit, mlstm_gate_bias, attn_sink,
                   w_out, norm2_w, w_gate, w_up, ffn_conv_w, w_down):
    bf16 = jnp.bfloat16
    f32 = jnp.float32
    wi = w_in[l]
    w_pb = jnp.concatenate([wi[:, _OFF_A:_OFF_CQ], wi[:, _OFF_CV:_OFF_GATES], wi[:, _OFF_D:N_IN]], axis=1)
    wgates = jnp.pad(wi[:, _OFF_GATES:_OFF_D], ((0, 0), (0, LANES - N_GATES)))
    w_pf = jnp.concatenate([wi[:, _OFF_CQ:_OFF_CV], wgates], axis=1)
    gbias = mlstm_gate_bias[l].astype(f32)
    dec = jnp.zeros((8, LANES), f32).at[0:2, 0:B_HEADS].set(ret_decay_logit[l].astype(f32))
    return dict(
        norm1=norm1_w[l].reshape(1, D_MODEL).astype(f32),
        w_pb=w_pb.astype(bf16), w_pf=w_pf.astype(bf16),
        wq_conv=qk_conv_w[l][:, :C_HEADS * C_DIM].astype(f32),
        wk_conv=qk_conv_w[l][:, C_HEADS * C_DIM:].astype(f32),
        gbias=gbias, dec=dec,
        diff_lambda=diff_lambda[l].astype(f32), sink=attn_sink[l].astype(f32),
        w_out=w_out[l].astype(bf16),
        norm2=norm2_w[l].reshape(1, D_MODEL).astype(f32),
        w_gate=w_gate[l].astype(bf16), w_up=w_up[l].astype(bf16),
        ffn_conv=ffn_conv_w[l].astype(f32), w_down=w_down[l].astype(bf16),
    )


def _tile(n, pref):
    t = min(pref, n)
    while n % t:
        t //= 2
    return t


def _layer(x2d, bsz, seq, layer_idx, p, slopes, final_w, final_norm):
    t = bsz * seq
    tm_in = _tile(t, 1024)
    pb, pf = _in_proj(x2d, p["norm1"], p["w_pb"], p["w_pf"], tm_in, _IN_PROJ_TN)
    pb = pb.reshape(bsz, seq, PB_WIDTH)
    pf = pf.reshape(bsz, seq, PF_WIDTH)
    gates_t = pf[:, :, _PF_GATES * LANES:_PF_GATES * LANES + N_GATES].reshape(bsz, seq, 4, C_HEADS)
    gates_t = jnp.transpose(gates_t, (0, 3, 2, 1)).reshape(bsz, C_HEADS, 4, seq // BLOCK, BLOCK)
    y_a = _diff_attention(pb, p["diff_lambda"], slopes[D_HEADS:], layer_idx,
                          _tile(seq // _ATTN_TILES_PER_BODY, _ATTN_TILE),
                          _tile(seq // _ATTN_TILES_PER_BODY, _ATTN_TILE))
    y_b = _retention(pb, p["dec"])
    y_c = _mlstm(pb, pf, gates_t, p["wq_conv"], p["wk_conv"], p["gbias"])
    y_d = _window_attention(pb, p["sink"], slopes[:D_HEADS])
    ys = [y.reshape(t, 512) for y in (y_a, y_b, y_c, y_d)]
    x1 = _out_proj(x2d, ys, p["w_out"], _tile(t, 1024))
    return _ffn(x1, seq, p["norm2"], p["w_gate"], p["w_up"], p["ffn_conv"], p["w_down"], final_w,
                final_norm, _tile(seq, 1024), 512)


def _trunk(x, layers, slopes, final_w):
    bsz, seq, d = x.shape
    x2d = x.reshape(bsz * seq, d)
    for l, p in enumerate(layers):
        x2d = _layer(x2d, bsz, seq, l, p, slopes, final_w, l == len(layers) - 1)
    return x2d.reshape(bsz, seq, d)


def kernel(x_prompt, x_sample, norm1_w, w_in, qk_conv_w, diff_lambda, ret_decay_logit, mlstm_gate_bias,
           attn_sink, w_out, norm2_w, w_gate, w_up, ffn_conv_w, w_down, final_norm_w):
    layers = [_prepare_layer(l, norm1_w, w_in, qk_conv_w, diff_lambda, ret_decay_logit, mlstm_gate_bias,
                             attn_sink, w_out, norm2_w, w_gate, w_up, ffn_conv_w, w_down)
              for l in range(DEPTH)]
    n = A_HEADS + D_HEADS
    slopes = 2.0 ** (-8.0 * jnp.arange(1, n + 1, dtype=jnp.float32) / n)
    final_w = final_norm_w.reshape(1, D_MODEL).astype(jnp.float32)
    return (_trunk(x_prompt, layers, slopes, final_w), _trunk(x_sample, layers, slopes, final_w))
```

```python
import functools
import math

import jax
import jax.numpy as jnp
from jax import lax
from jax.experimental import pallas as pl
from jax.experimental.pallas import tpu as pltpu

D_MODEL = 2048
DEPTH = 2
A_HEADS = 4
A_QK_DIM = 64
A_V_DIM = 128
B_HEADS = 4
B_DIM = 128
C_HEADS = 4
C_DIM = 128
D_HEADS = 8
D_KV_HEADS = 2
D_DIM = 64
WINDOW = 128
BLOCK = 128
D_FF = 5632
EPS = 1e-6
LANES = 128

_OFF_A = 0
_OFF_B = 1536
_OFF_CQ = 3584
_OFF_CV = 4608
_OFF_GATES = 5632
_OFF_D = 5648
N_IN = 6416

_PB_AQ, _PB_AK, _PB_AV = 0, 4, 8
_PB_BQ, _PB_BK, _PB_BV, _PB_BG = 12, 16, 20, 24
_PB_CV, _PB_CO = 28, 32
_PB_DQ512 = 9
_PB_DK, _PB_DV = 40, 41
PB_WIDTH = 42 * LANES
_IN_PROJ_TN = 1792
_PF_CQ, _PF_CK, _PF_GATES = 0, 4, 8
PF_WIDTH = 9 * LANES
N_GATES = 16

_VMEM_LIMIT = 56 * 1024 * 1024
_BIG_VMEM_LIMIT = 61 * 1024 * 1024
_SCAN_UNROLL = 4
_NEG_INF = float("-inf")


def _cparams(*sem, vmem_limit=_VMEM_LIMIT):
    return pltpu.CompilerParams(dimension_semantics=sem, vmem_limit_bytes=vmem_limit)


def _dot(a, b):
    return jnp.dot(a, b, preferred_element_type=jnp.float32)


def _dot_nt(a, b):
    return lax.dot_general(a, b, (((1,), (1,)), ((), ())), preferred_element_type=jnp.float32)


def _dot_tn(a, b):
    return lax.dot_general(a, b, (((0,), (0,)), ((), ())), preferred_element_type=jnp.float32)


def _log_sigmoid(x):
    return -(jnp.maximum(-x, 0.0) + jnp.log1p(jnp.exp(-jnp.abs(x))))


def _split_hi_lo(x):
    hi = x.astype(jnp.bfloat16)
    lo = (x - hi.astype(jnp.float32)).astype(jnp.bfloat16)
    return hi, lo


def _in_proj_kernel(x_ref, nw_ref, wb_ref, wf_ref, ob_ref, of_ref, h_ref, *, nb):
    j = pl.program_id(1)

    @pl.when(j == 0)
    def _():
        x = x_ref[...]
        ms = jnp.mean(x * x, axis=-1, keepdims=True)
        h = (x * lax.rsqrt(ms + EPS) * nw_ref[...]).astype(h_ref.dtype)
        h_ref[...] = h
        ob_ref[...] = _dot(h, wb_ref[...]).astype(ob_ref.dtype)

    @pl.when((j > 0) & (j < nb))
    def _():
        ob_ref[...] = _dot(h_ref[...], wb_ref[...]).astype(ob_ref.dtype)

    @pl.when(j == nb)
    def _():
        of_ref[...] = _dot(h_ref[...], wf_ref[...])


def _in_proj(x2d, nw, w_pb, w_pf, tm, tn):
    t, d = x2d.shape
    nb = PB_WIDTH // tn
    return pl.pallas_call(
        functools.partial(_in_proj_kernel, nb=nb),
        grid=(t // tm, nb + 1),
        in_specs=[
            pl.BlockSpec((tm, d), lambda i, j: (i, 0)),
            pl.BlockSpec((1, d), lambda i, j: (0, 0)),
            pl.BlockSpec((d, tn), lambda i, j: (0, jnp.minimum(j, nb - 1))),
            pl.BlockSpec((d, PF_WIDTH), lambda i, j: (0, 0), pipeline_mode=pl.Buffered(1)),
        ],
        out_specs=[pl.BlockSpec((tm, tn), lambda i, j: (i, jnp.minimum(j, nb - 1))),
                   pl.BlockSpec((tm, PF_WIDTH), lambda i, j: (i, 0))],
        out_shape=[jax.ShapeDtypeStruct((t, PB_WIDTH), jnp.bfloat16),
                   jax.ShapeDtypeStruct((t, PF_WIDTH), jnp.float32)],
        scratch_shapes=[pltpu.VMEM((tm, d), jnp.bfloat16)],
        compiler_params=_cparams("parallel", "arbitrary", vmem_limit=_BIG_VMEM_LIMIT),
        name="norm_in_proj",
    )(x2d, nw, w_pb, w_pf)


_LOG2E = 1.4426950408889634
_N_BIAS_LANES = 3
_V_ROWS = A_V_DIM + 16
_ATTN_TILES_PER_BODY = 4
_ATTN_TILE = 512


def _diff_attn_kernel(slopes_ref, q_ref, qn_ref, k_ref, v_ref, dl_ref, o_ref,
                      kx_ref, vt_ref, qt_ref, m_ref, acc_ref, s_ref, cmax_ref, bias_ref, *, seq, tq, tk,
                      lam_init):
    head = pl.program_id(1)
    qi = pl.program_id(2)
    f32 = jnp.float32
    bf16 = jnp.bfloat16
    sigma = slopes_ref[head] * _LOG2E
    nk = seq // tk
    prep_rows = tk

    @pl.when(qi == 0)
    def _():
        lane = lax.broadcasted_iota(jnp.int32, (prep_rows, LANES), 1)
        row = lax.broadcasted_iota(jnp.int32, (prep_rows, LANES), 0)
        ones_row = jnp.where(lax.broadcasted_iota(jnp.int32, (_V_ROWS - A_V_DIM, tk), 0) == 0, 1.0, 0.0)

        def prep(c, carry):
            r0 = pl.multiple_of(c * prep_rows, prep_rows)
            kk = k_ref[pl.ds(r0, prep_rows), :].astype(f32)
            col = sigma * (row + r0).astype(f32)
            c1 = col.astype(bf16).astype(f32)
            c2 = (col - c1).astype(bf16).astype(f32)
            c3 = col - c1 - c2
            bias = jnp.where(lane == A_QK_DIM, c1, jnp.where(lane == A_QK_DIM + 1, c2,
                             jnp.where(lane == A_QK_DIM + 2, c3, 0.0)))
            kx_ref[0, pl.ds(r0, prep_rows), :] = jnp.where(lane < A_QK_DIM, kk, bias).astype(bf16)
            kx_ref[1, pl.ds(r0, prep_rows), :] = jnp.where(
                lane < A_QK_DIM, pltpu.roll(kk, A_QK_DIM, 1), bias).astype(bf16)
            vt_ref[c, 0:A_V_DIM, :] = v_ref[pl.ds(r0, prep_rows), :].astype(f32).T.astype(bf16)
            vt_ref[c, A_V_DIM:_V_ROWS, :] = ones_row.astype(bf16)
            return carry

        lax.fori_loop(0, nk, prep, 0)
        rel = (lax.broadcasted_iota(jnp.int32, (tk, tq), 1)
               - lax.broadcasted_iota(jnp.int32, (tk, tq), 0)).astype(f32)
        bias_ref[...] = sigma * jnp.abs(rel)

    qrow = lax.broadcasted_iota(jnp.int32, (LANES, tq), 0)
    bias_rows = lax.broadcasted_iota(jnp.int32, (16, tq), 0) < _N_BIAS_LANES

    def build_q_variants(src_ref, base_slot):
        qt = (src_ref[...].astype(f32) * (A_QK_DIM ** -0.5 * _LOG2E)).T
        for mp in range(2):
            qm = qt if mp == 0 else pltpu.roll(qt, A_QK_DIM, 0)
            base = jnp.where(qrow < A_QK_DIM, qm, 0.0).astype(bf16)
            for var, sign in enumerate((1.0, 0.0, -1.0)):
                qt_ref[base_slot + 3 * mp + var] = base
                qt_ref[base_slot + 3 * mp + var, A_QK_DIM:A_QK_DIM + 16, :] = (
                    jnp.where(bias_rows, sign, 0.0).astype(bf16))

    def diagonal_scores(tile, base_slot, mp):
        k_start = pl.multiple_of(tile * tk, tk)
        s = _dot(kx_ref[mp, pl.ds(k_start, tk), :], qt_ref[base_slot + 3 * mp + 1]) - bias_ref[...]
        s_ref[0, mp] = s
        cmax_ref[0, mp] = jnp.max(s, axis=0, keepdims=True)

    qslot = (qi % 2) * 6
    qslot_next = 6 - qslot

    @pl.when(qi == 0)
    def _():
        build_q_variants(q_ref, 0)
        for mp in range(2):
            diagonal_scores(0, 0, mp)

    q0 = qi * tq
    rho = sigma * (lax.broadcasted_iota(jnp.int32, (1, tq), 1) + q0).astype(f32)
    m_ref[...] = jnp.full(m_ref.shape, _NEG_INF, f32)
    acc_ref[...] = jnp.zeros(acc_ref.shape, f32)

    kd = q0 // tk

    def key_tile(j):
        jm = jnp.minimum(j, nk - 1) - 1
        return jnp.where(j == 0, kd, jnp.where(jm < kd, jm, jm + 1))

    def scores(j, slot, mp):
        kb = key_tile(j)
        var = jnp.where(kb < kd, 0, jnp.where(kb > kd, 2, 1))
        k_start = pl.multiple_of(kb * tk, tk)
        s = _dot(kx_ref[mp, pl.ds(k_start, tk), :], qt_ref[qslot + 3 * mp + var])
        s_ref[slot, mp] = s
        cmax_ref[slot, mp] = jnp.max(s, axis=0, keepdims=True)

    def softmax_pv(j, slot, mp):
        kb = key_tile(j)
        shift = jnp.where(kb < kd, -1.0, jnp.where(kb > kd, 1.0, 0.0)) * rho
        m_old = m_ref[mp]
        m_new = jnp.maximum(m_old, cmax_ref[slot, mp] + shift)
        p = jnp.exp2(s_ref[slot, mp] - (m_new - shift))
        acc_ref[mp] = jnp.exp2(m_old - m_new) * acc_ref[mp] + _dot(vt_ref[kb], p.astype(bf16))
        m_ref[mp] = m_new

    def body(i, carry):
        for u in range(_ATTN_TILES_PER_BODY):
            j = _ATTN_TILES_PER_BODY * i + u
            for mp in range(2):
                scores(j + 1, (u + 1) % _ATTN_TILES_PER_BODY, mp)
                softmax_pv(j, u, mp)
        return carry

    n_bodies = nk // _ATTN_TILES_PER_BODY
    lax.fori_loop(0, n_bodies - 1, body, 0)

    build_q_variants(qn_ref, qslot_next)
    kd_next = jnp.minimum(qi + 1, seq // tq - 1)
    for u in range(_ATTN_TILES_PER_BODY):
        j = _ATTN_TILES_PER_BODY * (n_bodies - 1) + u
        for mp in range(2):
            if u + 1 < _ATTN_TILES_PER_BODY:
                scores(j + 1, u + 1, mp)
            else:
                diagonal_scores(kd_next, qslot_next, mp)
            softmax_pv(j, u, mp)

    dl = dl_ref[...]
    lam = (jnp.exp(jnp.sum(dl[0:1] * dl[1:2], axis=1, keepdims=True))
           - jnp.exp(jnp.sum(dl[2:3] * dl[3:4], axis=1, keepdims=True)) + lam_init)
    a0 = acc_ref[0]
    a1 = acc_ref[1]
    ot = (a0[0:A_V_DIM] / a0[A_V_DIM:A_V_DIM + 1] - lam * (a1[0:A_V_DIM] / a1[A_V_DIM:A_V_DIM + 1]))
    ot = ot * lax.rsqrt(jnp.mean(ot * ot, axis=0, keepdims=True) + EPS)
    o_ref[...] = (ot * (1.0 - lam_init)).T.astype(o_ref.dtype)


def _diff_attention(pb, diff_lambda, slopes_a, layer_idx, tq, tk):
    bsz, seq, _ = pb.shape
    assert seq % (_ATTN_TILES_PER_BODY * tk) == 0 and tk == tq
    lam_init = 0.8 - 0.6 * math.exp(-0.3 * layer_idx)
    kern = functools.partial(_diff_attn_kernel, seq=seq, tq=tq, tk=tk, lam_init=lam_init)
    return pl.pallas_call(
        kern,
        grid=(bsz, A_HEADS, seq // tq),
        in_specs=[
            pl.BlockSpec(memory_space=pltpu.SMEM),
            pl.BlockSpec((None, tq, LANES), lambda b, h, i: (b, i, _PB_AQ + h)),
            pl.BlockSpec((None, tq, LANES), lambda b, h, i: (b, jnp.minimum(i + 1, seq // tq - 1), _PB_AQ + h)),
            pl.BlockSpec((None, seq, LANES), lambda b, h, i: (b, 0, _PB_AK + h)),
            pl.BlockSpec((None, seq, LANES), lambda b, h, i: (b, 0, _PB_AV + h)),
            pl.BlockSpec((4, A_QK_DIM), lambda b, h, i: (0, 0)),
        ],
        out_specs=pl.BlockSpec((None, tq, LANES), lambda b, h, i: (b, i, h)),
        out_shape=jax.ShapeDtypeStruct((bsz, seq, A_HEADS * A_V_DIM), jnp.bfloat16),
        scratch_shapes=[
            pltpu.VMEM((2, seq, LANES), jnp.bfloat16),
            pltpu.VMEM((seq // tk, _V_ROWS, tk), jnp.bfloat16),
            pltpu.VMEM((12, LANES, tq), jnp.bfloat16),
            pltpu.VMEM((2, 1, tq), jnp.float32),
            pltpu.VMEM((2, _V_ROWS, tq), jnp.float32),
            pltpu.VMEM((_ATTN_TILES_PER_BODY, 2, tk, tq), jnp.float32),
            pltpu.VMEM((_ATTN_TILES_PER_BODY, 2, 1, tq), jnp.float32),
            pltpu.VMEM((tk, tq), jnp.float32),
        ],
        compiler_params=_cparams("parallel", "parallel", "arbitrary"),
        name="diff_attention",
    )(slopes_a, pb, pb, pb, pb, diff_lambda)


def _retention_kernel(q_ref, k_ref, v_ref, g_ref, dec_ref, o_ref, of_ref, ob_ref, *, seq):
    head = pl.program_id(1)
    nc = seq // BLOCK
    f32 = jnp.float32

    lgv = _log_sigmoid(dec_ref[...])
    rr = lax.broadcasted_iota(jnp.int32, lgv.shape, 0)
    cc = lax.broadcasted_iota(jnp.int32, lgv.shape, 1)

    def pick(row):
        sel = jnp.where((rr == row) & (cc == head), lgv, 0.0)
        return jnp.sum(jnp.sum(sel, axis=1, keepdims=True), axis=0, keepdims=True)

    lg_f, lg_b = pick(0), pick(1)
    ri = lax.broadcasted_iota(jnp.int32, (BLOCK, BLOCK), 0).astype(f32)
    ci = lax.broadcasted_iota(jnp.int32, (BLOCK, BLOCK), 1).astype(f32)
    diff = ri - ci
    decay_f = jnp.where(diff >= 0, jnp.exp(lg_f * jnp.maximum(diff, 0.0)), 0.0)
    decay_b = jnp.where(diff <= 0, jnp.exp(lg_b * jnp.maximum(-diff, 0.0)), 0.0)
    idx = lax.broadcasted_iota(jnp.int32, (BLOCK, 1), 0).astype(f32)
    qdec_f = jnp.exp(lg_f * (idx + 1.0))
    kdec_f = jnp.exp(lg_f * (BLOCK - 1.0 - idx))
    qdec_b = jnp.exp(lg_b * (BLOCK - idx))
    kdec_b = jnp.exp(lg_b * idx)
    cdec_f = jnp.exp(lg_f * BLOCK)
    cdec_b = jnp.exp(lg_b * BLOCK)
    kscale = B_DIM ** -0.5

    def scores_and_kv(t0, backward):
        qb = q_ref[pl.ds(t0, BLOCK), :]
        kb = k_ref[pl.ds(t0, BLOCK), :].astype(f32) * kscale
        vb = v_ref[pl.ds(t0, BLOCK), :]
        kdec = kdec_b if backward else kdec_f
        return dict(t0=t0, backward=backward, qb=qb, vb=vb,
                    qk=_dot_nt(qb, kb.astype(qb.dtype)), kv=_dot_tn((kb * kdec).astype(vb.dtype), vb))

    def inner_product(g):
        s = g["qk"] * (decay_b if g["backward"] else decay_f)
        g["inner"] = _dot(s.astype(g["vb"].dtype), g["vb"])

    def with_state(g, state, out_ref):
        qdec, cdec = (qdec_b, cdec_b) if g["backward"] else (qdec_f, cdec_f)
        cross = _dot(g["qb"], state.astype(g["qb"].dtype)) * qdec
        out_ref[pl.ds(g["t0"], BLOCK), :] = g["inner"] + cross
        return state * cdec + g["kv"]

    def body(it, carry):
        rf, rb = carry
        steps = []
        for u in range(_SCAN_UNROLL):
            c = it * _SCAN_UNROLL + u
            steps.append(scores_and_kv(pl.multiple_of(c * BLOCK, BLOCK), False))
            steps.append(scores_and_kv(pl.multiple_of((nc - 1 - c) * BLOCK, BLOCK), True))
        for g in steps:
            inner_product(g)
        for g in steps:
            if g["backward"]:
                rb = with_state(g, rb, ob_ref)
            else:
                rf = with_state(g, rf, of_ref)
        return rf, rb

    zero = jnp.zeros((B_DIM, B_DIM), f32)
    lax.fori_loop(0, nc // _SCAN_UNROLL, body, (zero, zero))

    def fin(c, carry):
        t0 = pl.multiple_of(c * BLOCK, BLOCK)
        r = of_ref[pl.ds(t0, BLOCK), :] + ob_ref[pl.ds(t0, BLOCK), :]
        r = r * lax.rsqrt(jnp.mean(r * r, axis=-1, keepdims=True) + EPS)
        g = g_ref[pl.ds(t0, BLOCK), :].astype(f32)
        o_ref[pl.ds(t0, BLOCK), :] = (g * jax.nn.sigmoid(g) * r).astype(o_ref.dtype)
        return carry

    lax.fori_loop(0, nc, fin, 0, unroll=4)


def _retention(pb, dec_pad):
    bsz, seq, _ = pb.shape
    kern = functools.partial(_retention_kernel, seq=seq)

    def col(base):
        return pl.BlockSpec((None, seq, LANES), lambda b, h: (b, 0, base + h))

    return pl.pallas_call(
        kern,
        grid=(bsz, B_HEADS),
        in_specs=[col(_PB_BQ), col(_PB_BK), col(_PB_BV), col(_PB_BG),
                  pl.BlockSpec((8, LANES), lambda b, h: (0, 0))],
        out_specs=pl.BlockSpec((None, seq, LANES), lambda b, h: (b, 0, h)),
        out_shape=jax.ShapeDtypeStruct((bsz, seq, B_HEADS * B_DIM), jnp.bfloat16),
        scratch_shapes=[pltpu.VMEM((seq, B_DIM), jnp.float32), pltpu.VMEM((seq, B_DIM), jnp.float32)],
        compiler_params=_cparams("parallel", "parallel"),
        name="retention",
    )(pb, pb, pb, pb, dec_pad)


_ROWS_PER_DIR = 6


def _mlstm_kernel(cq_ref, ck_ref, gt_ref, v_ref, og_ref, wq_ref, wk_ref, gb_ref, o_ref,
                  qc_ref, kt_ref, vx_ref, rows_ref, of_ref, ob_ref, *, seq):
    nc = seq // BLOCK
    f32 = jnp.float32
    bf16 = jnp.bfloat16
    rowi = lax.broadcasted_iota(jnp.int32, (BLOCK, LANES), 0)
    coli = lax.broadcasted_iota(jnp.int32, (BLOCK, LANES), 1)
    lane_n = lax.broadcasted_iota(jnp.int32, (nc, LANES), 1)
    tril = rowi >= coli
    triu = rowi <= coli
    tril_b = tril.astype(bf16)
    triu_b = triu.astype(bf16)
    ones_b = jnp.ones((BLOCK, LANES), bf16)
    head = pl.program_id(1)

    def cummax_lanes(x, backward):
        sh = 1
        while sh < LANES:
            if backward:
                x = jnp.maximum(x, jnp.where(lane_n < LANES - sh, pltpu.roll(x, LANES - sh, 1), _NEG_INF))
            else:
                x = jnp.maximum(x, jnp.where(lane_n >= sh, pltpu.roll(x, sh, 1), _NEG_INF))
            sh *= 2
        return x

    for d, (op, last) in enumerate(((triu_b, LANES - 1), (tril_b, 0))):
        lf = _log_sigmoid(gt_ref[2 * d + 1] + gb_ref[2 * d + 1, head]) * _LOG2E
        hi, lo = _split_hi_lo(lf)
        b = _dot(hi, op) + _dot(lo, op)
        a = (gt_ref[2 * d] + gb_ref[2 * d, head]) * _LOG2E - b
        bl = jnp.broadcast_to(b[:, last:last + 1], (nc, LANES))
        ml = jnp.broadcast_to(jnp.max(bl + a, axis=1, keepdims=True), (nc, LANES))
        for r, val in enumerate((a, cummax_lanes(a, d == 1), b, jnp.exp2(bl + a - ml), bl, ml)):
            rows_ref[d * _ROWS_PER_DIR + r] = val

    def conv_silu(src_ref, w, c, t0):
        x = src_ref[pl.ds(t0, BLOCK), :]
        prev = src_ref[pl.ds(jnp.maximum(t0 - 1, 0), 1), :] * jnp.where(c > 0, 1.0, 0.0)
        nxt = src_ref[pl.ds(jnp.minimum(t0 + BLOCK, seq - 1), 1), :] * jnp.where(c < nc - 1, 1.0, 0.0)
        xd = jnp.where(rowi == 0, prev, pltpu.roll(x, 1, 0))
        xu = jnp.where(rowi == BLOCK - 1, nxt, pltpu.roll(x, BLOCK - 1, 0))
        y = xd * w[0:1] + x * w[1:2] + xu * w[2:3]
        return y * jax.nn.sigmoid(y)

    def prep(c, carry):
        t0 = pl.multiple_of(c * BLOCK, BLOCK)
        qc_ref[pl.ds(t0, BLOCK), :] = (conv_silu(cq_ref, wq_ref[...], c, t0) * (C_DIM ** -0.5)).astype(bf16)
        kt_ref[c] = conv_silu(ck_ref, wk_ref[...], c, t0).T.astype(bf16)
        vx_ref[pl.ds(t0, BLOCK), 0:C_DIM] = v_ref[pl.ds(t0, BLOCK), :]
        vx_ref[pl.ds(t0, BLOCK), C_DIM:2 * C_DIM] = ones_b
        return carry

    lax.fori_loop(0, nc, prep, 0, unroll=2)

    def load_rows(ch, backward):
        t0 = pl.multiple_of(ch * BLOCK, BLOCK)
        base = _ROWS_PER_DIR if backward else 0
        a, c, b, w, bl, ml = (rows_ref[base + r, pl.ds(ch, 1), :] for r in range(_ROWS_PER_DIR))
        kt = kt_ref[ch]
        return dict(t0=t0, backward=backward, bl=bl[:, 0:1], m_loc=ml[:, 0:1],
                    a_t=jnp.where(triu if backward else tril, a, _NEG_INF),
                    b_full=jnp.broadcast_to(b, (BLOCK, LANES)).T,
                    c_full=jnp.broadcast_to(c, (BLOCK, LANES)).T,
                    kw=(kt.astype(f32) * w).astype(bf16), kt=kt)

    def first_products(g):
        vx = vx_ref[pl.ds(g["t0"], BLOCK), :]
        g["qk"] = _dot(qc_ref[pl.ds(g["t0"], BLOCK), :], g["kt"])
        g["kv"] = _dot(g["kw"], vx)

    def decayed_scores(g):
        g["s0"] = (g["qk"] * jnp.exp2(g["a_t"] - g["c_full"])).astype(bf16)

    def inner_products(g):
        g["p0"] = _dot(g["s0"], vx_ref[pl.ds(g["t0"], BLOCK), :])

    def with_state(g, state, out_ref):
        cx, m = state
        x = _dot(qc_ref[pl.ds(g["t0"], BLOCK), :], cx.astype(bf16))
        n = jnp.maximum(g["c_full"], m)
        e = jnp.exp2(g["c_full"] - n)
        iw = jnp.exp2(m - n)
        num = e * g["p0"][:, 0:C_DIM] + iw * x[:, 0:C_DIM]
        den = e * g["p0"][:, C_DIM:] + iw * x[:, C_DIM:]
        out_ref[pl.ds(g["t0"], BLOCK), :] = num / jnp.maximum(jnp.abs(den), jnp.exp2(-(g["b_full"] + n)))
        m_new = jnp.maximum(g["bl"] + m, g["m_loc"])
        cw = jnp.exp2(g["bl"] + m - m_new)
        f = jnp.exp2(g["m_loc"] - m_new)
        return cw * cx + f * g["kv"], m_new

    def body(it, carry):
        sf, sb = carry
        steps = []
        for u in range(_SCAN_UNROLL):
            c = it * _SCAN_UNROLL + u
            steps.append(load_rows(c, False))
            steps.append(load_rows(nc - 1 - c, True))
        for g in steps:
            first_products(g)
        for g in steps:
            decayed_scores(g)
        for g in steps:
            inner_products(g)
        for g in steps:
            if g["backward"]:
                sb = with_state(g, sb, ob_ref)
            else:
                sf = with_state(g, sf, of_ref)
        return sf, sb

    init = (jnp.zeros((C_DIM, 2 * C_DIM), f32), jnp.zeros((1, 1), f32))
    lax.fori_loop(0, nc // _SCAN_UNROLL, body, (init, init))

    def fin(c, carry):
        t0 = pl.multiple_of(c * BLOCK, BLOCK)
        r = of_ref[pl.ds(t0, BLOCK), :] + ob_ref[pl.ds(t0, BLOCK), :]
        r = r * lax.rsqrt(jnp.mean(r * r, axis=-1, keepdims=True) + EPS)
        og = og_ref[pl.ds(t0, BLOCK), :].astype(f32)
        o_ref[pl.ds(t0, BLOCK), :] = (jax.nn.sigmoid(og) * r).astype(o_ref.dtype)
        return carry

    lax.fori_loop(0, nc, fin, 0, unroll=4)


def _mlstm(pb, pf, gates_t, wq_conv, wk_conv, gate_bias):
    bsz, seq, _ = pb.shape
    assert seq % (8 * BLOCK) == 0
    kern = functools.partial(_mlstm_kernel, seq=seq)

    def col(base):
        return pl.BlockSpec((None, seq, LANES), lambda b, h: (b, 0, base + h))

    return pl.pallas_call(
        kern,
        grid=(bsz, C_HEADS),
        in_specs=[col(_PF_CQ), col(_PF_CK),
                  pl.BlockSpec((None, None, 4, seq // BLOCK, LANES), lambda b, h: (b, h, 0, 0, 0)),
                  col(_PB_CV), col(_PB_CO),
                  pl.BlockSpec((3, LANES), lambda b, h: (0, h)),
                  pl.BlockSpec((3, LANES), lambda b, h: (0, h)),
                  pl.BlockSpec(memory_space=pltpu.SMEM)],
        out_specs=pl.BlockSpec((None, seq, LANES), lambda b, h: (b, 0, h)),
        out_shape=jax.ShapeDtypeStruct((bsz, seq, C_HEADS * C_DIM), jnp.bfloat16),
        scratch_shapes=[pltpu.VMEM((seq, C_DIM), jnp.bfloat16),
                        pltpu.VMEM((seq // BLOCK, C_DIM, BLOCK), jnp.bfloat16),
                        pltpu.VMEM((seq, 2 * C_DIM), jnp.bfloat16),
                        pltpu.VMEM((2 * _ROWS_PER_DIR, seq // BLOCK, LANES), jnp.float32),
                        pltpu.VMEM((seq, C_DIM), jnp.float32), pltpu.VMEM((seq, C_DIM), jnp.float32)],
        compiler_params=_cparams("parallel", "parallel"),
        name="mlstm",
    )(pf, pf, gates_t, pb, pb, wq_conv, wk_conv, gate_bias)


def _window_attn_kernel(slopes_ref, sink_ref, q_ref, k_ref, v_ref, o_ref, kx_ref, vx_ref, *, seq):
    qi = pl.program_id(1)
    f32 = jnp.float32
    span = BLOCK + 2 * WINDOW
    group = D_HEADS // D_KV_HEADS

    @pl.when(qi == 0)
    def _():
        lane = lax.broadcasted_iota(jnp.int32, (seq, LANES), 1)
        ones = jnp.ones((seq, LANES), vx_ref.dtype)
        for src, dst in ((k_ref, kx_ref), (v_ref, vx_ref)):
            x = src[...].astype(f32)
            x0 = jnp.where(lane < D_DIM, x, 0.0)
            x1 = jnp.where(lane >= D_DIM, x, 0.0)
            for i, xi in enumerate((x0, pltpu.roll(x0, D_DIM, 1), pltpu.roll(x1, D_DIM, 1), x1)):
                dst[i, :, 0:LANES] = xi.astype(dst.dtype)
        for i in range(4):
            vx_ref[i, :, LANES:2 * LANES] = ones

    t0 = qi * BLOCK
    ws = pl.multiple_of(jnp.clip(t0 - WINDOW, 0, seq - span), BLOCK)
    rel = (lax.broadcasted_iota(jnp.int32, (BLOCK, span), 0)
           - lax.broadcasted_iota(jnp.int32, (BLOCK, span), 1)) + (t0 - ws)
    dist = jnp.abs(rel)
    neg_dist = jnp.where(dist <= WINDOW, dist.astype(f32) * (-_LOG2E), _NEG_INF)

    n_pairs = D_HEADS // 2
    lookahead = 2
    scores = {}

    def score_pair(pair):
        kv = (2 * pair) // group
        qp = (q_ref[:, pair * LANES:(pair + 1) * LANES].astype(f32) * (D_DIM ** -0.5 * _LOG2E)).astype(q_ref.dtype)
        for half in range(2):
            scores[2 * pair + half] = _dot_nt(qp, kx_ref[2 * kv + half, pl.ds(ws, span), :])

    def attend(hd, kv_slot):
        s = scores[hd] + slopes_ref[hd] * neg_dist
        snk = sink_ref[hd] * _LOG2E
        m = jnp.maximum(jnp.max(s, axis=1, keepdims=True), snk)
        o = _dot(jnp.exp2(s - m).astype(jnp.bfloat16), vx_ref[kv_slot, pl.ds(ws, span), :])
        return o[:, 0:LANES] / (o[:, LANES:] + jnp.exp2(snk - m))

    for pair in range(lookahead):
        score_pair(pair)
    for pair in range(n_pairs):
        kv = (2 * pair) // group
        out = attend(2 * pair, 2 * kv) + attend(2 * pair + 1, 2 * kv + 1)
        o_ref[:, pair * LANES:(pair + 1) * LANES] = out.astype(o_ref.dtype)
        if pair + lookahead < n_pairs:
            score_pair(pair + lookahead)


def _window_attention(pb, attn_sink, slopes_d):
    bsz, seq, _ = pb.shape
    kern = functools.partial(_window_attn_kernel, seq=seq)
    width = D_HEADS * D_DIM
    return pl.pallas_call(
        kern,
        grid=(bsz, seq // BLOCK),
        in_specs=[
            pl.BlockSpec(memory_space=pltpu.SMEM),
            pl.BlockSpec(memory_space=pltpu.SMEM),
            pl.BlockSpec((None, BLOCK, width), lambda b, i: (b, i, _PB_DQ512)),
            pl.BlockSpec((None, seq, LANES), lambda b, i: (b, 0, _PB_DK)),
            pl.BlockSpec((None, seq, LANES), lambda b, i: (b, 0, _PB_DV)),
        ],
        out_specs=pl.BlockSpec((None, BLOCK, width), lambda b, i: (b, i, 0)),
        out_shape=jax.ShapeDtypeStruct((bsz, seq, width), jnp.bfloat16),
        scratch_shapes=[pltpu.VMEM((4, seq, LANES), jnp.bfloat16), pltpu.VMEM((4, seq, 2 * LANES), jnp.bfloat16)],
        compiler_params=_cparams("parallel", "arbitrary"),
        name="window_attention",
    )(slopes_d, attn_sink, pb, pb, pb)


def _out_proj_kernel(x_ref, ya_ref, yb_ref, yc_ref, yd_ref, w_ref, o_ref):
    acc = x_ref[...]
    for g, y_ref in enumerate((ya_ref, yb_ref, yc_ref, yd_ref)):
        acc = acc + _dot(y_ref[...], w_ref[g * 512:(g + 1) * 512, :])
    o_ref[...] = acc


def _out_proj(x2d, ys, w_out, tm):
    t, d = x2d.shape
    yspec = pl.BlockSpec((tm, 512), lambda i: (i, 0))
    return pl.pallas_call(
        _out_proj_kernel,
        grid=(t // tm,),
        in_specs=[pl.BlockSpec((tm, d), lambda i: (i, 0)), yspec, yspec, yspec, yspec,
                  pl.BlockSpec((d, d), lambda i: (0, 0), pipeline_mode=pl.Buffered(1))],
        out_specs=pl.BlockSpec((tm, d), lambda i: (i, 0)),
        out_shape=jax.ShapeDtypeStruct((t, d), jnp.float32),
        compiler_params=_cparams("parallel", vmem_limit=_BIG_VMEM_LIMIT),
        name="out_proj",
    )(x2d, *ys, w_out)


def _ffn_kernel(x_ref, xp_ref, xn_ref, nw_ref, wg_ref, wu_ref, cw_ref, wd_ref, fw_ref, o_ref,
                h_ref, *, seq, tm, final_norm):
    i = pl.program_id(0)
    j = pl.program_id(1)
    f32 = jnp.float32

    def norm(x):
        return x * lax.rsqrt(jnp.mean(x * x, axis=-1, keepdims=True) + EPS) * nw_ref[...]

    def d_ff_step(first):
        gate = _dot(h_ref[...], wg_ref[...])
        gm = gate[0:tm]
        rowi = lax.broadcasted_iota(jnp.int32, gm.shape, 0)
        g_dn = jnp.where(rowi == 0, gate[tm + 7:tm + 8], pltpu.roll(gm, 1, 0))
        g_up = jnp.where(rowi == tm - 1, gate[tm + 8:tm + 9], pltpu.roll(gm, tm - 1, 0))
        cw = cw_ref[...]
        g = g_dn * cw[0:1] + gm * cw[1:2] + g_up * cw[2:3]
        up = _dot(h_ref[0:tm, :], wu_ref[...])
        act = (g * jax.nn.sigmoid(g) * up).astype(wd_ref.dtype)
        down = _dot(act, wd_ref[...])
        if first:
            o_ref[...] = down
        else:
            o_ref[...] += down

    @pl.when(j == 0)
    def _():
        h_ref[0:tm, :] = norm(x_ref[...]).astype(h_ref.dtype)
        has_prev = ((i * tm) % seq != 0).astype(f32)
        has_next = (((i + 1) * tm) % seq != 0).astype(f32)
        halo = jnp.concatenate([norm(xp_ref[...]) * has_prev, norm(xn_ref[...]) * has_next], axis=0)
        h_ref[tm:tm + 16, :] = halo.astype(h_ref.dtype)
        d_ff_step(True)

    @pl.when(j > 0)
    def _():
        d_ff_step(False)

    @pl.when(j == pl.num_programs(1) - 1)
    def _():
        y = x_ref[...] + o_ref[...]
        if final_norm:
            y = y * lax.rsqrt(jnp.mean(y * y, axis=-1, keepdims=True) + EPS) * fw_ref[...]
        o_ref[...] = y


def _ffn(x2d, seq, nw, wg, wu, cw, wd, fw, final_norm, tm, tn):
    t, d = x2d.shape
    nff = wg.shape[1]
    rb = tm // 8
    last8 = t // 8 - 1
    kern = functools.partial(_ffn_kernel, seq=seq, tm=tm, final_norm=final_norm)
    return pl.pallas_call(
        kern,
        grid=(t // tm, nff // tn),
        in_specs=[
            pl.BlockSpec((tm, d), lambda i, j: (i, 0)),
            pl.BlockSpec((8, d), lambda i, j: (jnp.maximum(i * rb - 1, 0), 0)),
            pl.BlockSpec((8, d), lambda i, j: (jnp.minimum((i + 1) * rb, last8), 0)),
            pl.BlockSpec((1, d), lambda i, j: (0, 0)),
            pl.BlockSpec((d, tn), lambda i, j: (0, j)),
            pl.BlockSpec((d, tn), lambda i, j: (0, j)),
            pl.BlockSpec((3, tn), lambda i, j: (0, j)),
            pl.BlockSpec((tn, d), lambda i, j: (j, 0)),
            pl.BlockSpec((1, d), lambda i, j: (0, 0)),
        ],
        out_specs=pl.BlockSpec((tm, d), lambda i, j: (i, 0)),
        out_shape=jax.ShapeDtypeStruct((t, d), jnp.float32),
        scratch_shapes=[pltpu.VMEM((tm + 16, d), jnp.bfloat16)],
        compiler_params=_cparams("parallel", "arbitrary", vmem_limit=_BIG_VMEM_LIMIT),
        name="ffn",
    )(x2d, x2d, x2d, nw, wg, wu, cw, wd, fw)


def _prepare_layer(l, norm1_w, w_in, qk_conv_w, diff_lambda, ret_decay_logit, mlstm_gate_bias, attn_sink,
                   w_out, norm2_w, w_gate, w_up, ffn_conv_w, w_down):
    bf16 = jnp.bfloat16
    f32 = jnp.float32
    wi = w_in[l]
    w_pb = jnp.concatenate([wi[:, _OFF_A:_OFF_CQ], wi[:, _OFF_CV:_OFF_GATES], wi[:, _OFF_D:N_IN]], axis=1)
    wgates = jnp.pad(wi[:, _OFF_GATES:_OFF_D], ((0, 0), (0, LANES - N_GATES)))
    w_pf = jnp.concatenate([wi[:, _OFF_CQ:_OFF_CV], wgates], axis=1)
    gbias = mlstm_gate_bias[l].astype(f32)
    dec = jnp.zeros((8, LANES), f32).at[0:2, 0:B_HEADS].set(ret_decay_logit[l].astype(f32))
    return dict(
        norm1=norm1_w[l].reshape(1, D_MODEL).astype(f32),
        w_pb=w_pb.astype(bf16), w_pf=w_pf.astype(bf16),
        wq_conv=qk_conv_w[l][:, :C_HEADS * C_DIM].astype(f32),
        wk_conv=qk_conv_w[l][:, C_HEADS * C_DIM:].astype(f32),
        gbias=gbias, dec=dec,
        diff_lambda=diff_lambda[l].astype(f32), sink=attn_sink[l].astype(f32),
        w_out=w_out[l].astype(bf16),
        norm2=norm2_w[l].reshape(1, D_MODEL).astype(f32),
        w_gate=w_gate[l].astype(bf16), w_up=w_up[l].astype(bf16),
        ffn_conv=ffn_conv_w[l].astype(f32), w_down=w_down[l].astype(bf16),
    )


def _tile(n, pref):
    t = min(pref, n)
    while n % t:
        t //= 2
    return t


def _layer(x2d, bsz, seq, layer_idx, p, slopes, final_w, final_norm):
    t = bsz * seq
    tm_in = _tile(t, 1024)
    pb, pf = _in_proj(x2d, p["norm1"], p["w_pb"], p["w_pf"], tm_in, _IN_PROJ_TN)
    pb = pb.reshape(bsz, seq, PB_WIDTH)
    pf = pf.reshape(bsz, seq, PF_WIDTH)
    gates_t = pf[:, :, _PF_GATES * LANES:_PF_GATES * LANES + N_GATES].reshape(bsz, seq, 4, C_HEADS)
    gates_t = jnp.transpose(gates_t, (0, 3, 2, 1)).reshape(bsz, C_HEADS, 4, seq // BLOCK, BLOCK)
    y_a = _diff_attention(pb, p["diff_lambda"], slopes[D_HEADS:], layer_idx,
                          _tile(seq // _ATTN_TILES_PER_BODY, _ATTN_TILE),
                          _tile(seq // _ATTN_TILES_PER_BODY, _ATTN_TILE))
    y_b = _retention(pb, p["dec"])
    y_c = _mlstm(pb, pf, gates_t, p["wq_conv"], p["wk_conv"], p["gbias"])
    y_d = _window_attention(pb, p["sink"], slopes[:D_HEADS])
    ys = [y.reshape(t, 512) for y in (y_a, y_b, y_c, y_d)]
    x1 = _out_proj(x2d, ys, p["w_out"], _tile(t, 1024))
    return _ffn(x1, seq, p["norm2"], p["w_gate"], p["w_up"], p["ffn_conv"], p["w_down"], final_w,
                final_norm, _tile(seq, 1024), 512)


def _trunk(x, layers, slopes, final_w):
    bsz, seq, d = x.shape
    x2d = x.reshape(bsz * seq, d)
    for l, p in enumerate(layers):
        x2d = _layer(x2d, bsz, seq, l, p, slopes, final_w, l == len(layers) - 1)
    return x2d.reshape(bsz, seq, d)


def kernel(x_prompt, x_sample, norm1_w, w_in, qk_conv_w, diff_lambda, ret_decay_logit, mlstm_gate_bias,
           attn_sink, w_out, norm2_w, w_gate, w_up, ffn_conv_w, w_down, final_norm_w):
    layers = [_prepare_layer(l, norm1_w, w_in, qk_conv_w, diff_lambda, ret_decay_logit, mlstm_gate_bias,
                             attn_sink, w_out, norm2_w, w_gate, w_up, ffn_conv_w, w_down)
              for l in range(DEPTH)]
    n = A_HEADS + D_HEADS
    slopes = 2.0 ** (-8.0 * jnp.arange(1, n + 1, dtype=jnp.float32) / n)
    final_w = final_norm_w.reshape(1, D_MODEL).astype(jnp.float32)
    return (_trunk(x_prompt, layers, slopes, final_w), _trunk(x_sample, layers, slopes, final_w))
```

```python
import functools
import math

import jax
import jax.numpy as jnp
from jax import lax
from jax.experimental import pallas as pl
from jax.experimental.pallas import tpu as pltpu

D_MODEL = 2048
DEPTH = 2
A_HEADS = 4
A_QK_DIM = 64
A_V_DIM = 128
B_HEADS = 4
B_DIM = 128
C_HEADS = 4
C_DIM = 128
D_HEADS = 8
D_KV_HEADS = 2
D_DIM = 64
WINDOW = 128
BLOCK = 128
D_FF = 5632
EPS = 1e-6
LANES = 128

_OFF_A = 0
_OFF_B = 1536
_OFF_CQ = 3584
_OFF_CV = 4608
_OFF_GATES = 5632
_OFF_D = 5648
N_IN = 6416

_PB_AQ, _PB_AK, _PB_AV = 0, 4, 8
_PB_BQ, _PB_BK, _PB_BV, _PB_BG = 12, 16, 20, 24
_PB_CV, _PB_CO = 28, 32
_PB_DQ512 = 9
_PB_DK, _PB_DV = 40, 41
PB_WIDTH = 42 * LANES
_IN_PROJ_TN = 1792
_PF_CQ, _PF_CK, _PF_GATES = 0, 4, 8
PF_WIDTH = 9 * LANES
N_GATES = 16

_VMEM_LIMIT = 56 * 1024 * 1024
_BIG_VMEM_LIMIT = 61 * 1024 * 1024
_SCAN_UNROLL = 4
_NEG_INF = float("-inf")


def _cparams(*sem, vmem_limit=_VMEM_LIMIT):
    return pltpu.CompilerParams(dimension_semantics=sem, vmem_limit_bytes=vmem_limit)


def _dot(a, b):
    return jnp.dot(a, b, preferred_element_type=jnp.float32)


def _dot_nt(a, b):
    return lax.dot_general(a, b, (((1,), (1,)), ((), ())), preferred_element_type=jnp.float32)


def _dot_tn(a, b):
    return lax.dot_general(a, b, (((0,), (0,)), ((), ())), preferred_element_type=jnp.float32)


def _log_sigmoid(x):
    return -(jnp.maximum(-x, 0.0) + jnp.log1p(jnp.exp(-jnp.abs(x))))


def _split_hi_lo(x):
    hi = x.astype(jnp.bfloat16)
    lo = (x - hi.astype(jnp.float32)).astype(jnp.bfloat16)
    return hi, lo


def _in_proj_kernel(x_ref, nw_ref, wb_ref, wf_ref, ob_ref, of_ref, h_ref, *, nb):
    j = pl.program_id(1)

    @pl.when(j == 0)
    def _():
        x = x_ref[...]
        ms = jnp.mean(x * x, axis=-1, keepdims=True)
        h = (x * lax.rsqrt(ms + EPS) * nw_ref[...]).astype(h_ref.dtype)
        h_ref[...] = h
        ob_ref[...] = _dot(h, wb_ref[...]).astype(ob_ref.dtype)

    @pl.when((j > 0) & (j < nb))
    def _():
        ob_ref[...] = _dot(h_ref[...], wb_ref[...]).astype(ob_ref.dtype)

    @pl.when(j == nb)
    def _():
        of_ref[...] = _dot(h_ref[...], wf_ref[...])


def _in_proj(x2d, nw, w_pb, w_pf, tm, tn):
    t, d = x2d.shape
    nb = PB_WIDTH // tn
    return pl.pallas_call(
        functools.partial(_in_proj_kernel, nb=nb),
        grid=(t // tm, nb + 1),
        in_specs=[
            pl.BlockSpec((tm, d), lambda i, j: (i, 0)),
            pl.BlockSpec((1, d), lambda i, j: (0, 0)),
            pl.BlockSpec((d, tn), lambda i, j: (0, jnp.minimum(j, nb - 1))),
            pl.BlockSpec((d, PF_WIDTH), lambda i, j: (0, 0), pipeline_mode=pl.Buffered(1)),
        ],
        out_specs=[pl.BlockSpec((tm, tn), lambda i, j: (i, jnp.minimum(j, nb - 1))),
                   pl.BlockSpec((tm, PF_WIDTH), lambda i, j: (i, 0))],
        out_shape=[jax.ShapeDtypeStruct((t, PB_WIDTH), jnp.bfloat16),
                   jax.ShapeDtypeStruct((t, PF_WIDTH), jnp.float32)],
        scratch_shapes=[pltpu.VMEM((tm, d), jnp.bfloat16)],
        compiler_params=_cparams("parallel", "arbitrary", vmem_limit=_BIG_VMEM_LIMIT),
        name="norm_in_proj",
    )(x2d, nw, w_pb, w_pf)


_LOG2E = 1.4426950408889634
_N_BIAS_LANES = 3
_V_ROWS = A_V_DIM + 16
_ATTN_TILES_PER_BODY = 4
_ATTN_TILE = 512


def _diff_attn_kernel(slopes_ref, q_ref, qn_ref, k_ref, v_ref, dl_ref, o_ref,
                      kx_ref, vt_ref, qt_ref, m_ref, acc_ref, s_ref, cmax_ref, bias_ref, *, seq, tq, tk,
                      lam_init):
    head = pl.program_id(1)
    qi = pl.program_id(2)
    f32 = jnp.float32
    bf16 = jnp.bfloat16
    sigma = slopes_ref[head] * _LOG2E
    nk = seq // tk
    prep_rows = tk

    @pl.when(qi == 0)
    def _():
        lane = lax.broadcasted_iota(jnp.int32, (prep_rows, LANES), 1)
        row = lax.broadcasted_iota(jnp.int32, (prep_rows, LANES), 0)
        ones_row = jnp.where(lax.broadcasted_iota(jnp.int32, (_V_ROWS - A_V_DIM, tk), 0) == 0, 1.0, 0.0)

        def prep(c, carry):
            r0 = pl.multiple_of(c * prep_rows, prep_rows)
            kk = k_ref[pl.ds(r0, prep_rows), :].astype(f32)
            col = sigma * (row + r0).astype(f32)
            c1 = col.astype(bf16).astype(f32)
            c2 = (col - c1).astype(bf16).astype(f32)
            c3 = col - c1 - c2
            bias = jnp.where(lane == A_QK_DIM, c1, jnp.where(lane == A_QK_DIM + 1, c2,
                             jnp.where(lane == A_QK_DIM + 2, c3, 0.0)))
            kx_ref[0, pl.ds(r0, prep_rows), :] = jnp.where(lane < A_QK_DIM, kk, bias).astype(bf16)
            kx_ref[1, pl.ds(r0, prep_rows), :] = jnp.where(
                lane < A_QK_DIM, pltpu.roll(kk, A_QK_DIM, 1), bias).astype(bf16)
            vt_ref[c, 0:A_V_DIM, :] = v_ref[pl.ds(r0, prep_rows), :].astype(f32).T.astype(bf16)
            vt_ref[c, A_V_DIM:_V_ROWS, :] = ones_row.astype(bf16)
            return carry

        lax.fori_loop(0, nk, prep, 0)
        rel = (lax.broadcasted_iota(jnp.int32, (tk, tq), 1)
               - lax.broadcasted_iota(jnp.int32, (tk, tq), 0)).astype(f32)
        bias_ref[...] = sigma * jnp.abs(rel)

    qrow = lax.broadcasted_iota(jnp.int32, (LANES, tq), 0)
    bias_rows = lax.broadcasted_iota(jnp.int32, (16, tq), 0) < _N_BIAS_LANES

    def build_q_variants(src_ref, base_slot):
        qt = (src_ref[...].astype(f32) * (A_QK_DIM ** -0.5 * _LOG2E)).T
        for mp in range(2):
            qm = qt if mp == 0 else pltpu.roll(qt, A_QK_DIM, 0)
            base = jnp.where(qrow < A_QK_DIM, qm, 0.0).astype(bf16)
            for var, sign in enumerate((1.0, 0.0, -1.0)):
                qt_ref[base_slot + 3 * mp + var] = base
                qt_ref[base_slot + 3 * mp + var, A_QK_DIM:A_QK_DIM + 16, :] = (
                    jnp.where(bias_rows, sign, 0.0).astype(bf16))

    def diagonal_scores(tile, base_slot, mp):
        k_start = pl.multiple_of(tile * tk, tk)
        s = _dot(kx_ref[mp, pl.ds(k_start, tk), :], qt_ref[base_slot + 3 * mp + 1]) - bias_ref[...]
        s_ref[0, mp] = s
        cmax_ref[0, mp] = jnp.max(s, axis=0, keepdims=True)

    qslot = (qi % 2) * 6
    qslot_next = 6 - qslot

    @pl.when(qi == 0)
    def _():
        build_q_variants(q_ref, 0)
        for mp in range(2):
            diagonal_scores(0, 0, mp)

    q0 = qi * tq
    rho = sigma * (lax.broadcasted_iota(jnp.int32, (1, tq), 1) + q0).astype(f32)
    m_ref[...] = jnp.full(m_ref.shape, _NEG_INF, f32)
    acc_ref[...] = jnp.zeros(acc_ref.shape, f32)

    kd = q0 // tk

    def key_tile(j):
        jm = jnp.minimum(j, nk - 1) - 1
        return jnp.where(j == 0, kd, jnp.where(jm < kd, jm, jm + 1))

    def scores(j, slot, mp):
        kb = key_tile(j)
        var = jnp.where(kb < kd, 0, jnp.where(kb > kd, 2, 1))
        k_start = pl.multiple_of(kb * tk, tk)
        s = _dot(kx_ref[mp, pl.ds(k_start, tk), :], qt_ref[qslot + 3 * mp + var])
        s_ref[slot, mp] = s
        cmax_ref[slot, mp] = jnp.max(s, axis=0, keepdims=True)

    def softmax_pv(j, slot, mp):
        kb = key_tile(j)
        shift = jnp.where(kb < kd, -1.0, jnp.where(kb > kd, 1.0, 0.0)) * rho
        m_old = m_ref[mp]
        m_new = jnp.maximum(m_old, cmax_ref[slot, mp] + shift)
        p = jnp.exp2(s_ref[slot, mp] - (m_new - shift))
        acc_ref[mp] = jnp.exp2(m_old - m_new) * acc_ref[mp] + _dot(vt_ref[kb], p.astype(bf16))
        m_ref[mp] = m_new

    def body(i, carry):
        for u in range(_ATTN_TILES_PER_BODY):
            j = _ATTN_TILES_PER_BODY * i + u
            for mp in range(2):
                scores(j + 1, (u + 1) % _ATTN_TILES_PER_BODY, mp)
                softmax_pv(j, u, mp)
        return carry

    n_bodies = nk // _ATTN_TILES_PER_BODY
    lax.fori_loop(0, n_bodies - 1, body, 0)

    build_q_variants(qn_ref, qslot_next)
    kd_next = jnp.minimum(qi + 1, seq // tq - 1)
    for u in range(_ATTN_TILES_PER_BODY):
        j = _ATTN_TILES_PER_BODY * (n_bodies - 1) + u
        for mp in range(2):
            if u + 1 < _ATTN_TILES_PER_BODY:
                scores(j + 1, u + 1, mp)
            else:
                diagonal_scores(kd_next, qslot_next, mp)
            softmax_pv(j, u, mp)

    dl = dl_ref[...]
    lam = (jnp.exp(jnp.sum(dl[0:1] * dl[1:2], axis=1, keepdims=True))
           - jnp.exp(jnp.sum(dl[2:3] * dl[3:4], axis=1, keepdims=True)) + lam_init)
    a0 = acc_ref[0]
    a1 = acc_ref[1]
    ot = (a0[0:A_V_DIM] / a0[A_V_DIM:A_V_DIM + 1] - lam * (a1[0:A_V_DIM] / a1[A_V_DIM:A_V_DIM + 1]))
    ot = ot * lax.rsqrt(jnp.mean(ot * ot, axis=0, keepdims=True) + EPS)
    o_ref[...] = (ot * (1.0 - lam_init)).T.astype(o_ref.dtype)


def _diff_attention(pb, diff_lambda, slopes_a, layer_idx, tq, tk):
    bsz, seq, _ = pb.shape
    assert seq % (_ATTN_TILES_PER_BODY * tk) == 0 and tk == tq
    lam_init = 0.8 - 0.6 * math.exp(-0.3 * layer_idx)
    kern = functools.partial(_diff_attn_kernel, seq=seq, tq=tq, tk=tk, lam_init=lam_init)
    return pl.pallas_call(
        kern,
        grid=(bsz, A_HEADS, seq // tq),
        in_specs=[
            pl.BlockSpec(memory_space=pltpu.SMEM),
            pl.BlockSpec((None, tq, LANES), lambda b, h, i: (b, i, _PB_AQ + h)),
            pl.BlockSpec((None, tq, LANES), lambda b, h, i: (b, jnp.minimum(i + 1, seq // tq - 1), _PB_AQ + h)),
            pl.BlockSpec((None, seq, LANES), lambda b, h, i: (b, 0, _PB_AK + h)),
            pl.BlockSpec((None, seq, LANES), lambda b, h, i: (b, 0, _PB_AV + h)),
            pl.BlockSpec((4, A_QK_DIM), lambda b, h, i: (0, 0)),
        ],
        out_specs=pl.BlockSpec((None, tq, LANES), lambda b, h, i: (b, i, h)),
        out_shape=jax.ShapeDtypeStruct((bsz, seq, A_HEADS * A_V_DIM), jnp.bfloat16),
        scratch_shapes=[
            pltpu.VMEM((2, seq, LANES), jnp.bfloat16),
            pltpu.VMEM((seq // tk, _V_ROWS, tk), jnp.bfloat16),
            pltpu.VMEM((12, LANES, tq), jnp.bfloat16),
            pltpu.VMEM((2, 1, tq), jnp.float32),
            pltpu.VMEM((2, _V_ROWS, tq), jnp.float32),
            pltpu.VMEM((_ATTN_TILES_PER_BODY, 2, tk, tq), jnp.float32),
            pltpu.VMEM((_ATTN_TILES_PER_BODY, 2, 1, tq), jnp.float32),
            pltpu.VMEM((tk, tq), jnp.float32),
        ],
        compiler_params=_cparams("parallel", "parallel", "arbitrary"),
        name="diff_attention",
    )(slopes_a, pb, pb, pb, pb, diff_lambda)


def _retention_kernel(q_ref, k_ref, v_ref, g_ref, dec_ref, o_ref, of_ref, ob_ref, *, seq):
    head = pl.program_id(1)
    nc = seq // BLOCK
    f32 = jnp.float32

    lgv = _log_sigmoid(dec_ref[...])
    rr = lax.broadcasted_iota(jnp.int32, lgv.shape, 0)
    cc = lax.broadcasted_iota(jnp.int32, lgv.shape, 1)

    def pick(row):
        sel = jnp.where((rr == row) & (cc == head), lgv, 0.0)
        return jnp.sum(jnp.sum(sel, axis=1, keepdims=True), axis=0, keepdims=True)

    lg_f, lg_b = pick(0), pick(1)
    ri = lax.broadcasted_iota(jnp.int32, (BLOCK, BLOCK), 0).astype(f32)
    ci = lax.broadcasted_iota(jnp.int32, (BLOCK, BLOCK), 1).astype(f32)
    diff = ri - ci
    decay_f = jnp.where(diff >= 0, jnp.exp(lg_f * jnp.maximum(diff, 0.0)), 0.0)
    decay_b = jnp.where(diff <= 0, jnp.exp(lg_b * jnp.maximum(-diff, 0.0)), 0.0)
    idx = lax.broadcasted_iota(jnp.int32, (BLOCK, 1), 0).astype(f32)
    qdec_f = jnp.exp(lg_f * (idx + 1.0))
    kdec_f = jnp.exp(lg_f * (BLOCK - 1.0 - idx))
    qdec_b = jnp.exp(lg_b * (BLOCK - idx))
    kdec_b = jnp.exp(lg_b * idx)
    cdec_f = jnp.exp(lg_f * BLOCK)
    cdec_b = jnp.exp(lg_b * BLOCK)
    kscale = B_DIM ** -0.5

    def scores_and_kv(t0, backward):
        qb = q_ref[pl.ds(t0, BLOCK), :]
        kb = k_ref[pl.ds(t0, BLOCK), :].astype(f32) * kscale
        vb = v_ref[pl.ds(t0, BLOCK), :]
        kdec = kdec_b if backward else kdec_f
        return dict(t0=t0, backward=backward, qb=qb, vb=vb,
                    qk=_dot_nt(qb, kb.astype(qb.dtype)), kv=_dot_tn((kb * kdec).astype(vb.dtype), vb))

    def inner_product(g):
        s = g["qk"] * (decay_b if g["backward"] else decay_f)
        g["inner"] = _dot(s.astype(g["vb"].dtype), g["vb"])

    def with_state(g, state, park_ref, other_ref):
        qdec, cdec = (qdec_b, cdec_b) if g["backward"] else (qdec_f, cdec_f)
        cross = _dot(g["qb"], state.astype(g["qb"].dtype)) * qdec
        rows = pl.ds(g["t0"], BLOCK)
        if other_ref is None:
            park_ref[rows, :] = g["inner"] + cross
        else:
            r = g["inner"] + cross + other_ref[rows, :]
            r = r * lax.rsqrt(jnp.mean(r * r, axis=-1, keepdims=True) + EPS)
            gate = g_ref[rows, :].astype(f32)
            o_ref[rows, :] = (gate * jax.nn.sigmoid(gate) * r).astype(o_ref.dtype)
        return state * cdec + g["kv"]

    def make_body(second_half):
        def body(it, carry):
            rf, rb = carry
            steps = []
            for u in range(_SCAN_UNROLL):
                c = it * _SCAN_UNROLL + u
                steps.append(scores_and_kv(pl.multiple_of(c * BLOCK, BLOCK), False))
                steps.append(scores_and_kv(pl.multiple_of((nc - 1 - c) * BLOCK, BLOCK), True))
            for g in steps:
                inner_product(g)
            for g in steps:
                if g["backward"]:
                    rb = with_state(g, rb, ob_ref, of_ref if second_half else None)
                else:
                    rf = with_state(g, rf, of_ref, ob_ref if second_half else None)
            return rf, rb
        return body

    zero = jnp.zeros((B_DIM, B_DIM), f32)
    n_bodies = nc // _SCAN_UNROLL
    state = lax.fori_loop(0, n_bodies // 2, make_body(False), (zero, zero))
    lax.fori_loop(n_bodies // 2, n_bodies, make_body(True), state)


def _retention(pb, dec_pad):
    bsz, seq, _ = pb.shape
    assert seq % (2 * _SCAN_UNROLL * BLOCK) == 0
    kern = functools.partial(_retention_kernel, seq=seq)

    def col(base):
        return pl.BlockSpec((None, seq, LANES), lambda b, h: (b, 0, base + h))

    return pl.pallas_call(
        kern,
        grid=(bsz, B_HEADS),
        in_specs=[col(_PB_BQ), col(_PB_BK), col(_PB_BV), col(_PB_BG),
                  pl.BlockSpec((8, LANES), lambda b, h: (0, 0))],
        out_specs=pl.BlockSpec((None, seq, LANES), lambda b, h: (b, 0, h)),
        out_shape=jax.ShapeDtypeStruct((bsz, seq, B_HEADS * B_DIM), jnp.bfloat16),
        scratch_shapes=[pltpu.VMEM((seq, B_DIM), jnp.float32), pltpu.VMEM((seq, B_DIM), jnp.float32)],
        compiler_params=_cparams("parallel", "parallel"),
        name="retention",
    )(pb, pb, pb, pb, dec_pad)


_ROWS_PER_DIR = 6


def _mlstm_kernel(cq_ref, ck_ref, gt_ref, v_ref, og_ref, wq_ref, wk_ref, gb_ref, o_ref,
                  qc_ref, kt_ref, vx_ref, rows_ref, of_ref, ob_ref, *, seq):
    nc = seq // BLOCK
    f32 = jnp.float32
    bf16 = jnp.bfloat16
    rowi = lax.broadcasted_iota(jnp.int32, (BLOCK, LANES), 0)
    coli = lax.broadcasted_iota(jnp.int32, (BLOCK, LANES), 1)
    lane_n = lax.broadcasted_iota(jnp.int32, (nc, LANES), 1)
    tril = rowi >= coli
    triu = rowi <= coli
    tril_b = tril.astype(bf16)
    triu_b = triu.astype(bf16)
    ones_b = jnp.ones((BLOCK, LANES), bf16)
    head = pl.program_id(1)

    def cummax_lanes(x, backward):
        sh = 1
        while sh < LANES:
            if backward:
                x = jnp.maximum(x, jnp.where(lane_n < LANES - sh, pltpu.roll(x, LANES - sh, 1), _NEG_INF))
            else:
                x = jnp.maximum(x, jnp.where(lane_n >= sh, pltpu.roll(x, sh, 1), _NEG_INF))
            sh *= 2
        return x

    for d, (op, last) in enumerate(((triu_b, LANES - 1), (tril_b, 0))):
        lf = _log_sigmoid(gt_ref[2 * d + 1] + gb_ref[2 * d + 1, head]) * _LOG2E
        hi, lo = _split_hi_lo(lf)
        b = _dot(hi, op) + _dot(lo, op)
        a = (gt_ref[2 * d] + gb_ref[2 * d, head]) * _LOG2E - b
        bl = jnp.broadcast_to(b[:, last:last + 1], (nc, LANES))
        ml = jnp.broadcast_to(jnp.max(bl + a, axis=1, keepdims=True), (nc, LANES))
        for r, val in enumerate((a, cummax_lanes(a, d == 1), b, jnp.exp2(bl + a - ml), bl, ml)):
            rows_ref[d * _ROWS_PER_DIR + r] = val

    def conv_silu(src_ref, w, c, t0):
        x = src_ref[pl.ds(t0, BLOCK), :]
        prev = src_ref[pl.ds(jnp.maximum(t0 - 1, 0), 1), :] * jnp.where(c > 0, 1.0, 0.0)
        nxt = src_ref[pl.ds(jnp.minimum(t0 + BLOCK, seq - 1), 1), :] * jnp.where(c < nc - 1, 1.0, 0.0)
        xd = jnp.where(rowi == 0, prev, pltpu.roll(x, 1, 0))
        xu = jnp.where(rowi == BLOCK - 1, nxt, pltpu.roll(x, BLOCK - 1, 0))
        y = xd * w[0:1] + x * w[1:2] + xu * w[2:3]
        return y * jax.nn.sigmoid(y)

    def prep(c, carry):
        t0 = pl.multiple_of(c * BLOCK, BLOCK)
        qc_ref[pl.ds(t0, BLOCK), :] = (conv_silu(cq_ref, wq_ref[...], c, t0) * (C_DIM ** -0.5)).astype(bf16)
        kt_ref[c] = conv_silu(ck_ref, wk_ref[...], c, t0).T.astype(bf16)
        vx_ref[pl.ds(t0, BLOCK), 0:C_DIM] = v_ref[pl.ds(t0, BLOCK), :]
        vx_ref[pl.ds(t0, BLOCK), C_DIM:2 * C_DIM] = ones_b
        return carry

    lax.fori_loop(0, nc, prep, 0, unroll=2)

    def load_rows(ch, backward):
        t0 = pl.multiple_of(ch * BLOCK, BLOCK)
        base = _ROWS_PER_DIR if backward else 0
        a, c, b, w, bl, ml = (rows_ref[base + r, pl.ds(ch, 1), :] for r in range(_ROWS_PER_DIR))
        kt = kt_ref[ch]
        return dict(t0=t0, backward=backward, bl=bl[:, 0:1], m_loc=ml[:, 0:1],
                    a_t=jnp.where(triu if backward else tril, a, _NEG_INF),
                    b_full=jnp.broadcast_to(b, (BLOCK, LANES)).T,
                    c_full=jnp.broadcast_to(c, (BLOCK, LANES)).T,
                    kw=(kt.astype(f32) * w).astype(bf16), kt=kt)

    def first_products(g):
        vx = vx_ref[pl.ds(g["t0"], BLOCK), :]
        g["qk"] = _dot(qc_ref[pl.ds(g["t0"], BLOCK), :], g["kt"])
        g["kv"] = _dot(g["kw"], vx)

    def decayed_scores(g):
        g["s0"] = (g["qk"] * jnp.exp2(g["a_t"] - g["c_full"])).astype(bf16)

    def inner_products(g):
        g["p0"] = _dot(g["s0"], vx_ref[pl.ds(g["t0"], BLOCK), :])

    def with_state(g, state, park_ref, other_ref):
        cx, m = state
        x = _dot(qc_ref[pl.ds(g["t0"], BLOCK), :], cx.astype(bf16))
        n = jnp.maximum(g["c_full"], m)
        e = jnp.exp2(g["c_full"] - n)
        iw = jnp.exp2(m - n)
        num = e * g["p0"][:, 0:C_DIM] + iw * x[:, 0:C_DIM]
        den = e * g["p0"][:, C_DIM:] + iw * x[:, C_DIM:]
        hh = num / jnp.maximum(jnp.abs(den), jnp.exp2(-(g["b_full"] + n)))
        rows = pl.ds(g["t0"], BLOCK)
        if other_ref is None:
            park_ref[rows, :] = hh
        else:
            r = hh + other_ref[rows, :]
            r = r * lax.rsqrt(jnp.mean(r * r, axis=-1, keepdims=True) + EPS)
            o_ref[rows, :] = (jax.nn.sigmoid(og_ref[rows, :].astype(f32)) * r).astype(o_ref.dtype)
        m_new = jnp.maximum(g["bl"] + m, g["m_loc"])
        cw = jnp.exp2(g["bl"] + m - m_new)
        f = jnp.exp2(g["m_loc"] - m_new)
        return cw * cx + f * g["kv"], m_new

    def make_body(second_half):
        def body(it, carry):
            sf, sb = carry
            steps = []
            for u in range(_SCAN_UNROLL):
                c = it * _SCAN_UNROLL + u
                steps.append(load_rows(c, False))
                steps.append(load_rows(nc - 1 - c, True))
            for g in steps:
                first_products(g)
            for g in steps:
                decayed_scores(g)
            for g in steps:
                inner_products(g)
            for g in steps:
                if g["backward"]:
                    sb = with_state(g, sb, ob_ref, of_ref if second_half else None)
                else:
                    sf = with_state(g, sf, of_ref, ob_ref if second_half else None)
            return sf, sb
        return body

    init = (jnp.zeros((C_DIM, 2 * C_DIM), f32), jnp.zeros((1, 1), f32))
    n_bodies = nc // _SCAN_UNROLL
    state = lax.fori_loop(0, n_bodies // 2, make_body(False), (init, init))
    lax.fori_loop(n_bodies // 2, n_bodies, make_body(True), state)


def _mlstm(pb, pf, gates_t, wq_conv, wk_conv, gate_bias):
    bsz, seq, _ = pb.shape
    assert seq % (8 * BLOCK) == 0
    kern = functools.partial(_mlstm_kernel, seq=seq)

    def col(base):
        return pl.BlockSpec((None, seq, LANES), lambda b, h: (b, 0, base + h))

    return pl.pallas_call(
        kern,
        grid=(bsz, C_HEADS),
        in_specs=[col(_PF_CQ), col(_PF_CK),
                  pl.BlockSpec((None, None, 4, seq // BLOCK, LANES), lambda b, h: (b, h, 0, 0, 0)),
                  col(_PB_CV), col(_PB_CO),
                  pl.BlockSpec((3, LANES), lambda b, h: (0, h)),
                  pl.BlockSpec((3, LANES), lambda b, h: (0, h)),
                  pl.BlockSpec(memory_space=pltpu.SMEM)],
        out_specs=pl.BlockSpec((None, seq, LANES), lambda b, h: (b, 0, h)),
        out_shape=jax.ShapeDtypeStruct((bsz, seq, C_HEADS * C_DIM), jnp.bfloat16),
        scratch_shapes=[pltpu.VMEM((seq, C_DIM), jnp.bfloat16),
                        pltpu.VMEM((seq // BLOCK, C_DIM, BLOCK), jnp.bfloat16),
                        pltpu.VMEM((seq, 2 * C_DIM), jnp.bfloat16),
                        pltpu.VMEM((2 * _ROWS_PER_DIR, seq // BLOCK, LANES), jnp.float32),
                        pltpu.VMEM((seq, C_DIM), jnp.float32), pltpu.VMEM((seq, C_DIM), jnp.float32)],
        compiler_params=_cparams("parallel", "parallel"),
        name="mlstm",
    )(pf, pf, gates_t, pb, pb, wq_conv, wk_conv, gate_bias)


def _window_attn_kernel(slopes_ref, sink_ref, q_ref, k_ref, v_ref, o_ref, kx_ref, vx_ref, *, seq):
    qi = pl.program_id(1)
    f32 = jnp.float32
    span = BLOCK + 2 * WINDOW
    group = D_HEADS // D_KV_HEADS

    @pl.when(qi == 0)
    def _():
        lane = lax.broadcasted_iota(jnp.int32, (seq, LANES), 1)
        ones = jnp.ones((seq, LANES), vx_ref.dtype)
        for src, dst in ((k_ref, kx_ref), (v_ref, vx_ref)):
            x = src[...].astype(f32)
            x0 = jnp.where(lane < D_DIM, x, 0.0)
            x1 = jnp.where(lane >= D_DIM, x, 0.0)
            for i, xi in enumerate((x0, pltpu.roll(x0, D_DIM, 1), pltpu.roll(x1, D_DIM, 1), x1)):
                dst[i, :, 0:LANES] = xi.astype(dst.dtype)
        for i in range(4):
            vx_ref[i, :, LANES:2 * LANES] = ones

    t0 = qi * BLOCK
    ws = pl.multiple_of(jnp.clip(t0 - WINDOW, 0, seq - span), BLOCK)
    rel = (lax.broadcasted_iota(jnp.int32, (BLOCK, span), 0)
           - lax.broadcasted_iota(jnp.int32, (BLOCK, span), 1)) + (t0 - ws)
    dist = jnp.abs(rel)
    neg_dist = jnp.where(dist <= WINDOW, dist.astype(f32) * (-_LOG2E), _NEG_INF)

    n_pairs = D_HEADS // 2
    lookahead = 2
    scores = {}

    def score_pair(pair):
        kv = (2 * pair) // group
        qp = (q_ref[:, pair * LANES:(pair + 1) * LANES].astype(f32) * (D_DIM ** -0.5 * _LOG2E)).astype(q_ref.dtype)
        for half in range(2):
            scores[2 * pair + half] = _dot_nt(qp, kx_ref[2 * kv + half, pl.ds(ws, span), :])

    def attend(hd, kv_slot):
        s = scores[hd] + slopes_ref[hd] * neg_dist
        snk = sink_ref[hd] * _LOG2E
        m = jnp.maximum(jnp.max(s, axis=1, keepdims=True), snk)
        o = _dot(jnp.exp2(s - m).astype(jnp.bfloat16), vx_ref[kv_slot, pl.ds(ws, span), :])
        return o[:, 0:LANES] / (o[:, LANES:] + jnp.exp2(snk - m))

    for pair in range(lookahead):
        score_pair(pair)
    for pair in range(n_pairs):
        kv = (2 * pair) // group
        out = attend(2 * pair, 2 * kv) + attend(2 * pair + 1, 2 * kv + 1)
        o_ref[:, pair * LANES:(pair + 1) * LANES] = out.astype(o_ref.dtype)
        if pair + lookahead < n_pairs:
            score_pair(pair + lookahead)


def _window_attention(pb, attn_sink, slopes_d):
    bsz, seq, _ = pb.shape
    kern = functools.partial(_window_attn_kernel, seq=seq)
    width = D_HEADS * D_DIM
    return pl.pallas_call(
        kern,
        grid=(bsz, seq // BLOCK),
        in_specs=[
            pl.BlockSpec(memory_space=pltpu.SMEM),
            pl.BlockSpec(memory_space=pltpu.SMEM),
            pl.BlockSpec((None, BLOCK, width), lambda b, i: (b, i, _PB_DQ512)),
            pl.BlockSpec((None, seq, LANES), lambda b, i: (b, 0, _PB_DK)),
            pl.BlockSpec((None, seq, LANES), lambda b, i: (b, 0, _PB_DV)),
        ],
        out_specs=pl.BlockSpec((None, BLOCK, width), lambda b, i: (b, i, 0)),
        out_shape=jax.ShapeDtypeStruct((bsz, seq, width), jnp.bfloat16),
        scratch_shapes=[pltpu.VMEM((4, seq, LANES), jnp.bfloat16), pltpu.VMEM((4, seq, 2 * LANES), jnp.bfloat16)],
        compiler_params=_cparams("parallel", "arbitrary"),
        name="window_attention",
    )(slopes_d, attn_sink, pb, pb, pb)


def _out_proj_kernel(x_ref, ya_ref, yb_ref, yc_ref, yd_ref, w_ref, o_ref):
    acc = x_ref[...]
    for g, y_ref in enumerate((ya_ref, yb_ref, yc_ref, yd_ref)):
        acc = acc + _dot(y_ref[...], w_ref[g * 512:(g + 1) * 512, :])
    o_ref[...] = acc


def _out_proj(x2d, ys, w_out, tm):
    t, d = x2d.shape
    yspec = pl.BlockSpec((tm, 512), lambda i: (i, 0))
    return pl.pallas_call(
        _out_proj_kernel,
        grid=(t // tm,),
        in_specs=[pl.BlockSpec((tm, d), lambda i: (i, 0)), yspec, yspec, yspec, yspec,
                  pl.BlockSpec((d, d), lambda i: (0, 0), pipeline_mode=pl.Buffered(1))],
        out_specs=pl.BlockSpec((tm, d), lambda i: (i, 0)),
        out_shape=jax.ShapeDtypeStruct((t, d), jnp.float32),
        compiler_params=_cparams("parallel", vmem_limit=_BIG_VMEM_LIMIT),
        name="out_proj",
    )(x2d, *ys, w_out)


def _ffn_kernel(x_ref, xp_ref, xn_ref, nw_ref, wg_ref, wu_ref, cw_ref, wd_ref, fw_ref, o_ref,
                h_ref, *, seq, tm, final_norm):
    i = pl.program_id(0)
    j = pl.program_id(1)
    f32 = jnp.float32

    def norm(x):
        return x * lax.rsqrt(jnp.mean(x * x, axis=-1, keepdims=True) + EPS) * nw_ref[...]

    def d_ff_step(first):
        gate = _dot(h_ref[...], wg_ref[...])
        gm = gate[0:tm]
        rowi = lax.broadcasted_iota(jnp.int32, gm.shape, 0)
        g_dn = jnp.where(rowi == 0, gate[tm + 7:tm + 8], pltpu.roll(gm, 1, 0))
        g_up = jnp.where(rowi == tm - 1, gate[tm + 8:tm + 9], pltpu.roll(gm, tm - 1, 0))
        cw = cw_ref[...]
        g = g_dn * cw[0:1] + gm * cw[1:2] + g_up * cw[2:3]
        up = _dot(h_ref[0:tm, :], wu_ref[...])
        act = (g * jax.nn.sigmoid(g) * up).astype(wd_ref.dtype)
        down = _dot(act, wd_ref[...])
        if first:
            o_ref[...] = down
        else:
            o_ref[...] += down

    @pl.when(j == 0)
    def _():
        h_ref[0:tm, :] = norm(x_ref[...]).astype(h_ref.dtype)
        has_prev = ((i * tm) % seq != 0).astype(f32)
        has_next = (((i + 1) * tm) % seq != 0).astype(f32)
        halo = jnp.concatenate([norm(xp_ref[...]) * has_prev, norm(xn_ref[...]) * has_next], axis=0)
        h_ref[tm:tm + 16, :] = halo.astype(h_ref.dtype)
        d_ff_step(True)

    @pl.when(j > 0)
    def _():
        d_ff_step(False)

    @pl.when(j == pl.num_programs(1) - 1)
    def _():
        y = x_ref[...] + o_ref[...]
        if final_norm:
            y = y * lax.rsqrt(jnp.mean(y * y, axis=-1, keepdims=True) + EPS) * fw_ref[...]
        o_ref[...] = y


def _ffn(x2d, seq, nw, wg, wu, cw, wd, fw, final_norm, tm, tn):
    t, d = x2d.shape
    nff = wg.shape[1]
    rb = tm // 8
    last8 = t // 8 - 1
    kern = functools.partial(_ffn_kernel, seq=seq, tm=tm, final_norm=final_norm)
    return pl.pallas_call(
        kern,
        grid=(t // tm, nff // tn),
        in_specs=[
            pl.BlockSpec((tm, d), lambda i, j: (i, 0)),
            pl.BlockSpec((8, d), lambda i, j: (jnp.maximum(i * rb - 1, 0), 0)),
            pl.BlockSpec((8, d), lambda i, j: (jnp.minimum((i + 1) * rb, last8), 0)),
            pl.BlockSpec((1, d), lambda i, j: (0, 0)),
            pl.BlockSpec((d, tn), lambda i, j: (0, j)),
            pl.BlockSpec((d, tn), lambda i, j: (0, j)),
            pl.BlockSpec((3, tn), lambda i, j: (0, j)),
            pl.BlockSpec((tn, d), lambda i, j: (j, 0)),
            pl.BlockSpec((1, d), lambda i, j: (0, 0)),
        ],
        out_specs=pl.BlockSpec((tm, d), lambda i, j: (i, 0)),
        out_shape=jax.ShapeDtypeStruct((t, d), jnp.float32),
        scratch_shapes=[pltpu.VMEM((tm + 16, d), jnp.bfloat16)],
        compiler_params=_cparams("parallel", "arbitrary", vmem_limit=_BIG_VMEM_LIMIT),
        name="ffn",
    )(x2d, x2d, x2d, nw, wg, wu, cw, wd, fw)


def _prepare_layer(l, norm1_w, w_in, qk_conv_w, diff_lambda, ret_decay_logit, mlstm_gate_bias, attn_sink,
                   w_out, norm2_w, w_gate, w_up, ffn_conv_w, w_down):
    bf16 = jnp.bfloat16
    f32 = jnp.float32
    wi = w_in[l]
    w_pb = jnp.concatenate([wi[:, _OFF_A:_OFF_CQ], wi[:, _OFF_CV:_OFF_GATES], wi[:, _OFF_D:N_IN]], axis=1)
    wgates = jnp.pad(wi[:, _OFF_GATES:_OFF_D], ((0, 0), (0, LANES - N_GATES)))
    w_pf = jnp.concatenate([wi[:, _OFF_CQ:_OFF_CV], wgates], axis=1)
    gbias = mlstm_gate_bias[l].astype(f32)
    dec = jnp.zeros((8, LANES), f32).at[0:2, 0:B_HEADS].set(ret_decay_logit[l].astype(f32))
    return dict(
        norm1=norm1_w[l].reshape(1, D_MODEL).astype(f32),
        w_pb=w_pb.astype(bf16), w_pf=w_pf.astype(bf16),
        wq_conv=qk_conv_w[l][:, :C_HEADS * C_DIM].astype(f32),
        wk_conv=qk_conv_w[l][:, C_HEADS * C_DIM:].astype(f32),
        gbias=gbias, dec=dec,
        diff_lambda=diff_lambda[l].astype(f32), sink=attn_sink[l].astype(f32),
        w_out=w_out[l].astype(bf16),
        norm2=norm2_w[l].reshape(1, D_MODEL).astype(f32),
        w_gate=w_gate[l].astype(bf16), w_up=w_up[l].astype(bf16),
        ffn_conv=ffn_conv_w[l].astype(f32), w_down=w_down[l].astype(bf16),
    )


def _tile(n, pref):
    t = min(pref, n)
    while n % t:
        t //= 2
    return t


def _layer(x2d, bsz, seq, layer_idx, p, slopes, final_w, final_norm):
    t = bsz * seq
    tm_in = _tile(t, 1024)
    pb, pf = _in_proj(x2d, p["norm1"], p["w_pb"], p["w_pf"], tm_in, _IN_PROJ_TN)
    pb = pb.reshape(bsz, seq, PB_WIDTH)
    pf = pf.reshape(bsz, seq, PF_WIDTH)
    gates_t = pf[:, :, _PF_GATES * LANES:_PF_GATES * LANES + N_GATES].reshape(bsz, seq, 4, C_HEADS)
    gates_t = jnp.transpose(gates_t, (0, 3, 2, 1)).reshape(bsz, C_HEADS, 4, seq // BLOCK, BLOCK)
    y_a = _diff_attention(pb, p["diff_lambda"], slopes[D_HEADS:], layer_idx,
                          _tile(seq // _ATTN_TILES_PER_BODY, _ATTN_TILE),
                          _tile(seq // _ATTN_TILES_PER_BODY, _ATTN_TILE))
    y_b = _retention(pb, p["dec"])
    y_c = _mlstm(pb, pf, gates_t, p["wq_conv"], p["wk_conv"], p["gbias"])
    y_d = _window_attention(pb, p["sink"], slopes[:D_HEADS])
    ys = [y.reshape(t, 512) for y in (y_a, y_b, y_c, y_d)]
    x1 = _out_proj(x2d, ys, p["w_out"], _tile(t, 1024))
    return _ffn(x1, seq, p["norm2"], p["w_gate"], p["w_up"], p["ffn_conv"], p["w_down"], final_w,
                final_norm, _tile(seq, 1024), 512)


def _trunk(x, layers, slopes, final_w):
    bsz, seq, d = x.shape
    x2d = x.reshape(bsz * seq, d)
    for l, p in enumerate(layers):
        x2d = _layer(x2d, bsz, seq, l, p, slopes, final_w, l == len(layers) - 1)
    return x2d.reshape(bsz, seq, d)


def kernel(x_prompt, x_sample, norm1_w, w_in, qk_conv_w, diff_lambda, ret_decay_logit, mlstm_gate_bias,
           attn_sink, w_out, norm2_w, w_gate, w_up, ffn_conv_w, w_down, final_norm_w):
    layers = [_prepare_layer(l, norm1_w, w_in, qk_conv_w, diff_lambda, ret_decay_logit, mlstm_gate_bias,
                             attn_sink, w_out, norm2_w, w_gate, w_up, ffn_conv_w, w_down)
              for l in range(DEPTH)]
    n = A_HEADS + D_HEADS
    slopes = 2.0 ** (-8.0 * jnp.arange(1, n + 1, dtype=jnp.float32) / n)
    final_w = final_norm_w.reshape(1, D_MODEL).astype(jnp.float32)
    return (_trunk(x_prompt, layers, slopes, final_w), _trunk(x_sample, layers, slopes, final_w))
```

```python
import functools
import math

import jax
import jax.numpy as jnp
from jax import lax
from jax.experimental import pallas as pl
from jax.experimental.pallas import tpu as pltpu

D_MODEL = 2048
DEPTH = 2
A_HEADS = 4
A_QK_DIM = 64
A_V_DIM = 128
B_HEADS = 4
B_DIM = 128
C_HEADS = 4
C_DIM = 128
D_HEADS = 8
D_KV_HEADS = 2
D_DIM = 64
WINDOW = 128
BLOCK = 128
D_FF = 5632
EPS = 1e-6
LANES = 128

_OFF_A = 0
_OFF_B = 1536
_OFF_CQ = 3584
_OFF_CV = 4608
_OFF_GATES = 5632
_OFF_D = 5648
N_IN = 6416

_PB_AQ, _PB_AK, _PB_AV = 0, 4, 8
_PB_BQ, _PB_BK, _PB_BV, _PB_BG = 12, 16, 20, 24
_PB_CV, _PB_CO = 28, 32
_PB_DQ512 = 9
_PB_DK, _PB_DV = 40, 41
PB_WIDTH = 42 * LANES
_IN_PROJ_TN = 1792
_PF_CQ, _PF_CK, _PF_GATES = 0, 4, 8
PF_WIDTH = 9 * LANES
N_GATES = 16

_VMEM_LIMIT = 56 * 1024 * 1024
_BIG_VMEM_LIMIT = 61 * 1024 * 1024
_SCAN_UNROLL = 4
_NEG_INF = float("-inf")


def _cparams(*sem, vmem_limit=_VMEM_LIMIT):
    return pltpu.CompilerParams(dimension_semantics=sem, vmem_limit_bytes=vmem_limit)


def _dot(a, b):
    return jnp.dot(a, b, preferred_element_type=jnp.float32)


def _dot_nt(a, b):
    return lax.dot_general(a, b, (((1,), (1,)), ((), ())), preferred_element_type=jnp.float32)


def _dot_tn(a, b):
    return lax.dot_general(a, b, (((0,), (0,)), ((), ())), preferred_element_type=jnp.float32)


def _log_sigmoid(x):
    return -(jnp.maximum(-x, 0.0) + jnp.log1p(jnp.exp(-jnp.abs(x))))


def _split_hi_lo(x):
    hi = x.astype(jnp.bfloat16)
    lo = (x - hi.astype(jnp.float32)).astype(jnp.bfloat16)
    return hi, lo


def _in_proj_kernel(x_ref, nw_ref, wb_ref, wf_ref, ob_ref, of_ref, h_ref, *, nb):
    j = pl.program_id(1)

    @pl.when(j == 0)
    def _():
        x = x_ref[...]
        ms = jnp.mean(x * x, axis=-1, keepdims=True)
        h = (x * lax.rsqrt(ms + EPS) * nw_ref[...]).astype(h_ref.dtype)
        h_ref[...] = h
        ob_ref[...] = _dot(h, wb_ref[...]).astype(ob_ref.dtype)

    @pl.when((j > 0) & (j < nb))
    def _():
        ob_ref[...] = _dot(h_ref[...], wb_ref[...]).astype(ob_ref.dtype)

    @pl.when(j == nb)
    def _():
        of_ref[...] = _dot(h_ref[...], wf_ref[...])


def _in_proj(x2d, nw, w_pb, w_pf, tm, tn):
    t, d = x2d.shape
    nb = PB_WIDTH // tn
    return pl.pallas_call(
        functools.partial(_in_proj_kernel, nb=nb),
        grid=(t // tm, nb + 1),
        in_specs=[
            pl.BlockSpec((tm, d), lambda i, j: (i, 0)),
            pl.BlockSpec((1, d), lambda i, j: (0, 0)),
            pl.BlockSpec((d, tn), lambda i, j: (0, jnp.minimum(j, nb - 1))),
            pl.BlockSpec((d, PF_WIDTH), lambda i, j: (0, 0), pipeline_mode=pl.Buffered(1)),
        ],
        out_specs=[pl.BlockSpec((tm, tn), lambda i, j: (i, jnp.minimum(j, nb - 1))),
                   pl.BlockSpec((tm, PF_WIDTH), lambda i, j: (i, 0))],
        out_shape=[jax.ShapeDtypeStruct((t, PB_WIDTH), jnp.bfloat16),
                   jax.ShapeDtypeStruct((t, PF_WIDTH), jnp.float32)],
        scratch_shapes=[pltpu.VMEM((tm, d), jnp.bfloat16)],
        compiler_params=_cparams("parallel", "arbitrary", vmem_limit=_BIG_VMEM_LIMIT),
        name="norm_in_proj",
    )(x2d, nw, w_pb, w_pf)


_LOG2E = 1.4426950408889634
_N_BIAS_LANES = 3
_V_ROWS = A_V_DIM + 16
_ATTN_TILES_PER_BODY = 8
_ATTN_TILE = 512


def _diff_attn_kernel(slopes_ref, q_ref, qn_ref, k_ref, v_ref, dl_ref, o_ref,
                      kx_ref, vt_ref, qt_ref, m_ref, acc_ref, s_ref, cmax_ref, bias_ref, *, seq, tq, tk,
                      lam_init):
    head = pl.program_id(1)
    qi = pl.program_id(2)
    f32 = jnp.float32
    bf16 = jnp.bfloat16
    sigma = slopes_ref[head] * _LOG2E
    nk = seq // tk
    prep_rows = tk

    @pl.when(qi == 0)
    def _():
        lane = lax.broadcasted_iota(jnp.int32, (prep_rows, LANES), 1)
        row = lax.broadcasted_iota(jnp.int32, (prep_rows, LANES), 0)
        ones_row = jnp.where(lax.broadcasted_iota(jnp.int32, (_V_ROWS - A_V_DIM, tk), 0) == 0, 1.0, 0.0)

        def prep(c, carry):
            r0 = pl.multiple_of(c * prep_rows, prep_rows)
            kk = k_ref[pl.ds(r0, prep_rows), :].astype(f32)
            col = sigma * (row + r0).astype(f32)
            c1 = col.astype(bf16).astype(f32)
            c2 = (col - c1).astype(bf16).astype(f32)
            c3 = col - c1 - c2
            bias = jnp.where(lane == A_QK_DIM, c1, jnp.where(lane == A_QK_DIM + 1, c2,
                             jnp.where(lane == A_QK_DIM + 2, c3, 0.0)))
            kx_ref[0, pl.ds(r0, prep_rows), :] = jnp.where(lane < A_QK_DIM, kk, bias).astype(bf16)
            kx_ref[1, pl.ds(r0, prep_rows), :] = jnp.where(
                lane < A_QK_DIM, pltpu.roll(kk, A_QK_DIM, 1), bias).astype(bf16)
            vt_ref[c, 0:A_V_DIM, :] = v_ref[pl.ds(r0, prep_rows), :].astype(f32).T.astype(bf16)
            vt_ref[c, A_V_DIM:_V_ROWS, :] = ones_row.astype(bf16)
            return carry

        lax.fori_loop(0, nk, prep, 0)
        rel = (lax.broadcasted_iota(jnp.int32, (tk, tq), 1)
               - lax.broadcasted_iota(jnp.int32, (tk, tq), 0)).astype(f32)
        bias_ref[...] = sigma * jnp.abs(rel)

    qrow = lax.broadcasted_iota(jnp.int32, (LANES, tq), 0)
    bias_rows = lax.broadcasted_iota(jnp.int32, (16, tq), 0) < _N_BIAS_LANES

    def build_q_variants(src_ref, base_slot):
        qt = (src_ref[...].astype(f32) * (A_QK_DIM ** -0.5 * _LOG2E)).T
        for mp in range(2):
            qm = qt if mp == 0 else pltpu.roll(qt, A_QK_DIM, 0)
            base = jnp.where(qrow < A_QK_DIM, qm, 0.0).astype(bf16)
            for var, sign in enumerate((1.0, 0.0, -1.0)):
                qt_ref[base_slot + 3 * mp + var] = base
                qt_ref[base_slot + 3 * mp + var, A_QK_DIM:A_QK_DIM + 16, :] = (
                    jnp.where(bias_rows, sign, 0.0).astype(bf16))

    def diagonal_scores(tile, base_slot, mp):
        k_start = pl.multiple_of(tile * tk, tk)
        s = _dot(kx_ref[mp, pl.ds(k_start, tk), :], qt_ref[base_slot + 3 * mp + 1]) - bias_ref[...]
        s_ref[0, mp] = s
        cmax_ref[0, mp] = jnp.max(s, axis=0, keepdims=True)

    qslot = (qi % 2) * 6
    qslot_next = 6 - qslot

    @pl.when(qi == 0)
    def _():
        build_q_variants(q_ref, 0)
        for mp in range(2):
            diagonal_scores(0, 0, mp)

    q0 = qi * tq
    rho = sigma * (lax.broadcasted_iota(jnp.int32, (1, tq), 1) + q0).astype(f32)
    m_ref[...] = jnp.full(m_ref.shape, _NEG_INF, f32)
    acc_ref[...] = jnp.zeros(acc_ref.shape, f32)

    kd = q0 // tk

    def key_tile(j):
        jm = jnp.minimum(j, nk - 1) - 1
        return jnp.where(j == 0, kd, jnp.where(jm < kd, jm, jm + 1))

    def scores(j, slot, mp):
        kb = key_tile(j)
        var = jnp.where(kb < kd, 0, jnp.where(kb > kd, 2, 1))
        k_start = pl.multiple_of(kb * tk, tk)
        s = _dot(kx_ref[mp, pl.ds(k_start, tk), :], qt_ref[qslot + 3 * mp + var])
        s_ref[slot, mp] = s
        cmax_ref[slot, mp] = jnp.max(s, axis=0, keepdims=True)

    def softmax_pv(j, slot, mp):
        kb = key_tile(j)
        shift = jnp.where(kb < kd, -1.0, jnp.where(kb > kd, 1.0, 0.0)) * rho
        m_old = m_ref[mp]
        m_new = jnp.maximum(m_old, cmax_ref[slot, mp] + shift)
        p = jnp.exp2(s_ref[slot, mp] - (m_new - shift))
        acc_ref[mp] = jnp.exp2(m_old - m_new) * acc_ref[mp] + _dot(vt_ref[kb], p.astype(bf16))
        m_ref[mp] = m_new

    def body(i, carry):
        for u in range(_ATTN_TILES_PER_BODY):
            j = _ATTN_TILES_PER_BODY * i + u
            for mp in range(2):
                scores(j + 1, (u + 1) % _ATTN_TILES_PER_BODY, mp)
                softmax_pv(j, u, mp)
        return carry

    n_bodies = nk // _ATTN_TILES_PER_BODY
    lax.fori_loop(0, n_bodies - 1, body, 0)

    build_q_variants(qn_ref, qslot_next)
    kd_next = jnp.minimum(qi + 1, seq // tq - 1)
    for u in range(_ATTN_TILES_PER_BODY):
        j = _ATTN_TILES_PER_BODY * (n_bodies - 1) + u
        for mp in range(2):
            if u + 1 < _ATTN_TILES_PER_BODY:
                scores(j + 1, u + 1, mp)
            else:
                diagonal_scores(kd_next, qslot_next, mp)
            softmax_pv(j, u, mp)

    dl = dl_ref[...]
    lam = (jnp.exp(jnp.sum(dl[0:1] * dl[1:2], axis=1, keepdims=True))
           - jnp.exp(jnp.sum(dl[2:3] * dl[3:4], axis=1, keepdims=True)) + lam_init)
    a0 = acc_ref[0]
    a1 = acc_ref[1]
    ot = (a0[0:A_V_DIM] / a0[A_V_DIM:A_V_DIM + 1] - lam * (a1[0:A_V_DIM] / a1[A_V_DIM:A_V_DIM + 1]))
    ot = ot * lax.rsqrt(jnp.mean(ot * ot, axis=0, keepdims=True) + EPS)
    o_ref[...] = (ot * (1.0 - lam_init)).T.astype(o_ref.dtype)


def _diff_attention(pb, diff_lambda, slopes_a, layer_idx, tq, tk):
    bsz, seq, _ = pb.shape
    assert seq % (_ATTN_TILES_PER_BODY * tk) == 0 and tk == tq
    lam_init = 0.8 - 0.6 * math.exp(-0.3 * layer_idx)
    kern = functools.partial(_diff_attn_kernel, seq=seq, tq=tq, tk=tk, lam_init=lam_init)
    return pl.pallas_call(
        kern,
        grid=(bsz, A_HEADS, seq // tq),
        in_specs=[
            pl.BlockSpec(memory_space=pltpu.SMEM),
            pl.BlockSpec((None, tq, LANES), lambda b, h, i: (b, i, _PB_AQ + h)),
            pl.BlockSpec((None, tq, LANES), lambda b, h, i: (b, jnp.minimum(i + 1, seq // tq - 1), _PB_AQ + h)),
            pl.BlockSpec((None, seq, LANES), lambda b, h, i: (b, 0, _PB_AK + h)),
            pl.BlockSpec((None, seq, LANES), lambda b, h, i: (b, 0, _PB_AV + h)),
            pl.BlockSpec((4, A_QK_DIM), lambda b, h, i: (0, 0)),
        ],
        out_specs=pl.BlockSpec((None, tq, LANES), lambda b, h, i: (b, i, h)),
        out_shape=jax.ShapeDtypeStruct((bsz, seq, A_HEADS * A_V_DIM), jnp.bfloat16),
        scratch_shapes=[
            pltpu.VMEM((2, seq, LANES), jnp.bfloat16),
            pltpu.VMEM((seq // tk, _V_ROWS, tk), jnp.bfloat16),
            pltpu.VMEM((12, LANES, tq), jnp.bfloat16),
            pltpu.VMEM((2, 1, tq), jnp.float32),
            pltpu.VMEM((2, _V_ROWS, tq), jnp.float32),
            pltpu.VMEM((_ATTN_TILES_PER_BODY, 2, tk, tq), jnp.float32),
            pltpu.VMEM((_ATTN_TILES_PER_BODY, 2, 1, tq), jnp.float32),
            pltpu.VMEM((tk, tq), jnp.float32),
        ],
        compiler_params=_cparams("parallel", "parallel", "arbitrary"),
        name="diff_attention",
    )(slopes_a, pb, pb, pb, pb, diff_lambda)


def _retention_kernel(q_ref, k_ref, v_ref, g_ref, dec_ref, o_ref, of_ref, ob_ref, *, seq):
    head = pl.program_id(1)
    nc = seq // BLOCK
    f32 = jnp.float32

    lgv = _log_sigmoid(dec_ref[...])
    rr = lax.broadcasted_iota(jnp.int32, lgv.shape, 0)
    cc = lax.broadcasted_iota(jnp.int32, lgv.shape, 1)

    def pick(row):
        sel = jnp.where((rr == row) & (cc == head), lgv, 0.0)
        return jnp.sum(jnp.sum(sel, axis=1, keepdims=True), axis=0, keepdims=True)

    lg_f, lg_b = pick(0), pick(1)
    ri = lax.broadcasted_iota(jnp.int32, (BLOCK, BLOCK), 0).astype(f32)
    ci = lax.broadcasted_iota(jnp.int32, (BLOCK, BLOCK), 1).astype(f32)
    diff = ri - ci
    decay_f = jnp.where(diff >= 0, jnp.exp(lg_f * jnp.maximum(diff, 0.0)), 0.0)
    decay_b = jnp.where(diff <= 0, jnp.exp(lg_b * jnp.maximum(-diff, 0.0)), 0.0)
    idx = lax.broadcasted_iota(jnp.int32, (BLOCK, 1), 0).astype(f32)
    qdec_f = jnp.exp(lg_f * (idx + 1.0))
    kdec_f = jnp.exp(lg_f * (BLOCK - 1.0 - idx))
    qdec_b = jnp.exp(lg_b * (BLOCK - idx))
    kdec_b = jnp.exp(lg_b * idx)
    cdec_f = jnp.exp(lg_f * BLOCK)
    cdec_b = jnp.exp(lg_b * BLOCK)
    kscale = B_DIM ** -0.5

    def scores_and_kv(t0, backward):
        qb = q_ref[pl.ds(t0, BLOCK), :]
        kb = k_ref[pl.ds(t0, BLOCK), :].astype(f32) * kscale
        vb = v_ref[pl.ds(t0, BLOCK), :]
        kdec = kdec_b if backward else kdec_f
        return dict(t0=t0, backward=backward, qb=qb, vb=vb,
                    qk=_dot_nt(qb, kb.astype(qb.dtype)), kv=_dot_tn((kb * kdec).astype(vb.dtype), vb))

    def inner_product(g):
        s = g["qk"] * (decay_b if g["backward"] else decay_f)
        g["inner"] = _dot(s.astype(g["vb"].dtype), g["vb"])

    def with_state(g, state, park_ref, other_ref):
        qdec, cdec = (qdec_b, cdec_b) if g["backward"] else (qdec_f, cdec_f)
        cross = _dot(g["qb"], state.astype(g["qb"].dtype)) * qdec
        rows = pl.ds(g["t0"], BLOCK)
        if other_ref is None:
            park_ref[rows, :] = g["inner"] + cross
        else:
            r = g["inner"] + cross + other_ref[rows, :]
            r = r * lax.rsqrt(jnp.mean(r * r, axis=-1, keepdims=True) + EPS)
            gate = g_ref[rows, :].astype(f32)
            o_ref[rows, :] = (gate * jax.nn.sigmoid(gate) * r).astype(o_ref.dtype)
        return state * cdec + g["kv"]

    def make_body(second_half):
        def body(it, carry):
            rf, rb = carry
            steps = []
            for u in range(_SCAN_UNROLL):
                c = it * _SCAN_UNROLL + u
                steps.append(scores_and_kv(pl.multiple_of(c * BLOCK, BLOCK), False))
                steps.append(scores_and_kv(pl.multiple_of((nc - 1 - c) * BLOCK, BLOCK), True))
            for g in steps:
                inner_product(g)
            for g in steps:
                if g["backward"]:
                    rb = with_state(g, rb, ob_ref, of_ref if second_half else None)
                else:
                    rf = with_state(g, rf, of_ref, ob_ref if second_half else None)
            return rf, rb
        return body

    zero = jnp.zeros((B_DIM, B_DIM), f32)
    n_bodies = nc // _SCAN_UNROLL
    state = lax.fori_loop(0, n_bodies // 2, make_body(False), (zero, zero))
    lax.fori_loop(n_bodies // 2, n_bodies, make_body(True), state)


def _retention(pb, dec_pad):
    bsz, seq, _ = pb.shape
    assert seq % (2 * _SCAN_UNROLL * BLOCK) == 0
    kern = functools.partial(_retention_kernel, seq=seq)

    def col(base):
        return pl.BlockSpec((None, seq, LANES), lambda b, h: (b, 0, base + h))

    return pl.pallas_call(
        kern,
        grid=(bsz, B_HEADS),
        in_specs=[col(_PB_BQ), col(_PB_BK), col(_PB_BV), col(_PB_BG),
                  pl.BlockSpec((8, LANES), lambda b, h: (0, 0))],
        out_specs=pl.BlockSpec((None, seq, LANES), lambda b, h: (b, 0, h)),
        out_shape=jax.ShapeDtypeStruct((bsz, seq, B_HEADS * B_DIM), jnp.bfloat16),
        scratch_shapes=[pltpu.VMEM((seq, B_DIM), jnp.float32), pltpu.VMEM((seq, B_DIM), jnp.float32)],
        compiler_params=_cparams("parallel", "parallel"),
        name="retention",
    )(pb, pb, pb, pb, dec_pad)


_ROWS_PER_DIR = 6


def _mlstm_kernel(cq_ref, ck_ref, gt_ref, v_ref, og_ref, wq_ref, wk_ref, gb_ref, o_ref,
                  qc_ref, kt_ref, vx_ref, rows_ref, of_ref, ob_ref, *, seq):
    nc = seq // BLOCK
    f32 = jnp.float32
    bf16 = jnp.bfloat16
    rowi = lax.broadcasted_iota(jnp.int32, (BLOCK, LANES), 0)
    coli = lax.broadcasted_iota(jnp.int32, (BLOCK, LANES), 1)
    lane_n = lax.broadcasted_iota(jnp.int32, (nc, LANES), 1)
    tril = rowi >= coli
    triu = rowi <= coli
    tril_b = tril.astype(bf16)
    triu_b = triu.astype(bf16)
    ones_b = jnp.ones((BLOCK, LANES), bf16)
    head = pl.program_id(1)

    def cummax_lanes(x, backward):
        sh = 1
        while sh < LANES:
            if backward:
                x = jnp.maximum(x, jnp.where(lane_n < LANES - sh, pltpu.roll(x, LANES - sh, 1), _NEG_INF))
            else:
                x = jnp.maximum(x, jnp.where(lane_n >= sh, pltpu.roll(x, sh, 1), _NEG_INF))
            sh *= 2
        return x

    for d, (op, last) in enumerate(((triu_b, LANES - 1), (tril_b, 0))):
        lf = _log_sigmoid(gt_ref[2 * d + 1] + gb_ref[2 * d + 1, head]) * _LOG2E
        hi, lo = _split_hi_lo(lf)
        b = _dot(hi, op) + _dot(lo, op)
        a = (gt_ref[2 * d] + gb_ref[2 * d, head]) * _LOG2E - b
        bl = jnp.broadcast_to(b[:, last:last + 1], (nc, LANES))
        ml = jnp.broadcast_to(jnp.max(bl + a, axis=1, keepdims=True), (nc, LANES))
        for r, val in enumerate((a, cummax_lanes(a, d == 1), b, jnp.exp2(bl + a - ml), bl, ml)):
            rows_ref[d * _ROWS_PER_DIR + r] = val

    def conv_silu(src_ref, w, c, t0):
        x = src_ref[pl.ds(t0, BLOCK), :]
        prev = src_ref[pl.ds(jnp.maximum(t0 - 1, 0), 1), :] * jnp.where(c > 0, 1.0, 0.0)
        nxt = src_ref[pl.ds(jnp.minimum(t0 + BLOCK, seq - 1), 1), :] * jnp.where(c < nc - 1, 1.0, 0.0)
        xd = jnp.where(rowi == 0, prev, pltpu.roll(x, 1, 0))
        xu = jnp.where(rowi == BLOCK - 1, nxt, pltpu.roll(x, BLOCK - 1, 0))
        y = xd * w[0:1] + x * w[1:2] + xu * w[2:3]
        return y * jax.nn.sigmoid(y)

    def prep(c, carry):
        t0 = pl.multiple_of(c * BLOCK, BLOCK)
        qc_ref[pl.ds(t0, BLOCK), :] = (conv_silu(cq_ref, wq_ref[...], c, t0) * (C_DIM ** -0.5)).astype(bf16)
        kt_ref[c] = conv_silu(ck_ref, wk_ref[...], c, t0).T.astype(bf16)
        vx_ref[pl.ds(t0, BLOCK), 0:C_DIM] = v_ref[pl.ds(t0, BLOCK), :]
        vx_ref[pl.ds(t0, BLOCK), C_DIM:2 * C_DIM] = ones_b
        return carry

    lax.fori_loop(0, nc, prep, 0, unroll=2)

    def load_rows(ch, backward):
        t0 = pl.multiple_of(ch * BLOCK, BLOCK)
        base = _ROWS_PER_DIR if backward else 0
        a, c, b, w, bl, ml = (rows_ref[base + r, pl.ds(ch, 1), :] for r in range(_ROWS_PER_DIR))
        kt = kt_ref[ch]
        return dict(t0=t0, backward=backward, bl=bl[:, 0:1], m_loc=ml[:, 0:1],
                    a_t=jnp.where(triu if backward else tril, a, _NEG_INF),
                    b_full=jnp.broadcast_to(b, (BLOCK, LANES)).T,
                    c_full=jnp.broadcast_to(c, (BLOCK, LANES)).T,
                    kw=(kt.astype(f32) * w).astype(bf16), kt=kt)

    def first_products(g):
        vx = vx_ref[pl.ds(g["t0"], BLOCK), :]
        g["qk"] = _dot(qc_ref[pl.ds(g["t0"], BLOCK), :], g["kt"])
        g["kv"] = _dot(g["kw"], vx)

    def decayed_scores(g):
        g["s0"] = (g["qk"] * jnp.exp2(g["a_t"] - g["c_full"])).astype(bf16)

    def inner_products(g):
        g["p0"] = _dot(g["s0"], vx_ref[pl.ds(g["t0"], BLOCK), :])

    def with_state(g, state, park_ref, other_ref):
        cx, m = state
        x = _dot(qc_ref[pl.ds(g["t0"], BLOCK), :], cx.astype(bf16))
        n = jnp.maximum(g["c_full"], m)
        e = jnp.exp2(g["c_full"] - n)
        iw = jnp.exp2(m - n)
        num = e * g["p0"][:, 0:C_DIM] + iw * x[:, 0:C_DIM]
        den = e * g["p0"][:, C_DIM:] + iw * x[:, C_DIM:]
        hh = num / jnp.maximum(jnp.abs(den), jnp.exp2(-(g["b_full"] + n)))
        rows = pl.ds(g["t0"], BLOCK)
        if other_ref is None:
            park_ref[rows, :] = hh
        else:
            r = hh + other_ref[rows, :]
            r = r * lax.rsqrt(jnp.mean(r * r, axis=-1, keepdims=True) + EPS)
            o_ref[rows, :] = (jax.nn.sigmoid(og_ref[rows, :].astype(f32)) * r).astype(o_ref.dtype)
        m_new = jnp.maximum(g["bl"] + m, g["m_loc"])
        cw = jnp.exp2(g["bl"] + m - m_new)
        f = jnp.exp2(g["m_loc"] - m_new)
        return cw * cx + f * g["kv"], m_new

    def make_body(second_half):
        def body(it, carry):
            sf, sb = carry
            steps = []
            for u in range(_SCAN_UNROLL):
                c = it * _SCAN_UNROLL + u
                steps.append(load_rows(c, False))
                steps.append(load_rows(nc - 1 - c, True))
            for g in steps:
                first_products(g)
            for g in steps:
                decayed_scores(g)
            for g in steps:
                inner_products(g)
            for g in steps:
                if g["backward"]:
                    sb = with_state(g, sb, ob_ref, of_ref if second_half else None)
                else:
                    sf = with_state(g, sf, of_ref, ob_ref if second_half else None)
            return sf, sb
        return body

    init = (jnp.zeros((C_DIM, 2 * C_DIM), f32), jnp.zeros((1, 1), f32))
    n_bodies = nc // _SCAN_UNROLL
    state = lax.fori_loop(0, n_bodies // 2, make_body(False), (init, init))
    lax.fori_loop(n_bodies // 2, n_bodies, make_body(True), state)


def _mlstm(pb, pf, gates_t, wq_conv, wk_conv, gate_bias):
    bsz, seq, _ = pb.shape
    assert seq % (8 * BLOCK) == 0
    kern = functools.partial(_mlstm_kernel, seq=seq)

    def col(base):
        return pl.BlockSpec((None, seq, LANES), lambda b, h: (b, 0, base + h))

    return pl.pallas_call(
        kern,
        grid=(bsz, C_HEADS),
        in_specs=[col(_PF_CQ), col(_PF_CK),
                  pl.BlockSpec((None, None, 4, seq // BLOCK, LANES), lambda b, h: (b, h, 0, 0, 0)),
                  col(_PB_CV), col(_PB_CO),
                  pl.BlockSpec((3, LANES), lambda b, h: (0, h)),
                  pl.BlockSpec((3, LANES), lambda b, h: (0, h)),
                  pl.BlockSpec(memory_space=pltpu.SMEM)],
        out_specs=pl.BlockSpec((None, seq, LANES), lambda b, h: (b, 0, h)),
        out_shape=jax.ShapeDtypeStruct((bsz, seq, C_HEADS * C_DIM), jnp.bfloat16),
        scratch_shapes=[pltpu.VMEM((seq, C_DIM), jnp.bfloat16),
                        pltpu.VMEM((seq // BLOCK, C_DIM, BLOCK), jnp.bfloat16),
                        pltpu.VMEM((seq, 2 * C_DIM), jnp.bfloat16),
                        pltpu.VMEM((2 * _ROWS_PER_DIR, seq // BLOCK, LANES), jnp.float32),
                        pltpu.VMEM((seq, C_DIM), jnp.float32), pltpu.VMEM((seq, C_DIM), jnp.float32)],
        compiler_params=_cparams("parallel", "parallel"),
        name="mlstm",
    )(pf, pf, gates_t, pb, pb, wq_conv, wk_conv, gate_bias)


def _window_attn_kernel(slopes_ref, sink_ref, q_ref, k_ref, v_ref, o_ref, kx_ref, vx_ref, *, seq):
    qi = pl.program_id(1)
    f32 = jnp.float32
    span = BLOCK + 2 * WINDOW
    group = D_HEADS // D_KV_HEADS

    @pl.when(qi == 0)
    def _():
        lane = lax.broadcasted_iota(jnp.int32, (seq, LANES), 1)
        ones = jnp.ones((seq, LANES), vx_ref.dtype)
        for src, dst in ((k_ref, kx_ref), (v_ref, vx_ref)):
            x = src[...].astype(f32)
            x0 = jnp.where(lane < D_DIM, x, 0.0)
            x1 = jnp.where(lane >= D_DIM, x, 0.0)
            for i, xi in enumerate((x0, pltpu.roll(x0, D_DIM, 1), pltpu.roll(x1, D_DIM, 1), x1)):
                dst[i, :, 0:LANES] = xi.astype(dst.dtype)
        for i in range(4):
            vx_ref[i, :, LANES:2 * LANES] = ones

    t0 = qi * BLOCK
    ws = pl.multiple_of(jnp.clip(t0 - WINDOW, 0, seq - span), BLOCK)
    rel = (lax.broadcasted_iota(jnp.int32, (BLOCK, span), 0)
           - lax.broadcasted_iota(jnp.int32, (BLOCK, span), 1)) + (t0 - ws)
    dist = jnp.abs(rel)
    neg_dist = jnp.where(dist <= WINDOW, dist.astype(f32) * (-_LOG2E), _NEG_INF)

    n_pairs = D_HEADS // 2
    lookahead = 2
    scores = {}

    def score_pair(pair):
        kv = (2 * pair) // group
        qp = (q_ref[:, pair * LANES:(pair + 1) * LANES].astype(f32) * (D_DIM ** -0.5 * _LOG2E)).astype(q_ref.dtype)
        for half in range(2):
            scores[2 * pair + half] = _dot_nt(qp, kx_ref[2 * kv + half, pl.ds(ws, span), :])

    def attend(hd, kv_slot):
        s = scores[hd] + slopes_ref[hd] * neg_dist
        snk = sink_ref[hd] * _LOG2E
        m = jnp.maximum(jnp.max(s, axis=1, keepdims=True), snk)
        o = _dot(jnp.exp2(s - m).astype(jnp.bfloat16), vx_ref[kv_slot, pl.ds(ws, span), :])
        return o[:, 0:LANES] / (o[:, LANES:] + jnp.exp2(snk - m))

    for pair in range(lookahead):
        score_pair(pair)
    for pair in range(n_pairs):
        kv = (2 * pair) // group
        out = attend(2 * pair, 2 * kv) + attend(2 * pair + 1, 2 * kv + 1)
        o_ref[:, pair * LANES:(pair + 1) * LANES] = out.astype(o_ref.dtype)
        if pair + lookahead < n_pairs:
            score_pair(pair + lookahead)


def _window_attention(pb, attn_sink, slopes_d):
    bsz, seq, _ = pb.shape
    kern = functools.partial(_window_attn_kernel, seq=seq)
    width = D_HEADS * D_DIM
    return pl.pallas_call(
        kern,
        grid=(bsz, seq // BLOCK),
        in_specs=[
            pl.BlockSpec(memory_space=pltpu.SMEM),
            pl.BlockSpec(memory_space=pltpu.SMEM),
            pl.BlockSpec((None, BLOCK, width), lambda b, i: (b, i, _PB_DQ512)),
            pl.BlockSpec((None, seq, LANES), lambda b, i: (b, 0, _PB_DK)),
            pl.BlockSpec((None, seq, LANES), lambda b, i: (b, 0, _PB_DV)),
        ],
        out_specs=pl.BlockSpec((None, BLOCK, width), lambda b, i: (b, i, 0)),
        out_shape=jax.ShapeDtypeStruct((bsz, seq, width), jnp.bfloat16),
        scratch_shapes=[pltpu.VMEM((4, seq, LANES), jnp.bfloat16), pltpu.VMEM((4, seq, 2 * LANES), jnp.bfloat16)],
        compiler_params=_cparams("parallel", "arbitrary"),
        name="window_attention",
    )(slopes_d, attn_sink, pb, pb, pb)


def _out_proj_kernel(x_ref, ya_ref, yb_ref, yc_ref, yd_ref, w_ref, o_ref):
    acc = x_ref[...]
    for g, y_ref in enumerate((ya_ref, yb_ref, yc_ref, yd_ref)):
        acc = acc + _dot(y_ref[...], w_ref[g * 512:(g + 1) * 512, :])
    o_ref[...] = acc


def _out_proj(x2d, ys, w_out, tm):
    t, d = x2d.shape
    yspec = pl.BlockSpec((tm, 512), lambda i: (i, 0))
    return pl.pallas_call(
        _out_proj_kernel,
        grid=(t // tm,),
        in_specs=[pl.BlockSpec((tm, d), lambda i: (i, 0)), yspec, yspec, yspec, yspec,
                  pl.BlockSpec((d, d), lambda i: (0, 0), pipeline_mode=pl.Buffered(1))],
        out_specs=pl.BlockSpec((tm, d), lambda i: (i, 0)),
        out_shape=jax.ShapeDtypeStruct((t, d), jnp.float32),
        compiler_params=_cparams("parallel", vmem_limit=_BIG_VMEM_LIMIT),
        name="out_proj",
    )(x2d, *ys, w_out)


def _ffn_kernel(x_ref, xp_ref, xn_ref, nw_ref, wg_ref, wu_ref, cw_ref, wd_ref, fw_ref, o_ref,
                h_ref, *, seq, tm, final_norm):
    i = pl.program_id(0)
    j = pl.program_id(1)
    f32 = jnp.float32

    def norm(x):
        return x * lax.rsqrt(jnp.mean(x * x, axis=-1, keepdims=True) + EPS) * nw_ref[...]

    def d_ff_step(first):
        gate = _dot(h_ref[...], wg_ref[...])
        gm = gate[0:tm]
        rowi = lax.broadcasted_iota(jnp.int32, gm.shape, 0)
        g_dn = jnp.where(rowi == 0, gate[tm + 7:tm + 8], pltpu.roll(gm, 1, 0))
        g_up = jnp.where(rowi == tm - 1, gate[tm + 8:tm + 9], pltpu.roll(gm, tm - 1, 0))
        cw = cw_ref[...]
        g = g_dn * cw[0:1] + gm * cw[1:2] + g_up * cw[2:3]
        up = _dot(h_ref[0:tm, :], wu_ref[...])
        act = (g * jax.nn.sigmoid(g) * up).astype(wd_ref.dtype)
        down = _dot(act, wd_ref[...])
        if first:
            o_ref[...] = down
        else:
            o_ref[...] += down

    @pl.when(j == 0)
    def _():
        h_ref[0:tm, :] = norm(x_ref[...]).astype(h_ref.dtype)
        has_prev = ((i * tm) % seq != 0).astype(f32)
        has_next = (((i + 1) * tm) % seq != 0).astype(f32)
        halo = jnp.concatenate([norm(xp_ref[...]) * has_prev, norm(xn_ref[...]) * has_next], axis=0)
        h_ref[tm:tm + 16, :] = halo.astype(h_ref.dtype)
        d_ff_step(True)

    @pl.when(j > 0)
    def _():
        d_ff_step(False)

    @pl.when(j == pl.num_programs(1) - 1)
    def _():
        y = x_ref[...] + o_ref[...]
        if final_norm:
            y = y * lax.rsqrt(jnp.mean(y * y, axis=-1, keepdims=True) + EPS) * fw_ref[...]
        o_ref[...] = y


def _ffn(x2d, seq, nw, wg, wu, cw, wd, fw, final_norm, tm, tn):
    t, d = x2d.shape
    nff = wg.shape[1]
    rb = tm // 8
    last8 = t // 8 - 1
    kern = functools.partial(_ffn_kernel, seq=seq, tm=tm, final_norm=final_norm)
    return pl.pallas_call(
        kern,
        grid=(t // tm, nff // tn),
        in_specs=[
            pl.BlockSpec((tm, d), lambda i, j: (i, 0)),
            pl.BlockSpec((8, d), lambda i, j: (jnp.maximum(i * rb - 1, 0), 0)),
            pl.BlockSpec((8, d), lambda i, j: (jnp.minimum((i + 1) * rb, last8), 0)),
            pl.BlockSpec((1, d), lambda i, j: (0, 0)),
            pl.BlockSpec((d, tn), lambda i, j: (0, j)),
            pl.BlockSpec((d, tn), lambda i, j: (0, j)),
            pl.BlockSpec((3, tn), lambda i, j: (0, j)),
            pl.BlockSpec((tn, d), lambda i, j: (j, 0)),
            pl.BlockSpec((1, d), lambda i, j: (0, 0)),
        ],
        out_specs=pl.BlockSpec((tm, d), lambda i, j: (i, 0)),
        out_shape=jax.ShapeDtypeStruct((t, d), jnp.float32),
        scratch_shapes=[pltpu.VMEM((tm + 16, d), jnp.bfloat16)],
        compiler_params=_cparams("parallel", "arbitrary", vmem_limit=_BIG_VMEM_LIMIT),
        name="ffn",
    )(x2d, x2d, x2d, nw, wg, wu, cw, wd, fw)


def _prepare_layer(l, norm1_w, w_in, qk_conv_w, diff_lambda, ret_decay_logit, mlstm_gate_bias, attn_sink,
                   w_out, norm2_w, w_gate, w_up, ffn_conv_w, w_down):
    bf16 = jnp.bfloat16
    f32 = jnp.float32
    wi = w_in[l]
    w_pb = jnp.concatenate([wi[:, _OFF_A:_OFF_CQ], wi[:, _OFF_CV:_OFF_GATES], wi[:, _OFF_D:N_IN]], axis=1)
    wgates = jnp.pad(wi[:, _OFF_GATES:_OFF_D], ((0, 0), (0, LANES - N_GATES)))
    w_pf = jnp.concatenate([wi[:, _OFF_CQ:_OFF_CV], wgates], axis=1)
    gbias = mlstm_gate_bias[l].astype(f32)
    dec = jnp.zeros((8, LANES), f32).at[0:2, 0:B_HEADS].set(ret_decay_logit[l].astype(f32))
    return dict(
        norm1=norm1_w[l].reshape(1, D_MODEL).astype(f32),
        w_pb=w_pb.astype(bf16), w_pf=w_pf.astype(bf16),
        wq_conv=qk_conv_w[l][:, :C_HEADS * C_DIM].astype(f32),
        wk_conv=qk_conv_w[l][:, C_HEADS * C_DIM:].astype(f32),
        gbias=gbias, dec=dec,
        diff_lambda=diff_lambda[l].astype(f32), sink=attn_sink[l].astype(f32),
        w_out=w_out[l].astype(bf16),
        norm2=norm2_w[l].reshape(1, D_MODEL).astype(f32),
        w_gate=w_gate[l].astype(bf16), w_up=w_up[l].astype(bf16),
        ffn_conv=ffn_conv_w[l].astype(f32), w_down=w_down[l].astype(bf16),
    )


def _tile(n, pref):
    t = min(pref, n)
    while n % t:
        t //= 2
    return t


def _layer(x2d, bsz, seq, layer_idx, p, slopes, final_w, final_norm):
    t = bsz * seq
    tm_in = _tile(t, 1024)
    pb, pf = _in_proj(x2d, p["norm1"], p["w_pb"], p["w_pf"], tm_in, _IN_PROJ_TN)
    pb = pb.reshape(bsz, seq, PB_WIDTH)
    pf = pf.reshape(bsz, seq, PF_WIDTH)
    gates_t = pf[:, :, _PF_GATES * LANES:_PF_GATES * LANES + N_GATES].reshape(bsz, seq, 4, C_HEADS)
    gates_t = jnp.transpose(gates_t, (0, 3, 2, 1)).reshape(bsz, C_HEADS, 4, seq // BLOCK, BLOCK)
    y_a = _diff_attention(pb, p["diff_lambda"], slopes[D_HEADS:], layer_idx,
                          _tile(seq // _ATTN_TILES_PER_BODY, _ATTN_TILE),
                          _tile(seq // _ATTN_TILES_PER_BODY, _ATTN_TILE))
    y_b = _retention(pb, p["dec"])
    y_c = _mlstm(pb, pf, gates_t, p["wq_conv"], p["wk_conv"], p["gbias"])
    y_d = _window_attention(pb, p["sink"], slopes[:D_HEADS])
    ys = [y.reshape(t, 512) for y in (y_a, y_b, y_c, y_d)]
    x1 = _out_proj(x2d, ys, p["w_out"], _tile(t, 1024))
    return _ffn(x1, seq, p["norm2"], p["w_gate"], p["w_up"], p["ffn_conv"], p["w_down"], final_w,
                final_norm, _tile(seq, 1024), 512)


def _trunk(x, layers, slopes, final_w):
    bsz, seq, d = x.shape
    x2d = x.reshape(bsz * seq, d)
    for l, p in enumerate(layers):
        x2d = _layer(x2d, bsz, seq, l, p, slopes, final_w, l == len(layers) - 1)
    return x2d.reshape(bsz, seq, d)


def kernel(x_prompt, x_sample, norm1_w, w_in, qk_conv_w, diff_lambda, ret_decay_logit, mlstm_gate_bias,
           attn_sink, w_out, norm2_w, w_gate, w_up, ffn_conv_w, w_down, final_norm_w):
    layers = [_prepare_layer(l, norm1_w, w_in, qk_conv_w, diff_lambda, ret_decay_logit, mlstm_gate_bias,
                             attn_sink, w_out, norm2_w, w_gate, w_up, ffn_conv_w, w_down)
              for l in range(DEPTH)]
    n = A_HEADS + D_HEADS
    slopes = 2.0 ** (-8.0 * jnp.arange(1, n + 1, dtype=jnp.float32) / n)
    final_w = final_norm_w.reshape(1, D_MODEL).astype(jnp.float32)
    return (_trunk(x_prompt, layers, slopes, final_w), _trunk(x_sample, layers, slopes, final_w))
```

```python
import functools
import math

import jax
import jax.numpy as jnp
from jax import lax
from jax.experimental import pallas as pl
from jax.experimental.pallas import tpu as pltpu

D_MODEL = 2048
DEPTH = 2
A_HEADS = 4
A_QK_DIM = 64
A_V_DIM = 128
B_HEADS = 4
B_DIM = 128
C_HEADS = 4
C_DIM = 128
D_HEADS = 8
D_KV_HEADS = 2
D_DIM = 64
WINDOW = 128
BLOCK = 128
D_FF = 5632
EPS = 1e-6
LANES = 128

_OFF_A = 0
_OFF_B = 1536
_OFF_CQ = 3584
_OFF_CV = 4608
_OFF_GATES = 5632
_OFF_D = 5648
N_IN = 6416

_PB_AQ, _PB_AK, _PB_AV = 0, 4, 8
_PB_BQ, _PB_BK, _PB_BV, _PB_BG = 12, 16, 20, 24
_PB_CV, _PB_CO = 28, 32
_PB_DQ512 = 9
_PB_DK, _PB_DV = 40, 41
PB_WIDTH = 42 * LANES
_IN_PROJ_TN = 1792
_PF_CQ, _PF_CK, _PF_GATES = 0, 4, 8
PF_WIDTH = 9 * LANES
N_GATES = 16

_VMEM_LIMIT = 56 * 1024 * 1024
_BIG_VMEM_LIMIT = 61 * 1024 * 1024
_SCAN_UNROLL = 4
_RETENTION_UNROLL = 8
_NEG_INF = float("-inf")


def _cparams(*sem, vmem_limit=_VMEM_LIMIT):
    return pltpu.CompilerParams(dimension_semantics=sem, vmem_limit_bytes=vmem_limit)


def _dot(a, b):
    return jnp.dot(a, b, preferred_element_type=jnp.float32)


def _dot_nt(a, b):
    return lax.dot_general(a, b, (((1,), (1,)), ((), ())), preferred_element_type=jnp.float32)


def _dot_tn(a, b):
    return lax.dot_general(a, b, (((0,), (0,)), ((), ())), preferred_element_type=jnp.float32)


def _log_sigmoid(x):
    return -(jnp.maximum(-x, 0.0) + jnp.log1p(jnp.exp(-jnp.abs(x))))


def _split_hi_lo(x):
    hi = x.astype(jnp.bfloat16)
    lo = (x - hi.astype(jnp.float32)).astype(jnp.bfloat16)
    return hi, lo


def _in_proj_kernel(x_ref, nw_ref, wb_ref, wf_ref, ob_ref, of_ref, h_ref, *, nb):
    j = pl.program_id(1)

    @pl.when(j == 0)
    def _():
        x = x_ref[...]
        ms = jnp.mean(x * x, axis=-1, keepdims=True)
        h = (x * lax.rsqrt(ms + EPS) * nw_ref[...]).astype(h_ref.dtype)
        h_ref[...] = h
        ob_ref[...] = _dot(h, wb_ref[...]).astype(ob_ref.dtype)

    @pl.when((j > 0) & (j < nb))
    def _():
        ob_ref[...] = _dot(h_ref[...], wb_ref[...]).astype(ob_ref.dtype)

    @pl.when(j == nb)
    def _():
        of_ref[...] = _dot(h_ref[...], wf_ref[...])


def _in_proj(x2d, nw, w_pb, w_pf, tm, tn):
    t, d = x2d.shape
    nb = PB_WIDTH // tn
    return pl.pallas_call(
        functools.partial(_in_proj_kernel, nb=nb),
        grid=(t // tm, nb + 1),
        in_specs=[
            pl.BlockSpec((tm, d), lambda i, j: (i, 0)),
            pl.BlockSpec((1, d), lambda i, j: (0, 0)),
            pl.BlockSpec((d, tn), lambda i, j: (0, jnp.minimum(j, nb - 1))),
            pl.BlockSpec((d, PF_WIDTH), lambda i, j: (0, 0), pipeline_mode=pl.Buffered(1)),
        ],
        out_specs=[pl.BlockSpec((tm, tn), lambda i, j: (i, jnp.minimum(j, nb - 1))),
                   pl.BlockSpec((tm, PF_WIDTH), lambda i, j: (i, 0))],
        out_shape=[jax.ShapeDtypeStruct((t, PB_WIDTH), jnp.bfloat16),
                   jax.ShapeDtypeStruct((t, PF_WIDTH), jnp.float32)],
        scratch_shapes=[pltpu.VMEM((tm, d), jnp.bfloat16)],
        compiler_params=_cparams("parallel", "arbitrary", vmem_limit=_BIG_VMEM_LIMIT),
        name="norm_in_proj",
    )(x2d, nw, w_pb, w_pf)


_LOG2E = 1.4426950408889634
_N_BIAS_LANES = 3
_V_ROWS = A_V_DIM + 16
_ATTN_MAX_TILES_PER_BODY = 8
_ATTN_TILE = 512


def _diff_attn_kernel(slopes_ref, q_ref, qn_ref, k_ref, v_ref, dl_ref, o_ref,
                      kx_ref, vt_ref, qt_ref, m_ref, acc_ref, s_ref, cmax_ref, bias_ref, *, seq, tq, tk,
                      tpb, lam_init):
    head = pl.program_id(1)
    qi = pl.program_id(2)
    f32 = jnp.float32
    bf16 = jnp.bfloat16
    sigma = slopes_ref[head] * _LOG2E
    nk = seq // tk
    prep_rows = tk

    @pl.when(qi == 0)
    def _():
        lane = lax.broadcasted_iota(jnp.int32, (prep_rows, LANES), 1)
        row = lax.broadcasted_iota(jnp.int32, (prep_rows, LANES), 0)
        ones_row = jnp.where(lax.broadcasted_iota(jnp.int32, (_V_ROWS - A_V_DIM, tk), 0) == 0, 1.0, 0.0)

        def prep(c, carry):
            r0 = pl.multiple_of(c * prep_rows, prep_rows)
            kk = k_ref[pl.ds(r0, prep_rows), :].astype(f32)
            col = sigma * (row + r0).astype(f32)
            c1 = col.astype(bf16).astype(f32)
            c2 = (col - c1).astype(bf16).astype(f32)
            c3 = col - c1 - c2
            bias = jnp.where(lane == A_QK_DIM, c1, jnp.where(lane == A_QK_DIM + 1, c2,
                             jnp.where(lane == A_QK_DIM + 2, c3, 0.0)))
            kx_ref[0, pl.ds(r0, prep_rows), :] = jnp.where(lane < A_QK_DIM, kk, bias).astype(bf16)
            kx_ref[1, pl.ds(r0, prep_rows), :] = jnp.where(
                lane < A_QK_DIM, pltpu.roll(kk, A_QK_DIM, 1), bias).astype(bf16)
            vt_ref[c, 0:A_V_DIM, :] = v_ref[pl.ds(r0, prep_rows), :].astype(f32).T.astype(bf16)
            vt_ref[c, A_V_DIM:_V_ROWS, :] = ones_row.astype(bf16)
            return carry

        lax.fori_loop(0, nk, prep, 0)
        rel = (lax.broadcasted_iota(jnp.int32, (tk, tq), 1)
               - lax.broadcasted_iota(jnp.int32, (tk, tq), 0)).astype(f32)
        bias_ref[...] = sigma * jnp.abs(rel)

    qrow = lax.broadcasted_iota(jnp.int32, (LANES, tq), 0)
    bias_rows = lax.broadcasted_iota(jnp.int32, (16, tq), 0) < _N_BIAS_LANES

    def build_q_variants(src_ref, base_slot):
        qt = (src_ref[...].astype(f32) * (A_QK_DIM ** -0.5 * _LOG2E)).T
        for mp in range(2):
            qm = qt if mp == 0 else pltpu.roll(qt, A_QK_DIM, 0)
            base = jnp.where(qrow < A_QK_DIM, qm, 0.0).astype(bf16)
            for var, sign in enumerate((1.0, 0.0, -1.0)):
                qt_ref[base_slot + 3 * mp + var] = base
                qt_ref[base_slot + 3 * mp + var, A_QK_DIM:A_QK_DIM + 16, :] = (
                    jnp.where(bias_rows, sign, 0.0).astype(bf16))

    def diagonal_scores(tile, base_slot, mp):
        k_start = pl.multiple_of(tile * tk, tk)
        s = _dot(kx_ref[mp, pl.ds(k_start, tk), :], qt_ref[base_slot + 3 * mp + 1]) - bias_ref[...]
        s_ref[0, mp] = s
        cmax_ref[0, mp] = jnp.max(s, axis=0, keepdims=True)

    qslot = (qi % 2) * 6
    qslot_next = 6 - qslot

    @pl.when(qi == 0)
    def _():
        build_q_variants(q_ref, 0)
        for mp in range(2):
            diagonal_scores(0, 0, mp)

    q0 = qi * tq
    rho = sigma * (lax.broadcasted_iota(jnp.int32, (1, tq), 1) + q0).astype(f32)
    m_ref[...] = jnp.full(m_ref.shape, _NEG_INF, f32)
    acc_ref[...] = jnp.zeros(acc_ref.shape, f32)

    kd = q0 // tk

    def key_tile(j):
        jm = jnp.minimum(j, nk - 1) - 1
        return jnp.where(j == 0, kd, jnp.where(jm < kd, jm, jm + 1))

    def scores(j, slot, mp):
        kb = key_tile(j)
        var = jnp.where(kb < kd, 0, jnp.where(kb > kd, 2, 1))
        k_start = pl.multiple_of(kb * tk, tk)
        s = _dot(kx_ref[mp, pl.ds(k_start, tk), :], qt_ref[qslot + 3 * mp + var])
        s_ref[slot, mp] = s
        cmax_ref[slot, mp] = jnp.max(s, axis=0, keepdims=True)

    def softmax_pv(j, slot, mp):
        kb = key_tile(j)
        shift = jnp.where(kb < kd, -1.0, jnp.where(kb > kd, 1.0, 0.0)) * rho
        m_old = m_ref[mp]
        m_new = jnp.maximum(m_old, cmax_ref[slot, mp] + shift)
        p = jnp.exp2(s_ref[slot, mp] - (m_new - shift))
        acc_ref[mp] = jnp.exp2(m_old - m_new) * acc_ref[mp] + _dot(vt_ref[kb], p.astype(bf16))
        m_ref[mp] = m_new

    def body(i, carry):
        for u in range(tpb):
            j = tpb * i + u
            for mp in range(2):
                scores(j + 1, (u + 1) % tpb, mp)
                softmax_pv(j, u, mp)
        return carry

    n_bodies = nk // tpb
    lax.fori_loop(0, n_bodies - 1, body, 0)

    build_q_variants(qn_ref, qslot_next)
    kd_next = jnp.minimum(qi + 1, seq // tq - 1)
    for u in range(tpb):
        j = tpb * (n_bodies - 1) + u
        for mp in range(2):
            if u + 1 < tpb:
                scores(j + 1, u + 1, mp)
            else:
                diagonal_scores(kd_next, qslot_next, mp)
            softmax_pv(j, u, mp)

    dl = dl_ref[...]
    lam = (jnp.exp(jnp.sum(dl[0:1] * dl[1:2], axis=1, keepdims=True))
           - jnp.exp(jnp.sum(dl[2:3] * dl[3:4], axis=1, keepdims=True)) + lam_init)
    a0 = acc_ref[0]
    a1 = acc_ref[1]
    ot = (a0[0:A_V_DIM] / a0[A_V_DIM:A_V_DIM + 1] - lam * (a1[0:A_V_DIM] / a1[A_V_DIM:A_V_DIM + 1]))
    ot = ot * lax.rsqrt(jnp.mean(ot * ot, axis=0, keepdims=True) + EPS)
    o_ref[...] = (ot * (1.0 - lam_init)).T.astype(o_ref.dtype)


def _diff_attention(pb, diff_lambda, slopes_a, layer_idx):
    bsz, seq, _ = pb.shape
    tq = tk = _tile(seq, _ATTN_TILE)
    tpb = _tile(seq // (2 * tk), _ATTN_MAX_TILES_PER_BODY)
    lam_init = 0.8 - 0.6 * math.exp(-0.3 * layer_idx)
    kern = functools.partial(_diff_attn_kernel, seq=seq, tq=tq, tk=tk, tpb=tpb, lam_init=lam_init)
    return pl.pallas_call(
        kern,
        grid=(bsz, A_HEADS, seq // tq),
        in_specs=[
            pl.BlockSpec(memory_space=pltpu.SMEM),
            pl.BlockSpec((None, tq, LANES), lambda b, h, i: (b, i, _PB_AQ + h)),
            pl.BlockSpec((None, tq, LANES), lambda b, h, i: (b, jnp.minimum(i + 1, seq // tq - 1), _PB_AQ + h)),
            pl.BlockSpec((None, seq, LANES), lambda b, h, i: (b, 0, _PB_AK + h)),
            pl.BlockSpec((None, seq, LANES), lambda b, h, i: (b, 0, _PB_AV + h)),
            pl.BlockSpec((4, A_QK_DIM), lambda b, h, i: (0, 0)),
        ],
        out_specs=pl.BlockSpec((None, tq, LANES), lambda b, h, i: (b, i, h)),
        out_shape=jax.ShapeDtypeStruct((bsz, seq, A_HEADS * A_V_DIM), jnp.bfloat16),
        scratch_shapes=[
            pltpu.VMEM((2, seq, LANES), jnp.bfloat16),
            pltpu.VMEM((seq // tk, _V_ROWS, tk), jnp.bfloat16),
            pltpu.VMEM((12, LANES, tq), jnp.bfloat16),
            pltpu.VMEM((2, 1, tq), jnp.float32),
            pltpu.VMEM((2, _V_ROWS, tq), jnp.float32),
            pltpu.VMEM((tpb, 2, tk, tq), jnp.float32),
            pltpu.VMEM((tpb, 2, 1, tq), jnp.float32),
            pltpu.VMEM((tk, tq), jnp.float32),
        ],
        compiler_params=_cparams("parallel", "parallel", "arbitrary"),
        name="diff_attention",
    )(slopes_a, pb, pb, pb, pb, diff_lambda)


def _retention_kernel(q_ref, k_ref, v_ref, g_ref, dec_ref, o_ref, of_ref, ob_ref, *, seq):
    head = pl.program_id(1)
    nc = seq // BLOCK
    f32 = jnp.float32

    lgv = _log_sigmoid(dec_ref[...])
    rr = lax.broadcasted_iota(jnp.int32, lgv.shape, 0)
    cc = lax.broadcasted_iota(jnp.int32, lgv.shape, 1)

    def pick(row):
        sel = jnp.where((rr == row) & (cc == head), lgv, 0.0)
        return jnp.sum(jnp.sum(sel, axis=1, keepdims=True), axis=0, keepdims=True)

    lg_f, lg_b = pick(0), pick(1)
    ri = lax.broadcasted_iota(jnp.int32, (BLOCK, BLOCK), 0).astype(f32)
    ci = lax.broadcasted_iota(jnp.int32, (BLOCK, BLOCK), 1).astype(f32)
    diff = ri - ci
    decay_f = jnp.where(diff >= 0, jnp.exp(lg_f * jnp.maximum(diff, 0.0)), 0.0)
    decay_b = jnp.where(diff <= 0, jnp.exp(lg_b * jnp.maximum(-diff, 0.0)), 0.0)
    idx = lax.broadcasted_iota(jnp.int32, (BLOCK, 1), 0).astype(f32)
    qdec_f = jnp.exp(lg_f * (idx + 1.0))
    kdec_f = jnp.exp(lg_f * (BLOCK - 1.0 - idx))
    qdec_b = jnp.exp(lg_b * (BLOCK - idx))
    kdec_b = jnp.exp(lg_b * idx)
    cdec_f = jnp.exp(lg_f * BLOCK)
    cdec_b = jnp.exp(lg_b * BLOCK)
    kscale = B_DIM ** -0.5

    def scores_and_kv(t0, backward):
        qb = q_ref[pl.ds(t0, BLOCK), :]
        kb = k_ref[pl.ds(t0, BLOCK), :].astype(f32) * kscale
        vb = v_ref[pl.ds(t0, BLOCK), :]
        kdec = kdec_b if backward else kdec_f
        return dict(t0=t0, backward=backward, qb=qb, vb=vb,
                    qk=_dot_nt(qb, kb.astype(qb.dtype)), kv=_dot_tn((kb * kdec).astype(vb.dtype), vb))

    def inner_product(g):
        s = g["qk"] * (decay_b if g["backward"] else decay_f)
        g["inner"] = _dot(s.astype(g["vb"].dtype), g["vb"])

    def with_state(g, state, park_ref, other_ref):
        qdec, cdec = (qdec_b, cdec_b) if g["backward"] else (qdec_f, cdec_f)
        cross = _dot(g["qb"], state.astype(g["qb"].dtype)) * qdec
        rows = pl.ds(g["t0"], BLOCK)
        if other_ref is None:
            park_ref[rows, :] = g["inner"] + cross
        else:
            r = g["inner"] + cross + other_ref[rows, :]
            r = r * lax.rsqrt(jnp.mean(r * r, axis=-1, keepdims=True) + EPS)
            gate = g_ref[rows, :].astype(f32)
            o_ref[rows, :] = (gate * jax.nn.sigmoid(gate) * r).astype(o_ref.dtype)
        return state * cdec + g["kv"]

    def make_body(second_half):
        def body(it, carry):
            rf, rb = carry
            steps = []
            for u in range(_RETENTION_UNROLL):
                c = it * _RETENTION_UNROLL + u
                steps.append(scores_and_kv(pl.multiple_of(c * BLOCK, BLOCK), False))
                steps.append(scores_and_kv(pl.multiple_of((nc - 1 - c) * BLOCK, BLOCK), True))
            for g in steps:
                inner_product(g)
            for g in steps:
                if g["backward"]:
                    rb = with_state(g, rb, ob_ref, of_ref if second_half else None)
                else:
                    rf = with_state(g, rf, of_ref, ob_ref if second_half else None)
            return rf, rb
        return body

    zero = jnp.zeros((B_DIM, B_DIM), f32)
    n_bodies = nc // _RETENTION_UNROLL
    state = lax.fori_loop(0, n_bodies // 2, make_body(False), (zero, zero))
    lax.fori_loop(n_bodies // 2, n_bodies, make_body(True), state)


def _retention(pb, dec_pad):
    bsz, seq, _ = pb.shape
    assert seq % (2 * _RETENTION_UNROLL * BLOCK) == 0
    kern = functools.partial(_retention_kernel, seq=seq)

    def col(base):
        return pl.BlockSpec((None, seq, LANES), lambda b, h: (b, 0, base + h))

    return pl.pallas_call(
        kern,
        grid=(bsz, B_HEADS),
        in_specs=[col(_PB_BQ), col(_PB_BK), col(_PB_BV), col(_PB_BG),
                  pl.BlockSpec((8, LANES), lambda b, h: (0, 0))],
        out_specs=pl.BlockSpec((None, seq, LANES), lambda b, h: (b, 0, h)),
        out_shape=jax.ShapeDtypeStruct((bsz, seq, B_HEADS * B_DIM), jnp.bfloat16),
        scratch_shapes=[pltpu.VMEM((seq, B_DIM), jnp.float32), pltpu.VMEM((seq, B_DIM), jnp.float32)],
        compiler_params=_cparams("parallel", "parallel"),
        name="retention",
    )(pb, pb, pb, pb, dec_pad)


_ROWS_PER_DIR = 6


def _mlstm_kernel(cq_ref, ck_ref, gt_ref, v_ref, og_ref, wq_ref, wk_ref, gb_ref, o_ref,
                  qc_ref, kt_ref, vx_ref, rows_ref, of_ref, ob_ref, *, seq):
    nc = seq // BLOCK
    f32 = jnp.float32
    bf16 = jnp.bfloat16
    rowi = lax.broadcasted_iota(jnp.int32, (BLOCK, LANES), 0)
    coli = lax.broadcasted_iota(jnp.int32, (BLOCK, LANES), 1)
    lane_n = lax.broadcasted_iota(jnp.int32, (nc, LANES), 1)
    tril = rowi >= coli
    triu = rowi <= coli
    tril_b = tril.astype(bf16)
    triu_b = triu.astype(bf16)
    ones_b = jnp.ones((BLOCK, LANES), bf16)
    head = pl.program_id(1)

    def cummax_lanes(x, backward):
        sh = 1
        while sh < LANES:
            if backward:
                x = jnp.maximum(x, jnp.where(lane_n < LANES - sh, pltpu.roll(x, LANES - sh, 1), _NEG_INF))
            else:
                x = jnp.maximum(x, jnp.where(lane_n >= sh, pltpu.roll(x, sh, 1), _NEG_INF))
            sh *= 2
        return x

    for d, (op, last) in enumerate(((triu_b, LANES - 1), (tril_b, 0))):
        lf = _log_sigmoid(gt_ref[2 * d + 1] + gb_ref[2 * d + 1, head]) * _LOG2E
        hi, lo = _split_hi_lo(lf)
        b = _dot(hi, op) + _dot(lo, op)
        a = (gt_ref[2 * d] + gb_ref[2 * d, head]) * _LOG2E - b
        bl = jnp.broadcast_to(b[:, last:last + 1], (nc, LANES))
        ml = jnp.broadcast_to(jnp.max(bl + a, axis=1, keepdims=True), (nc, LANES))
        for r, val in enumerate((a, cummax_lanes(a, d == 1), b, jnp.exp2(bl + a - ml), bl, ml)):
            rows_ref[d * _ROWS_PER_DIR + r] = val

    def conv_silu(src_ref, w, c, t0):
        x = src_ref[pl.ds(t0, BLOCK), :]
        prev = src_ref[pl.ds(jnp.maximum(t0 - 1, 0), 1), :] * jnp.where(c > 0, 1.0, 0.0)
        nxt = src_ref[pl.ds(jnp.minimum(t0 + BLOCK, seq - 1), 1), :] * jnp.where(c < nc - 1, 1.0, 0.0)
        xd = jnp.where(rowi == 0, prev, pltpu.roll(x, 1, 0))
        xu = jnp.where(rowi == BLOCK - 1, nxt, pltpu.roll(x, BLOCK - 1, 0))
        y = xd * w[0:1] + x * w[1:2] + xu * w[2:3]
        return y * jax.nn.sigmoid(y)

    def prep(c, carry):
        t0 = pl.multiple_of(c * BLOCK, BLOCK)
        qc_ref[pl.ds(t0, BLOCK), :] = (conv_silu(cq_ref, wq_ref[...], c, t0) * (C_DIM ** -0.5)).astype(bf16)
        kt_ref[c] = conv_silu(ck_ref, wk_ref[...], c, t0).T.astype(bf16)
        vx_ref[pl.ds(t0, BLOCK), 0:C_DIM] = v_ref[pl.ds(t0, BLOCK), :]
        vx_ref[pl.ds(t0, BLOCK), C_DIM:2 * C_DIM] = ones_b
        return carry

    lax.fori_loop(0, nc, prep, 0, unroll=2)

    def load_rows(ch, backward):
        t0 = pl.multiple_of(ch * BLOCK, BLOCK)
        base = _ROWS_PER_DIR if backward else 0
        a, c, b, w, bl, ml = (rows_ref[base + r, pl.ds(ch, 1), :] for r in range(_ROWS_PER_DIR))
        kt = kt_ref[ch]
        return dict(t0=t0, backward=backward, bl=bl[:, 0:1], m_loc=ml[:, 0:1],
                    a_t=jnp.where(triu if backward else tril, a, _NEG_INF),
                    b_full=jnp.broadcast_to(b, (BLOCK, LANES)).T,
                    c_full=jnp.broadcast_to(c, (BLOCK, LANES)).T,
                    kw=(kt.astype(f32) * w).astype(bf16), kt=kt)

    def first_products(g):
        vx = vx_ref[pl.ds(g["t0"], BLOCK), :]
        g["qk"] = _dot(qc_ref[pl.ds(g["t0"], BLOCK), :], g["kt"])
        g["kv"] = _dot(g["kw"], vx)

    def decayed_scores(g):
        g["s0"] = (g["qk"] * jnp.exp2(g["a_t"] - g["c_full"])).astype(bf16)

    def inner_products(g):
        g["p0"] = _dot(g["s0"], vx_ref[pl.ds(g["t0"], BLOCK), :])

    def with_state(g, state, park_ref, other_ref):
        cx, m = state
        x = _dot(qc_ref[pl.ds(g["t0"], BLOCK), :], cx.astype(bf16))
        n = jnp.maximum(g["c_full"], m)
        e = jnp.exp2(g["c_full"] - n)
        iw = jnp.exp2(m - n)
        num = e * g["p0"][:, 0:C_DIM] + iw * x[:, 0:C_DIM]
        den = e * g["p0"][:, C_DIM:] + iw * x[:, C_DIM:]
        hh = num / jnp.maximum(jnp.abs(den), jnp.exp2(-(g["b_full"] + n)))
        rows = pl.ds(g["t0"], BLOCK)
        if other_ref is None:
            park_ref[rows, :] = hh
        else:
            r = hh + other_ref[rows, :]
            r = r * lax.rsqrt(jnp.mean(r * r, axis=-1, keepdims=True) + EPS)
            o_ref[rows, :] = (jax.nn.sigmoid(og_ref[rows, :].astype(f32)) * r).astype(o_ref.dtype)
        m_new = jnp.maximum(g["bl"] + m, g["m_loc"])
        cw = jnp.exp2(g["bl"] + m - m_new)
        f = jnp.exp2(g["m_loc"] - m_new)
        return cw * cx + f * g["kv"], m_new

    def make_body(second_half):
        def body(it, carry):
            sf, sb = carry
            steps = []
            for u in range(_SCAN_UNROLL):
                c = it * _SCAN_UNROLL + u
                steps.append(load_rows(c, False))
                steps.append(load_rows(nc - 1 - c, True))
            for g in steps:
                first_products(g)
            for g in steps:
                decayed_scores(g)
            for g in steps:
                inner_products(g)
            for g in steps:
                if g["backward"]:
                    sb = with_state(g, sb, ob_ref, of_ref if second_half else None)
                else:
                    sf = with_state(g, sf, of_ref, ob_ref if second_half else None)
            return sf, sb
        return body

    init = (jnp.zeros((C_DIM, 2 * C_DIM), f32), jnp.zeros((1, 1), f32))
    n_bodies = nc // _SCAN_UNROLL
    state = lax.fori_loop(0, n_bodies // 2, make_body(False), (init, init))
    lax.fori_loop(n_bodies // 2, n_bodies, make_body(True), state)


def _mlstm(pb, pf, gates_t, wq_conv, wk_conv, gate_bias):
    bsz, seq, _ = pb.shape
    assert seq % (8 * BLOCK) == 0
    kern = functools.partial(_mlstm_kernel, seq=seq)

    def col(base):
        return pl.BlockSpec((None, seq, LANES), lambda b, h: (b, 0, base + h))

    return pl.pallas_call(
        kern,
        grid=(bsz, C_HEADS),
        in_specs=[col(_PF_CQ), col(_PF_CK),
                  pl.BlockSpec((None, None, 4, seq // BLOCK, LANES), lambda b, h: (b, h, 0, 0, 0)),
                  col(_PB_CV), col(_PB_CO),
                  pl.BlockSpec((3, LANES), lambda b, h: (0, h)),
                  pl.BlockSpec((3, LANES), lambda b, h: (0, h)),
                  pl.BlockSpec(memory_space=pltpu.SMEM)],
        out_specs=pl.BlockSpec((None, seq, LANES), lambda b, h: (b, 0, h)),
        out_shape=jax.ShapeDtypeStruct((bsz, seq, C_HEADS * C_DIM), jnp.bfloat16),
        scratch_shapes=[pltpu.VMEM((seq, C_DIM), jnp.bfloat16),
                        pltpu.VMEM((seq // BLOCK, C_DIM, BLOCK), jnp.bfloat16),
                        pltpu.VMEM((seq, 2 * C_DIM), jnp.bfloat16),
                        pltpu.VMEM((2 * _ROWS_PER_DIR, seq // BLOCK, LANES), jnp.float32),
                        pltpu.VMEM((seq, C_DIM), jnp.float32), pltpu.VMEM((seq, C_DIM), jnp.float32)],
        compiler_params=_cparams("parallel", "parallel"),
        name="mlstm",
    )(pf, pf, gates_t, pb, pb, wq_conv, wk_conv, gate_bias)


def _window_attn_kernel(slopes_ref, sink_ref, q_ref, k_ref, v_ref, o_ref, kx_ref, vx_ref, *, seq):
    qi = pl.program_id(1)
    f32 = jnp.float32
    span = BLOCK + 2 * WINDOW
    group = D_HEADS // D_KV_HEADS

    @pl.when(qi == 0)
    def _():
        lane = lax.broadcasted_iota(jnp.int32, (seq, LANES), 1)
        ones = jnp.ones((seq, LANES), vx_ref.dtype)
        for src, dst in ((k_ref, kx_ref), (v_ref, vx_ref)):
            x = src[...].astype(f32)
            x0 = jnp.where(lane < D_DIM, x, 0.0)
            x1 = jnp.where(lane >= D_DIM, x, 0.0)
            for i, xi in enumerate((x0, pltpu.roll(x0, D_DIM, 1), pltpu.roll(x1, D_DIM, 1), x1)):
                dst[i, :, 0:LANES] = xi.astype(dst.dtype)
        for i in range(4):
            vx_ref[i, :, LANES:2 * LANES] = ones

    t0 = qi * BLOCK
    ws = pl.multiple_of(jnp.clip(t0 - WINDOW, 0, seq - span), BLOCK)
    rel = (lax.broadcasted_iota(jnp.int32, (BLOCK, span), 0)
           - lax.broadcasted_iota(jnp.int32, (BLOCK, span), 1)) + (t0 - ws)
    dist = jnp.abs(rel)
    neg_dist = jnp.where(dist <= WINDOW, dist.astype(f32) * (-_LOG2E), _NEG_INF)

    n_pairs = D_HEADS // 2
    lookahead = 2
    scores = {}

    def score_pair(pair):
        kv = (2 * pair) // group
        qp = (q_ref[:, pair * LANES:(pair + 1) * LANES].astype(f32) * (D_DIM ** -0.5 * _LOG2E)).astype(q_ref.dtype)
        for half in range(2):
            scores[2 * pair + half] = _dot_nt(qp, kx_ref[2 * kv + half, pl.ds(ws, span), :])

    def attend(hd, kv_slot):
        s = scores[hd] + slopes_ref[hd] * neg_dist
        snk = sink_ref[hd] * _LOG2E
        m = jnp.maximum(jnp.max(s, axis=1, keepdims=True), snk)
        o = _dot(jnp.exp2(s - m).astype(jnp.bfloat16), vx_ref[kv_slot, pl.ds(ws, span), :])
        return o[:, 0:LANES] / (o[:, LANES:] + jnp.exp2(snk - m))

    for pair in range(lookahead):
        score_pair(pair)
    for pair in range(n_pairs):
        kv = (2 * pair) // group
        out = attend(2 * pair, 2 * kv) + attend(2 * pair + 1, 2 * kv + 1)
        o_ref[:, pair * LANES:(pair + 1) * LANES] = out.astype(o_ref.dtype)
        if pair + lookahead < n_pairs:
            score_pair(pair + lookahead)


def _window_attention(pb, attn_sink, slopes_d):
    bsz, seq, _ = pb.shape
    kern = functools.partial(_window_attn_kernel, seq=seq)
    width = D_HEADS * D_DIM
    return pl.pallas_call(
        kern,
        grid=(bsz, seq // BLOCK),
        in_specs=[
            pl.BlockSpec(memory_space=pltpu.SMEM),
            pl.BlockSpec(memory_space=pltpu.SMEM),
            pl.BlockSpec((None, BLOCK, width), lambda b, i: (b, i, _PB_DQ512)),
            pl.BlockSpec((None, seq, LANES), lambda b, i: (b, 0, _PB_DK)),
            pl.BlockSpec((None, seq, LANES), lambda b, i: (b, 0, _PB_DV)),
        ],
        out_specs=pl.BlockSpec((None, BLOCK, width), lambda b, i: (b, i, 0)),
        out_shape=jax.ShapeDtypeStruct((bsz, seq, width), jnp.bfloat16),
        scratch_shapes=[pltpu.VMEM((4, seq, LANES), jnp.bfloat16), pltpu.VMEM((4, seq, 2 * LANES), jnp.bfloat16)],
        compiler_params=_cparams("parallel", "arbitrary"),
        name="window_attention",
    )(slopes_d, attn_sink, pb, pb, pb)


def _out_proj_kernel(x_ref, ya_ref, yb_ref, yc_ref, yd_ref, w_ref, o_ref):
    acc = x_ref[...]
    for g, y_ref in enumerate((ya_ref, yb_ref, yc_ref, yd_ref)):
        acc = acc + _dot(y_ref[...], w_ref[g * 512:(g + 1) * 512, :])
    o_ref[...] = acc


def _out_proj(x2d, ys, w_out, tm):
    t, d = x2d.shape
    yspec = pl.BlockSpec((tm, 512), lambda i: (i, 0))
    return pl.pallas_call(
        _out_proj_kernel,
        grid=(t // tm,),
        in_specs=[pl.BlockSpec((tm, d), lambda i: (i, 0)), yspec, yspec, yspec, yspec,
                  pl.BlockSpec((d, d), lambda i: (0, 0), pipeline_mode=pl.Buffered(1))],
        out_specs=pl.BlockSpec((tm, d), lambda i: (i, 0)),
        out_shape=jax.ShapeDtypeStruct((t, d), jnp.float32),
        compiler_params=_cparams("parallel", vmem_limit=_BIG_VMEM_LIMIT),
        name="out_proj",
    )(x2d, *ys, w_out)


def _ffn_kernel(x_ref, xp_ref, xn_ref, nw_ref, wg_ref, wu_ref, cw_ref, wd_ref, fw_ref, o_ref,
                h_ref, *, seq, tm, final_norm):
    i = pl.program_id(0)
    j = pl.program_id(1)
    f32 = jnp.float32

    def norm(x):
        return x * lax.rsqrt(jnp.mean(x * x, axis=-1, keepdims=True) + EPS) * nw_ref[...]

    def d_ff_step(first):
        gate = _dot(h_ref[...], wg_ref[...])
        gm = gate[0:tm]
        rowi = lax.broadcasted_iota(jnp.int32, gm.shape, 0)
        g_dn = jnp.where(rowi == 0, gate[tm + 7:tm + 8], pltpu.roll(gm, 1, 0))
        g_up = jnp.where(rowi == tm - 1, gate[tm + 8:tm + 9], pltpu.roll(gm, tm - 1, 0))
        cw = cw_ref[...]
        g = g_dn * cw[0:1] + gm * cw[1:2] + g_up * cw[2:3]
        up = _dot(h_ref[0:tm, :], wu_ref[...])
        act = (g * jax.nn.sigmoid(g) * up).astype(wd_ref.dtype)
        down = _dot(act, wd_ref[...])
        if first:
            o_ref[...] = down
        else:
            o_ref[...] += down

    @pl.when(j == 0)
    def _():
        h_ref[0:tm, :] = norm(x_ref[...]).astype(h_ref.dtype)
        has_prev = ((i * tm) % seq != 0).astype(f32)
        has_next = (((i + 1) * tm) % seq != 0).astype(f32)
        halo = jnp.concatenate([norm(xp_ref[...]) * has_prev, norm(xn_ref[...]) * has_next], axis=0)
        h_ref[tm:tm + 16, :] = halo.astype(h_ref.dtype)
        d_ff_step(True)

    @pl.when(j > 0)
    def _():
        d_ff_step(False)

    @pl.when(j == pl.num_programs(1) - 1)
    def _():
        y = x_ref[...] + o_ref[...]
        if final_norm:
            y = y * lax.rsqrt(jnp.mean(y * y, axis=-1, keepdims=True) + EPS) * fw_ref[...]
        o_ref[...] = y


def _ffn(x2d, seq, nw, wg, wu, cw, wd, fw, final_norm, tm, tn):
    t, d = x2d.shape
    nff = wg.shape[1]
    rb = tm // 8
    last8 = t // 8 - 1
    kern = functools.partial(_ffn_kernel, seq=seq, tm=tm, final_norm=final_norm)
    return pl.pallas_call(
        kern,
        grid=(t // tm, nff // tn),
        in_specs=[
            pl.BlockSpec((tm, d), lambda i, j: (i, 0)),
            pl.BlockSpec((8, d), lambda i, j: (jnp.maximum(i * rb - 1, 0), 0)),
            pl.BlockSpec((8, d), lambda i, j: (jnp.minimum((i + 1) * rb, last8), 0)),
            pl.BlockSpec((1, d), lambda i, j: (0, 0)),
            pl.BlockSpec((d, tn), lambda i, j: (0, j)),
            pl.BlockSpec((d, tn), lambda i, j: (0, j)),
            pl.BlockSpec((3, tn), lambda i, j: (0, j)),
            pl.BlockSpec((tn, d), lambda i, j: (j, 0)),
            pl.BlockSpec((1, d), lambda i, j: (0, 0)),
        ],
        out_specs=pl.BlockSpec((tm, d), lambda i, j: (i, 0)),
        out_shape=jax.ShapeDtypeStruct((t, d), jnp.float32),
        scratch_shapes=[pltpu.VMEM((tm + 16, d), jnp.bfloat16)],
        compiler_params=_cparams("parallel", "arbitrary", vmem_limit=_BIG_VMEM_LIMIT),
        name="ffn",
    )(x2d, x2d, x2d, nw, wg, wu, cw, wd, fw)


def _prepare_layer(l, norm1_w, w_in, qk_conv_w, diff_lambda, ret_decay_logit, mlstm_gate_bias, attn_sink,
                   w_out, norm2_w, w_gate, w_up, ffn_conv_w, w_down):
    bf16 = jnp.bfloat16
    f32 = jnp.float32
    wi = w_in[l]
    w_pb = jnp.concatenate([wi[:, _OFF_A:_OFF_CQ], wi[:, _OFF_CV:_OFF_GATES], wi[:, _OFF_D:N_IN]], axis=1)
    wgates = jnp.pad(wi[:, _OFF_GATES:_OFF_D], ((0, 0), (0, LANES - N_GATES)))
    w_pf = jnp.concatenate([wi[:, _OFF_CQ:_OFF_CV], wgates], axis=1)
    gbias = mlstm_gate_bias[l].astype(f32)
    dec = jnp.zeros((8, LANES), f32).at[0:2, 0:B_HEADS].set(ret_decay_logit[l].astype(f32))
    return dict(
        norm1=norm1_w[l].reshape(1, D_MODEL).astype(f32),
        w_pb=w_pb.astype(bf16), w_pf=w_pf.astype(bf16),
        wq_conv=qk_conv_w[l][:, :C_HEADS * C_DIM].astype(f32),
        wk_conv=qk_conv_w[l][:, C_HEADS * C_DIM:].astype(f32),
        gbias=gbias, dec=dec,
        diff_lambda=diff_lambda[l].astype(f32), sink=attn_sink[l].astype(f32),
        w_out=w_out[l].astype(bf16),
        norm2=norm2_w[l].reshape(1, D_MODEL).astype(f32),
        w_gate=w_gate[l].astype(bf16), w_up=w_up[l].astype(bf16),
        ffn_conv=ffn_conv_w[l].astype(f32), w_down=w_down[l].astype(bf16),
    )


def _tile(n, pref):
    t = min(pref, n)
    while n % t:
        t //= 2
    return t


def _layer(x2d, bsz, seq, layer_idx, p, slopes, final_w, final_norm):
    t = bsz * seq
    tm_in = _tile(t, 1024)
    pb, pf = _in_proj(x2d, p["norm1"], p["w_pb"], p["w_pf"], tm_in, _IN_PROJ_TN)
    pb = pb.reshape(bsz, seq, PB_WIDTH)
    pf = pf.reshape(bsz, seq, PF_WIDTH)
    gates_t = pf[:, :, _PF_GATES * LANES:_PF_GATES * LANES + N_GATES].reshape(bsz, seq, 4, C_HEADS)
    gates_t = jnp.transpose(gates_t, (0, 3, 2, 1)).reshape(bsz, C_HEADS, 4, seq // BLOCK, BLOCK)
    y_a = _diff_attention(pb, p["diff_lambda"], slopes[D_HEADS:], layer_idx)
    y_b = _retention(pb, p["dec"])
    y_c = _mlstm(pb, pf, gates_t, p["wq_conv"], p["wk_conv"], p["gbias"])
    y_d = _window_attention(pb, p["sink"], slopes[:D_HEADS])
    ys = [y.reshape(t, 512) for y in (y_a, y_b, y_c, y_d)]
    x1 = _out_proj(x2d, ys, p["w_out"], _tile(t, 1024))
    return _ffn(x1, seq, p["norm2"], p["w_gate"], p["w_up"], p["ffn_conv"], p["w_down"], final_w,
                final_norm, _tile(seq, 1024), 512)


def _trunk(x, layers, slopes, final_w):
    bsz, seq, d = x.shape
    x2d = x.reshape(bsz * seq, d)
    for l, p in enumerate(layers):
        x2d = _layer(x2d, bsz, seq, l, p, slopes, final_w, l == len(layers) - 1)
    return x2d.reshape(bsz, seq, d)


def kernel(x_prompt, x_sample, norm1_w, w_in, qk_conv_w, diff_lambda, ret_decay_logit, mlstm_gate_bias,
           attn_sink, w_out, norm2_w, w_gate, w_up, ffn_conv_w, w_down, final_norm_w):
    layers = [_prepare_layer(l, norm1_w, w_in, qk_conv_w, diff_lambda, ret_decay_logit, mlstm_gate_bias,
                             attn_sink, w_out, norm2_w, w_gate, w_up, ffn_conv_w, w_down)
              for l in range(DEPTH)]
    n = A_HEADS + D_HEADS
    slopes = 2.0 ** (-8.0 * jnp.arange(1, n + 1, dtype=jnp.float32) / n)
    final_w = final_norm_w.reshape(1, D_MODEL).astype(jnp.float32)
    return (_trunk(x_prompt, layers, slopes, final_w), _trunk(x_sample, layers, slopes, final_w))
```

```python
import functools
import math

import jax
import jax.numpy as jnp
from jax import lax
from jax.experimental import pallas as pl
from jax.experimental.pallas import tpu as pltpu

D_MODEL = 2048
DEPTH = 2
A_HEADS = 4
A_QK_DIM = 64
A_V_DIM = 128
B_HEADS = 4
B_DIM = 128
C_HEADS = 4
C_DIM = 128
D_HEADS = 8
D_KV_HEADS = 2
D_DIM = 64
WINDOW = 128
BLOCK = 128
D_FF = 5632
EPS = 1e-6
LANES = 128

_OFF_A = 0
_OFF_B = 1536
_OFF_CQ = 3584
_OFF_CV = 4608
_OFF_GATES = 5632
_OFF_D = 5648
N_IN = 6416

_PB_AQ, _PB_AK, _PB_AV = 0, 4, 8
_PB_BQ, _PB_BK, _PB_BV, _PB_BG = 12, 16, 20, 24
_PB_CV, _PB_CO = 28, 32
_PB_DQ512 = 9
_PB_DK, _PB_DV = 40, 41
PB_WIDTH = 42 * LANES
_IN_PROJ_TN = 1792
_PF_CQ, _PF_CK, _PF_GATES = 0, 4, 8
PF_WIDTH = 9 * LANES
N_GATES = 16

MIXER_WIDTH = 512
_TOKEN_TILE = 1024
_FFN_TN = 512
_VMEM_LIMIT = 56 * 1024 * 1024
_BIG_VMEM_LIMIT = 61 * 1024 * 1024
_SCAN_UNROLL = 4
_RETENTION_UNROLL = 8
_NEG_INF = float("-inf")


def _cparams(*sem, vmem_limit=_VMEM_LIMIT):
    return pltpu.CompilerParams(dimension_semantics=sem, vmem_limit_bytes=vmem_limit)


def _dot(a, b):
    return jnp.dot(a, b, preferred_element_type=jnp.float32)


def _dot_nt(a, b):
    return lax.dot_general(a, b, (((1,), (1,)), ((), ())), preferred_element_type=jnp.float32)


def _dot_tn(a, b):
    return lax.dot_general(a, b, (((0,), (0,)), ((), ())), preferred_element_type=jnp.float32)


def _log_sigmoid(x):
    return -(jnp.maximum(-x, 0.0) + jnp.log1p(jnp.exp(-jnp.abs(x))))


def _split_hi_lo(x):
    hi = x.astype(jnp.bfloat16)
    lo = (x - hi.astype(jnp.float32)).astype(jnp.bfloat16)
    return hi, lo


def _in_proj_kernel(x_ref, nw_ref, wb_ref, wf_ref, ob_ref, of_ref, h_ref, *, nb):
    j = pl.program_id(1)

    @pl.when(j == 0)
    def _():
        x = x_ref[...]
        ms = jnp.mean(x * x, axis=-1, keepdims=True)
        h = (x * lax.rsqrt(ms + EPS) * nw_ref[...]).astype(h_ref.dtype)
        h_ref[...] = h
        ob_ref[...] = _dot(h, wb_ref[...]).astype(ob_ref.dtype)

    @pl.when((j > 0) & (j < nb))
    def _():
        ob_ref[...] = _dot(h_ref[...], wb_ref[...]).astype(ob_ref.dtype)

    @pl.when(j == nb)
    def _():
        of_ref[...] = _dot(h_ref[...], wf_ref[...])


def _in_proj(x2d, nw, w_pb, w_pf, tm, tn):
    t, d = x2d.shape
    nb = PB_WIDTH // tn
    return pl.pallas_call(
        functools.partial(_in_proj_kernel, nb=nb),
        grid=(t // tm, nb + 1),
        in_specs=[
            pl.BlockSpec((tm, d), lambda i, j: (i, 0)),
            pl.BlockSpec((1, d), lambda i, j: (0, 0)),
            pl.BlockSpec((d, tn), lambda i, j: (0, jnp.minimum(j, nb - 1))),
            pl.BlockSpec((d, PF_WIDTH), lambda i, j: (0, 0), pipeline_mode=pl.Buffered(1)),
        ],
        out_specs=[pl.BlockSpec((tm, tn), lambda i, j: (i, jnp.minimum(j, nb - 1))),
                   pl.BlockSpec((tm, PF_WIDTH), lambda i, j: (i, 0))],
        out_shape=[jax.ShapeDtypeStruct((t, PB_WIDTH), jnp.bfloat16),
                   jax.ShapeDtypeStruct((t, PF_WIDTH), jnp.float32)],
        scratch_shapes=[pltpu.VMEM((tm, d), jnp.bfloat16)],
        compiler_params=_cparams("parallel", "arbitrary", vmem_limit=_BIG_VMEM_LIMIT),
        name="norm_in_proj",
    )(x2d, nw, w_pb, w_pf)


_LOG2E = 1.4426950408889634
_N_BIAS_LANES = 3
_V_ROWS = A_V_DIM + 16
_ATTN_MAX_TILES_PER_BODY = 8
_ATTN_TILE = 512


def _diff_attn_kernel(slopes_ref, q_ref, qn_ref, k_ref, v_ref, dl_ref, o_ref,
                      kx_ref, vt_ref, qt_ref, m_ref, acc_ref, s_ref, cmax_ref, bias_ref, *, seq, tq, tk,
                      tpb, lam_init):
    head = pl.program_id(1)
    qi = pl.program_id(2)
    f32 = jnp.float32
    bf16 = jnp.bfloat16
    sigma = slopes_ref[head] * _LOG2E
    nk = seq // tk
    prep_rows = tk

    @pl.when(qi == 0)
    def _():
        lane = lax.broadcasted_iota(jnp.int32, (prep_rows, LANES), 1)
        row = lax.broadcasted_iota(jnp.int32, (prep_rows, LANES), 0)
        ones_row = jnp.where(lax.broadcasted_iota(jnp.int32, (_V_ROWS - A_V_DIM, tk), 0) == 0, 1.0, 0.0)

        def prep(c, carry):
            r0 = pl.multiple_of(c * prep_rows, prep_rows)
            kk = k_ref[pl.ds(r0, prep_rows), :].astype(f32)
            col = sigma * (row + r0).astype(f32)
            c1 = col.astype(bf16).astype(f32)
            c2 = (col - c1).astype(bf16).astype(f32)
            c3 = col - c1 - c2
            bias = jnp.where(lane == A_QK_DIM, c1, jnp.where(lane == A_QK_DIM + 1, c2,
                             jnp.where(lane == A_QK_DIM + 2, c3, 0.0)))
            kx_ref[0, pl.ds(r0, prep_rows), :] = jnp.where(lane < A_QK_DIM, kk, bias).astype(bf16)
            kx_ref[1, pl.ds(r0, prep_rows), :] = jnp.where(
                lane < A_QK_DIM, pltpu.roll(kk, A_QK_DIM, 1), bias).astype(bf16)
            vt_ref[c, 0:A_V_DIM, :] = v_ref[pl.ds(r0, prep_rows), :].astype(f32).T.astype(bf16)
            vt_ref[c, A_V_DIM:_V_ROWS, :] = ones_row.astype(bf16)
            return carry

        lax.fori_loop(0, nk, prep, 0)
        rel = (lax.broadcasted_iota(jnp.int32, (tk, tq), 1)
               - lax.broadcasted_iota(jnp.int32, (tk, tq), 0)).astype(f32)
        bias_ref[...] = sigma * jnp.abs(rel)

    qrow = lax.broadcasted_iota(jnp.int32, (LANES, tq), 0)
    bias_rows = lax.broadcasted_iota(jnp.int32, (16, tq), 0) < _N_BIAS_LANES

    def build_q_variants(src_ref, base_slot):
        qt = (src_ref[...].astype(f32) * (A_QK_DIM ** -0.5 * _LOG2E)).T
        for mp in range(2):
            qm = qt if mp == 0 else pltpu.roll(qt, A_QK_DIM, 0)
            base = jnp.where(qrow < A_QK_DIM, qm, 0.0).astype(bf16)
            for var, sign in enumerate((1.0, 0.0, -1.0)):
                qt_ref[base_slot + 3 * mp + var] = base
                qt_ref[base_slot + 3 * mp + var, A_QK_DIM:A_QK_DIM + 16, :] = (
                    jnp.where(bias_rows, sign, 0.0).astype(bf16))

    def diagonal_scores(tile, base_slot, mp):
        k_start = pl.multiple_of(tile * tk, tk)
        s = _dot(kx_ref[mp, pl.ds(k_start, tk), :], qt_ref[base_slot + 3 * mp + 1]) - bias_ref[...]
        s_ref[0, mp] = s
        cmax_ref[0, mp] = jnp.max(s, axis=0, keepdims=True)

    qslot = (qi % 2) * 6
    qslot_next = 6 - qslot

    @pl.when(qi == 0)
    def _():
        build_q_variants(q_ref, 0)
        for mp in range(2):
            diagonal_scores(0, 0, mp)

    q0 = qi * tq
    rho = sigma * (lax.broadcasted_iota(jnp.int32, (1, tq), 1) + q0).astype(f32)
    m_ref[...] = jnp.full(m_ref.shape, _NEG_INF, f32)
    acc_ref[...] = jnp.zeros(acc_ref.shape, f32)

    kd = q0 // tk

    def key_tile(j):
        jm = jnp.minimum(j, nk - 1) - 1
        return jnp.where(j == 0, kd, jnp.where(jm < kd, jm, jm + 1))

    def scores(j, slot, mp):
        kb = key_tile(j)
        var = jnp.where(kb < kd, 0, jnp.where(kb > kd, 2, 1))
        k_start = pl.multiple_of(kb * tk, tk)
        s = _dot(kx_ref[mp, pl.ds(k_start, tk), :], qt_ref[qslot + 3 * mp + var])
        s_ref[slot, mp] = s
        cmax_ref[slot, mp] = jnp.max(s, axis=0, keepdims=True)

    def softmax_pv(j, slot, mp):
        kb = key_tile(j)
        shift = jnp.where(kb < kd, -1.0, jnp.where(kb > kd, 1.0, 0.0)) * rho
        m_old = m_ref[mp]
        m_new = jnp.maximum(m_old, cmax_ref[slot, mp] + shift)
        p = jnp.exp2(s_ref[slot, mp] - (m_new - shift))
        acc_ref[mp] = jnp.exp2(m_old - m_new) * acc_ref[mp] + _dot(vt_ref[kb], p.astype(bf16))
        m_ref[mp] = m_new

    def body(i, carry):
        for u in range(tpb):
            j = tpb * i + u
            for mp in range(2):
                scores(j + 1, (u + 1) % tpb, mp)
                softmax_pv(j, u, mp)
        return carry

    n_bodies = nk // tpb
    lax.fori_loop(0, n_bodies - 1, body, 0)

    build_q_variants(qn_ref, qslot_next)
    kd_next = jnp.minimum(qi + 1, seq // tq - 1)
    for u in range(tpb):
        j = tpb * (n_bodies - 1) + u
        for mp in range(2):
            if u + 1 < tpb:
                scores(j + 1, u + 1, mp)
            else:
                diagonal_scores(kd_next, qslot_next, mp)
            softmax_pv(j, u, mp)

    dl = dl_ref[...]
    lam = (jnp.exp(jnp.sum(dl[0:1] * dl[1:2], axis=1, keepdims=True))
           - jnp.exp(jnp.sum(dl[2:3] * dl[3:4], axis=1, keepdims=True)) + lam_init)
    a0 = acc_ref[0]
    a1 = acc_ref[1]
    ot = (a0[0:A_V_DIM] / a0[A_V_DIM:A_V_DIM + 1] - lam * (a1[0:A_V_DIM] / a1[A_V_DIM:A_V_DIM + 1]))
    ot = ot * lax.rsqrt(jnp.mean(ot * ot, axis=0, keepdims=True) + EPS)
    o_ref[...] = (ot * (1.0 - lam_init)).T.astype(o_ref.dtype)


def _diff_attention(pb, diff_lambda, slopes_a, layer_idx):
    bsz, seq, _ = pb.shape
    tq = tk = _tile(seq, _ATTN_TILE)
    tpb = _tile(seq // (2 * tk), _ATTN_MAX_TILES_PER_BODY)
    lam_init = 0.8 - 0.6 * math.exp(-0.3 * layer_idx)
    kern = functools.partial(_diff_attn_kernel, seq=seq, tq=tq, tk=tk, tpb=tpb, lam_init=lam_init)
    return pl.pallas_call(
        kern,
        grid=(bsz, A_HEADS, seq // tq),
        in_specs=[
            pl.BlockSpec(memory_space=pltpu.SMEM),
            pl.BlockSpec((None, tq, LANES), lambda b, h, i: (b, i, _PB_AQ + h)),
            pl.BlockSpec((None, tq, LANES), lambda b, h, i: (b, jnp.minimum(i + 1, seq // tq - 1), _PB_AQ + h)),
            pl.BlockSpec((None, seq, LANES), lambda b, h, i: (b, 0, _PB_AK + h)),
            pl.BlockSpec((None, seq, LANES), lambda b, h, i: (b, 0, _PB_AV + h)),
            pl.BlockSpec((4, A_QK_DIM), lambda b, h, i: (0, 0)),
        ],
        out_specs=pl.BlockSpec((None, tq, LANES), lambda b, h, i: (b, i, h)),
        out_shape=jax.ShapeDtypeStruct((bsz, seq, A_HEADS * A_V_DIM), jnp.bfloat16),
        scratch_shapes=[
            pltpu.VMEM((2, seq, LANES), jnp.bfloat16),
            pltpu.VMEM((seq // tk, _V_ROWS, tk), jnp.bfloat16),
            pltpu.VMEM((12, LANES, tq), jnp.bfloat16),
            pltpu.VMEM((2, 1, tq), jnp.float32),
            pltpu.VMEM((2, _V_ROWS, tq), jnp.float32),
            pltpu.VMEM((tpb, 2, tk, tq), jnp.float32),
            pltpu.VMEM((tpb, 2, 1, tq), jnp.float32),
            pltpu.VMEM((tk, tq), jnp.float32),
        ],
        compiler_params=_cparams("parallel", "parallel", "arbitrary"),
        name="diff_attention",
    )(slopes_a, pb, pb, pb, pb, diff_lambda)


def _retention_kernel(q_ref, k_ref, v_ref, g_ref, dec_ref, o_ref, of_ref, ob_ref, *, seq):
    head = pl.program_id(1)
    nc = seq // BLOCK
    f32 = jnp.float32

    lgv = _log_sigmoid(dec_ref[...])
    rr = lax.broadcasted_iota(jnp.int32, lgv.shape, 0)
    cc = lax.broadcasted_iota(jnp.int32, lgv.shape, 1)

    def pick(row):
        sel = jnp.where((rr == row) & (cc == head), lgv, 0.0)
        return jnp.sum(jnp.sum(sel, axis=1, keepdims=True), axis=0, keepdims=True)

    lg_f, lg_b = pick(0), pick(1)
    ri = lax.broadcasted_iota(jnp.int32, (BLOCK, BLOCK), 0).astype(f32)
    ci = lax.broadcasted_iota(jnp.int32, (BLOCK, BLOCK), 1).astype(f32)
    diff = ri - ci
    decay_f = jnp.where(diff >= 0, jnp.exp(lg_f * jnp.maximum(diff, 0.0)), 0.0)
    decay_b = jnp.where(diff <= 0, jnp.exp(lg_b * jnp.maximum(-diff, 0.0)), 0.0)
    idx = lax.broadcasted_iota(jnp.int32, (BLOCK, 1), 0).astype(f32)
    qdec_f = jnp.exp(lg_f * (idx + 1.0))
    kdec_f = jnp.exp(lg_f * (BLOCK - 1.0 - idx))
    qdec_b = jnp.exp(lg_b * (BLOCK - idx))
    kdec_b = jnp.exp(lg_b * idx)
    cdec_f = jnp.exp(lg_f * BLOCK)
    cdec_b = jnp.exp(lg_b * BLOCK)
    kscale = B_DIM ** -0.5

    def scores_and_kv(t0, backward):
        qb = q_ref[pl.ds(t0, BLOCK), :]
        kb = k_ref[pl.ds(t0, BLOCK), :].astype(f32) * kscale
        vb = v_ref[pl.ds(t0, BLOCK), :]
        kdec = kdec_b if backward else kdec_f
        return dict(t0=t0, backward=backward, qb=qb, vb=vb,
                    qk=_dot_nt(qb, kb.astype(qb.dtype)), kv=_dot_tn((kb * kdec).astype(vb.dtype), vb))

    def inner_product(g):
        s = g["qk"] * (decay_b if g["backward"] else decay_f)
        g["inner"] = _dot(s.astype(g["vb"].dtype), g["vb"])

    def with_state(g, state, park_ref, other_ref):
        qdec, cdec = (qdec_b, cdec_b) if g["backward"] else (qdec_f, cdec_f)
        cross = _dot(g["qb"], state.astype(g["qb"].dtype)) * qdec
        rows = pl.ds(g["t0"], BLOCK)
        if other_ref is None:
            park_ref[rows, :] = g["inner"] + cross
        else:
            r = g["inner"] + cross + other_ref[rows, :]
            r = r * lax.rsqrt(jnp.mean(r * r, axis=-1, keepdims=True) + EPS)
            gate = g_ref[rows, :].astype(f32)
            o_ref[rows, :] = (gate * jax.nn.sigmoid(gate) * r).astype(o_ref.dtype)
        return state * cdec + g["kv"]

    def make_body(second_half):
        def body(it, carry):
            rf, rb = carry
            steps = []
            for u in range(_RETENTION_UNROLL):
                c = it * _RETENTION_UNROLL + u
                steps.append(scores_and_kv(pl.multiple_of(c * BLOCK, BLOCK), False))
                steps.append(scores_and_kv(pl.multiple_of((nc - 1 - c) * BLOCK, BLOCK), True))
            for g in steps:
                inner_product(g)
            for g in steps:
                if g["backward"]:
                    rb = with_state(g, rb, ob_ref, of_ref if second_half else None)
                else:
                    rf = with_state(g, rf, of_ref, ob_ref if second_half else None)
            return rf, rb
        return body

    zero = jnp.zeros((B_DIM, B_DIM), f32)
    n_bodies = nc // _RETENTION_UNROLL
    state = lax.fori_loop(0, n_bodies // 2, make_body(False), (zero, zero))
    lax.fori_loop(n_bodies // 2, n_bodies, make_body(True), state)


def _retention(pb, dec_pad):
    bsz, seq, _ = pb.shape
    assert seq % (2 * _RETENTION_UNROLL * BLOCK) == 0
    kern = functools.partial(_retention_kernel, seq=seq)

    def col(base):
        return pl.BlockSpec((None, seq, LANES), lambda b, h: (b, 0, base + h))

    return pl.pallas_call(
        kern,
        grid=(bsz, B_HEADS),
        in_specs=[col(_PB_BQ), col(_PB_BK), col(_PB_BV), col(_PB_BG),
                  pl.BlockSpec((8, LANES), lambda b, h: (0, 0))],
        out_specs=pl.BlockSpec((None, seq, LANES), lambda b, h: (b, 0, h)),
        out_shape=jax.ShapeDtypeStruct((bsz, seq, B_HEADS * B_DIM), jnp.bfloat16),
        scratch_shapes=[pltpu.VMEM((seq, B_DIM), jnp.float32), pltpu.VMEM((seq, B_DIM), jnp.float32)],
        compiler_params=_cparams("parallel", "parallel"),
        name="retention",
    )(pb, pb, pb, pb, dec_pad)


_ROWS_PER_DIR = 6


def _mlstm_kernel(cq_ref, ck_ref, gt_ref, v_ref, og_ref, wq_ref, wk_ref, gb_ref, o_ref,
                  qc_ref, kt_ref, vx_ref, rows_ref, of_ref, ob_ref, *, seq):
    nc = seq // BLOCK
    f32 = jnp.float32
    bf16 = jnp.bfloat16
    rowi = lax.broadcasted_iota(jnp.int32, (BLOCK, LANES), 0)
    coli = lax.broadcasted_iota(jnp.int32, (BLOCK, LANES), 1)
    lane_n = lax.broadcasted_iota(jnp.int32, (nc, LANES), 1)
    tril = rowi >= coli
    triu = rowi <= coli
    tril_b = tril.astype(bf16)
    triu_b = triu.astype(bf16)
    ones_b = jnp.ones((BLOCK, LANES), bf16)
    head = pl.program_id(1)

    def cummax_lanes(x, backward):
        sh = 1
        while sh < LANES:
            if backward:
                x = jnp.maximum(x, jnp.where(lane_n < LANES - sh, pltpu.roll(x, LANES - sh, 1), _NEG_INF))
            else:
                x = jnp.maximum(x, jnp.where(lane_n >= sh, pltpu.roll(x, sh, 1), _NEG_INF))
            sh *= 2
        return x

    for d, (op, last) in enumerate(((triu_b, LANES - 1), (tril_b, 0))):
        lf = _log_sigmoid(gt_ref[2 * d + 1] + gb_ref[2 * d + 1, head]) * _LOG2E
        hi, lo = _split_hi_lo(lf)
        b = _dot(hi, op) + _dot(lo, op)
        a = (gt_ref[2 * d] + gb_ref[2 * d, head]) * _LOG2E - b
        bl = jnp.broadcast_to(b[:, last:last + 1], (nc, LANES))
        ml = jnp.broadcast_to(jnp.max(bl + a, axis=1, keepdims=True), (nc, LANES))
        for r, val in enumerate((a, cummax_lanes(a, d == 1), b, jnp.exp2(bl + a - ml), bl, ml)):
            rows_ref[d * _ROWS_PER_DIR + r] = val

    def conv_silu(src_ref, w, c, t0):
        x = src_ref[pl.ds(t0, BLOCK), :]
        prev = src_ref[pl.ds(jnp.maximum(t0 - 1, 0), 1), :] * jnp.where(c > 0, 1.0, 0.0)
        nxt = src_ref[pl.ds(jnp.minimum(t0 + BLOCK, seq - 1), 1), :] * jnp.where(c < nc - 1, 1.0, 0.0)
        xd = jnp.where(rowi == 0, prev, pltpu.roll(x, 1, 0))
        xu = jnp.where(rowi == BLOCK - 1, nxt, pltpu.roll(x, BLOCK - 1, 0))
        y = xd * w[0:1] + x * w[1:2] + xu * w[2:3]
        return y * jax.nn.sigmoid(y)

    def prep(c, carry):
        t0 = pl.multiple_of(c * BLOCK, BLOCK)
        qc_ref[pl.ds(t0, BLOCK), :] = (conv_silu(cq_ref, wq_ref[...], c, t0) * (C_DIM ** -0.5)).astype(bf16)
        kt_ref[c] = conv_silu(ck_ref, wk_ref[...], c, t0).T.astype(bf16)
        vx_ref[pl.ds(t0, BLOCK), 0:C_DIM] = v_ref[pl.ds(t0, BLOCK), :]
        vx_ref[pl.ds(t0, BLOCK), C_DIM:2 * C_DIM] = ones_b
        return carry

    lax.fori_loop(0, nc, prep, 0, unroll=2)

    def load_rows(ch, backward):
        t0 = pl.multiple_of(ch * BLOCK, BLOCK)
        base = _ROWS_PER_DIR if backward else 0
        a, c, b, w, bl, ml = (rows_ref[base + r, pl.ds(ch, 1), :] for r in range(_ROWS_PER_DIR))
        kt = kt_ref[ch]
        return dict(t0=t0, backward=backward, bl=bl[:, 0:1], m_loc=ml[:, 0:1],
                    a_t=jnp.where(triu if backward else tril, a, _NEG_INF),
                    b_full=jnp.broadcast_to(b, (BLOCK, LANES)).T,
                    c_full=jnp.broadcast_to(c, (BLOCK, LANES)).T,
                    kw=(kt.astype(f32) * w).astype(bf16), kt=kt)

    def first_products(g):
        vx = vx_ref[pl.ds(g["t0"], BLOCK), :]
        g["qk"] = _dot(qc_ref[pl.ds(g["t0"], BLOCK), :], g["kt"])
        g["kv"] = _dot(g["kw"], vx)

    def decayed_scores(g):
        g["s0"] = (g["qk"] * jnp.exp2(g["a_t"] - g["c_full"])).astype(bf16)

    def inner_products(g):
        g["p0"] = _dot(g["s0"], vx_ref[pl.ds(g["t0"], BLOCK), :])

    def with_state(g, state, park_ref, other_ref):
        cx, m = state
        x = _dot(qc_ref[pl.ds(g["t0"], BLOCK), :], cx.astype(bf16))
        n = jnp.maximum(g["c_full"], m)
        e = jnp.exp2(g["c_full"] - n)
        iw = jnp.exp2(m - n)
        num = e * g["p0"][:, 0:C_DIM] + iw * x[:, 0:C_DIM]
        den = e * g["p0"][:, C_DIM:] + iw * x[:, C_DIM:]
        hh = num / jnp.maximum(jnp.abs(den), jnp.exp2(-(g["b_full"] + n)))
        rows = pl.ds(g["t0"], BLOCK)
        if other_ref is None:
            park_ref[rows, :] = hh
        else:
            r = hh + other_ref[rows, :]
            r = r * lax.rsqrt(jnp.mean(r * r, axis=-1, keepdims=True) + EPS)
            o_ref[rows, :] = (jax.nn.sigmoid(og_ref[rows, :].astype(f32)) * r).astype(o_ref.dtype)
        m_new = jnp.maximum(g["bl"] + m, g["m_loc"])
        cw = jnp.exp2(g["bl"] + m - m_new)
        f = jnp.exp2(g["m_loc"] - m_new)
        return cw * cx + f * g["kv"], m_new

    def make_body(second_half):
        def body(it, carry):
            sf, sb = carry
            steps = []
            for u in range(_SCAN_UNROLL):
                c = it * _SCAN_UNROLL + u
                steps.append(load_rows(c, False))
                steps.append(load_rows(nc - 1 - c, True))
            for g in steps:
                first_products(g)
            for g in steps:
                decayed_scores(g)
            for g in steps:
                inner_products(g)
            for g in steps:
                if g["backward"]:
                    sb = with_state(g, sb, ob_ref, of_ref if second_half else None)
                else:
                    sf = with_state(g, sf, of_ref, ob_ref if second_half else None)
            return sf, sb
        return body

    init = (jnp.zeros((C_DIM, 2 * C_DIM), f32), jnp.zeros((1, 1), f32))
    n_bodies = nc // _SCAN_UNROLL
    state = lax.fori_loop(0, n_bodies // 2, make_body(False), (init, init))
    lax.fori_loop(n_bodies // 2, n_bodies, make_body(True), state)


def _mlstm(pb, pf, gates_t, wq_conv, wk_conv, gate_bias):
    bsz, seq, _ = pb.shape
    assert seq % (8 * BLOCK) == 0
    kern = functools.partial(_mlstm_kernel, seq=seq)

    def col(base):
        return pl.BlockSpec((None, seq, LANES), lambda b, h: (b, 0, base + h))

    return pl.pallas_call(
        kern,
        grid=(bsz, C_HEADS),
        in_specs=[col(_PF_CQ), col(_PF_CK),
                  pl.BlockSpec((None, None, 4, seq // BLOCK, LANES), lambda b, h: (b, h, 0, 0, 0)),
                  col(_PB_CV), col(_PB_CO),
                  pl.BlockSpec((3, LANES), lambda b, h: (0, h)),
                  pl.BlockSpec((3, LANES), lambda b, h: (0, h)),
                  pl.BlockSpec(memory_space=pltpu.SMEM)],
        out_specs=pl.BlockSpec((None, seq, LANES), lambda b, h: (b, 0, h)),
        out_shape=jax.ShapeDtypeStruct((bsz, seq, C_HEADS * C_DIM), jnp.bfloat16),
        scratch_shapes=[pltpu.VMEM((seq, C_DIM), jnp.bfloat16),
                        pltpu.VMEM((seq // BLOCK, C_DIM, BLOCK), jnp.bfloat16),
                        pltpu.VMEM((seq, 2 * C_DIM), jnp.bfloat16),
                        pltpu.VMEM((2 * _ROWS_PER_DIR, seq // BLOCK, LANES), jnp.float32),
                        pltpu.VMEM((seq, C_DIM), jnp.float32), pltpu.VMEM((seq, C_DIM), jnp.float32)],
        compiler_params=_cparams("parallel", "parallel"),
        name="mlstm",
    )(pf, pf, gates_t, pb, pb, wq_conv, wk_conv, gate_bias)


def _window_attn_kernel(slopes_ref, sink_ref, q_ref, k_ref, v_ref, o_ref, kx_ref, vx_ref, *, seq):
    qi = pl.program_id(1)
    f32 = jnp.float32
    span = BLOCK + 2 * WINDOW
    group = D_HEADS // D_KV_HEADS

    @pl.when(qi == 0)
    def _():
        lane = lax.broadcasted_iota(jnp.int32, (seq, LANES), 1)
        ones = jnp.ones((seq, LANES), vx_ref.dtype)
        for src, dst in ((k_ref, kx_ref), (v_ref, vx_ref)):
            x = src[...].astype(f32)
            x0 = jnp.where(lane < D_DIM, x, 0.0)
            x1 = jnp.where(lane >= D_DIM, x, 0.0)
            for i, xi in enumerate((x0, pltpu.roll(x0, D_DIM, 1), pltpu.roll(x1, D_DIM, 1), x1)):
                dst[i, :, 0:LANES] = xi.astype(dst.dtype)
        for i in range(4):
            vx_ref[i, :, LANES:2 * LANES] = ones

    t0 = qi * BLOCK
    ws = pl.multiple_of(jnp.clip(t0 - WINDOW, 0, seq - span), BLOCK)
    rel = (lax.broadcasted_iota(jnp.int32, (BLOCK, span), 0)
           - lax.broadcasted_iota(jnp.int32, (BLOCK, span), 1)) + (t0 - ws)
    dist = jnp.abs(rel)
    neg_dist = jnp.where(dist <= WINDOW, dist.astype(f32) * (-_LOG2E), _NEG_INF)

    n_pairs = D_HEADS // 2
    lookahead = 2
    scores = {}

    def score_pair(pair):
        kv = (2 * pair) // group
        qp = (q_ref[:, pair * LANES:(pair + 1) * LANES].astype(f32) * (D_DIM ** -0.5 * _LOG2E)).astype(q_ref.dtype)
        for half in range(2):
            scores[2 * pair + half] = _dot_nt(qp, kx_ref[2 * kv + half, pl.ds(ws, span), :])

    def attend(hd, kv_slot):
        s = scores[hd] + slopes_ref[hd] * neg_dist
        snk = sink_ref[hd] * _LOG2E
        m = jnp.maximum(jnp.max(s, axis=1, keepdims=True), snk)
        o = _dot(jnp.exp2(s - m).astype(jnp.bfloat16), vx_ref[kv_slot, pl.ds(ws, span), :])
        return o[:, 0:LANES] / (o[:, LANES:] + jnp.exp2(snk - m))

    for pair in range(lookahead):
        score_pair(pair)
    for pair in range(n_pairs):
        kv = (2 * pair) // group
        out = attend(2 * pair, 2 * kv) + attend(2 * pair + 1, 2 * kv + 1)
        o_ref[:, pair * LANES:(pair + 1) * LANES] = out.astype(o_ref.dtype)
        if pair + lookahead < n_pairs:
            score_pair(pair + lookahead)


def _window_attention(pb, attn_sink, slopes_d):
    bsz, seq, _ = pb.shape
    kern = functools.partial(_window_attn_kernel, seq=seq)
    width = D_HEADS * D_DIM
    return pl.pallas_call(
        kern,
        grid=(bsz, seq // BLOCK),
        in_specs=[
            pl.BlockSpec(memory_space=pltpu.SMEM),
            pl.BlockSpec(memory_space=pltpu.SMEM),
            pl.BlockSpec((None, BLOCK, width), lambda b, i: (b, i, _PB_DQ512)),
            pl.BlockSpec((None, seq, LANES), lambda b, i: (b, 0, _PB_DK)),
            pl.BlockSpec((None, seq, LANES), lambda b, i: (b, 0, _PB_DV)),
        ],
        out_specs=pl.BlockSpec((None, BLOCK, width), lambda b, i: (b, i, 0)),
        out_shape=jax.ShapeDtypeStruct((bsz, seq, width), jnp.bfloat16),
        scratch_shapes=[pltpu.VMEM((4, seq, LANES), jnp.bfloat16), pltpu.VMEM((4, seq, 2 * LANES), jnp.bfloat16)],
        compiler_params=_cparams("parallel", "arbitrary"),
        name="window_attention",
    )(slopes_d, attn_sink, pb, pb, pb)


def _out_proj_kernel(x_ref, ya_ref, yb_ref, yc_ref, yd_ref, w_ref, o_ref):
    acc = x_ref[...]
    for g, y_ref in enumerate((ya_ref, yb_ref, yc_ref, yd_ref)):
        acc = acc + _dot(y_ref[...], w_ref[g * MIXER_WIDTH:(g + 1) * MIXER_WIDTH, :])
    o_ref[...] = acc


def _out_proj(x2d, ys, w_out, tm):
    t, d = x2d.shape
    yspec = pl.BlockSpec((tm, MIXER_WIDTH), lambda i: (i, 0))
    return pl.pallas_call(
        _out_proj_kernel,
        grid=(t // tm,),
        in_specs=[pl.BlockSpec((tm, d), lambda i: (i, 0)), yspec, yspec, yspec, yspec,
                  pl.BlockSpec((d, d), lambda i: (0, 0), pipeline_mode=pl.Buffered(1))],
        out_specs=pl.BlockSpec((tm, d), lambda i: (i, 0)),
        out_shape=jax.ShapeDtypeStruct((t, d), jnp.float32),
        compiler_params=_cparams("parallel", vmem_limit=_BIG_VMEM_LIMIT),
        name="out_proj",
    )(x2d, *ys, w_out)


def _ffn_kernel(x_ref, xp_ref, xn_ref, nw_ref, wg_ref, wu_ref, cw_ref, wd_ref, fw_ref, o_ref,
                h_ref, *, seq, tm, final_norm):
    i = pl.program_id(0)
    j = pl.program_id(1)
    f32 = jnp.float32

    def norm(x):
        return x * lax.rsqrt(jnp.mean(x * x, axis=-1, keepdims=True) + EPS) * nw_ref[...]

    def d_ff_step(first):
        gate = _dot(h_ref[...], wg_ref[...])
        gm = gate[0:tm]
        rowi = lax.broadcasted_iota(jnp.int32, gm.shape, 0)
        g_dn = jnp.where(rowi == 0, gate[tm + 7:tm + 8], pltpu.roll(gm, 1, 0))
        g_up = jnp.where(rowi == tm - 1, gate[tm + 8:tm + 9], pltpu.roll(gm, tm - 1, 0))
        cw = cw_ref[...]
        g = g_dn * cw[0:1] + gm * cw[1:2] + g_up * cw[2:3]
        up = _dot(h_ref[0:tm, :], wu_ref[...])
        act = (g * jax.nn.sigmoid(g) * up).astype(wd_ref.dtype)
        down = _dot(act, wd_ref[...])
        if first:
            o_ref[...] = down
        else:
            o_ref[...] += down

    @pl.when(j == 0)
    def _():
        h_ref[0:tm, :] = norm(x_ref[...]).astype(h_ref.dtype)
        has_prev = ((i * tm) % seq != 0).astype(f32)
        has_next = (((i + 1) * tm) % seq != 0).astype(f32)
        halo = jnp.concatenate([norm(xp_ref[...]) * has_prev, norm(xn_ref[...]) * has_next], axis=0)
        h_ref[tm:tm + 16, :] = halo.astype(h_ref.dtype)
        d_ff_step(True)

    @pl.when(j > 0)
    def _():
        d_ff_step(False)

    @pl.when(j == pl.num_programs(1) - 1)
    def _():
        y = x_ref[...] + o_ref[...]
        if final_norm:
            y = y * lax.rsqrt(jnp.mean(y * y, axis=-1, keepdims=True) + EPS) * fw_ref[...]
        o_ref[...] = y


def _ffn(x2d, seq, nw, wg, wu, cw, wd, fw, final_norm, tm, tn):
    t, d = x2d.shape
    nff = wg.shape[1]
    rb = tm // 8
    last8 = t // 8 - 1
    kern = functools.partial(_ffn_kernel, seq=seq, tm=tm, final_norm=final_norm)
    return pl.pallas_call(
        kern,
        grid=(t // tm, nff // tn),
        in_specs=[
            pl.BlockSpec((tm, d), lambda i, j: (i, 0)),
            pl.BlockSpec((8, d), lambda i, j: (jnp.maximum(i * rb - 1, 0), 0)),
            pl.BlockSpec((8, d), lambda i, j: (jnp.minimum((i + 1) * rb, last8), 0)),
            pl.BlockSpec((1, d), lambda i, j: (0, 0)),
            pl.BlockSpec((d, tn), lambda i, j: (0, j)),
            pl.BlockSpec((d, tn), lambda i, j: (0, j)),
            pl.BlockSpec((3, tn), lambda i, j: (0, j)),
            pl.BlockSpec((tn, d), lambda i, j: (j, 0)),
            pl.BlockSpec((1, d), lambda i, j: (0, 0)),
        ],
        out_specs=pl.BlockSpec((tm, d), lambda i, j: (i, 0)),
        out_shape=jax.ShapeDtypeStruct((t, d), jnp.float32),
        scratch_shapes=[pltpu.VMEM((tm + 16, d), jnp.bfloat16)],
        compiler_params=_cparams("parallel", "arbitrary", vmem_limit=_BIG_VMEM_LIMIT),
        name="ffn",
    )(x2d, x2d, x2d, nw, wg, wu, cw, wd, fw)


def _prepare_layer(l, norm1_w, w_in, qk_conv_w, diff_lambda, ret_decay_logit, mlstm_gate_bias, attn_sink,
                   w_out, norm2_w, w_gate, w_up, ffn_conv_w, w_down):
    bf16 = jnp.bfloat16
    f32 = jnp.float32
    wi = w_in[l]
    w_pb = jnp.concatenate([wi[:, _OFF_A:_OFF_CQ], wi[:, _OFF_CV:_OFF_GATES], wi[:, _OFF_D:N_IN]], axis=1)
    wgates = jnp.pad(wi[:, _OFF_GATES:_OFF_D], ((0, 0), (0, LANES - N_GATES)))
    w_pf = jnp.concatenate([wi[:, _OFF_CQ:_OFF_CV], wgates], axis=1)
    gbias = mlstm_gate_bias[l].astype(f32)
    dec = jnp.zeros((8, LANES), f32).at[0:2, 0:B_HEADS].set(ret_decay_logit[l].astype(f32))
    return dict(
        norm1=norm1_w[l].reshape(1, D_MODEL).astype(f32),
        w_pb=w_pb.astype(bf16), w_pf=w_pf.astype(bf16),
        wq_conv=qk_conv_w[l][:, :C_HEADS * C_DIM].astype(f32),
        wk_conv=qk_conv_w[l][:, C_HEADS * C_DIM:].astype(f32),
        gbias=gbias, dec=dec,
        diff_lambda=diff_lambda[l].astype(f32), sink=attn_sink[l].astype(f32),
        w_out=w_out[l].astype(bf16),
        norm2=norm2_w[l].reshape(1, D_MODEL).astype(f32),
        w_gate=w_gate[l].astype(bf16), w_up=w_up[l].astype(bf16),
        ffn_conv=ffn_conv_w[l].astype(f32), w_down=w_down[l].astype(bf16),
    )


def _tile(n, pref):
    t = min(pref, n)
    while n % t:
        t //= 2
    return t


def _layer(x2d, bsz, seq, layer_idx, p, slopes, final_w, final_norm):
    t = bsz * seq
    pb, pf = _in_proj(x2d, p["norm1"], p["w_pb"], p["w_pf"], _tile(t, _TOKEN_TILE), _IN_PROJ_TN)
    pb = pb.reshape(bsz, seq, PB_WIDTH)
    pf = pf.reshape(bsz, seq, PF_WIDTH)
    gates_t = pf[:, :, _PF_GATES * LANES:_PF_GATES * LANES + N_GATES].reshape(bsz, seq, 4, C_HEADS)
    gates_t = jnp.transpose(gates_t, (0, 3, 2, 1)).reshape(bsz, C_HEADS, 4, seq // BLOCK, BLOCK)
    y_a = _diff_attention(pb, p["diff_lambda"], slopes[D_HEADS:], layer_idx)
    y_b = _retention(pb, p["dec"])
    y_c = _mlstm(pb, pf, gates_t, p["wq_conv"], p["wk_conv"], p["gbias"])
    y_d = _window_attention(pb, p["sink"], slopes[:D_HEADS])
    ys = [y.reshape(t, MIXER_WIDTH) for y in (y_a, y_b, y_c, y_d)]
    x1 = _out_proj(x2d, ys, p["w_out"], _tile(t, _TOKEN_TILE))
    return _ffn(x1, seq, p["norm2"], p["w_gate"], p["w_up"], p["ffn_conv"], p["w_down"], final_w,
                final_norm, _tile(seq, _TOKEN_TILE), _FFN_TN)


def _trunk(x, layers, slopes, final_w):
    bsz, seq, d = x.shape
    x2d = x.reshape(bsz * seq, d)
    for l, p in enumerate(layers):
        x2d = _layer(x2d, bsz, seq, l, p, slopes, final_w, l == len(layers) - 1)
    return x2d.reshape(bsz, seq, d)


def kernel(x_prompt, x_sample, norm1_w, w_in, qk_conv_w, diff_lambda, ret_decay_logit, mlstm_gate_bias,
           attn_sink, w_out, norm2_w, w_gate, w_up, ffn_conv_w, w_down, final_norm_w):
    layers = [_prepare_layer(l, norm1_w, w_in, qk_conv_w, diff_lambda, ret_decay_logit, mlstm_gate_bias,
                             attn_sink, w_out, norm2_w, w_gate, w_up, ffn_conv_w, w_down)
              for l in range(DEPTH)]
    n = A_HEADS + D_HEADS
    slopes = 2.0 ** (-8.0 * jnp.arange(1, n + 1, dtype=jnp.float32) / n)
    final_w = final_norm_w.reshape(1, D_MODEL).astype(jnp.float32)
    return (_trunk(x_prompt, layers, slopes, final_w), _trunk(x_sample, layers, slopes, final_w))
```

```python
import functools
import math

import jax
import jax.numpy as jnp
from jax import lax
from jax.experimental import pallas as pl
from jax.experimental.pallas import tpu as pltpu

D_MODEL = 2048
DEPTH = 2
A_HEADS = 4
A_QK_DIM = 64
A_V_DIM = 128
B_HEADS = 4
B_DIM = 128
C_HEADS = 4
C_DIM = 128
D_HEADS = 8
D_KV_HEADS = 2
D_DIM = 64
WINDOW = 128
BLOCK = 128
D_FF = 5632
EPS = 1e-6
LANES = 128

_OFF_A = 0
_OFF_B = 1536
_OFF_CQ = 3584
_OFF_CV = 4608
_OFF_GATES = 5632
_OFF_D = 5648
N_IN = 6416

_PB_AQ, _PB_AK, _PB_AV = 0, 4, 8
_PB_BQ, _PB_BK, _PB_BV, _PB_BG = 12, 16, 20, 24
_PB_CV, _PB_CO = 28, 32
_PB_DQ512 = 9
_PB_DK, _PB_DV = 40, 41
PB_WIDTH = 42 * LANES
_IN_PROJ_TN = 1792
_PF_CQ, _PF_CK, _PF_GATES = 0, 4, 8
PF_WIDTH = 9 * LANES
N_GATES = 16

MIXER_WIDTH = 512
_TOKEN_TILE = 1024
_FFN_TN = 512
_VMEM_LIMIT = 56 * 1024 * 1024
_BIG_VMEM_LIMIT = 61 * 1024 * 1024
_SCAN_UNROLL = 4
_RETENTION_UNROLL = 8
_NEG_INF = float("-inf")


def _cparams(*sem, vmem_limit=_VMEM_LIMIT):
    return pltpu.CompilerParams(dimension_semantics=sem, vmem_limit_bytes=vmem_limit)


def _dot(a, b):
    return jnp.dot(a, b, preferred_element_type=jnp.float32)


def _dot_nt(a, b):
    return lax.dot_general(a, b, (((1,), (1,)), ((), ())), preferred_element_type=jnp.float32)


def _dot_tn(a, b):
    return lax.dot_general(a, b, (((0,), (0,)), ((), ())), preferred_element_type=jnp.float32)


def _log_sigmoid(x):
    return -(jnp.maximum(-x, 0.0) + jnp.log1p(jnp.exp(-jnp.abs(x))))


def _split_hi_lo(x):
    hi = x.astype(jnp.bfloat16)
    lo = (x - hi.astype(jnp.float32)).astype(jnp.bfloat16)
    return hi, lo


def _in_proj_kernel(x_ref, nw_ref, wb_ref, wf_ref, ob_ref, of_ref, h_ref, *, nb):
    j = pl.program_id(1)

    @pl.when(j == 0)
    def _():
        x = x_ref[...]
        ms = jnp.mean(x * x, axis=-1, keepdims=True)
        h = (x * lax.rsqrt(ms + EPS) * nw_ref[...]).astype(h_ref.dtype)
        h_ref[...] = h
        ob_ref[...] = _dot(h, wb_ref[...]).astype(ob_ref.dtype)

    @pl.when((j > 0) & (j < nb))
    def _():
        ob_ref[...] = _dot(h_ref[...], wb_ref[...]).astype(ob_ref.dtype)

    @pl.when(j == nb)
    def _():
        of_ref[...] = _dot(h_ref[...], wf_ref[...])


def _in_proj(x2d, nw, w_pb, w_pf, tm, tn):
    t, d = x2d.shape
    nb = PB_WIDTH // tn
    return pl.pallas_call(
        functools.partial(_in_proj_kernel, nb=nb),
        grid=(t // tm, nb + 1),
        in_specs=[
            pl.BlockSpec((tm, d), lambda i, j: (i, 0)),
            pl.BlockSpec((1, d), lambda i, j: (0, 0)),
            pl.BlockSpec((d, tn), lambda i, j: (0, jnp.minimum(j, nb - 1))),
            pl.BlockSpec((d, PF_WIDTH), lambda i, j: (0, 0), pipeline_mode=pl.Buffered(1)),
        ],
        out_specs=[pl.BlockSpec((tm, tn), lambda i, j: (i, jnp.minimum(j, nb - 1))),
                   pl.BlockSpec((tm, PF_WIDTH), lambda i, j: (i, 0))],
        out_shape=[jax.ShapeDtypeStruct((t, PB_WIDTH), jnp.bfloat16),
                   jax.ShapeDtypeStruct((t, PF_WIDTH), jnp.float32)],
        scratch_shapes=[pltpu.VMEM((tm, d), jnp.bfloat16)],
        compiler_params=_cparams("parallel", "arbitrary", vmem_limit=_BIG_VMEM_LIMIT),
        name="norm_in_proj",
    )(x2d, nw, w_pb, w_pf)


_LOG2E = 1.4426950408889634
_N_BIAS_LANES = 3
_V_ROWS = A_V_DIM + 16
_ATTN_MAX_TILES_PER_BODY = 8
_ATTN_TILE = 512


def _diff_attn_kernel(slopes_ref, q_ref, qn_ref, k_ref, v_ref, dl_ref, o_ref,
                      kx_ref, vt_ref, qt_ref, m_ref, acc_ref, s_ref, cmax_ref, bias_ref, *, seq, tq, tk,
                      tpb, lam_init):
    head = pl.program_id(1)
    qi = pl.program_id(2)
    f32 = jnp.float32
    bf16 = jnp.bfloat16
    sigma = slopes_ref[head] * _LOG2E
    nk = seq // tk
    prep_rows = tk

    @pl.when(qi == 0)
    def _():
        lane = lax.broadcasted_iota(jnp.int32, (prep_rows, LANES), 1)
        row = lax.broadcasted_iota(jnp.int32, (prep_rows, LANES), 0)
        ones_row = jnp.where(lax.broadcasted_iota(jnp.int32, (_V_ROWS - A_V_DIM, tk), 0) == 0, 1.0, 0.0)

        def prep(c, carry):
            r0 = pl.multiple_of(c * prep_rows, prep_rows)
            kk = k_ref[pl.ds(r0, prep_rows), :].astype(f32)
            col = sigma * (row + r0).astype(f32)
            c1 = col.astype(bf16).astype(f32)
            c2 = (col - c1).astype(bf16).astype(f32)
            c3 = col - c1 - c2
            bias = jnp.where(lane == A_QK_DIM, c1, jnp.where(lane == A_QK_DIM + 1, c2,
                             jnp.where(lane == A_QK_DIM + 2, c3, 0.0)))
            kx_ref[0, pl.ds(r0, prep_rows), :] = jnp.where(lane < A_QK_DIM, kk, bias).astype(bf16)
            kx_ref[1, pl.ds(r0, prep_rows), :] = jnp.where(
                lane < A_QK_DIM, pltpu.roll(kk, A_QK_DIM, 1), bias).astype(bf16)
            vt_ref[c, 0:A_V_DIM, :] = v_ref[pl.ds(r0, prep_rows), :].astype(f32).T.astype(bf16)
            vt_ref[c, A_V_DIM:_V_ROWS, :] = ones_row.astype(bf16)
            return carry

        lax.fori_loop(0, nk, prep, 0)
        rel = (lax.broadcasted_iota(jnp.int32, (tk, tq), 1)
               - lax.broadcasted_iota(jnp.int32, (tk, tq), 0)).astype(f32)
        bias_ref[...] = sigma * jnp.abs(rel)

    qrow = lax.broadcasted_iota(jnp.int32, (LANES, tq), 0)
    bias_rows = lax.broadcasted_iota(jnp.int32, (16, tq), 0) < _N_BIAS_LANES

    def build_q_variants(src_ref, base_slot):
        qt = (src_ref[...].astype(f32) * (A_QK_DIM ** -0.5 * _LOG2E)).T
        for mp in range(2):
            qm = qt if mp == 0 else pltpu.roll(qt, A_QK_DIM, 0)
            base = jnp.where(qrow < A_QK_DIM, qm, 0.0).astype(bf16)
            for var, sign in enumerate((1.0, 0.0, -1.0)):
                qt_ref[base_slot + 3 * mp + var] = base
                qt_ref[base_slot + 3 * mp + var, A_QK_DIM:A_QK_DIM + 16, :] = (
                    jnp.where(bias_rows, sign, 0.0).astype(bf16))

    def diagonal_scores(tile, base_slot, mp):
        k_start = pl.multiple_of(tile * tk, tk)
        s = _dot(kx_ref[mp, pl.ds(k_start, tk), :], qt_ref[base_slot + 3 * mp + 1]) - bias_ref[...]
        s_ref[0, mp] = s
        cmax_ref[0, mp] = jnp.max(s, axis=0, keepdims=True)

    qslot = (qi % 2) * 6
    qslot_next = 6 - qslot

    @pl.when(qi == 0)
    def _():
        build_q_variants(q_ref, 0)
        for mp in range(2):
            diagonal_scores(0, 0, mp)

    q0 = qi * tq
    rho = sigma * (lax.broadcasted_iota(jnp.int32, (1, tq), 1) + q0).astype(f32)
    m_ref[...] = jnp.full(m_ref.shape, _NEG_INF, f32)
    acc_ref[...] = jnp.zeros(acc_ref.shape, f32)

    kd = q0 // tk

    def key_tile(j):
        jm = jnp.minimum(j, nk - 1) - 1
        return jnp.where(j == 0, kd, jnp.where(jm < kd, jm, jm + 1))

    def scores(j, slot, mp):
        kb = key_tile(j)
        var = jnp.where(kb < kd, 0, jnp.where(kb > kd, 2, 1))
        k_start = pl.multiple_of(kb * tk, tk)
        s = _dot(kx_ref[mp, pl.ds(k_start, tk), :], qt_ref[qslot + 3 * mp + var])
        s_ref[slot, mp] = s
        cmax_ref[slot, mp] = jnp.max(s, axis=0, keepdims=True)

    def softmax_pv(j, slot, mp):
        kb = key_tile(j)
        shift = jnp.where(kb < kd, -1.0, jnp.where(kb > kd, 1.0, 0.0)) * rho
        m_old = m_ref[mp]
        m_new = jnp.maximum(m_old, cmax_ref[slot, mp] + shift)
        p = jnp.exp2(s_ref[slot, mp] - (m_new - shift))
        acc_ref[mp] = jnp.exp2(m_old - m_new) * acc_ref[mp] + _dot(vt_ref[kb], p.astype(bf16))
        m_ref[mp] = m_new

    def body(i, carry):
        for u in range(tpb):
            j = tpb * i + u
            for mp in range(2):
                scores(j + 1, (u + 1) % tpb, mp)
                softmax_pv(j, u, mp)
        return carry

    n_bodies = nk // tpb
    lax.fori_loop(0, n_bodies - 1, body, 0)

    build_q_variants(qn_ref, qslot_next)
    kd_next = jnp.minimum(qi + 1, seq // tq - 1)
    for u in range(tpb):
        j = tpb * (n_bodies - 1) + u
        for mp in range(2):
            if u + 1 < tpb:
                scores(j + 1, u + 1, mp)
            else:
                diagonal_scores(kd_next, qslot_next, mp)
            softmax_pv(j, u, mp)

    dl = dl_ref[...]
    lam = (jnp.exp(jnp.sum(dl[0:1] * dl[1:2], axis=1, keepdims=True))
           - jnp.exp(jnp.sum(dl[2:3] * dl[3:4], axis=1, keepdims=True)) + lam_init)
    a0 = acc_ref[0]
    a1 = acc_ref[1]
    ot = (a0[0:A_V_DIM] / a0[A_V_DIM:A_V_DIM + 1] - lam * (a1[0:A_V_DIM] / a1[A_V_DIM:A_V_DIM + 1]))
    ot = ot * lax.rsqrt(jnp.mean(ot * ot, axis=0, keepdims=True) + EPS)
    o_ref[...] = (ot * (1.0 - lam_init)).T.astype(o_ref.dtype)


def _diff_attention(pb, diff_lambda, slopes_a, layer_idx):
    bsz, seq, _ = pb.shape
    tq = tk = _tile(seq, _ATTN_TILE)
    tpb = _tile(seq // (2 * tk), _ATTN_MAX_TILES_PER_BODY)
    lam_init = 0.8 - 0.6 * math.exp(-0.3 * layer_idx)
    kern = functools.partial(_diff_attn_kernel, seq=seq, tq=tq, tk=tk, tpb=tpb, lam_init=lam_init)
    return pl.pallas_call(
        kern,
        grid=(bsz, A_HEADS, seq // tq),
        in_specs=[
            pl.BlockSpec(memory_space=pltpu.SMEM),
            pl.BlockSpec((None, tq, LANES), lambda b, h, i: (b, i, _PB_AQ + h)),
            pl.BlockSpec((None, tq, LANES), lambda b, h, i: (b, jnp.minimum(i + 1, seq // tq - 1), _PB_AQ + h)),
            pl.BlockSpec((None, seq, LANES), lambda b, h, i: (b, 0, _PB_AK + h)),
            pl.BlockSpec((None, seq, LANES), lambda b, h, i: (b, 0, _PB_AV + h)),
            pl.BlockSpec((4, A_QK_DIM), lambda b, h, i: (0, 0)),
        ],
        out_specs=pl.BlockSpec((None, tq, LANES), lambda b, h, i: (b, i, h)),
        out_shape=jax.ShapeDtypeStruct((bsz, seq, A_HEADS * A_V_DIM), jnp.bfloat16),
        scratch_shapes=[
            pltpu.VMEM((2, seq, LANES), jnp.bfloat16),
            pltpu.VMEM((seq // tk, _V_ROWS, tk), jnp.bfloat16),
            pltpu.VMEM((12, LANES, tq), jnp.bfloat16),
            pltpu.VMEM((2, 1, tq), jnp.float32),
            pltpu.VMEM((2, _V_ROWS, tq), jnp.float32),
            pltpu.VMEM((tpb, 2, tk, tq), jnp.float32),
            pltpu.VMEM((tpb, 2, 1, tq), jnp.float32),
            pltpu.VMEM((tk, tq), jnp.float32),
        ],
        compiler_params=_cparams("parallel", "parallel", "arbitrary"),
        name="diff_attention",
    )(slopes_a, pb, pb, pb, pb, diff_lambda)


def _retention_kernel(q_ref, k_ref, v_ref, g_ref, dec_ref, o_ref, of_ref, ob_ref, *, seq):
    head = pl.program_id(1)
    nc = seq // BLOCK
    f32 = jnp.float32

    lgv = _log_sigmoid(dec_ref[...])
    rr = lax.broadcasted_iota(jnp.int32, lgv.shape, 0)
    cc = lax.broadcasted_iota(jnp.int32, lgv.shape, 1)

    def pick(row):
        sel = jnp.where((rr == row) & (cc == head), lgv, 0.0)
        return jnp.sum(jnp.sum(sel, axis=1, keepdims=True), axis=0, keepdims=True)

    lg_f, lg_b = pick(0), pick(1)
    ri = lax.broadcasted_iota(jnp.int32, (BLOCK, BLOCK), 0).astype(f32)
    ci = lax.broadcasted_iota(jnp.int32, (BLOCK, BLOCK), 1).astype(f32)
    diff = ri - ci
    decay_f = jnp.where(diff >= 0, jnp.exp(lg_f * jnp.maximum(diff, 0.0)), 0.0)
    decay_b = jnp.where(diff <= 0, jnp.exp(lg_b * jnp.maximum(-diff, 0.0)), 0.0)
    idx = lax.broadcasted_iota(jnp.int32, (BLOCK, 1), 0).astype(f32)
    qdec_f = jnp.exp(lg_f * (idx + 1.0))
    kdec_f = jnp.exp(lg_f * (BLOCK - 1.0 - idx))
    qdec_b = jnp.exp(lg_b * (BLOCK - idx))
    kdec_b = jnp.exp(lg_b * idx)
    cdec_f = jnp.exp(lg_f * BLOCK)
    cdec_b = jnp.exp(lg_b * BLOCK)
    kscale = B_DIM ** -0.5

    def scores_and_kv(t0, backward):
        qb = q_ref[pl.ds(t0, BLOCK), :]
        kb = k_ref[pl.ds(t0, BLOCK), :].astype(f32) * kscale
        vb = v_ref[pl.ds(t0, BLOCK), :]
        kdec = kdec_b if backward else kdec_f
        return dict(t0=t0, backward=backward, qb=qb, vb=vb,
                    qk=_dot_nt(qb, kb.astype(qb.dtype)), kv=_dot_tn((kb * kdec).astype(vb.dtype), vb))

    def inner_product(g):
        s = g["qk"] * (decay_b if g["backward"] else decay_f)
        g["inner"] = _dot(s.astype(g["vb"].dtype), g["vb"])

    def with_state(g, state, park_ref, other_ref):
        qdec, cdec = (qdec_b, cdec_b) if g["backward"] else (qdec_f, cdec_f)
        cross = _dot(g["qb"], state.astype(g["qb"].dtype)) * qdec
        rows = pl.ds(g["t0"], BLOCK)
        if other_ref is None:
            park_ref[rows, :] = g["inner"] + cross
        else:
            r = g["inner"] + cross + other_ref[rows, :]
            r = r * lax.rsqrt(jnp.mean(r * r, axis=-1, keepdims=True) + EPS)
            gate = g_ref[rows, :].astype(f32)
            o_ref[rows, :] = (gate * jax.nn.sigmoid(gate) * r).astype(o_ref.dtype)
        return state * cdec + g["kv"]

    def make_body(second_half):
        def body(it, carry):
            rf, rb = carry
            steps = []
            for u in range(_RETENTION_UNROLL):
                c = it * _RETENTION_UNROLL + u
                steps.append(scores_and_kv(pl.multiple_of(c * BLOCK, BLOCK), False))
                steps.append(scores_and_kv(pl.multiple_of((nc - 1 - c) * BLOCK, BLOCK), True))
            for g in steps:
                inner_product(g)
            for g in steps:
                if g["backward"]:
                    rb = with_state(g, rb, ob_ref, of_ref if second_half else None)
                else:
                    rf = with_state(g, rf, of_ref, ob_ref if second_half else None)
            return rf, rb
        return body

    zero = jnp.zeros((B_DIM, B_DIM), f32)
    n_bodies = nc // _RETENTION_UNROLL
    state = lax.fori_loop(0, n_bodies // 2, make_body(False), (zero, zero))
    lax.fori_loop(n_bodies // 2, n_bodies, make_body(True), state)


def _retention(pb, dec_pad):
    bsz, seq, _ = pb.shape
    assert seq % (2 * _RETENTION_UNROLL * BLOCK) == 0
    kern = functools.partial(_retention_kernel, seq=seq)

    def col(base):
        return pl.BlockSpec((None, seq, LANES), lambda b, h: (b, 0, base + h))

    return pl.pallas_call(
        kern,
        grid=(bsz, B_HEADS),
        in_specs=[col(_PB_BQ), col(_PB_BK), col(_PB_BV), col(_PB_BG),
                  pl.BlockSpec((8, LANES), lambda b, h: (0, 0))],
        out_specs=pl.BlockSpec((None, seq, LANES), lambda b, h: (b, 0, h)),
        out_shape=jax.ShapeDtypeStruct((bsz, seq, B_HEADS * B_DIM), jnp.bfloat16),
        scratch_shapes=[pltpu.VMEM((seq, B_DIM), jnp.float32), pltpu.VMEM((seq, B_DIM), jnp.float32)],
        compiler_params=_cparams("parallel", "parallel"),
        name="retention",
    )(pb, pb, pb, pb, dec_pad)


_ROWS_PER_DIR = 6


def _mlstm_kernel(cq_ref, ck_ref, gt_ref, v_ref, og_ref, wq_ref, wk_ref, gb_ref, o_ref,
                  qc_ref, kt_ref, vx_ref, rows_ref, of_ref, ob_ref, *, seq):
    nc = seq // BLOCK
    f32 = jnp.float32
    bf16 = jnp.bfloat16
    rowi = lax.broadcasted_iota(jnp.int32, (BLOCK, LANES), 0)
    coli = lax.broadcasted_iota(jnp.int32, (BLOCK, LANES), 1)
    lane_n = lax.broadcasted_iota(jnp.int32, (nc, LANES), 1)
    tril = rowi >= coli
    triu = rowi <= coli
    tril_b = tril.astype(bf16)
    triu_b = triu.astype(bf16)
    ones_b = jnp.ones((BLOCK, LANES), bf16)
    head = pl.program_id(1)

    def cummax_lanes(x, backward):
        sh = 1
        while sh < LANES:
            if backward:
                x = jnp.maximum(x, jnp.where(lane_n < LANES - sh, pltpu.roll(x, LANES - sh, 1), _NEG_INF))
            else:
                x = jnp.maximum(x, jnp.where(lane_n >= sh, pltpu.roll(x, sh, 1), _NEG_INF))
            sh *= 2
        return x

    for d, (op, last) in enumerate(((triu_b, LANES - 1), (tril_b, 0))):
        lf = _log_sigmoid(gt_ref[2 * d + 1] + gb_ref[2 * d + 1, head]) * _LOG2E
        hi, lo = _split_hi_lo(lf)
        b = _dot(hi, op) + _dot(lo, op)
        a = (gt_ref[2 * d] + gb_ref[2 * d, head]) * _LOG2E - b
        bl = jnp.broadcast_to(b[:, last:last + 1], (nc, LANES))
        ml = jnp.broadcast_to(jnp.max(bl + a, axis=1, keepdims=True), (nc, LANES))
        for r, val in enumerate((a, cummax_lanes(a, d == 1), b, jnp.exp2(bl + a - ml), bl, ml)):
            rows_ref[d * _ROWS_PER_DIR + r] = val

    def conv_silu(src_ref, w, c, t0):
        x = src_ref[pl.ds(t0, BLOCK), :]
        prev = src_ref[pl.ds(jnp.maximum(t0 - 1, 0), 1), :] * jnp.where(c > 0, 1.0, 0.0)
        nxt = src_ref[pl.ds(jnp.minimum(t0 + BLOCK, seq - 1), 1), :] * jnp.where(c < nc - 1, 1.0, 0.0)
        xd = jnp.where(rowi == 0, prev, pltpu.roll(x, 1, 0))
        xu = jnp.where(rowi == BLOCK - 1, nxt, pltpu.roll(x, BLOCK - 1, 0))
        y = xd * w[0:1] + x * w[1:2] + xu * w[2:3]
        return y * jax.nn.sigmoid(y)

    def prep(c, carry):
        t0 = pl.multiple_of(c * BLOCK, BLOCK)
        qc_ref[pl.ds(t0, BLOCK), :] = (conv_silu(cq_ref, wq_ref[...], c, t0) * (C_DIM ** -0.5)).astype(bf16)
        kt_ref[c] = conv_silu(ck_ref, wk_ref[...], c, t0).T.astype(bf16)
        vx_ref[pl.ds(t0, BLOCK), 0:C_DIM] = v_ref[pl.ds(t0, BLOCK), :]
        vx_ref[pl.ds(t0, BLOCK), C_DIM:2 * C_DIM] = ones_b
        return carry

    lax.fori_loop(0, nc, prep, 0, unroll=2)

    def load_rows(ch, backward):
        t0 = pl.multiple_of(ch * BLOCK, BLOCK)
        base = _ROWS_PER_DIR if backward else 0
        a, c, b, w, bl, ml = (rows_ref[base + r, pl.ds(ch, 1), :] for r in range(_ROWS_PER_DIR))
        kt = kt_ref[ch]
        return dict(t0=t0, backward=backward, bl=bl[:, 0:1], m_loc=ml[:, 0:1],
                    a_t=jnp.where(triu if backward else tril, a, _NEG_INF),
                    b_full=jnp.broadcast_to(b, (BLOCK, LANES)).T,
                    c_full=jnp.broadcast_to(c, (BLOCK, LANES)).T,
                    kw=(kt.astype(f32) * w).astype(bf16), kt=kt)

    def first_products(g):
        vx = vx_ref[pl.ds(g["t0"], BLOCK), :]
        g["qk"] = _dot(qc_ref[pl.ds(g["t0"], BLOCK), :], g["kt"])
        g["kv"] = _dot(g["kw"], vx)

    def decayed_scores(g):
        g["s0"] = (g["qk"] * jnp.exp2(g["a_t"] - g["c_full"])).astype(bf16)

    def inner_products(g):
        g["p0"] = _dot(g["s0"], vx_ref[pl.ds(g["t0"], BLOCK), :])

    def with_state(g, state, park_ref, other_ref):
        cx, m = state
        x = _dot(qc_ref[pl.ds(g["t0"], BLOCK), :], cx.astype(bf16))
        n = jnp.maximum(g["c_full"], m)
        e = jnp.exp2(g["c_full"] - n)
        iw = jnp.exp2(m - n)
        num = e * g["p0"][:, 0:C_DIM] + iw * x[:, 0:C_DIM]
        den = e * g["p0"][:, C_DIM:] + iw * x[:, C_DIM:]
        hh = num / jnp.maximum(jnp.abs(den), jnp.exp2(-(g["b_full"] + n)))
        rows = pl.ds(g["t0"], BLOCK)
        if other_ref is None:
            park_ref[rows, :] = hh
        else:
            r = hh + other_ref[rows, :]
            r = r * lax.rsqrt(jnp.mean(r * r, axis=-1, keepdims=True) + EPS)
            o_ref[rows, :] = (jax.nn.sigmoid(og_ref[rows, :].astype(f32)) * r).astype(o_ref.dtype)
        m_new = jnp.maximum(g["bl"] + m, g["m_loc"])
        cw = jnp.exp2(g["bl"] + m - m_new)
        f = jnp.exp2(g["m_loc"] - m_new)
        return cw * cx + f * g["kv"], m_new

    def make_body(second_half):
        def body(it, carry):
            sf, sb = carry
            steps = []
            for u in range(_SCAN_UNROLL):
                c = it * _SCAN_UNROLL + u
                steps.append(load_rows(c, False))
                steps.append(load_rows(nc - 1 - c, True))
            for g in steps:
                first_products(g)
            for g in steps:
                decayed_scores(g)
            for g in steps:
                inner_products(g)
            for g in steps:
                if g["backward"]:
                    sb = with_state(g, sb, ob_ref, of_ref if second_half else None)
                else:
                    sf = with_state(g, sf, of_ref, ob_ref if second_half else None)
            return sf, sb
        return body

    init = (jnp.zeros((C_DIM, 2 * C_DIM), f32), jnp.zeros((1, 1), f32))
    n_bodies = nc // _SCAN_UNROLL
    state = lax.fori_loop(0, n_bodies // 2, make_body(False), (init, init))
    lax.fori_loop(n_bodies // 2, n_bodies, make_body(True), state)


def _mlstm(pb, pf, gates_t, wq_conv, wk_conv, gate_bias):
    bsz, seq, _ = pb.shape
    assert seq % (8 * BLOCK) == 0
    kern = functools.partial(_mlstm_kernel, seq=seq)

    def col(base):
        return pl.BlockSpec((None, seq, LANES), lambda b, h: (b, 0, base + h))

    return pl.pallas_call(
        kern,
        grid=(bsz, C_HEADS),
        in_specs=[col(_PF_CQ), col(_PF_CK),
                  pl.BlockSpec((None, None, 4, seq // BLOCK, LANES), lambda b, h: (b, h, 0, 0, 0)),
                  col(_PB_CV), col(_PB_CO),
                  pl.BlockSpec((3, LANES), lambda b, h: (0, h)),
                  pl.BlockSpec((3, LANES), lambda b, h: (0, h)),
                  pl.BlockSpec(memory_space=pltpu.SMEM)],
        out_specs=pl.BlockSpec((None, seq, LANES), lambda b, h: (b, 0, h)),
        out_shape=jax.ShapeDtypeStruct((bsz, seq, C_HEADS * C_DIM), jnp.bfloat16),
        scratch_shapes=[pltpu.VMEM((seq, C_DIM), jnp.bfloat16),
                        pltpu.VMEM((seq // BLOCK, C_DIM, BLOCK), jnp.bfloat16),
                        pltpu.VMEM((seq, 2 * C_DIM), jnp.bfloat16),
                        pltpu.VMEM((2 * _ROWS_PER_DIR, seq // BLOCK, LANES), jnp.float32),
                        pltpu.VMEM((seq, C_DIM), jnp.float32), pltpu.VMEM((seq, C_DIM), jnp.float32)],
        compiler_params=_cparams("parallel", "parallel"),
        name="mlstm",
    )(pf, pf, gates_t, pb, pb, wq_conv, wk_conv, gate_bias)


def _window_attn_kernel(slopes_ref, sink_ref, q_ref, k_ref, v_ref, o_ref, kx_ref, vx_ref, *, seq):
    qi = pl.program_id(1)
    f32 = jnp.float32
    span = BLOCK + 2 * WINDOW
    group = D_HEADS // D_KV_HEADS

    @pl.when(qi == 0)
    def _():
        lane = lax.broadcasted_iota(jnp.int32, (seq, LANES), 1)
        ones = jnp.ones((seq, LANES), vx_ref.dtype)
        for src, dst in ((k_ref, kx_ref), (v_ref, vx_ref)):
            x = src[...].astype(f32)
            x0 = jnp.where(lane < D_DIM, x, 0.0)
            x1 = jnp.where(lane >= D_DIM, x, 0.0)
            for i, xi in enumerate((x0, pltpu.roll(x0, D_DIM, 1), pltpu.roll(x1, D_DIM, 1), x1)):
                dst[i, :, 0:LANES] = xi.astype(dst.dtype)
        for i in range(4):
            vx_ref[i, :, LANES:2 * LANES] = ones

    t0 = qi * BLOCK
    ws = pl.multiple_of(jnp.clip(t0 - WINDOW, 0, seq - span), BLOCK)
    rel = (lax.broadcasted_iota(jnp.int32, (BLOCK, span), 0)
           - lax.broadcasted_iota(jnp.int32, (BLOCK, span), 1)) + (t0 - ws)
    dist = jnp.abs(rel)
    neg_dist = jnp.where(dist <= WINDOW, dist.astype(f32) * (-_LOG2E), _NEG_INF)

    n_pairs = D_HEADS // 2
    lookahead = 2
    scores = {}

    def score_pair(pair):
        kv = (2 * pair) // group
        qp = (q_ref[:, pair * LANES:(pair + 1) * LANES].astype(f32) * (D_DIM ** -0.5 * _LOG2E)).astype(q_ref.dtype)
        for half in range(2):
            scores[2 * pair + half] = _dot_nt(qp, kx_ref[2 * kv + half, pl.ds(ws, span), :])

    def attend(hd, kv_slot):
        s = scores[hd] + slopes_ref[hd] * neg_dist
        snk = sink_ref[hd] * _LOG2E
        m = jnp.maximum(jnp.max(s, axis=1, keepdims=True), snk)
        o = _dot(jnp.exp2(s - m).astype(jnp.bfloat16), vx_ref[kv_slot, pl.ds(ws, span), :])
        return o[:, 0:LANES] / (o[:, LANES:] + jnp.exp2(snk - m))

    for pair in range(lookahead):
        score_pair(pair)
    for pair in range(n_pairs):
        kv = (2 * pair) // group
        out = attend(2 * pair, 2 * kv) + attend(2 * pair + 1, 2 * kv + 1)
        o_ref[:, pair * LANES:(pair + 1) * LANES] = out.astype(o_ref.dtype)
        if pair + lookahead < n_pairs:
            score_pair(pair + lookahead)


def _window_attention(pb, attn_sink, slopes_d):
    bsz, seq, _ = pb.shape
    kern = functools.partial(_window_attn_kernel, seq=seq)
    width = D_HEADS * D_DIM
    return pl.pallas_call(
        kern,
        grid=(bsz, seq // BLOCK),
        in_specs=[
            pl.BlockSpec(memory_space=pltpu.SMEM),
            pl.BlockSpec(memory_space=pltpu.SMEM),
            pl.BlockSpec((None, BLOCK, width), lambda b, i: (b, i, _PB_DQ512)),
            pl.BlockSpec((None, seq, LANES), lambda b, i: (b, 0, _PB_DK)),
            pl.BlockSpec((None, seq, LANES), lambda b, i: (b, 0, _PB_DV)),
        ],
        out_specs=pl.BlockSpec((None, BLOCK, width), lambda b, i: (b, i, 0)),
        out_shape=jax.ShapeDtypeStruct((bsz, seq, width), jnp.bfloat16),
        scratch_shapes=[pltpu.VMEM((4, seq, LANES), jnp.bfloat16), pltpu.VMEM((4, seq, 2 * LANES), jnp.bfloat16)],
        compiler_params=_cparams("parallel", "arbitrary"),
        name="window_attention",
    )(slopes_d, attn_sink, pb, pb, pb)


def _out_proj_kernel(x_ref, ya_ref, yb_ref, yc_ref, yd_ref, w_ref, o_ref):
    acc = x_ref[...]
    for g, y_ref in enumerate((ya_ref, yb_ref, yc_ref, yd_ref)):
        acc = acc + _dot(y_ref[...], w_ref[g * MIXER_WIDTH:(g + 1) * MIXER_WIDTH, :])
    o_ref[...] = acc


def _out_proj(x2d, ys, w_out, tm):
    t, d = x2d.shape
    yspec = pl.BlockSpec((tm, MIXER_WIDTH), lambda i: (i, 0))
    return pl.pallas_call(
        _out_proj_kernel,
        grid=(t // tm,),
        in_specs=[pl.BlockSpec((tm, d), lambda i: (i, 0)), yspec, yspec, yspec, yspec,
                  pl.BlockSpec((d, d), lambda i: (0, 0), pipeline_mode=pl.Buffered(1))],
        out_specs=pl.BlockSpec((tm, d), lambda i: (i, 0)),
        out_shape=jax.ShapeDtypeStruct((t, d), jnp.float32),
        compiler_params=_cparams("parallel", vmem_limit=_BIG_VMEM_LIMIT),
        name="out_proj",
    )(x2d, *ys, w_out)


def _ffn_kernel(x_ref, xp_ref, xn_ref, nw_ref, wg_ref, wu_ref, cw_ref, wd_ref, fw_ref, o_ref,
                h_ref, *, seq, tm, final_norm):
    i = pl.program_id(0)
    j = pl.program_id(1)
    f32 = jnp.float32

    def norm(x):
        return x * lax.rsqrt(jnp.mean(x * x, axis=-1, keepdims=True) + EPS) * nw_ref[...]

    def d_ff_step(first=False, last=False):
        gate = _dot(h_ref[...], wg_ref[...])
        gm = gate[0:tm]
        rowi = lax.broadcasted_iota(jnp.int32, gm.shape, 0)
        g_dn = jnp.where(rowi == 0, gate[tm + 7:tm + 8], pltpu.roll(gm, 1, 0))
        g_up = jnp.where(rowi == tm - 1, gate[tm + 8:tm + 9], pltpu.roll(gm, tm - 1, 0))
        cw = cw_ref[...]
        g = g_dn * cw[0:1] + gm * cw[1:2] + g_up * cw[2:3]
        up = _dot(h_ref[0:tm, :], wu_ref[...])
        act = (g * jax.nn.sigmoid(g) * up).astype(wd_ref.dtype)
        down = _dot(act, wd_ref[...])
        if first:
            o_ref[...] = down
        elif last:
            y = x_ref[...] + o_ref[...] + down
            if final_norm:
                y = y * lax.rsqrt(jnp.mean(y * y, axis=-1, keepdims=True) + EPS) * fw_ref[...]
            o_ref[...] = y
        else:
            o_ref[...] += down

    @pl.when(j == 0)
    def _():
        h_ref[0:tm, :] = norm(x_ref[...]).astype(h_ref.dtype)
        has_prev = ((i * tm) % seq != 0).astype(f32)
        has_next = (((i + 1) * tm) % seq != 0).astype(f32)
        halo = jnp.concatenate([norm(xp_ref[...]) * has_prev, norm(xn_ref[...]) * has_next], axis=0)
        h_ref[tm:tm + 16, :] = halo.astype(h_ref.dtype)
        d_ff_step(True)

    last_j = pl.num_programs(1) - 1

    @pl.when((j > 0) & (j < last_j))
    def _():
        d_ff_step()

    @pl.when(j == last_j)
    def _():
        d_ff_step(last=True)


def _ffn(x2d, seq, nw, wg, wu, cw, wd, fw, final_norm, tm, tn):
    t, d = x2d.shape
    nff = wg.shape[1]
    rb = tm // 8
    last8 = t // 8 - 1
    kern = functools.partial(_ffn_kernel, seq=seq, tm=tm, final_norm=final_norm)
    return pl.pallas_call(
        kern,
        grid=(t // tm, nff // tn),
        in_specs=[
            pl.BlockSpec((tm, d), lambda i, j: (i, 0)),
            pl.BlockSpec((8, d), lambda i, j: (jnp.maximum(i * rb - 1, 0), 0)),
            pl.BlockSpec((8, d), lambda i, j: (jnp.minimum((i + 1) * rb, last8), 0)),
            pl.BlockSpec((1, d), lambda i, j: (0, 0)),
            pl.BlockSpec((d, tn), lambda i, j: (0, j)),
            pl.BlockSpec((d, tn), lambda i, j: (0, j)),
            pl.BlockSpec((3, tn), lambda i, j: (0, j)),
            pl.BlockSpec((tn, d), lambda i, j: (j, 0)),
            pl.BlockSpec((1, d), lambda i, j: (0, 0)),
        ],
        out_specs=pl.BlockSpec((tm, d), lambda i, j: (i, 0)),
        out_shape=jax.ShapeDtypeStruct((t, d), jnp.float32),
        scratch_shapes=[pltpu.VMEM((tm + 16, d), jnp.bfloat16)],
        compiler_params=_cparams("parallel", "arbitrary", vmem_limit=_BIG_VMEM_LIMIT),
        name="ffn",
    )(x2d, x2d, x2d, nw, wg, wu, cw, wd, fw)


def _prepare_layer(l, norm1_w, w_in, qk_conv_w, diff_lambda, ret_decay_logit, mlstm_gate_bias, attn_sink,
                   w_out, norm2_w, w_gate, w_up, ffn_conv_w, w_down):
    bf16 = jnp.bfloat16
    f32 = jnp.float32
    wi = w_in[l]
    w_pb = jnp.concatenate([wi[:, _OFF_A:_OFF_CQ], wi[:, _OFF_CV:_OFF_GATES], wi[:, _OFF_D:N_IN]], axis=1)
    wgates = jnp.pad(wi[:, _OFF_GATES:_OFF_D], ((0, 0), (0, LANES - N_GATES)))
    w_pf = jnp.concatenate([wi[:, _OFF_CQ:_OFF_CV], wgates], axis=1)
    gbias = mlstm_gate_bias[l].astype(f32)
    dec = jnp.zeros((8, LANES), f32).at[0:2, 0:B_HEADS].set(ret_decay_logit[l].astype(f32))
    return dict(
        norm1=norm1_w[l].reshape(1, D_MODEL).astype(f32),
        w_pb=w_pb.astype(bf16), w_pf=w_pf.astype(bf16),
        wq_conv=qk_conv_w[l][:, :C_HEADS * C_DIM].astype(f32),
        wk_conv=qk_conv_w[l][:, C_HEADS * C_DIM:].astype(f32),
        gbias=gbias, dec=dec,
        diff_lambda=diff_lambda[l].astype(f32), sink=attn_sink[l].astype(f32),
        w_out=w_out[l].astype(bf16),
        norm2=norm2_w[l].reshape(1, D_MODEL).astype(f32),
        w_gate=w_gate[l].astype(bf16), w_up=w_up[l].astype(bf16),
        ffn_conv=ffn_conv_w[l].astype(f32), w_down=w_down[l].astype(bf16),
    )


def _tile(n, pref):
    t = min(pref, n)
    while n % t:
        t //= 2
    return t


def _layer(x2d, bsz, seq, layer_idx, p, slopes, final_w, final_norm):
    t = bsz * seq
    pb, pf = _in_proj(x2d, p["norm1"], p["w_pb"], p["w_pf"], _tile(t, _TOKEN_TILE), _IN_PROJ_TN)
    pb = pb.reshape(bsz, seq, PB_WIDTH)
    pf = pf.reshape(bsz, seq, PF_WIDTH)
    gates_t = pf[:, :, _PF_GATES * LANES:_PF_GATES * LANES + N_GATES].reshape(bsz, seq, 4, C_HEADS)
    gates_t = jnp.transpose(gates_t, (0, 3, 2, 1)).reshape(bsz, C_HEADS, 4, seq // BLOCK, BLOCK)
    y_a = _diff_attention(pb, p["diff_lambda"], slopes[D_HEADS:], layer_idx)
    y_b = _retention(pb, p["dec"])
    y_c = _mlstm(pb, pf, gates_t, p["wq_conv"], p["wk_conv"], p["gbias"])
    y_d = _window_attention(pb, p["sink"], slopes[:D_HEADS])
    ys = [y.reshape(t, MIXER_WIDTH) for y in (y_a, y_b, y_c, y_d)]
    x1 = _out_proj(x2d, ys, p["w_out"], _tile(t, _TOKEN_TILE))
    return _ffn(x1, seq, p["norm2"], p["w_gate"], p["w_up"], p["ffn_conv"], p["w_down"], final_w,
                final_norm, _tile(seq, _TOKEN_TILE), _FFN_TN)


def _trunk(x, layers, slopes, final_w):
    bsz, seq, d = x.shape
    x2d = x.reshape(bsz * seq, d)
    for l, p in enumerate(layers):
        x2d = _layer(x2d, bsz, seq, l, p, slopes, final_w, l == len(layers) - 1)
    return x2d.reshape(bsz, seq, d)


def kernel(x_prompt, x_sample, norm1_w, w_in, qk_conv_w, diff_lambda, ret_decay_logit, mlstm_gate_bias,
           attn_sink, w_out, norm2_w, w_gate, w_up, ffn_conv_w, w_down, final_norm_w):
    layers = [_prepare_layer(l, norm1_w, w_in, qk_conv_w, diff_lambda, ret_decay_logit, mlstm_gate_bias,
                             attn_sink, w_out, norm2_w, w_gate, w_up, ffn_conv_w, w_down)
              for l in range(DEPTH)]
    n = A_HEADS + D_HEADS
    slopes = 2.0 ** (-8.0 * jnp.arange(1, n + 1, dtype=jnp.float32) / n)
    final_w = final_norm_w.reshape(1, D_MODEL).astype(jnp.float32)
    return (_trunk(x_prompt, layers, slopes, final_w), _trunk(x_sample, layers, slopes, final_w))
```

```python
import functools
import math

import jax
import jax.numpy as jnp
from jax import lax
from jax.experimental import pallas as pl
from jax.experimental.pallas import tpu as pltpu

D_MODEL = 2048
DEPTH = 2
A_HEADS = 4
A_QK_DIM = 64
A_V_DIM = 128
B_HEADS = 4
B_DIM = 128
C_HEADS = 4
C_DIM = 128
D_HEADS = 8
D_KV_HEADS = 2
D_DIM = 64
WINDOW = 128
BLOCK = 128
D_FF = 5632
EPS = 1e-6
LANES = 128

_OFF_A = 0
_OFF_B = 1536
_OFF_CQ = 3584
_OFF_CV = 4608
_OFF_GATES = 5632
_OFF_D = 5648
N_IN = 6416

_PB_AQ, _PB_AK, _PB_AV = 0, 4, 8
_PB_BQ, _PB_BK, _PB_BV, _PB_BG = 12, 16, 20, 24
_PB_CV, _PB_CO = 28, 32
_PB_DQ512 = 9
_PB_DK, _PB_DV = 40, 41
PB_WIDTH = 42 * LANES
_IN_PROJ_TN = 1792
_PF_CQ, _PF_CK, _PF_GATES = 0, 4, 8
PF_WIDTH = 9 * LANES
N_GATES = 16

MIXER_WIDTH = 512
_TOKEN_TILE = 1024
_FFN_TN = 512
_VMEM_LIMIT = 56 * 1024 * 1024
_BIG_VMEM_LIMIT = 61 * 1024 * 1024
_SCAN_UNROLL = 4
_RETENTION_UNROLL = 8
_NEG_INF = float("-inf")


def _cparams(*sem, vmem_limit=_VMEM_LIMIT):
    return pltpu.CompilerParams(dimension_semantics=sem, vmem_limit_bytes=vmem_limit)


def _dot(a, b):
    return jnp.dot(a, b, preferred_element_type=jnp.float32)


def _dot_nt(a, b):
    return lax.dot_general(a, b, (((1,), (1,)), ((), ())), preferred_element_type=jnp.float32)


def _dot_tn(a, b):
    return lax.dot_general(a, b, (((0,), (0,)), ((), ())), preferred_element_type=jnp.float32)


def _log_sigmoid(x):
    return -(jnp.maximum(-x, 0.0) + jnp.log1p(jnp.exp(-jnp.abs(x))))


def _split_hi_lo(x):
    hi = x.astype(jnp.bfloat16)
    lo = (x - hi.astype(jnp.float32)).astype(jnp.bfloat16)
    return hi, lo


def _in_proj_kernel(x_ref, nw_ref, wb_ref, wf_ref, ob_ref, of_ref, h_ref, *, nb):
    j = pl.program_id(1)

    @pl.when(j == 0)
    def _():
        x = x_ref[...]
        ms = jnp.mean(x * x, axis=-1, keepdims=True)
        h = (x * lax.rsqrt(ms + EPS) * nw_ref[...]).astype(h_ref.dtype)
        h_ref[...] = h
        ob_ref[...] = _dot(h, wb_ref[...]).astype(ob_ref.dtype)

    @pl.when((j > 0) & (j < nb))
    def _():
        ob_ref[...] = _dot(h_ref[...], wb_ref[...]).astype(ob_ref.dtype)

    @pl.when(j == nb)
    def _():
        of_ref[...] = _dot(h_ref[...], wf_ref[...])


def _in_proj(x2d, nw, w_pb, w_pf, tm, tn):
    t, d = x2d.shape
    nb = PB_WIDTH // tn
    return pl.pallas_call(
        functools.partial(_in_proj_kernel, nb=nb),
        grid=(t // tm, nb + 1),
        in_specs=[
            pl.BlockSpec((tm, d), lambda i, j: (i, 0)),
            pl.BlockSpec((1, d), lambda i, j: (0, 0)),
            pl.BlockSpec((d, tn), lambda i, j: (0, jnp.minimum(j, nb - 1))),
            pl.BlockSpec((d, PF_WIDTH), lambda i, j: (0, 0), pipeline_mode=pl.Buffered(1)),
        ],
        out_specs=[pl.BlockSpec((tm, tn), lambda i, j: (i, jnp.minimum(j, nb - 1))),
                   pl.BlockSpec((tm, PF_WIDTH), lambda i, j: (i, 0))],
        out_shape=[jax.ShapeDtypeStruct((t, PB_WIDTH), jnp.bfloat16),
                   jax.ShapeDtypeStruct((t, PF_WIDTH), jnp.float32)],
        scratch_shapes=[pltpu.VMEM((tm, d), jnp.bfloat16)],
        compiler_params=_cparams("parallel", "arbitrary", vmem_limit=_BIG_VMEM_LIMIT),
        name="norm_in_proj",
    )(x2d, nw, w_pb, w_pf)


_LOG2E = 1.4426950408889634
_N_BIAS_LANES = 3
_V_ROWS = A_V_DIM + 16
_ATTN_MAX_TILES_PER_BODY = 8
_ATTN_TILE = 512


def _diff_attn_kernel(slopes_ref, q_ref, qn_ref, k_ref, v_ref, dl_ref, o_ref,
                      kx_ref, vt_ref, qt_ref, m_ref, acc_ref, s_ref, cmax_ref, bias_ref, *, seq, tq, tk,
                      tpb, lam_init):
    head = pl.program_id(1)
    qi = pl.program_id(2)
    f32 = jnp.float32
    bf16 = jnp.bfloat16
    sigma = slopes_ref[head] * _LOG2E
    nk = seq // tk
    prep_rows = tk

    @pl.when(qi == 0)
    def _():
        lane = lax.broadcasted_iota(jnp.int32, (prep_rows, LANES), 1)
        row = lax.broadcasted_iota(jnp.int32, (prep_rows, LANES), 0)
        ones_row = jnp.where(lax.broadcasted_iota(jnp.int32, (_V_ROWS - A_V_DIM, tk), 0) == 0, 1.0, 0.0)

        def prep(c, carry):
            r0 = pl.multiple_of(c * prep_rows, prep_rows)
            kk = k_ref[pl.ds(r0, prep_rows), :].astype(f32)
            col = sigma * (row + r0).astype(f32)
            c1 = col.astype(bf16).astype(f32)
            c2 = (col - c1).astype(bf16).astype(f32)
            c3 = col - c1 - c2
            bias = jnp.where(lane == A_QK_DIM, c1, jnp.where(lane == A_QK_DIM + 1, c2,
                             jnp.where(lane == A_QK_DIM + 2, c3, 0.0)))
            kx_ref[0, pl.ds(r0, prep_rows), :] = jnp.where(lane < A_QK_DIM, kk, bias).astype(bf16)
            kx_ref[1, pl.ds(r0, prep_rows), :] = jnp.where(
                lane < A_QK_DIM, pltpu.roll(kk, A_QK_DIM, 1), bias).astype(bf16)
            vt_ref[c, 0:A_V_DIM, :] = v_ref[pl.ds(r0, prep_rows), :].astype(f32).T.astype(bf16)
            vt_ref[c, A_V_DIM:_V_ROWS, :] = ones_row.astype(bf16)
            return carry

        lax.fori_loop(0, nk, prep, 0)
        rel = (lax.broadcasted_iota(jnp.int32, (tk, tq), 1)
               - lax.broadcasted_iota(jnp.int32, (tk, tq), 0)).astype(f32)
        bias_ref[...] = sigma * jnp.abs(rel)

    qrow = lax.broadcasted_iota(jnp.int32, (LANES, tq), 0)
    bias_rows = lax.broadcasted_iota(jnp.int32, (16, tq), 0) < _N_BIAS_LANES

    def build_q_variants(src_ref, base_slot):
        qt = (src_ref[...].astype(f32) * (A_QK_DIM ** -0.5 * _LOG2E)).T
        for mp in range(2):
            qm = qt if mp == 0 else pltpu.roll(qt, A_QK_DIM, 0)
            base = jnp.where(qrow < A_QK_DIM, qm, 0.0).astype(bf16)
            for var, sign in enumerate((1.0, 0.0, -1.0)):
                qt_ref[base_slot + 3 * mp + var] = base
                qt_ref[base_slot + 3 * mp + var, A_QK_DIM:A_QK_DIM + 16, :] = (
                    jnp.where(bias_rows, sign, 0.0).astype(bf16))

    def diagonal_scores(tile, base_slot, mp):
        k_start = pl.multiple_of(tile * tk, tk)
        s = _dot(kx_ref[mp, pl.ds(k_start, tk), :], qt_ref[base_slot + 3 * mp + 1]) - bias_ref[...]
        s_ref[0, mp] = s
        cmax_ref[0, mp] = jnp.max(s, axis=0, keepdims=True)

    qslot = (qi % 2) * 6
    qslot_next = 6 - qslot

    @pl.when(qi == 0)
    def _():
        build_q_variants(q_ref, 0)
        for mp in range(2):
            diagonal_scores(0, 0, mp)

    q0 = qi * tq
    rho = sigma * (lax.broadcasted_iota(jnp.int32, (1, tq), 1) + q0).astype(f32)
    m_ref[...] = jnp.full(m_ref.shape, _NEG_INF, f32)
    acc_ref[...] = jnp.zeros(acc_ref.shape, f32)

    kd = q0 // tk

    def key_tile(j):
        jm = jnp.minimum(j, nk - 1) - 1
        return jnp.where(j == 0, kd, jnp.where(jm < kd, jm, jm + 1))

    def scores(j, slot, mp):
        kb = key_tile(j)
        var = jnp.where(kb < kd, 0, jnp.where(kb > kd, 2, 1))
        k_start = pl.multiple_of(kb * tk, tk)
        s = _dot(kx_ref[mp, pl.ds(k_start, tk), :], qt_ref[qslot + 3 * mp + var])
        s_ref[slot, mp] = s
        cmax_ref[slot, mp] = jnp.max(s, axis=0, keepdims=True)

    def softmax_pv(j, slot, mp):
        kb = key_tile(j)
        shift = jnp.where(kb < kd, -1.0, jnp.where(kb > kd, 1.0, 0.0)) * rho
        m_old = m_ref[mp]
        m_new = jnp.maximum(m_old, cmax_ref[slot, mp] + shift)
        p = jnp.exp2(s_ref[slot, mp] - (m_new - shift))
        acc_ref[mp] = jnp.exp2(m_old - m_new) * acc_ref[mp] + _dot(vt_ref[kb], p.astype(bf16))
        m_ref[mp] = m_new

    def body(i, carry):
        for u in range(tpb):
            j = tpb * i + u
            for mp in range(2):
                scores(j + 1, (u + 1) % tpb, mp)
                softmax_pv(j, u, mp)
        return carry

    n_bodies = nk // tpb
    lax.fori_loop(0, n_bodies - 1, body, 0)

    build_q_variants(qn_ref, qslot_next)
    kd_next = jnp.minimum(qi + 1, seq // tq - 1)
    for u in range(tpb):
        j = tpb * (n_bodies - 1) + u
        for mp in range(2):
            if u + 1 < tpb:
                scores(j + 1, u + 1, mp)
            else:
                diagonal_scores(kd_next, qslot_next, mp)
            softmax_pv(j, u, mp)

    dl = dl_ref[...]
    lam = (jnp.exp(jnp.sum(dl[0:1] * dl[1:2], axis=1, keepdims=True))
           - jnp.exp(jnp.sum(dl[2:3] * dl[3:4], axis=1, keepdims=True)) + lam_init)
    a0 = acc_ref[0]
    a1 = acc_ref[1]
    ot = (a0[0:A_V_DIM] / a0[A_V_DIM:A_V_DIM + 1] - lam * (a1[0:A_V_DIM] / a1[A_V_DIM:A_V_DIM + 1]))
    ot = ot * lax.rsqrt(jnp.mean(ot * ot, axis=0, keepdims=True) + EPS)
    o_ref[...] = (ot * (1.0 - lam_init)).T.astype(o_ref.dtype)


def _diff_attention(pb, diff_lambda, slopes_a, layer_idx):
    bsz, seq, _ = pb.shape
    tq = tk = _tile(seq, _ATTN_TILE)
    tpb = _tile(seq // (2 * tk), _ATTN_MAX_TILES_PER_BODY)
    lam_init = 0.8 - 0.6 * math.exp(-0.3 * layer_idx)
    kern = functools.partial(_diff_attn_kernel, seq=seq, tq=tq, tk=tk, tpb=tpb, lam_init=lam_init)
    return pl.pallas_call(
        kern,
        grid=(bsz, A_HEADS, seq // tq),
        in_specs=[
            pl.BlockSpec(memory_space=pltpu.SMEM),
            pl.BlockSpec((None, tq, LANES), lambda b, h, i: (b, i, _PB_AQ + h)),
            pl.BlockSpec((None, tq, LANES), lambda b, h, i: (b, jnp.minimum(i + 1, seq // tq - 1), _PB_AQ + h)),
            pl.BlockSpec((None, seq, LANES), lambda b, h, i: (b, 0, _PB_AK + h)),
            pl.BlockSpec((None, seq, LANES), lambda b, h, i: (b, 0, _PB_AV + h)),
            pl.BlockSpec((4, A_QK_DIM), lambda b, h, i: (0, 0)),
        ],
        out_specs=pl.BlockSpec((None, tq, LANES), lambda b, h, i: (b, i, h)),
        out_shape=jax.ShapeDtypeStruct((bsz, seq, A_HEADS * A_V_DIM), jnp.bfloat16),
        scratch_shapes=[
            pltpu.VMEM((2, seq, LANES), jnp.bfloat16),
            pltpu.VMEM((seq // tk, _V_ROWS, tk), jnp.bfloat16),
            pltpu.VMEM((12, LANES, tq), jnp.bfloat16),
            pltpu.VMEM((2, 1, tq), jnp.float32),
            pltpu.VMEM((2, _V_ROWS, tq), jnp.float32),
            pltpu.VMEM((tpb, 2, tk, tq), jnp.float32),
            pltpu.VMEM((tpb, 2, 1, tq), jnp.float32),
            pltpu.VMEM((tk, tq), jnp.float32),
        ],
        compiler_params=_cparams("parallel", "parallel", "arbitrary"),
        name="diff_attention",
    )(slopes_a, pb, pb, pb, pb, diff_lambda)


def _retention_kernel(q_ref, k_ref, v_ref, g_ref, dec_ref, o_ref, of_ref, ob_ref, *, seq):
    head = pl.program_id(1)
    nc = seq // BLOCK
    f32 = jnp.float32

    lgv = _log_sigmoid(dec_ref[...])
    rr = lax.broadcasted_iota(jnp.int32, lgv.shape, 0)
    cc = lax.broadcasted_iota(jnp.int32, lgv.shape, 1)

    def pick(row):
        sel = jnp.where((rr == row) & (cc == head), lgv, 0.0)
        return jnp.sum(jnp.sum(sel, axis=1, keepdims=True), axis=0, keepdims=True)

    lg_f, lg_b = pick(0), pick(1)
    ri = lax.broadcasted_iota(jnp.int32, (BLOCK, BLOCK), 0).astype(f32)
    ci = lax.broadcasted_iota(jnp.int32, (BLOCK, BLOCK), 1).astype(f32)
    diff = ri - ci
    decay_f = jnp.where(diff >= 0, jnp.exp(lg_f * jnp.maximum(diff, 0.0)), 0.0)
    decay_b = jnp.where(diff <= 0, jnp.exp(lg_b * jnp.maximum(-diff, 0.0)), 0.0)
    idx = lax.broadcasted_iota(jnp.int32, (BLOCK, 1), 0).astype(f32)
    qdec_f = jnp.exp(lg_f * (idx + 1.0))
    kdec_f = jnp.exp(lg_f * (BLOCK - 1.0 - idx))
    qdec_b = jnp.exp(lg_b * (BLOCK - idx))
    kdec_b = jnp.exp(lg_b * idx)
    cdec_f = jnp.exp(lg_f * BLOCK)
    cdec_b = jnp.exp(lg_b * BLOCK)
    kscale = B_DIM ** -0.5

    def scores_and_kv(t0, backward):
        qb = q_ref[pl.ds(t0, BLOCK), :]
        kb = k_ref[pl.ds(t0, BLOCK), :].astype(f32) * kscale
        vb = v_ref[pl.ds(t0, BLOCK), :]
        kdec = kdec_b if backward else kdec_f
        return dict(t0=t0, backward=backward, qb=qb, vb=vb,
                    qk=_dot_nt(qb, kb.astype(qb.dtype)), kv=_dot_tn((kb * kdec).astype(vb.dtype), vb))

    def inner_product(g):
        s = g["qk"] * (decay_b if g["backward"] else decay_f)
        g["inner"] = _dot(s.astype(g["vb"].dtype), g["vb"])

    def with_state(g, state, park_ref, other_ref):
        qdec, cdec = (qdec_b, cdec_b) if g["backward"] else (qdec_f, cdec_f)
        cross = _dot(g["qb"], state.astype(g["qb"].dtype)) * qdec
        rows = pl.ds(g["t0"], BLOCK)
        if other_ref is None:
            park_ref[rows, :] = g["inner"] + cross
        else:
            r = g["inner"] + cross + other_ref[rows, :]
            r = r * lax.rsqrt(jnp.mean(r * r, axis=-1, keepdims=True) + EPS)
            gate = g_ref[rows, :].astype(f32)
            o_ref[rows, :] = (gate * jax.nn.sigmoid(gate) * r).astype(o_ref.dtype)
        return state * cdec + g["kv"]

    def make_body(second_half):
        def body(it, carry):
            rf, rb = carry
            steps = []
            for u in range(_RETENTION_UNROLL):
                c = it * _RETENTION_UNROLL + u
                steps.append(scores_and_kv(pl.multiple_of(c * BLOCK, BLOCK), False))
                steps.append(scores_and_kv(pl.multiple_of((nc - 1 - c) * BLOCK, BLOCK), True))
            for g in steps:
                inner_product(g)
            for g in steps:
                if g["backward"]:
                    rb = with_state(g, rb, ob_ref, of_ref if second_half else None)
                else:
                    rf = with_state(g, rf, of_ref, ob_ref if second_half else None)
            return rf, rb
        return body

    zero = jnp.zeros((B_DIM, B_DIM), f32)
    n_bodies = nc // _RETENTION_UNROLL
    state = lax.fori_loop(0, n_bodies // 2, make_body(False), (zero, zero))
    lax.fori_loop(n_bodies // 2, n_bodies, make_body(True), state)


def _retention(pb, dec_pad):
    bsz, seq, _ = pb.shape
    assert seq % (2 * _RETENTION_UNROLL * BLOCK) == 0
    kern = functools.partial(_retention_kernel, seq=seq)

    def col(base):
        return pl.BlockSpec((None, seq, LANES), lambda b, h: (b, 0, base + h))

    return pl.pallas_call(
        kern,
        grid=(bsz, B_HEADS),
        in_specs=[col(_PB_BQ), col(_PB_BK), col(_PB_BV), col(_PB_BG),
                  pl.BlockSpec((8, LANES), lambda b, h: (0, 0))],
        out_specs=pl.BlockSpec((None, seq, LANES), lambda b, h: (b, 0, h)),
        out_shape=jax.ShapeDtypeStruct((bsz, seq, B_HEADS * B_DIM), jnp.bfloat16),
        scratch_shapes=[pltpu.VMEM((seq, B_DIM), jnp.float32), pltpu.VMEM((seq, B_DIM), jnp.float32)],
        compiler_params=_cparams("parallel", "parallel"),
        name="retention",
    )(pb, pb, pb, pb, dec_pad)


_ROWS_PER_DIR = 6


def _mlstm_kernel(cq_ref, ck_ref, gt_ref, v_ref, og_ref, wq_ref, wk_ref, gb_ref, o_ref,
                  qc_ref, kt_ref, vx_ref, rows_ref, of_ref, ob_ref, *, seq):
    nc = seq // BLOCK
    f32 = jnp.float32
    bf16 = jnp.bfloat16
    rowi = lax.broadcasted_iota(jnp.int32, (BLOCK, LANES), 0)
    coli = lax.broadcasted_iota(jnp.int32, (BLOCK, LANES), 1)
    lane_n = lax.broadcasted_iota(jnp.int32, (nc, LANES), 1)
    tril = rowi >= coli
    triu = rowi <= coli
    tril_b = tril.astype(bf16)
    triu_b = triu.astype(bf16)
    ones_b = jnp.ones((BLOCK, LANES), bf16)
    head = pl.program_id(1)

    def cummax_lanes(x, backward):
        sh = 1
        while sh < LANES:
            if backward:
                x = jnp.maximum(x, jnp.where(lane_n < LANES - sh, pltpu.roll(x, LANES - sh, 1), _NEG_INF))
            else:
                x = jnp.maximum(x, jnp.where(lane_n >= sh, pltpu.roll(x, sh, 1), _NEG_INF))
            sh *= 2
        return x

    for d, (op, last) in enumerate(((triu_b, LANES - 1), (tril_b, 0))):
        lf = _log_sigmoid(gt_ref[2 * d + 1] + gb_ref[2 * d + 1, head]) * _LOG2E
        hi, lo = _split_hi_lo(lf)
        b = _dot(hi, op) + _dot(lo, op)
        a = (gt_ref[2 * d] + gb_ref[2 * d, head]) * _LOG2E - b
        bl = jnp.broadcast_to(b[:, last:last + 1], (nc, LANES))
        ml = jnp.broadcast_to(jnp.max(bl + a, axis=1, keepdims=True), (nc, LANES))
        for r, val in enumerate((a, cummax_lanes(a, d == 1), b, jnp.exp2(bl + a - ml), bl, ml)):
            rows_ref[d * _ROWS_PER_DIR + r] = val

    def conv_silu(src_ref, w, c, t0):
        x = src_ref[pl.ds(t0, BLOCK), :]
        prev = src_ref[pl.ds(jnp.maximum(t0 - 1, 0), 1), :] * jnp.where(c > 0, 1.0, 0.0)
        nxt = src_ref[pl.ds(jnp.minimum(t0 + BLOCK, seq - 1), 1), :] * jnp.where(c < nc - 1, 1.0, 0.0)
        xd = jnp.where(rowi == 0, prev, pltpu.roll(x, 1, 0))
        xu = jnp.where(rowi == BLOCK - 1, nxt, pltpu.roll(x, BLOCK - 1, 0))
        y = xd * w[0:1] + x * w[1:2] + xu * w[2:3]
        return y * jax.nn.sigmoid(y)

    def prep(c, carry):
        t0 = pl.multiple_of(c * BLOCK, BLOCK)
        qc_ref[pl.ds(t0, BLOCK), :] = (conv_silu(cq_ref, wq_ref[...], c, t0) * (C_DIM ** -0.5)).astype(bf16)
        kt_ref[c] = conv_silu(ck_ref, wk_ref[...], c, t0).T.astype(bf16)
        vx_ref[pl.ds(t0, BLOCK), 0:C_DIM] = v_ref[pl.ds(t0, BLOCK), :]
        vx_ref[pl.ds(t0, BLOCK), C_DIM:2 * C_DIM] = ones_b
        return carry

    lax.fori_loop(0, nc, prep, 0, unroll=2)

    def load_rows(ch, backward):
        t0 = pl.multiple_of(ch * BLOCK, BLOCK)
        base = _ROWS_PER_DIR if backward else 0
        a, c, b, w, bl, ml = (rows_ref[base + r, pl.ds(ch, 1), :] for r in range(_ROWS_PER_DIR))
        kt = kt_ref[ch]
        return dict(t0=t0, backward=backward, bl=bl[:, 0:1], m_loc=ml[:, 0:1],
                    a_t=jnp.where(triu if backward else tril, a, _NEG_INF),
                    b_full=jnp.broadcast_to(b, (BLOCK, LANES)).T,
                    c_full=jnp.broadcast_to(c, (BLOCK, LANES)).T,
                    kw=(kt.astype(f32) * w).astype(bf16), kt=kt)

    def first_products(g):
        vx = vx_ref[pl.ds(g["t0"], BLOCK), :]
        g["qk"] = _dot(qc_ref[pl.ds(g["t0"], BLOCK), :], g["kt"])
        g["kv"] = _dot(g["kw"], vx)

    def decayed_scores(g):
        g["s0"] = (g["qk"] * jnp.exp2(g["a_t"] - g["c_full"])).astype(bf16)

    def inner_products(g):
        g["p0"] = _dot(g["s0"], vx_ref[pl.ds(g["t0"], BLOCK), :])

    def with_state(g, state, park_ref, other_ref):
        cx, m = state
        x = _dot(qc_ref[pl.ds(g["t0"], BLOCK), :], cx.astype(bf16))
        n = jnp.maximum(g["c_full"], m)
        e = jnp.exp2(g["c_full"] - n)
        iw = jnp.exp2(m - n)
        num = e * g["p0"][:, 0:C_DIM] + iw * x[:, 0:C_DIM]
        den = e * g["p0"][:, C_DIM:] + iw * x[:, C_DIM:]
        hh = num / jnp.maximum(jnp.abs(den), jnp.exp2(-(g["b_full"] + n)))
        rows = pl.ds(g["t0"], BLOCK)
        if other_ref is None:
            park_ref[rows, :] = hh
        else:
            r = hh + other_ref[rows, :]
            r = r * lax.rsqrt(jnp.mean(r * r, axis=-1, keepdims=True) + EPS)
            o_ref[rows, :] = (jax.nn.sigmoid(og_ref[rows, :].astype(f32)) * r).astype(o_ref.dtype)
        m_new = jnp.maximum(g["bl"] + m, g["m_loc"])
        cw = jnp.exp2(g["bl"] + m - m_new)
        f = jnp.exp2(g["m_loc"] - m_new)
        return cw * cx + f * g["kv"], m_new

    def make_body(second_half):
        def body(it, carry):
            sf, sb = carry
            steps = []
            for u in range(_SCAN_UNROLL):
                c = it * _SCAN_UNROLL + u
                steps.append(load_rows(c, False))
                steps.append(load_rows(nc - 1 - c, True))
            for g in steps:
                first_products(g)
            for g in steps:
                decayed_scores(g)
            for g in steps:
                inner_products(g)
            for g in steps:
                if g["backward"]:
                    sb = with_state(g, sb, ob_ref, of_ref if second_half else None)
                else:
                    sf = with_state(g, sf, of_ref, ob_ref if second_half else None)
            return sf, sb
        return body

    init = (jnp.zeros((C_DIM, 2 * C_DIM), f32), jnp.zeros((1, 1), f32))
    n_bodies = nc // _SCAN_UNROLL
    state = lax.fori_loop(0, n_bodies // 2, make_body(False), (init, init))
    lax.fori_loop(n_bodies // 2, n_bodies, make_body(True), state)


def _mlstm(pb, pf, gates_t, wq_conv, wk_conv, gate_bias):
    bsz, seq, _ = pb.shape
    assert seq % (8 * BLOCK) == 0
    kern = functools.partial(_mlstm_kernel, seq=seq)

    def col(base):
        return pl.BlockSpec((None, seq, LANES), lambda b, h: (b, 0, base + h))

    return pl.pallas_call(
        kern,
        grid=(bsz, C_HEADS),
        in_specs=[col(_PF_CQ), col(_PF_CK),
                  pl.BlockSpec((None, None, 4, seq // BLOCK, LANES), lambda b, h: (b, h, 0, 0, 0)),
                  col(_PB_CV), col(_PB_CO),
                  pl.BlockSpec((3, LANES), lambda b, h: (0, h)),
                  pl.BlockSpec((3, LANES), lambda b, h: (0, h)),
                  pl.BlockSpec(memory_space=pltpu.SMEM)],
        out_specs=pl.BlockSpec((None, seq, LANES), lambda b, h: (b, 0, h)),
        out_shape=jax.ShapeDtypeStruct((bsz, seq, C_HEADS * C_DIM), jnp.bfloat16),
        scratch_shapes=[pltpu.VMEM((seq, C_DIM), jnp.bfloat16),
                        pltpu.VMEM((seq // BLOCK, C_DIM, BLOCK), jnp.bfloat16),
                        pltpu.VMEM((seq, 2 * C_DIM), jnp.bfloat16),
                        pltpu.VMEM((2 * _ROWS_PER_DIR, seq // BLOCK, LANES), jnp.float32),
                        pltpu.VMEM((seq, C_DIM), jnp.float32), pltpu.VMEM((seq, C_DIM), jnp.float32)],
        compiler_params=_cparams("parallel", "parallel"),
        name="mlstm",
    )(pf, pf, gates_t, pb, pb, wq_conv, wk_conv, gate_bias)


def _window_attn_kernel(slopes_ref, sink_ref, q_ref, k_ref, v_ref, o_ref, kx_ref, vx_ref, *, seq):
    qi = pl.program_id(1)
    f32 = jnp.float32
    span = BLOCK + 2 * WINDOW
    group = D_HEADS // D_KV_HEADS

    @pl.when(qi == 0)
    def _():
        lane = lax.broadcasted_iota(jnp.int32, (seq, LANES), 1)
        ones = jnp.ones((seq, LANES), vx_ref.dtype)
        for src, dst in ((k_ref, kx_ref), (v_ref, vx_ref)):
            x = src[...].astype(f32)
            x0 = jnp.where(lane < D_DIM, x, 0.0)
            x1 = jnp.where(lane >= D_DIM, x, 0.0)
            for i, xi in enumerate((x0, pltpu.roll(x0, D_DIM, 1), pltpu.roll(x1, D_DIM, 1), x1)):
                dst[i, :, 0:LANES] = xi.astype(dst.dtype)
        for i in range(4):
            vx_ref[i, :, LANES:2 * LANES] = ones

    t0 = qi * BLOCK
    ws = pl.multiple_of(jnp.clip(t0 - WINDOW, 0, seq - span), BLOCK)
    rel = (lax.broadcasted_iota(jnp.int32, (BLOCK, span), 0)
           - lax.broadcasted_iota(jnp.int32, (BLOCK, span), 1)) + (t0 - ws)
    dist = jnp.abs(rel)
    neg_dist = jnp.where(dist <= WINDOW, dist.astype(f32) * (-_LOG2E), _NEG_INF)

    n_pairs = D_HEADS // 2
    lookahead = 2
    scores = {}

    def score_pair(pair):
        kv = (2 * pair) // group
        qp = (q_ref[:, pair * LANES:(pair + 1) * LANES].astype(f32) * (D_DIM ** -0.5 * _LOG2E)).astype(q_ref.dtype)
        for half in range(2):
            scores[2 * pair + half] = _dot_nt(qp, kx_ref[2 * kv + half, pl.ds(ws, span), :])

    def attend(hd, kv_slot):
        s = scores[hd] + slopes_ref[hd] * neg_dist
        snk = sink_ref[hd] * _LOG2E
        m = jnp.maximum(jnp.max(s, axis=1, keepdims=True), snk)
        o = _dot(jnp.exp2(s - m).astype(jnp.bfloat16), vx_ref[kv_slot, pl.ds(ws, span), :])
        return o[:, 0:LANES] / (o[:, LANES:] + jnp.exp2(snk - m))

    for pair in range(lookahead):
        score_pair(pair)
    for pair in range(n_pairs):
        kv = (2 * pair) // group
        out = attend(2 * pair, 2 * kv) + attend(2 * pair + 1, 2 * kv + 1)
        o_ref[:, pair * LANES:(pair + 1) * LANES] = out.astype(o_ref.dtype)
        if pair + lookahead < n_pairs:
            score_pair(pair + lookahead)


def _window_attention(pb, attn_sink, slopes_d):
    bsz, seq, _ = pb.shape
    kern = functools.partial(_window_attn_kernel, seq=seq)
    width = D_HEADS * D_DIM
    return pl.pallas_call(
        kern,
        grid=(bsz, seq // BLOCK),
        in_specs=[
            pl.BlockSpec(memory_space=pltpu.SMEM),
            pl.BlockSpec(memory_space=pltpu.SMEM),
            pl.BlockSpec((None, BLOCK, width), lambda b, i: (b, i, _PB_DQ512)),
            pl.BlockSpec((None, seq, LANES), lambda b, i: (b, 0, _PB_DK)),
            pl.BlockSpec((None, seq, LANES), lambda b, i: (b, 0, _PB_DV)),
        ],
        out_specs=pl.BlockSpec((None, BLOCK, width), lambda b, i: (b, i, 0)),
        out_shape=jax.ShapeDtypeStruct((bsz, seq, width), jnp.bfloat16),
        scratch_shapes=[pltpu.VMEM((4, seq, LANES), jnp.bfloat16), pltpu.VMEM((4, seq, 2 * LANES), jnp.bfloat16)],
        compiler_params=_cparams("parallel", "arbitrary"),
        name="window_attention",
    )(slopes_d, attn_sink, pb, pb, pb)


def _out_proj_kernel(x_ref, ya_ref, yb_ref, yc_ref, yd_ref, w_ref, o_ref):
    mixed = jnp.concatenate([ya_ref[...], yb_ref[...], yc_ref[...], yd_ref[...]], axis=1)
    o_ref[...] = x_ref[...] + _dot(mixed, w_ref[...])


def _out_proj(x2d, ys, w_out, tm):
    t, d = x2d.shape
    yspec = pl.BlockSpec((tm, MIXER_WIDTH), lambda i: (i, 0))
    return pl.pallas_call(
        _out_proj_kernel,
        grid=(t // tm,),
        in_specs=[pl.BlockSpec((tm, d), lambda i: (i, 0)), yspec, yspec, yspec, yspec,
                  pl.BlockSpec((d, d), lambda i: (0, 0), pipeline_mode=pl.Buffered(1))],
        out_specs=pl.BlockSpec((tm, d), lambda i: (i, 0)),
        out_shape=jax.ShapeDtypeStruct((t, d), jnp.float32),
        compiler_params=_cparams("parallel", vmem_limit=_BIG_VMEM_LIMIT),
        name="out_proj",
    )(x2d, *ys, w_out)


def _ffn_kernel(x_ref, xp_ref, xn_ref, nw_ref, wg_ref, wu_ref, cw_ref, wd_ref, fw_ref, o_ref,
                h_ref, *, seq, tm, final_norm):
    i = pl.program_id(0)
    j = pl.program_id(1)
    f32 = jnp.float32

    def norm(x):
        return x * lax.rsqrt(jnp.mean(x * x, axis=-1, keepdims=True) + EPS) * nw_ref[...]

    def d_ff_step(first=False, last=False):
        gate = _dot(h_ref[...], wg_ref[...])
        gm = gate[0:tm]
        rowi = lax.broadcasted_iota(jnp.int32, gm.shape, 0)
        g_dn = jnp.where(rowi == 0, gate[tm + 7:tm + 8], pltpu.roll(gm, 1, 0))
        g_up = jnp.where(rowi == tm - 1, gate[tm + 8:tm + 9], pltpu.roll(gm, tm - 1, 0))
        cw = cw_ref[...]
        g = g_dn * cw[0:1] + gm * cw[1:2] + g_up * cw[2:3]
        up = _dot(h_ref[0:tm, :], wu_ref[...])
        act = (g * jax.nn.sigmoid(g) * up).astype(wd_ref.dtype)
        down = _dot(act, wd_ref[...])
        if first:
            o_ref[...] = down
        elif last:
            y = x_ref[...] + o_ref[...] + down
            if final_norm:
                y = y * lax.rsqrt(jnp.mean(y * y, axis=-1, keepdims=True) + EPS) * fw_ref[...]
            o_ref[...] = y
        else:
            o_ref[...] += down

    @pl.when(j == 0)
    def _():
        h_ref[0:tm, :] = norm(x_ref[...]).astype(h_ref.dtype)
        has_prev = ((i * tm) % seq != 0).astype(f32)
        has_next = (((i + 1) * tm) % seq != 0).astype(f32)
        halo = jnp.concatenate([norm(xp_ref[...]) * has_prev, norm(xn_ref[...]) * has_next], axis=0)
        h_ref[tm:tm + 16, :] = halo.astype(h_ref.dtype)
        d_ff_step(True)

    last_j = pl.num_programs(1) - 1

    @pl.when((j > 0) & (j < last_j))
    def _():
        d_ff_step()

    @pl.when(j == last_j)
    def _():
        d_ff_step(last=True)


def _ffn(x2d, seq, nw, wg, wu, cw, wd, fw, final_norm, tm, tn):
    t, d = x2d.shape
    nff = wg.shape[1]
    rb = tm // 8
    last8 = t // 8 - 1
    kern = functools.partial(_ffn_kernel, seq=seq, tm=tm, final_norm=final_norm)
    return pl.pallas_call(
        kern,
        grid=(t // tm, nff // tn),
        in_specs=[
            pl.BlockSpec((tm, d), lambda i, j: (i, 0)),
            pl.BlockSpec((8, d), lambda i, j: (jnp.maximum(i * rb - 1, 0), 0)),
            pl.BlockSpec((8, d), lambda i, j: (jnp.minimum((i + 1) * rb, last8), 0)),
            pl.BlockSpec((1, d), lambda i, j: (0, 0)),
            pl.BlockSpec((d, tn), lambda i, j: (0, j)),
            pl.BlockSpec((d, tn), lambda i, j: (0, j)),
            pl.BlockSpec((3, tn), lambda i, j: (0, j)),
            pl.BlockSpec((tn, d), lambda i, j: (j, 0)),
            pl.BlockSpec((1, d), lambda i, j: (0, 0)),
        ],
        out_specs=pl.BlockSpec((tm, d), lambda i, j: (i, 0)),
        out_shape=jax.ShapeDtypeStruct((t, d), jnp.float32),
        scratch_shapes=[pltpu.VMEM((tm + 16, d), jnp.bfloat16)],
        compiler_params=_cparams("parallel", "arbitrary", vmem_limit=_BIG_VMEM_LIMIT),
        name="ffn",
    )(x2d, x2d, x2d, nw, wg, wu, cw, wd, fw)


def _prepare_layer(l, norm1_w, w_in, qk_conv_w, diff_lambda, ret_decay_logit, mlstm_gate_bias, attn_sink,
                   w_out, norm2_w, w_gate, w_up, ffn_conv_w, w_down):
    bf16 = jnp.bfloat16
    f32 = jnp.float32
    wi = w_in[l]
    w_pb = jnp.concatenate([wi[:, _OFF_A:_OFF_CQ], wi[:, _OFF_CV:_OFF_GATES], wi[:, _OFF_D:N_IN]], axis=1)
    wgates = jnp.pad(wi[:, _OFF_GATES:_OFF_D], ((0, 0), (0, LANES - N_GATES)))
    w_pf = jnp.concatenate([wi[:, _OFF_CQ:_OFF_CV], wgates], axis=1)
    gbias = mlstm_gate_bias[l].astype(f32)
    dec = jnp.zeros((8, LANES), f32).at[0:2, 0:B_HEADS].set(ret_decay_logit[l].astype(f32))
    return dict(
        norm1=norm1_w[l].reshape(1, D_MODEL).astype(f32),
        w_pb=w_pb.astype(bf16), w_pf=w_pf.astype(bf16),
        wq_conv=qk_conv_w[l][:, :C_HEADS * C_DIM].astype(f32),
        wk_conv=qk_conv_w[l][:, C_HEADS * C_DIM:].astype(f32),
        gbias=gbias, dec=dec,
        diff_lambda=diff_lambda[l].astype(f32), sink=attn_sink[l].astype(f32),
        w_out=w_out[l].astype(bf16),
        norm2=norm2_w[l].reshape(1, D_MODEL).astype(f32),
        w_gate=w_gate[l].astype(bf16), w_up=w_up[l].astype(bf16),
        ffn_conv=ffn_conv_w[l].astype(f32), w_down=w_down[l].astype(bf16),
    )


def _tile(n, pref):
    t = min(pref, n)
    while n % t:
        t //= 2
    return t


def _layer(x2d, bsz, seq, layer_idx, p, slopes, final_w, final_norm):
    t = bsz * seq
    pb, pf = _in_proj(x2d, p["norm1"], p["w_pb"], p["w_pf"], _tile(t, _TOKEN_TILE), _IN_PROJ_TN)
    pb = pb.reshape(bsz, seq, PB_WIDTH)
    pf = pf.reshape(bsz, seq, PF_WIDTH)
    gates_t = pf[:, :, _PF_GATES * LANES:_PF_GATES * LANES + N_GATES].reshape(bsz, seq, 4, C_HEADS)
    gates_t = jnp.transpose(gates_t, (0, 3, 2, 1)).reshape(bsz, C_HEADS, 4, seq // BLOCK, BLOCK)
    y_a = _diff_attention(pb, p["diff_lambda"], slopes[D_HEADS:], layer_idx)
    y_b = _retention(pb, p["dec"])
    y_c = _mlstm(pb, pf, gates_t, p["wq_conv"], p["wk_conv"], p["gbias"])
    y_d = _window_attention(pb, p["sink"], slopes[:D_HEADS])
    ys = [y.reshape(t, MIXER_WIDTH) for y in (y_a, y_b, y_c, y_d)]
    x1 = _out_proj(x2d, ys, p["w_out"], _tile(t, _TOKEN_TILE))
    return _ffn(x1, seq, p["norm2"], p["w_gate"], p["w_up"], p["ffn_conv"], p["w_down"], final_w,
                final_norm, _tile(seq, _TOKEN_TILE), _FFN_TN)


def _trunk(x, layers, slopes, final_w):
    bsz, seq, d = x.shape
    x2d = x.reshape(bsz * seq, d)
    for l, p in enumerate(layers):
        x2d = _layer(x2d, bsz, seq, l, p, slopes, final_w, l == len(layers) - 1)
    return x2d.reshape(bsz, seq, d)


def kernel(x_prompt, x_sample, norm1_w, w_in, qk_conv_w, diff_lambda, ret_decay_logit, mlstm_gate_bias,
           attn_sink, w_out, norm2_w, w_gate, w_up, ffn_conv_w, w_down, final_norm_w):
    layers = [_prepare_layer(l, norm1_w, w_in, qk_conv_w, diff_lambda, ret_decay_logit, mlstm_gate_bias,
                             attn_sink, w_out, norm2_w, w_gate, w_up, ffn_conv_w, w_down)
              for l in range(DEPTH)]
    n = A_HEADS + D_HEADS
    slopes = 2.0 ** (-8.0 * jnp.arange(1, n + 1, dtype=jnp.float32) / n)
    final_w = final_norm_w.reshape(1, D_MODEL).astype(jnp.float32)
    return (_trunk(x_prompt, layers, slopes, final_w), _trunk(x_sample, layers, slopes, final_w))
```
